```python
import math
import jax, jax.numpy as jnp
from jax import lax
import numpy as np

D_MODEL = 1024
BATCH = 8
SEQ = 2048
DEPTH = 2

N_GROUPS = 4
GROUP_WIDTH = D_MODEL // N_GROUPS
HEAD_DIM = 64
GROUP_HEADS = GROUP_WIDTH // HEAD_DIM
CONV_CH = GROUP_WIDTH
CONV_WIDTH = 3
NSA_HEADS = GROUP_HEADS
CMP_LEN = 32
CMP_STRIDE = 16
SLC_LEN = 64
N_SLC = 16
WINDOW = 512
CMP_HIDDEN = GROUP_WIDTH
MLA_HEADS = GROUP_HEADS
Q_LORA = 3 * GROUP_WIDTH // 4
KV_LORA = GROUP_WIDTH // 2
NOPE_DIM = HEAD_DIM
ROPE_DIM = HEAD_DIM // 2
V_DIM = HEAD_DIM
ROPE_THETA = 10000.0
SB_HEADS = GROUP_HEADS
N_BUCKETS = 32
MAX_DISTANCE = 128
D_FF = 4 * D_MODEL
Q_BLOCK = 128
EPS = 1e-6

A_SIZES = [CONV_CH] * 3
NSA_SIZES = [NSA_HEADS * HEAD_DIM] + [HEAD_DIM] * 6 + [3 * NSA_HEADS]
MLA_SIZES = [Q_LORA, KV_LORA, ROPE_DIM]
SB_SIZES = [SB_HEADS * HEAD_DIM] * 3
IN_SIZES = A_SIZES + NSA_SIZES + MLA_SIZES + SB_SIZES
IN_COLS = sum(IN_SIZES)

kernel_name = "hymba_style_conv_nsa_mla_stickbreaking_hybrid"

f32 = jnp.float32


def rms_norm(x, w):
    x32 = x.astype(f32)
    y = x32 * lax.rsqrt(jnp.mean(x32 * x32, axis=-1, keepdims=True) + EPS)
    return (y * w.astype(f32)).astype(x.dtype)


def masked_softmax(logits, mask):
    logits = jnp.where(mask, logits, -jnp.inf)
    m = jnp.max(logits, axis=-1, keepdims=True)
    m = jnp.where(jnp.isfinite(m), m, 0.0)
    p = jnp.exp(logits - m)
    s = jnp.sum(p, axis=-1, keepdims=True)
    return p / jnp.where(s > 0, s, 1.0)


def t5_bucket(dist):
    max_exact = N_BUCKETS // 2
    d = jnp.maximum(dist, 0)
    large = max_exact + (jnp.log(jnp.maximum(d, 1).astype(f32) / max_exact)
                         / math.log(MAX_DISTANCE / max_exact) * (N_BUCKETS - max_exact)).astype(jnp.int32)
    large = jnp.minimum(large, N_BUCKETS - 1)
    return jnp.where(d < max_exact, d, large)


def rope_tables(S):
    inv = 1.0 / (ROPE_THETA ** (jnp.arange(0, ROPE_DIM, 2, dtype=f32) / ROPE_DIM))
    ang = jnp.arange(S, dtype=f32)[:, None] * inv[None, :]
    return jnp.cos(ang), jnp.sin(ang)


def apply_rope(x, cos, sin):
    x32 = x.astype(f32)
    x1, x2 = jnp.split(x32, 2, axis=-1)
    c = cos[:, None, :]
    s = sin[:, None, :]
    return jnp.concatenate([x1 * c - x2 * s, x1 * s + x2 * c], axis=-1).astype(x.dtype)


def split_cols(h, sizes):
    return jnp.split(h, np.cumsum(sizes)[:-1].tolist(), axis=-1)


def short_conv_mixer(b_gate, c_gate, u, conv_w, conv_b):
    v = c_gate * u
    y = lax.conv_general_dilated(v, conv_w[:, None, :].astype(v.dtype), window_strides=(1,),
                                 padding=[(CONV_WIDTH - 1, 0)], dimension_numbers=('NWC', 'WIO', 'NWC'),
                                 feature_group_count=v.shape[-1])
    return b_gate * (y + conv_b)


def compress_tokens(kv, cmp_idx, pos, w1, w2):
    B = kv.shape[0]
    blocks = kv[:, cmp_idx] + pos
    hdn = jax.nn.silu(blocks.reshape(B, blocks.shape[1], -1) @ w1)
    return hdn @ w2


def cmp_slc_overlap(n_cmp, n_slc):
    starts = np.arange(n_cmp) * CMP_STRIDE
    ends = starts + CMP_LEN
    s0 = np.arange(n_slc) * SLC_LEN
    s1 = s0 + SLC_LEN
    ov = np.clip(np.minimum(ends[:, None], s1[None]) - np.maximum(starts[:, None], s0[None]), 0, None)
    return (ov / CMP_LEN).astype(np.float32)


def nsa_mixer(q, kc_raw, vc_raw, ks, vs, kw, vw, gate_logits, qn, kn, cmp_pos, cmp_w1, cmp_w2, rel_bias):
    B, S, H, Dh = q.shape
    scale = Dh ** -0.5
    q = rms_norm(q, qn)
    tpos = jnp.arange(S)
    n_cmp = (S - CMP_LEN) // CMP_STRIDE + 1
    cmp_idx = np.arange(n_cmp)[:, None] * CMP_STRIDE + np.arange(CMP_LEN)[None, :]
    kc = rms_norm(compress_tokens(kc_raw, cmp_idx, cmp_pos[0], cmp_w1[0], cmp_w2[0]), kn[0])
    vc = compress_tokens(vc_raw, cmp_idx, cmp_pos[1], cmp_w1[1], cmp_w2[1])
    cmp_end = jnp.arange(n_cmp) * CMP_STRIDE + CMP_LEN - 1
    dist_c = tpos[:, None] - cmp_end[None, :]
    bias_c = rel_bias[t5_bucket(dist_c)].transpose(2, 0, 1)
    logits_c = jnp.einsum('bshd,bnd->bhsn', q, kc).astype(f32) * scale + bias_c
    p_c = masked_softmax(logits_c, dist_c >= 0)
    o_cmp = jnp.einsum('bhsn,bnd->bshd', p_c.astype(vc.dtype), vc)
    n_slc = S // SLC_LEN
    n_sel = min(N_SLC, n_slc)
    score = jnp.einsum('bhsn,nj->bsj', p_c, jnp.asarray(cmp_slc_overlap(n_cmp, n_slc)))
    blk = jnp.arange(n_slc)[None, :]
    t_blk = (tpos // SLC_LEN)[:, None]
    valid = blk * SLC_LEN <= tpos[:, None]
    forced = (blk == 0) | (blk == t_blk) | (blk == t_blk - 1)
    score = jnp.where(forced, jnp.inf, jnp.where(valid, score, -jnp.inf))
    _, sel = lax.top_k(score, n_sel)
    nq = S // Q_BLOCK
    ks_blocks = rms_norm(ks, kn[1]).reshape(B, n_slc, SLC_LEN, Dh)
    vs_blocks = vs.reshape(B, n_slc, SLC_LEN, Dh)
    kw_pad = jnp.pad(rms_norm(kw, kn[2]), ((0, 0), (WINDOW, 0), (0, 0)))
    vw_pad = jnp.pad(vw, ((0, 0), (WINDOW, 0), (0, 0)))
    band = WINDOW + Q_BLOCK
    qi = jnp.arange(Q_BLOCK)
    ki = jnp.arange(band)
    dist_w = qi[:, None] + WINDOW - ki[None, :]
    bias_w = rel_bias[t5_bucket(dist_w)].transpose(2, 0, 1)
    in_window = (dist_w >= 0) & (dist_w < WINDOW)
    bidx = jnp.arange(B)[:, None, None]

    def block(args):
        q_blk, sel_blk, t0 = args
        t_q = t0 + qi
        kg = ks_blocks[bidx, sel_blk]
        vg = vs_blocks[bidx, sel_blk]
        kpos = (sel_blk[..., None] * SLC_LEN + jnp.arange(SLC_LEN)).reshape(B, Q_BLOCK, -1)
        dist_s = t_q[None, :, None] - kpos
        bias_s = rel_bias[t5_bucket(dist_s)].transpose(0, 3, 1, 2)
        logits_s = jnp.einsum('bqhd,bqnkd->bhqnk', q_blk, kg).reshape(B, H, Q_BLOCK, -1).astype(f32) * scale + bias_s
        p_s = masked_softmax(logits_s, (dist_s >= 0)[:, None])
        o_s = jnp.einsum('bhqnk,bqnkd->bqhd', p_s.reshape(B, H, Q_BLOCK, n_sel, SLC_LEN).astype(vg.dtype), vg)
        kb = lax.dynamic_slice_in_dim(kw_pad, t0, band, axis=1)
        vb = lax.dynamic_slice_in_dim(vw_pad, t0, band, axis=1)
        mask_w = in_window & (t0 - WINDOW + ki >= 0)[None, :]
        logits_w = jnp.einsum('bqhd,bkd->bhqk', q_blk, kb).astype(f32) * scale + bias_w
        p_w = masked_softmax(logits_w, mask_w)
        o_w = jnp.einsum('bhqk,bkd->bqhd', p_w.astype(vb.dtype), vb)
        return o_s, o_w

    q_blocks = q.reshape(B, nq, Q_BLOCK, H, Dh).swapaxes(0, 1)
    sel_blocks = sel.reshape(B, nq, Q_BLOCK, n_sel).swapaxes(0, 1)
    o_slc, o_win = lax.map(block, (q_blocks, sel_blocks, jnp.arange(nq) * Q_BLOCK))
    o_slc = o_slc.swapaxes(0, 1).reshape(B, S, H, Dh)
    o_win = o_win.swapaxes(0, 1).reshape(B, S, H, Dh)
    g = jax.nn.sigmoid(gate_logits.astype(f32)).reshape(B, S, H, 3).astype(q.dtype)
    out = g[..., 0:1] * o_cmp + g[..., 1:2] * o_slc + g[..., 2:3] * o_win
    return out.reshape(B, S, H * Dh)


def causal_block_attention(q, k, v, scale):
    B, S, H, _ = q.shape
    nq = S // Q_BLOCK
    kpos = jnp.arange(S)

    def one(args):
        q_blk, t0 = args
        logits = jnp.einsum('bqhd,bkhd->bhqk', q_blk, k).astype(f32) * scale
        mask = kpos[None, :] <= (t0 + jnp.arange(Q_BLOCK))[:, None]
        p = masked_softmax(logits, mask)
        return jnp.einsum('bhqk,bkhd->bqhd', p.astype(v.dtype), v)

    qb = q.reshape(B, nq, Q_BLOCK, H, -1).swapaxes(0, 1)
    out = lax.map(one, (qb, jnp.arange(nq) * Q_BLOCK))
    return out.swapaxes(0, 1).reshape(B, S, H, -1)


def mla_mixer(c_q, c_kv, k_rope, qa_norm, kv_norm, wq_b, wkv_b, qn, kn, cos, sin):
    B, S, _ = c_q.shape
    q = (rms_norm(c_q, qa_norm) @ wq_b).reshape(B, S, MLA_HEADS, NOPE_DIM + ROPE_DIM)
    q = jnp.concatenate([q[..., :NOPE_DIM], apply_rope(q[..., NOPE_DIM:], cos, sin)], axis=-1)
    kv = (rms_norm(c_kv, kv_norm) @ wkv_b).reshape(B, S, MLA_HEADS, NOPE_DIM + V_DIM)
    k_nope, v = kv[..., :NOPE_DIM], kv[..., NOPE_DIM:]
    k_r = apply_rope(k_rope[:, :, None, :], cos, sin)
    k = jnp.concatenate([k_nope, jnp.broadcast_to(k_r, (B, S, MLA_HEADS, ROPE_DIM))], axis=-1)
    q = rms_norm(q, qn)
    k = rms_norm(k, kn)
    o = causal_block_attention(q, k, v, (NOPE_DIM + ROPE_DIM) ** -0.5)
    return o.reshape(B, S, MLA_HEADS * V_DIM)


def stick_breaking_mixer(q, k, v):
    B, S, H, Dh = q.shape
    scale = Dh ** -0.5
    nq = S // Q_BLOCK
    kpos = jnp.arange(S)

    def one(args):
        q_blk, t0 = args
        z = jnp.einsum('bqhd,bkhd->bhqk', q_blk, k).astype(f32) * scale
        strict = kpos[None, :] < (t0 + jnp.arange(Q_BLOCK))[:, None]
        log_1m = jnp.where(strict, jax.nn.log_sigmoid(-z), 0.0)
        rem = lax.cumsum(log_1m, axis=3, reverse=True) - log_1m
        a = jnp.where(strict, jnp.exp(jax.nn.log_sigmoid(z) + rem), 0.0)
        return jnp.einsum('bhqk,bkhd->bqhd', a.astype(v.dtype), v)

    qb = q.reshape(B, nq, Q_BLOCK, H, Dh).swapaxes(0, 1)
    out = lax.map(one, (qb, jnp.arange(nq) * Q_BLOCK))
    return out.swapaxes(0, 1).reshape(B, S, H * Dh)


def setup_inputs(seed: int = 0) -> dict:
    key = jax.random.key(seed)
    ks = jax.random.split(key, 24)
    L = DEPTH
    nrm = lambda k, shape, s: jax.random.normal(k, shape, f32) * s
    gain = lambda k, shape: 1.0 + 0.02 * jax.random.normal(k, shape, f32)
    qk_dim = NOPE_DIM + ROPE_DIM
    return {
        "x": jax.random.normal(ks[0], (BATCH, SEQ, D_MODEL), f32),
        "rel_bias": nrm(ks[1], (N_BUCKETS, NSA_HEADS), 0.5),
        "norm1_w": gain(ks[2], (L, D_MODEL)),
        "w_in": nrm(ks[3], (L, D_MODEL, IN_COLS), D_MODEL ** -0.5),
        "conv_w": nrm(ks[4], (L, CONV_WIDTH, CONV_CH), CONV_WIDTH ** -0.5),
        "conv_b": nrm(ks[5], (L, CONV_CH), 0.02),
        "nsa_q_norm": gain(ks[6], (L, HEAD_DIM)),
        "nsa_k_norm": gain(ks[7], (L, 3, HEAD_DIM)),
        "cmp_pos": nrm(ks[8], (L, 2, CMP_LEN, HEAD_DIM), 0.1),
        "cmp_w1": nrm(ks[9], (L, 2, CMP_LEN * HEAD_DIM, CMP_HIDDEN), (CMP_LEN * HEAD_DIM) ** -0.5),
        "cmp_w2": nrm(ks[10], (L, 2, CMP_HIDDEN, HEAD_DIM), CMP_HIDDEN ** -0.5),
        "mla_q_a_norm": gain(ks[11], (L, Q_LORA)),
        "mla_kv_norm": gain(ks[12], (L, KV_LORA)),
        "mla_wq_b": nrm(ks[13], (L, Q_LORA, MLA_HEADS * qk_dim), Q_LORA ** -0.5),
        "mla_wkv_b": nrm(ks[14], (L, KV_LORA, MLA_HEADS * (NOPE_DIM + V_DIM)), KV_LORA ** -0.5),
        "mla_q_norm": gain(ks[15], (L, qk_dim)),
        "mla_k_norm": gain(ks[16], (L, qk_dim)),
        "out_norm_w": gain(ks[17], (L, D_MODEL)),
        "w_out": nrm(ks[18], (L, D_MODEL, D_MODEL), D_MODEL ** -0.5),
        "norm2_w": gain(ks[19], (L, D_MODEL)),
        "ffn_w1": nrm(ks[20], (L, D_MODEL, D_FF), D_MODEL ** -0.5),
        "ffn_w2": nrm(ks[21], (L, D_FF, D_MODEL), D_FF ** -0.5),
    }


def reference(x, rel_bias, norm1_w, w_in, conv_w, conv_b, nsa_q_norm, nsa_k_norm, cmp_pos, cmp_w1, cmp_w2,
              mla_q_a_norm, mla_kv_norm, mla_wq_b, mla_wkv_b, mla_q_norm, mla_k_norm, out_norm_w, w_out,
              norm2_w, ffn_w1, ffn_w2):
    B, S, D = x.shape
    cos, sin = rope_tables(S)
    for l in range(DEPTH):
        h = rms_norm(x, norm1_w[l])
        cols = split_cols(h @ w_in[l], IN_SIZES)
        a_b, a_c, a_u = cols[0:3]
        n_q, n_kc, n_vc, n_ks, n_vs, n_kw, n_vw, n_g = cols[3:11]
        m_cq, m_ckv, m_kr = cols[11:14]
        s_q, s_k, s_v = cols[14:17]
        y_a = short_conv_mixer(a_b, a_c, a_u, conv_w[l], conv_b[l])
        y_b = nsa_mixer(n_q.reshape(B, S, NSA_HEADS, HEAD_DIM), n_kc, n_vc, n_ks, n_vs, n_kw, n_vw, n_g,
                        nsa_q_norm[l], nsa_k_norm[l], cmp_pos[l], cmp_w1[l], cmp_w2[l], rel_bias)
        y_c = mla_mixer(m_cq, m_ckv, m_kr, mla_q_a_norm[l], mla_kv_norm[l], mla_wq_b[l], mla_wkv_b[l],
                        mla_q_norm[l], mla_k_norm[l], cos, sin)
        y_d = stick_breaking_mixer(s_q.reshape(B, S, SB_HEADS, HEAD_DIM), s_k.reshape(B, S, SB_HEADS, HEAD_DIM),
                                   s_v.reshape(B, S, SB_HEADS, HEAD_DIM))
        y = jnp.concatenate([y_a, y_b, y_c, y_d], axis=-1).reshape(B, S, N_GROUPS, GROUP_WIDTH)
        y = rms_norm(y, out_norm_w[l].reshape(N_GROUPS, GROUP_WIDTH)).reshape(B, S, D)
        x = x + y @ w_out[l]
        h2 = rms_norm(x, norm2_w[l])
        x = x + jnp.square(jax.nn.relu(h2 @ ffn_w1[l])) @ ffn_w2[l]
    return x
```

```python
import functools
import math

import jax
import jax.numpy as jnp
import numpy as np
from jax import lax
from jax.experimental import pallas as pl
from jax.experimental.pallas import tpu as pltpu

F32 = jnp.float32
BF16 = jnp.bfloat16

D_MODEL = 1024
GROUP_WIDTH = 256
HEAD_DIM = 64
N_HEADS = 4
LANES = 128
CMP_LEN = 32
CMP_STRIDE = 16
SLC_LEN = 64
N_SEL = 16
WINDOW = 512
CMP_HIDDEN = 256
Q_LORA = 192
KV_LORA = 128
ROPE_DIM = 32
QK_DIM = 96
ROPE_THETA = 10000.0
N_BUCKETS = 32
MAX_DISTANCE = 128
D_FF = 4096
EPS = 1e-6
NEG = -1e30

NP = 2816
CB_NQ = 3
CB_KCVC = 8
CB_KSVS = 9
CB_KWVW = 10
CB_GATE = 11
CB_CQ = 6
CB_CKV = 14
CB_KR = 15
CB_SQ = 8
CB_SK = 9
CB_SV = 10

VMEM_LIMIT = 56 * 1024 * 1024

NT_DIMS = (((1,), (1,)), ((), ()))


def _params(sem):
    return pltpu.CompilerParams(dimension_semantics=sem, vmem_limit_bytes=VMEM_LIMIT)


def _nt(a, b):
    return lax.dot_general(a, b, NT_DIMS, preferred_element_type=F32)


def _dot(a, b):
    return jnp.dot(a, b, preferred_element_type=F32)


def _lane(shape):
    return lax.broadcasted_iota(jnp.int32, shape, len(shape) - 1)


def _row(shape):
    return lax.broadcasted_iota(jnp.int32, shape, len(shape) - 2)


def _inproj_kernel(x_ref, nw_ref, w_ref, o_ref):
    x = x_ref[...]
    ms = jnp.mean(x * x, axis=-1, keepdims=True)
    h = (x * lax.rsqrt(ms + EPS) * nw_ref[...]).astype(BF16)
    o_ref[...] = _dot(h, w_ref[...])


def _inproj(x2d, nw, w, tm):
    t = x2d.shape[0]
    return pl.pallas_call(
        _inproj_kernel,
        grid=(t // tm,),
        in_specs=[
            pl.BlockSpec((tm, D_MODEL), lambda i: (i, 0)),
            pl.BlockSpec((1, D_MODEL), lambda i: (0, 0)),
            pl.BlockSpec((D_MODEL, NP), lambda i: (0, 0), pipeline_mode=pl.Buffered(1)),
        ],
        out_specs=pl.BlockSpec((tm, NP), lambda i: (i, 0)),
        out_shape=jax.ShapeDtypeStruct((t, NP), F32),
        compiler_params=_params(("arbitrary",)),
        name="inproj",
    )(x2d, nw, w)


def _compress_kernel(x_ref, pos_ref, w1_ref, w2_ref, knw_ref, o_ref):
    ng = x_ref.shape[1]
    acc_a = jnp.zeros((ng, 2 * CMP_HIDDEN), F32)
    acc_b = jnp.zeros((ng, 2 * CMP_HIDDEN), F32)
    for i in range(CMP_STRIDE):
        x = x_ref[0, :, i, :]
        xa = (x + pos_ref[i:i + 1, :]).astype(BF16)
        xb = (x + pos_ref[CMP_STRIDE + i:CMP_STRIDE + i + 1, :]).astype(BF16)
        acc_a += _dot(xa, w1_ref[i])
        acc_b += _dot(xb, w1_ref[CMP_STRIDE + i])
    pre = acc_a + pltpu.roll(acc_b, ng - 1, 0)
    hdn = pre * jax.nn.sigmoid(pre)
    out = _dot(hdn.astype(BF16), w2_ref[...])
    lane = _lane(out.shape)
    is_k = lane < HEAD_DIM
    ss = jnp.sum(jnp.where(is_k, out * out, 0.0), axis=-1, keepdims=True) * (1.0 / HEAD_DIM)
    o_ref[0] = jnp.where(is_k, out * lax.rsqrt(ss + EPS) * knw_ref[...], out)


def _compress(proj, pos, w1, w2, knw, b, s):
    ng = s // CMP_STRIDE
    x4 = proj.reshape(b, ng, CMP_STRIDE, NP)
    return pl.pallas_call(
        _compress_kernel,
        grid=(b,),
        in_specs=[
            pl.BlockSpec((1, ng, CMP_STRIDE, LANES), lambda i: (i, 0, 0, CB_KCVC)),
            pl.BlockSpec((CMP_LEN, LANES), lambda i: (0, 0)),
            pl.BlockSpec((CMP_LEN, LANES, 2 * CMP_HIDDEN), lambda i: (0, 0, 0)),
            pl.BlockSpec((2 * CMP_HIDDEN, LANES), lambda i: (0, 0)),
            pl.BlockSpec((1, LANES), lambda i: (0, 0)),
        ],
        out_specs=pl.BlockSpec((1, ng, LANES), lambda i: (i, 0, 0)),
        out_shape=jax.ShapeDtypeStruct((b, ng, LANES), F32),
        compiler_params=_params(("arbitrary",)),
        name="nsa_compress",
    )(x4, pos, w1, w2, knw)


def _bucket_bias(bidx, relb_ref, h, fill):
    acc = jnp.full(bidx.shape, fill, F32)
    for bk in range(N_BUCKETS):
        acc = jnp.where(bidx == bk, relb_ref[bk, h], acc)
    return acc


def _dup_halves(x, take_low):
    lane = _lane(x.shape)
    keep = (lane < HEAD_DIM) if take_low else (lane >= HEAD_DIM)
    y = jnp.where(keep, x, 0.0)
    return y + pltpu.roll(y, HEAD_DIM, 1)


def _softmax_step(s, v_tile, m_s, l_s, acc_s):
    m_old = m_s[...]
    m_new = jnp.maximum(m_old, jnp.max(s, axis=-1, keepdims=True))
    alpha = jnp.exp(m_old - m_new)
    p = jnp.exp(s - m_new)
    l_s[...] = alpha * l_s[...] + jnp.sum(p, axis=-1, keepdims=True)
    acc_s[...] = alpha * acc_s[...] + _dot(p.astype(BF16), v_tile)
    m_s[...] = m_new


def _nsa_kernel(relb_ref, q_ref, g_ref, kcvc_ref, ksvs_ref, kwvw_ref, bidxc_ref, bidx2_ref, ov_ref, e_ref,
                qnw_ref, knw_ref, o_ref,
                biasc_s, bias2_s, ks_s, vs_s, kw_s, vw_s, kc_s, vc_s, m_s, l_s, acc_s, *, tq, s_len):
    b = pl.program_id(0)
    i = pl.program_id(1)
    m_rows = N_HEADS * tq
    n_win = WINDOW // tq

    @pl.when((b == 0) & (i == 0))
    def _build_bias_tables():
        r = _row((tq, tq))
        c = _lane((tq, tq))
        for h in range(N_HEADS):
            rows = slice(h * tq, (h + 1) * tq)
            far = jnp.full((tq, tq), relb_ref[N_BUCKETS - 1, h], F32)
            bias2_s[0, rows, :] = jnp.where(r >= c, _bucket_bias(bidx2_ref[0], relb_ref, h, NEG), NEG)
            bias2_s[1, rows, :] = _bucket_bias(bidx2_ref[1], relb_ref, h, NEG)
            bias2_s[2, rows, :] = far
            bias2_s[3, rows, :] = jnp.where(c > r, far, NEG)

        def body(t, carry):
            bi = bidxc_ref[pl.ds(pl.multiple_of(t * tq, tq), tq), :]
            for h in range(N_HEADS):
                biasc_s[t, h * tq:(h + 1) * tq, :] = _bucket_bias(bi, relb_ref, h, NEG)
            return carry

        lax.fori_loop(0, s_len // tq, body, 0)

    @pl.when(i == 0)
    def _prep_kv():
        ch = 256

        def body(t, carry):
            rows = pl.ds(pl.multiple_of(t * ch, ch), ch)
            for src, kdst, vdst, widx in ((ksvs_ref, ks_s, vs_s, 0), (kwvw_ref, kw_s, vw_s, 1)):
                x = src[rows, :]
                lane = _lane(x.shape)
                is_k = lane < HEAD_DIM
                ss = jnp.sum(jnp.where(is_k, x * x, 0.0), axis=-1, keepdims=True) * (1.0 / HEAD_DIM)
                kn = x * lax.rsqrt(ss + EPS) * knw_ref[widx:widx + 1, :]
                kdst[rows, :] = _dup_halves(kn, True).astype(BF16)
                vdst[rows, :] = _dup_halves(x, False).astype(BF16)
            return carry

        lax.fori_loop(0, s_len // ch, body, 0)
        xc = kcvc_ref[0]
        kc_s[...] = _dup_halves(xc, True).astype(BF16)
        vc_s[...] = _dup_halves(xc, False).astype(BF16)

    q = q_ref[...]
    lane = _lane((tq, LANES))
    qs = []
    for h in range(N_HEADS):
        x = q[:, LANES * (h // 2):LANES * (h // 2 + 1)]
        mine = (lane < HEAD_DIM) if h % 2 == 0 else (lane >= HEAD_DIM)
        xm = jnp.where(mine, x, 0.0)
        ss = jnp.sum(xm * xm, axis=-1, keepdims=True) * (1.0 / HEAD_DIM)
        qs.append((xm * lax.rsqrt(ss + EPS) * qnw_ref[...] * (HEAD_DIM ** -0.5)).astype(BF16))
    qst = jnp.concatenate(qs, axis=0)

    lc = _nt(qst, kc_s[...]) + biasc_s[i]
    mc = jnp.max(lc, axis=-1, keepdims=True)
    pc = jnp.where(lc > 0.5 * NEG, jnp.exp(lc - mc), 0.0)
    den = jnp.sum(pc, axis=-1, keepdims=True)
    pc = pc / jnp.where(den > 0.0, den, 1.0)
    o_cmp = _dot(pc.astype(BF16), vc_s[...])
    psum = pc[0:tq] + pc[tq:2 * tq] + pc[2 * tq:3 * tq] + pc[3 * tq:4 * tq]
    p_hi = psum.astype(BF16)
    p_lo = (psum - p_hi.astype(F32)).astype(BF16)
    score = _dot(p_hi, ov_ref[...]) + _dot(p_lo, ov_ref[...])

    n_slc = s_len // SLC_LEN
    tpos = i * tq + _row((tq, LANES))
    tblk = tpos // SLC_LEN
    valid = (lane * SLC_LEN <= tpos) & (lane < n_slc)
    forced = (lane == 0) | (lane == tblk) | (lane == tblk - 1)
    sc = jnp.where(forced & valid, jnp.inf, jnp.where(valid, score, -jnp.inf))
    rank = jnp.zeros((tq, LANES), F32)
    for k in range(n_slc):
        ck = sc[:, k:k + 1]
        beats = (ck > sc) | ((ck == sc) & (lane > k))
        rank += jnp.where(beats, 1.0, 0.0)
    sel = jnp.where((rank < float(min(N_SEL, n_slc))) & valid, 1.0, 0.0).astype(BF16)

    def reset():
        m_s[...] = jnp.full(m_s.shape, NEG, F32)
        l_s[...] = jnp.zeros(l_s.shape, F32)
        acc_s[...] = jnp.zeros(acc_s.shape, F32)

    def finish():
        return acc_s[...] / l_s[...]

    reset()

    def sel_body(jj, carry):
        j = i - jj
        rows = pl.ds(pl.multiple_of(j * tq, tq), tq)
        s = _nt(qst, ks_s[rows, :]) + bias2_s[jnp.minimum(jj, 2)]
        keep = _dot(sel, e_ref[j])
        keep = jnp.concatenate([keep] * N_HEADS, axis=0)
        _softmax_step(jnp.where(keep > 0.5, s, NEG), vs_s[rows, :], m_s, l_s, acc_s)
        return carry

    lax.fori_loop(0, i + 1, sel_body, 0)
    o_slc = finish()

    reset()

    def win_body(jj, carry):
        j = i - jj
        rows = pl.ds(pl.multiple_of(j * tq, tq), tq)
        kind = jnp.where(jj == n_win, 3, jnp.minimum(jj, 2))
        s = _nt(qst, kw_s[rows, :]) + bias2_s[kind]
        _softmax_step(s, vw_s[rows, :], m_s, l_s, acc_s)
        return carry

    lax.fori_loop(0, jnp.minimum(i, n_win) + 1, win_body, 0)
    o_win = finish()

    g = jax.nn.sigmoid(g_ref[...])
    for cgrp in range(2):
        ys = []
        for half in range(2):
            h = 2 * cgrp + half
            rows = slice(h * tq, (h + 1) * tq)
            ys.append(g[:, 3 * h:3 * h + 1] * o_cmp[rows] + g[:, 3 * h + 1:3 * h + 2] * o_slc[rows]
                      + g[:, 3 * h + 2:3 * h + 3] * o_win[rows])
        o_ref[:, LANES * cgrp:LANES * (cgrp + 1)] = jnp.where(lane < HEAD_DIM, ys[0], ys[1])


def _nsa_attention(proj, kcvc, rel_bias, bidxc, bidx2, ov, emat, qnw, knw, b, s, tq):
    t = b * s
    nq = s // tq
    m_rows = N_HEADS * tq
    ng = s // CMP_STRIDE
    kern = functools.partial(_nsa_kernel, tq=tq, s_len=s)
    return pl.pallas_call(
        kern,
        grid=(b, nq),
        in_specs=[
            pl.BlockSpec(memory_space=pltpu.SMEM),
            pl.BlockSpec((tq, 2 * LANES), lambda bi, i: (bi * nq + i, CB_NQ)),
            pl.BlockSpec((tq, LANES), lambda bi, i: (bi * nq + i, CB_GATE)),
            pl.BlockSpec((1, ng, LANES), lambda bi, i: (bi, 0, 0)),
            pl.BlockSpec((s, LANES), lambda bi, i: (bi, CB_KSVS)),
            pl.BlockSpec((s, LANES), lambda bi, i: (bi, CB_KWVW)),
            pl.BlockSpec((s, LANES), lambda bi, i: (0, 0)),
            pl.BlockSpec((2, tq, tq), lambda bi, i: (0, 0, 0)),
            pl.BlockSpec((ng, LANES), lambda bi, i: (0, 0)),
            pl.BlockSpec((nq, LANES, tq), lambda bi, i: (0, 0, 0)),
            pl.BlockSpec((1, LANES), lambda bi, i: (0, 0)),
            pl.BlockSpec((2, LANES), lambda bi, i: (0, 0)),
        ],
        out_specs=pl.BlockSpec((tq, 2 * LANES), lambda bi, i: (bi * nq + i, 0)),
        out_shape=jax.ShapeDtypeStruct((t, GROUP_WIDTH), F32),
        scratch_shapes=[
            pltpu.VMEM((nq, m_rows, LANES), F32),
            pltpu.VMEM((4, m_rows, tq), F32),
            pltpu.VMEM((s, LANES), BF16),
            pltpu.VMEM((s, LANES), BF16),
            pltpu.VMEM((s, LANES), BF16),
            pltpu.VMEM((s, LANES), BF16),
            pltpu.VMEM((ng, LANES), BF16),
            pltpu.VMEM((ng, LANES), BF16),
            pltpu.VMEM((m_rows, 1), F32),
            pltpu.VMEM((m_rows, 1), F32),
            pltpu.VMEM((m_rows, LANES), F32),
        ],
        compiler_params=_params(("arbitrary", "arbitrary")),
        name="nsa_attention",
    )(rel_bias, proj, proj, kcvc, proj, proj, bidxc, bidx2, ov, emat, qnw, knw)


def _mla_prep_kernel(cq_ref, ckv_ref, kr_ref, qaw_ref, kvw_ref, wq_ref, wkv_ref, qnw_ref, knw_ref,
                     cq_t_ref, s1_t_ref, s2_t_ref, ck_t_ref, sk_t_ref, qo_ref, ko_ref, vo_ref):
    cq = cq_ref[...]
    ms = jnp.sum(cq * cq, axis=-1, keepdims=True) * (1.0 / Q_LORA)
    hq = (cq * lax.rsqrt(ms + EPS) * qaw_ref[...]).astype(BF16)
    qf = _dot(hq, wq_ref[...])
    ckv = ckv_ref[...]
    ms = jnp.mean(ckv * ckv, axis=-1, keepdims=True)
    hkv = (ckv * lax.rsqrt(ms + EPS) * kvw_ref[...]).astype(BF16)
    kvf = _dot(hkv, wkv_ref[...])
    krb = kr_ref[...]
    kr_rot = krb * ck_t_ref[...] + pltpu.roll(krb, HEAD_DIM, 1) * sk_t_ref[...]
    scale = QK_DIM ** -0.5
    for h in range(N_HEADS):
        cols = slice(LANES * h, LANES * (h + 1))
        x = qf[:, cols]
        x = x * cq_t_ref[...] + pltpu.roll(x, LANES - 16, 1) * s1_t_ref[...] + pltpu.roll(x, 16, 1) * s2_t_ref[...]
        ss = jnp.sum(x * x, axis=-1, keepdims=True) * (1.0 / QK_DIM)
        qo_ref[:, cols] = (x * lax.rsqrt(ss + EPS) * qnw_ref[...] * scale).astype(BF16)
        k = kvf[:, cols] + kr_rot
        ss = jnp.sum(k * k, axis=-1, keepdims=True) * (1.0 / QK_DIM)
        ko_ref[:, cols] = (k * lax.rsqrt(ss + EPS) * knw_ref[...]).astype(BF16)
    vo_ref[...] = kvf[:, N_HEADS * LANES:].astype(BF16)


def _mla_prep(proj, qaw, kvw, wq, wkv, qnw, knw, tabs, s, tm):
    t = proj.shape[0]
    npos = s // tm
    row = lambda i: (i, 0)
    const = lambda i: (0, 0)
    tab = pl.BlockSpec((tm, LANES), lambda i: (i % npos, 0))
    out = jax.ShapeDtypeStruct((t, N_HEADS * LANES), BF16)
    return pl.pallas_call(
        _mla_prep_kernel,
        grid=(t // tm,),
        in_specs=[
            pl.BlockSpec((tm, 2 * LANES), lambda i: (i, CB_CQ)),
            pl.BlockSpec((tm, LANES), lambda i: (i, CB_CKV)),
            pl.BlockSpec((tm, LANES), lambda i: (i, CB_KR)),
            pl.BlockSpec((1, 2 * LANES), const),
            pl.BlockSpec((1, LANES), const),
            pl.BlockSpec((2 * LANES, N_HEADS * LANES), const),
            pl.BlockSpec((LANES, 2 * N_HEADS * LANES), const),
            pl.BlockSpec((1, LANES), const),
            pl.BlockSpec((1, LANES), const),
            tab, tab, tab, tab, tab,
        ],
        out_specs=[pl.BlockSpec((tm, N_HEADS * LANES), row)] * 3,
        out_shape=[out, out, out],
        compiler_params=_params(("arbitrary",)),
        name="mla_prep",
    )(proj, proj, proj, qaw, kvw, wq, wkv, qnw, knw, *tabs)


def _mla_attn_kernel(q_ref, k_ref, v_ref, o_ref, m_s, l_s, acc_s, *, tq):
    i = pl.program_id(1)
    r = _row((tq, tq))
    c = _lane((tq, tq))
    outs = []
    for h in range(N_HEADS):
        cols = slice(LANES * h, LANES * (h + 1))
        qh = q_ref[:, cols]
        m_s[...] = jnp.full(m_s.shape, NEG, F32)
        l_s[...] = jnp.zeros(l_s.shape, F32)
        acc_s[...] = jnp.zeros(acc_s.shape, F32)
        rows = pl.ds(pl.multiple_of(i * tq, tq), tq)
        s = jnp.where(r >= c, _nt(qh, k_ref[rows, cols]), NEG)
        _softmax_step(s, v_ref[rows, cols], m_s, l_s, acc_s)

        def body(jj, carry):
            rows = pl.ds(pl.multiple_of((i - jj) * tq, tq), tq)
            _softmax_step(_nt(qh, k_ref[rows, cols]), v_ref[rows, cols], m_s, l_s, acc_s)
            return carry

        lax.fori_loop(1, i + 1, body, 0)
        outs.append(acc_s[...] / l_s[...])
    o_ref[:, 0:LANES] = outs[0] + outs[1]
    o_ref[:, LANES:2 * LANES] = outs[2] + outs[3]


def _mla_attention(qm, km, vm, b, s, tq):
    t = b * s
    nq = s // tq
    w = N_HEADS * LANES
    return pl.pallas_call(
        functools.partial(_mla_attn_kernel, tq=tq),
        grid=(b, nq),
        in_specs=[
            pl.BlockSpec((tq, w), lambda bi, i: (bi * nq + i, 0)),
            pl.BlockSpec((s, w), lambda bi, i: (bi, 0)),
            pl.BlockSpec((s, w), lambda bi, i: (bi, 0)),
        ],
        out_specs=pl.BlockSpec((tq, 2 * LANES), lambda bi, i: (bi * nq + i, 0)),
        out_shape=jax.ShapeDtypeStruct((t, GROUP_WIDTH), F32),
        scratch_shapes=[pltpu.VMEM((tq, 1), F32), pltpu.VMEM((tq, 1), F32), pltpu.VMEM((tq, LANES), F32)],
        compiler_params=_params(("arbitrary", "arbitrary")),
        name="mla_attention",
    )(qm, km, vm)


def _sb_kernel(q_ref, k_ref, v_ref, o_ref, kb_s, vb_s, r_s, acc_s, *, tq, s_len):
    i = pl.program_id(1)

    @pl.when(i == 0)
    def _cast_kv():
        ch = 256

        def body(t, carry):
            rows = pl.ds(pl.multiple_of(t * ch, ch), ch)
            kb_s[rows, :] = k_ref[rows, :].astype(BF16)
            vb_s[rows, :] = v_ref[rows, :].astype(BF16)
            return carry

        lax.fori_loop(0, s_len // ch, body, 0)

    r = _row((tq, tq))
    c = _lane((tq, tq))
    strict = r > c
    tri = jnp.where(r >= c, 1.0, 0.0).astype(BF16)
    lane = _lane((tq, LANES))
    q = q_ref[...]
    outs = []
    for h in range(N_HEADS):
        cols = slice(LANES * (h // 2), LANES * (h // 2 + 1))
        mine = (lane < HEAD_DIM) if h % 2 == 0 else (lane >= HEAD_DIM)
        qh = jnp.where(mine, q[:, cols] * (HEAD_DIM ** -0.5), 0.0).astype(BF16)

        def tile(j, masked):
            rows = pl.ds(pl.multiple_of(j * tq, tq), tq)
            z = _nt(qh, kb_s[rows, cols])
            lg = -(jnp.maximum(z, 0.0) + jnp.log(1.0 + jnp.exp(-jnp.abs(z))))
            if masked:
                lg = jnp.where(strict, lg, 0.0)
            hi = lg.astype(BF16)
            lo = (lg - hi.astype(F32)).astype(BF16)
            csum = _dot(hi, tri) + _dot(lo, tri) + r_s[...]
            a = jnp.exp(z + csum)
            if masked:
                a = jnp.where(strict, a, 0.0)
            acc_s[...] += _dot(a.astype(BF16), vb_s[rows, cols])
            r_s[...] = csum[:, 0:1]

        r_s[...] = jnp.zeros(r_s.shape, F32)
        acc_s[...] = jnp.zeros(acc_s.shape, F32)
        tile(i, True)

        def body(jj, carry):
            tile(i - jj, False)
            return carry

        lax.fori_loop(1, i + 1, body, 0)
        outs.append(acc_s[...])
    o_ref[:, 0:LANES] = jnp.where(lane < HEAD_DIM, outs[0], outs[1])
    o_ref[:, LANES:2 * LANES] = jnp.where(lane < HEAD_DIM, outs[2], outs[3])


def _sb_attention(proj, b, s, tq):
    t = b * s
    nq = s // tq
    w = 2 * LANES
    return pl.pallas_call(
        functools.partial(_sb_kernel, tq=tq, s_len=s),
        grid=(b, nq),
        in_specs=[
            pl.BlockSpec((tq, w), lambda bi, i: (bi * nq + i, CB_SQ)),
            pl.BlockSpec((s, w), lambda bi, i: (bi, CB_SK)),
            pl.BlockSpec((s, w), lambda bi, i: (bi, CB_SV)),
        ],
        out_specs=pl.BlockSpec((tq, w), lambda bi, i: (bi * nq + i, 0)),
        out_shape=jax.ShapeDtypeStruct((t, GROUP_WIDTH), F32),
        scratch_shapes=[pltpu.VMEM((s, w), BF16), pltpu.VMEM((s, w), BF16),
                        pltpu.VMEM((tq, 1), F32), pltpu.VMEM((tq, LANES), F32)],
        compiler_params=_params(("arbitrary", "arbitrary")),
        name="sb_attention",
    )(proj, proj, proj)


def _post_kernel(a_ref, ap_ref, yb_ref, yc_ref, yd_ref, x_ref, cw_ref, cb_ref, onw_ref, wo_ref, n2w_ref,
                 w1_ref, w2_ref, o_ref, *, tm, s_len, ffc):
    i = pl.program_id(0)
    a = a_ref[...]
    gw = GROUP_WIDTH
    v = a[:, gw:2 * gw] * a[:, 2 * gw:3 * gw]
    ap = ap_ref[...]
    first = (i * tm) % s_len == 0
    vp = jnp.where(first, 0.0, ap[:, gw:2 * gw] * ap[:, 2 * gw:3 * gw])
    row = _row(v.shape)
    v1 = jnp.where(row == 0, vp[7:8, :], pltpu.roll(v, 1, 0))
    v2 = jnp.where(row == 0, vp[6:7, :], jnp.where(row == 1, vp[7:8, :], pltpu.roll(v, 2, 0)))
    conv = cw_ref[0:1, :] * v2 + cw_ref[1:2, :] * v1 + cw_ref[2:3, :] * v
    ya = a[:, 0:gw] * (conv + cb_ref[...])

    mix = None
    for g, y in enumerate((ya, yb_ref[...], yc_ref[...], yd_ref[...])):
        ms = jnp.mean(y * y, axis=-1, keepdims=True)
        yn = (y * lax.rsqrt(ms + EPS) * onw_ref[:, gw * g:gw * (g + 1)]).astype(BF16)
        part = _dot(yn, wo_ref[gw * g:gw * (g + 1), :])
        mix = part if mix is None else mix + part
    x1 = x_ref[...] + mix

    ms = jnp.mean(x1 * x1, axis=-1, keepdims=True)
    h2 = (x1 * lax.rsqrt(ms + EPS) * n2w_ref[...]).astype(BF16)
    ff = None
    for cidx in range(D_FF // ffc):
        u = _dot(h2, w1_ref[:, ffc * cidx:ffc * (cidx + 1)])
        u = jnp.square(jnp.maximum(u, 0.0)).astype(BF16)
        part = _dot(u, w2_ref[ffc * cidx:ffc * (cidx + 1), :])
        ff = part if ff is None else ff + part
    o_ref[...] = x1 + ff


def _post(proj, yb, yc, yd, x2d, cw, cb, onw, wo, n2w, w1, w2, s, tm):
    t = x2d.shape[0]
    gw = GROUP_WIDTH
    row = lambda i: (i, 0)
    const = lambda i: (0, 0)
    once = pl.Buffered(1)
    kern = functools.partial(_post_kernel, tm=tm, s_len=s, ffc=1024)
    return pl.pallas_call(
        kern,
        grid=(t // tm,),
        in_specs=[
            pl.BlockSpec((tm, 3 * gw), row),
            pl.BlockSpec((8, 3 * gw), lambda i: (jnp.maximum(i * (tm // 8) - 1, 0), 0)),
            pl.BlockSpec((tm, gw), row),
            pl.BlockSpec((tm, gw), row),
            pl.BlockSpec((tm, gw), row),
            pl.BlockSpec((tm, D_MODEL), row),
            pl.BlockSpec((3, gw), const),
            pl.BlockSpec((1, gw), const),
            pl.BlockSpec((1, D_MODEL), const),
            pl.BlockSpec((D_MODEL, D_MODEL), const, pipeline_mode=once),
            pl.BlockSpec((1, D_MODEL), const),
            pl.BlockSpec((D_MODEL, D_FF), const, pipeline_mode=once),
            pl.BlockSpec((D_FF, D_MODEL), const, pipeline_mode=once),
        ],
        out_specs=pl.BlockSpec((tm, D_MODEL), row),
        out_shape=jax.ShapeDtypeStruct((t, D_MODEL), F32),
        compiler_params=_params(("arbitrary",)),
        name="post",
    )(proj, proj, yb, yc, yd, x2d, cw, cb, onw, wo, n2w, w1, w2)


def _t5_bucket(dist):
    max_exact = N_BUCKETS // 2
    d = jnp.maximum(dist, 0)
    large = max_exact + (jnp.log(jnp.maximum(d, 1).astype(F32) / max_exact)
                         / math.log(MAX_DISTANCE / max_exact) * (N_BUCKETS - max_exact)).astype(jnp.int32)
    large = jnp.minimum(large, N_BUCKETS - 1)
    return jnp.where(d < max_exact, d, large)


def _tables(s, tq_nsa, tm_prep):
    n_cmp = (s - CMP_LEN) // CMP_STRIDE + 1
    ng = s // CMP_STRIDE
    n_slc = s // SLC_LEN
    tpos = jnp.arange(s)[:, None]
    n = jnp.arange(LANES)[None, :]
    dist_c = tpos - (n * CMP_STRIDE + CMP_LEN - 1)
    bidxc = jnp.where((dist_c >= 0) & (n < n_cmp), _t5_bucket(dist_c), -1).astype(jnp.int32)
    r = jnp.arange(tq_nsa)[:, None]
    c = jnp.arange(tq_nsa)[None, :]
    bidx2 = jnp.stack([_t5_bucket(r - c), _t5_bucket(tq_nsa + r - c)]).astype(jnp.int32)
    starts = np.arange(n_cmp) * CMP_STRIDE
    ends = starts + CMP_LEN
    s0 = np.arange(n_slc) * SLC_LEN
    s1 = s0 + SLC_LEN
    ovl = np.clip(np.minimum(ends[:, None], s1[None]) - np.maximum(starts[:, None], s0[None]), 0, None) / CMP_LEN
    ov = np.zeros((ng, LANES), np.float32)
    ov[:n_cmp, :n_slc] = ovl
    key_blk = (np.arange(s) // SLC_LEN).reshape(s // tq_nsa, 1, tq_nsa)
    emat = (np.arange(LANES).reshape(1, LANES, 1) == key_blk).astype(np.float32)
    inv = 1.0 / (ROPE_THETA ** (jnp.arange(0, ROPE_DIM, 2, dtype=F32) / ROPE_DIM))
    ang = jnp.arange(s, dtype=F32)[:, None] * inv[None, :]
    cos, sin = jnp.cos(ang), jnp.sin(ang)
    z16 = jnp.zeros((s, 16), F32)
    z32 = jnp.zeros((s, 32), F32)
    z64 = jnp.zeros((s, 64), F32)
    one64 = jnp.ones((s, 64), F32)
    cq_t = jnp.concatenate([one64, cos, cos, z32], axis=1)
    s1_t = jnp.concatenate([z64, -sin, z16, z32], axis=1)
    s2_t = jnp.concatenate([z64, z16, sin, z32], axis=1)
    ck_t = jnp.concatenate([z64, cos, cos, z32], axis=1)
    sk_t = jnp.concatenate([z64, -sin, sin, z32], axis=1)
    return dict(bidxc=bidxc, bidx2=bidx2, ov=jnp.asarray(ov, BF16), emat=jnp.asarray(emat, BF16),
                rope=(cq_t, s1_t, s2_t, ck_t, sk_t))


def _pad_cols(w, width):
    return jnp.pad(w, ((0, 0), (0, width - w.shape[1])))


def _layer_weights(l, w_in, conv_w, conv_b, nsa_q_norm, nsa_k_norm, cmp_pos, cmp_w1, cmp_w2, mla_q_a_norm,
                   mla_kv_norm, mla_wq_b, mla_wkv_b, mla_q_norm, mla_k_norm, out_norm_w, w_out, norm2_w,
                   ffn_w1, ffn_w2):
    wi = w_in[l]
    kr = wi[:, 1740:1772]
    z32 = jnp.zeros((D_MODEL, 32), F32)
    w_in_p = jnp.concatenate([
        wi[:, 0:1408],
        _pad_cols(wi[:, 1408:1420], LANES),
        _pad_cols(wi[:, 1420:1612], 2 * LANES),
        wi[:, 1612:1740],
        kr[:, 16:32], kr[:, 0:16], z32, kr, z32,
        wi[:, 1772:2540],
    ], axis=1).astype(BF16)
    w1 = cmp_w1[l].reshape(2, CMP_LEN, HEAD_DIM, CMP_HIDDEN)
    zw = jnp.zeros((CMP_LEN, HEAD_DIM, CMP_HIDDEN), F32)
    cw1 = jnp.concatenate([jnp.concatenate([w1[0], zw], axis=2), jnp.concatenate([zw, w1[1]], axis=2)],
                          axis=1).astype(BF16)
    zc = jnp.zeros((CMP_HIDDEN, HEAD_DIM), F32)
    cw2 = jnp.concatenate([jnp.concatenate([cmp_w2[l, 0], zc], axis=1),
                           jnp.concatenate([zc, cmp_w2[l, 1]], axis=1)], axis=0).astype(BF16)
    cpos = jnp.concatenate([cmp_pos[l, 0], cmp_pos[l, 1]], axis=1)
    kn = nsa_k_norm[l]
    ones64 = jnp.ones((HEAD_DIM,), F32)
    knw_c = jnp.concatenate([kn[0], ones64])[None, :]
    knw_sw = jnp.stack([jnp.concatenate([kn[1], ones64]), jnp.concatenate([kn[2], ones64])])
    qnw = jnp.concatenate([nsa_q_norm[l], nsa_q_norm[l]])[None, :]
    wq = mla_wq_b[l].reshape(Q_LORA, N_HEADS, QK_DIM)
    wq = jnp.pad(wq, ((0, 2 * LANES - Q_LORA), (0, 0), (0, LANES - QK_DIM))).reshape(2 * LANES, N_HEADS * LANES)
    wkv = mla_wkv_b[l].reshape(KV_LORA, N_HEADS, 2 * HEAD_DIM)
    zk = jnp.zeros((KV_LORA, N_HEADS, HEAD_DIM), F32)
    wk_part = jnp.concatenate([wkv[:, :, :HEAD_DIM], zk], axis=2)
    v = wkv[:, :, HEAD_DIM:]
    odd = (jnp.arange(N_HEADS) % 2 == 1)[None, :, None]
    wv_part = jnp.concatenate([jnp.where(odd, 0.0, v), jnp.where(odd, v, 0.0)], axis=2)
    wkv_p = jnp.concatenate([wk_part.reshape(KV_LORA, -1), wv_part.reshape(KV_LORA, -1)], axis=1)
    return dict(
        w_in=w_in_p, cw1=cw1, cw2=cw2, cpos=cpos, knw_c=knw_c, knw_sw=knw_sw, qnw=qnw,
        qaw=_pad_cols(mla_q_a_norm[l][None, :], 2 * LANES), kvw=mla_kv_norm[l][None, :],
        wq=wq.astype(BF16), wkv=wkv_p.astype(BF16),
        mqn=_pad_cols(mla_q_norm[l][None, :], LANES), mkn=_pad_cols(mla_k_norm[l][None, :], LANES),
        cw=conv_w[l], cb=conv_b[l][None, :], onw=out_norm_w[l][None, :], wo=w_out[l].astype(BF16),
        n2w=norm2_w[l][None, :], w1=ffn_w1[l].astype(BF16), w2=ffn_w2[l].astype(BF16))


TM_PROJ = 512
TM_PREP = 512
TM_POST = 512
TQ_NSA = 128
TQ_MLA = 256
TQ_SB = 256


def kernel(x, rel_bias, norm1_w, w_in, conv_w, conv_b, nsa_q_norm, nsa_k_norm, cmp_pos, cmp_w1, cmp_w2,
           mla_q_a_norm, mla_kv_norm, mla_wq_b, mla_wkv_b, mla_q_norm, mla_k_norm, out_norm_w, w_out, norm2_w,
           ffn_w1, ffn_w2):
    b, s, d = x.shape
    depth = w_in.shape[0]
    tabs = _tables(s, TQ_NSA, TM_PREP)
    x2d = x.reshape(b * s, d)
    for l in range(depth):
        w = _layer_weights(l, w_in, conv_w, conv_b, nsa_q_norm, nsa_k_norm, cmp_pos, cmp_w1, cmp_w2,
                           mla_q_a_norm, mla_kv_norm, mla_wq_b, mla_wkv_b, mla_q_norm, mla_k_norm, out_norm_w,
                           w_out, norm2_w, ffn_w1, ffn_w2)
        proj = _inproj(x2d, norm1_w[l][None, :], w["w_in"], TM_PROJ)
        kcvc = _compress(proj, w["cpos"], w["cw1"], w["cw2"], w["knw_c"], b, s)
        yb = _nsa_attention(proj, kcvc, rel_bias, tabs["bidxc"], tabs["bidx2"], tabs["ov"], tabs["emat"],
                            w["qnw"], w["knw_sw"], b, s, TQ_NSA)
        qm, km, vm = _mla_prep(proj, w["qaw"], w["kvw"], w["wq"], w["wkv"], w["mqn"], w["mkn"], tabs["rope"],
                               s, TM_PREP)
        yc = _mla_attention(qm, km, vm, b, s, TQ_MLA)
        yd = _sb_attention(proj, b, s, TQ_SB)
        x2d = _post(proj, yb, yc, yd, x2d, w["cw"], w["cb"], w["onw"], w["wo"], w["n2w"], w["w1"], w["w2"],
                    s, TM_POST)
    return x2d.reshape(b, s, d)
```

```python
import functools
import math

import jax
import jax.numpy as jnp
import numpy as np
from jax import lax
from jax.experimental import pallas as pl
from jax.experimental.pallas import tpu as pltpu

F32 = jnp.float32
BF16 = jnp.bfloat16

D_MODEL = 1024
GROUP_WIDTH = 256
HEAD_DIM = 64
N_HEADS = 4
LANES = 128
CMP_LEN = 32
CMP_STRIDE = 16
SLC_LEN = 64
N_SEL = 16
WINDOW = 512
CMP_HIDDEN = 256
Q_LORA = 192
KV_LORA = 128
ROPE_DIM = 32
QK_DIM = 96
ROPE_THETA = 10000.0
N_BUCKETS = 32
MAX_DISTANCE = 128
D_FF = 4096
EPS = 1e-6
NEG = -1e30
LOG2E = math.log2(math.e)
DV_PAD = 80

NP = 2816
CB_NQ = 3
CB_KCVC = 8
CB_KSVS = 9
CB_KWVW = 10
CB_GATE = 11
CB_CQ = 6
CB_CKV = 14
CB_KR = 15
CB_SQ = 8
CB_SK = 9
CB_SV = 10

VMEM_LIMIT = 56 * 1024 * 1024

NT_DIMS = (((1,), (1,)), ((), ()))


def _params(sem):
    return pltpu.CompilerParams(dimension_semantics=sem, vmem_limit_bytes=VMEM_LIMIT)


def _nt(a, b):
    return lax.dot_general(a, b, NT_DIMS, preferred_element_type=F32)


def _dot(a, b):
    return jnp.dot(a, b, preferred_element_type=F32)


def _lane(shape):
    return lax.broadcasted_iota(jnp.int32, shape, len(shape) - 1)


def _row(shape):
    return lax.broadcasted_iota(jnp.int32, shape, len(shape) - 2)


def _lane_tile(j, width):
    return pl.ds(pl.multiple_of(j * width, width), width)


def _inproj_kernel(x_ref, nw_ref, w_ref, o_ref):
    x = x_ref[...]
    ms = jnp.mean(x * x, axis=-1, keepdims=True)
    h = (x * lax.rsqrt(ms + EPS) * nw_ref[...]).astype(BF16)
    o_ref[...] = _dot(h, w_ref[...])


def _inproj(x2d, nw, w, tm):
    t = x2d.shape[0]
    return pl.pallas_call(
        _inproj_kernel,
        grid=(t // tm,),
        in_specs=[
            pl.BlockSpec((tm, D_MODEL), lambda i: (i, 0)),
            pl.BlockSpec((1, D_MODEL), lambda i: (0, 0)),
            pl.BlockSpec((D_MODEL, NP), lambda i: (0, 0), pipeline_mode=pl.Buffered(1)),
        ],
        out_specs=pl.BlockSpec((tm, NP), lambda i: (i, 0)),
        out_shape=jax.ShapeDtypeStruct((t, NP), F32),
        compiler_params=_params(("arbitrary",)),
        name="inproj",
    )(x2d, nw, w)


def _compress_kernel(x_ref, pos_ref, w1_ref, w2_ref, w2t_ref, knw_ref, o_ref, ot_ref):
    ng = x_ref.shape[1]
    acc_a = jnp.zeros((ng, 2 * CMP_HIDDEN), F32)
    acc_b = jnp.zeros((ng, 2 * CMP_HIDDEN), F32)
    for i in range(CMP_STRIDE):
        x = x_ref[0, :, i, :]
        xa = (x + pos_ref[i:i + 1, :]).astype(BF16)
        xb = (x + pos_ref[CMP_STRIDE + i:CMP_STRIDE + i + 1, :]).astype(BF16)
        acc_a += _dot(xa, w1_ref[i])
        acc_b += _dot(xb, w1_ref[CMP_STRIDE + i])
    pre = acc_a + pltpu.roll(acc_b, ng - 1, 0)
    hdn = (pre * jax.nn.sigmoid(pre)).astype(BF16)
    out = _dot(hdn, w2_ref[...])
    lane = _lane(out.shape)
    is_k = lane < HEAD_DIM
    ss = jnp.sum(jnp.where(is_k, out * out, 0.0), axis=-1, keepdims=True) * (1.0 / HEAD_DIM)
    o_ref[0] = jnp.where(is_k, out * lax.rsqrt(ss + EPS) * knw_ref[...], out)
    ot_ref[0] = _nt(w2t_ref[...], hdn)


def _compress(proj, pos, w1, w2, w2t, knw, b, s):
    ng = s // CMP_STRIDE
    x4 = proj.reshape(b, ng, CMP_STRIDE, NP)
    return pl.pallas_call(
        _compress_kernel,
        grid=(b,),
        in_specs=[
            pl.BlockSpec((1, ng, CMP_STRIDE, LANES), lambda i: (i, 0, 0, CB_KCVC)),
            pl.BlockSpec((CMP_LEN, LANES), lambda i: (0, 0)),
            pl.BlockSpec((CMP_LEN, LANES, 2 * CMP_HIDDEN), lambda i: (0, 0, 0)),
            pl.BlockSpec((2 * CMP_HIDDEN, LANES), lambda i: (0, 0)),
            pl.BlockSpec((LANES, 2 * CMP_HIDDEN), lambda i: (0, 0)),
            pl.BlockSpec((1, LANES), lambda i: (0, 0)),
        ],
        out_specs=[pl.BlockSpec((1, ng, LANES), lambda i: (i, 0, 0)),
                   pl.BlockSpec((1, LANES, ng), lambda i: (i, 0, 0))],
        out_shape=[jax.ShapeDtypeStruct((b, ng, LANES), F32), jax.ShapeDtypeStruct((b, LANES, ng), F32)],
        compiler_params=_params(("arbitrary",)),
        name="nsa_compress",
    )(x4, pos, w1, w2, w2t, knw)


def _softmax_steps(sts, vts, m_refs, acc_refs):
    m_old = [r[...] for r in m_refs]
    acc_old = [r[...] for r in acc_refs]
    m_new = [jnp.maximum(m, jnp.max(st, axis=0, keepdims=True)) for m, st in zip(m_old, sts)]
    ps = [jnp.exp2(st - m).astype(BF16) for st, m in zip(sts, m_new)]
    alphas = [jnp.exp2(mo - mn) for mo, mn in zip(m_old, m_new)]
    acc_new = [al * acc + _dot(vt, p) for al, acc, vt, p in zip(alphas, acc_old, vts, ps)]
    for r, v in zip(m_refs, m_new):
        r[...] = v
    for r, v in zip(acc_refs, acc_new):
        r[...] = v


def _with_ones_row(vt):
    pad = jnp.where(_row((DV_PAD - HEAD_DIM, vt.shape[1])) == 0, 1.0, 0.0).astype(vt.dtype)
    return jnp.concatenate([vt, pad], axis=0)


def _bucket_bias(bidx, relb_ref, h, fill):
    acc = jnp.full(bidx.shape, fill, F32)
    for bk in range(N_BUCKETS):
        acc = jnp.where(bidx == bk, relb_ref[bk, h] * LOG2E, acc)
    return acc


def _dup_low_half(x):
    y = jnp.where(_lane(x.shape) < HEAD_DIM, x, 0.0)
    return y + pltpu.roll(y, HEAD_DIM, 1)


def _nsa_kernel(relb_ref, q_ref, g_ref, kcvc_ref, kcvct_ref, ksvs_ref, kwvw_ref, bidxct_ref, bidx2t_ref, ovt_ref,
                emt_ref, qnw_ref, knw_ref, o_ref,
                biasc_s, bias2_s, ks_s, vst_s, kw_s, vwt_s, kc_s, vct_s, m_s, acc_s, *, tq, s_len):
    b = pl.program_id(0)
    i = pl.program_id(1)
    n_win = WINDOW // tq
    ng = s_len // CMP_STRIDE

    @pl.when((b == 0) & (i == 0))
    def _build_bias_tables():
        key = _row((tq, tq))
        qry = _lane((tq, tq))
        for h in range(N_HEADS):
            cols = slice(h * tq, (h + 1) * tq)
            far = jnp.full((tq, tq), relb_ref[N_BUCKETS - 1, h] * LOG2E, F32)
            bias2_s[0, :, cols] = jnp.where(key <= qry, _bucket_bias(bidx2t_ref[0], relb_ref, h, NEG), NEG)
            bias2_s[1, :, cols] = _bucket_bias(bidx2t_ref[1], relb_ref, h, NEG)
            bias2_s[2, :, cols] = far
            bias2_s[3, :, cols] = jnp.where(key > qry, far, NEG)

        def body(t, carry):
            bi = bidxct_ref[:, _lane_tile(t, tq)]
            for h in range(N_HEADS):
                biasc_s[t, :, h * tq:(h + 1) * tq] = _bucket_bias(bi, relb_ref, h, NEG)
            return carry

        lax.fori_loop(0, s_len // tq, body, 0)

    @pl.when(i == 0)
    def _prep_kv():
        ch = 256

        def body(t, carry):
            rows = pl.ds(pl.multiple_of(t * ch, ch), ch)
            for src, kdst, vdst, widx in ((ksvs_ref, ks_s, vst_s, 0), (kwvw_ref, kw_s, vwt_s, 1)):
                x = src[rows, :]
                is_k = _lane(x.shape) < HEAD_DIM
                ss = jnp.sum(jnp.where(is_k, x * x, 0.0), axis=-1, keepdims=True) * (1.0 / HEAD_DIM)
                kn = x * lax.rsqrt(ss + EPS) * knw_ref[widx:widx + 1, :]
                kdst[rows, :] = _dup_low_half(kn).astype(BF16)
                vdst[:, _lane_tile(t, ch)] = _with_ones_row(x.T[HEAD_DIM:, :]).astype(BF16)
            return carry

        lax.fori_loop(0, s_len // ch, body, 0)
        kc_s[...] = _dup_low_half(kcvc_ref[0]).astype(BF16)
        vct_s[...] = kcvct_ref[0][HEAD_DIM:, :].astype(BF16)

    q = q_ref[...]
    lane = _lane((tq, LANES))
    qs = []
    for h in range(N_HEADS):
        x = q[:, LANES * (h // 2):LANES * (h // 2 + 1)]
        mine = (lane < HEAD_DIM) if h % 2 == 0 else (lane >= HEAD_DIM)
        xm = jnp.where(mine, x, 0.0)
        ss = jnp.sum(xm * xm, axis=-1, keepdims=True) * (1.0 / HEAD_DIM)
        qs.append((xm * lax.rsqrt(ss + EPS) * qnw_ref[...] * (HEAD_DIM ** -0.5 * LOG2E)).astype(BF16))
    qst = jnp.concatenate(qs, axis=0)

    lc = _nt(kc_s[...], qst) + biasc_s[i]
    mc = jnp.max(lc, axis=0, keepdims=True)
    pc = jnp.where(lc > 0.5 * NEG, jnp.exp2(lc - mc), 0.0)
    den = jnp.sum(pc, axis=0, keepdims=True)
    pc = pc / jnp.where(den > 0.0, den, 1.0)
    o_cmp = _dot(vct_s[...], pc.astype(BF16))
    psum = pc[:, 0:tq] + pc[:, tq:2 * tq] + pc[:, 2 * tq:3 * tq] + pc[:, 3 * tq:4 * tq]
    p_hi = psum.astype(BF16)
    p_lo = (psum - p_hi.astype(F32)).astype(BF16)
    score = _dot(ovt_ref[...], p_hi) + _dot(ovt_ref[...], p_lo)

    n_slc = s_len // SLC_LEN
    blk = _row((n_slc, tq))
    tpos = i * tq + _lane((n_slc, tq))
    tblk = tpos // SLC_LEN
    valid = blk * SLC_LEN <= tpos
    forced = (blk == 0) | (blk == tblk) | (blk == tblk - 1)
    sc = jnp.where(forced, jnp.inf, jnp.where(valid, score[0:n_slc], -jnp.inf))
    rank = jnp.zeros((n_slc, tq), F32)
    for k in range(n_slc):
        ck = sc[k:k + 1, :]
        beats = (ck > sc) | ((ck == sc) & (blk > k))
        rank += jnp.where(beats, 1.0, 0.0)
    sel = jnp.where((rank < float(min(N_SEL, n_slc))) & valid, 1.0, 0.0)
    sel = jnp.concatenate([sel, jnp.zeros((LANES - n_slc, tq), F32)], axis=0).astype(BF16)

    def reset():
        m_s[...] = jnp.full(m_s.shape, NEG, F32)
        acc_s[...] = jnp.zeros(acc_s.shape, F32)

    def finish():
        return acc_s[0:HEAD_DIM, :] / acc_s[HEAD_DIM:HEAD_DIM + 1, :]

    reset()

    def sel_body(jj, carry):
        j = i - jj
        rows = pl.ds(pl.multiple_of(j * tq, tq), tq)
        st = _nt(ks_s[rows, :], qst) + bias2_s[jnp.minimum(jj, 2)]
        keep = _dot(emt_ref[j], sel)
        keep = jnp.concatenate([keep] * N_HEADS, axis=1)
        _softmax_steps([jnp.where(keep > 0.5, st, NEG)], [vst_s[:, _lane_tile(j, tq)]], [m_s], [acc_s])
        return carry

    lax.fori_loop(0, i + 1, sel_body, 0)
    o_slc = finish()

    reset()

    def win_body(jj, carry):
        j = i - jj
        rows = pl.ds(pl.multiple_of(j * tq, tq), tq)
        kind = jnp.where(jj == n_win, 3, jnp.minimum(jj, 2))
        st = _nt(kw_s[rows, :], qst) + bias2_s[kind]
        _softmax_steps([st], [vwt_s[:, _lane_tile(j, tq)]], [m_s], [acc_s])
        return carry

    lax.fori_loop(0, jnp.minimum(i, n_win) + 1, win_body, 0)
    o_win = finish()

    gt = jax.nn.sigmoid(g_ref[...]).T
    ys = []
    for h in range(N_HEADS):
        cols = slice(h * tq, (h + 1) * tq)
        ys.append(gt[3 * h:3 * h + 1, :] * o_cmp[:, cols] + gt[3 * h + 1:3 * h + 2, :] * o_slc[:, cols]
                  + gt[3 * h + 2:3 * h + 3, :] * o_win[:, cols])
    o_ref[...] = jnp.concatenate(ys, axis=0).T


def _nsa_attention(proj, kcvc, kcvct, rel_bias, bidxct, bidx2t, ovt, emt, qnw, knw, b, s, tq):
    t = b * s
    nq = s // tq
    m_rows = N_HEADS * tq
    ng = s // CMP_STRIDE
    kern = functools.partial(_nsa_kernel, tq=tq, s_len=s)
    return pl.pallas_call(
        kern,
        grid=(b, nq),
        in_specs=[
            pl.BlockSpec(memory_space=pltpu.SMEM),
            pl.BlockSpec((tq, 2 * LANES), lambda bi, i: (bi * nq + i, CB_NQ)),
            pl.BlockSpec((tq, LANES), lambda bi, i: (bi * nq + i, CB_GATE)),
            pl.BlockSpec((1, ng, LANES), lambda bi, i: (bi, 0, 0)),
            pl.BlockSpec((1, LANES, ng), lambda bi, i: (bi, 0, 0)),
            pl.BlockSpec((s, LANES), lambda bi, i: (bi, CB_KSVS)),
            pl.BlockSpec((s, LANES), lambda bi, i: (bi, CB_KWVW)),
            pl.BlockSpec((ng, s), lambda bi, i: (0, 0)),
            pl.BlockSpec((2, tq, tq), lambda bi, i: (0, 0, 0)),
            pl.BlockSpec((LANES, ng), lambda bi, i: (0, 0)),
            pl.BlockSpec((nq, tq, LANES), lambda bi, i: (0, 0, 0)),
            pl.BlockSpec((1, LANES), lambda bi, i: (0, 0)),
            pl.BlockSpec((2, LANES), lambda bi, i: (0, 0)),
        ],
        out_specs=pl.BlockSpec((tq, 2 * LANES), lambda bi, i: (bi * nq + i, 0)),
        out_shape=jax.ShapeDtypeStruct((t, GROUP_WIDTH), F32),
        scratch_shapes=[
            pltpu.VMEM((nq, ng, m_rows), F32),
            pltpu.VMEM((4, tq, m_rows), F32),
            pltpu.VMEM((s, LANES), BF16),
            pltpu.VMEM((DV_PAD, s), BF16),
            pltpu.VMEM((s, LANES), BF16),
            pltpu.VMEM((DV_PAD, s), BF16),
            pltpu.VMEM((ng, LANES), BF16),
            pltpu.VMEM((HEAD_DIM, ng), BF16),
            pltpu.VMEM((1, m_rows), F32),
            pltpu.VMEM((DV_PAD, m_rows), F32),
        ],
        compiler_params=_params(("arbitrary", "arbitrary")),
        name="nsa_attention",
    )(rel_bias, proj, proj, kcvc, kcvct, proj, proj, bidxct, bidx2t, ovt, emt, qnw, knw)


def _mla_prep_kernel(cq_ref, ckv_ref, kr_ref, qaw_ref, kvw_ref, wq_ref, wk_ref, wvt_ref, qnw_ref, knw_ref,
                     cq_t_ref, s1_t_ref, s2_t_ref, ck_t_ref, sk_t_ref, qo_ref, ko_ref, vto_ref):
    cq = cq_ref[...]
    ms = jnp.sum(cq * cq, axis=-1, keepdims=True) * (1.0 / Q_LORA)
    hq = (cq * lax.rsqrt(ms + EPS) * qaw_ref[...]).astype(BF16)
    qf = _dot(hq, wq_ref[...])
    ckv = ckv_ref[...]
    ms = jnp.mean(ckv * ckv, axis=-1, keepdims=True)
    hkv = (ckv * lax.rsqrt(ms + EPS) * kvw_ref[...]).astype(BF16)
    kf = _dot(hkv, wk_ref[...])
    vt = _nt(wvt_ref[...], hkv)
    vto_ref[...] = jnp.where(_row(vt.shape) % DV_PAD == HEAD_DIM, 1.0, vt).astype(BF16)
    krb = kr_ref[...]
    kr_rot = krb * ck_t_ref[...] + pltpu.roll(krb, HEAD_DIM, 1) * sk_t_ref[...]
    scale = QK_DIM ** -0.5 * LOG2E
    for h in range(N_HEADS):
        cols = slice(LANES * h, LANES * (h + 1))
        x = qf[:, cols]
        x = x * cq_t_ref[...] + pltpu.roll(x, LANES - 16, 1) * s1_t_ref[...] + pltpu.roll(x, 16, 1) * s2_t_ref[...]
        ss = jnp.sum(x * x, axis=-1, keepdims=True) * (1.0 / QK_DIM)
        qo_ref[:, cols] = (x * lax.rsqrt(ss + EPS) * qnw_ref[...] * scale).astype(BF16)
        k = kf[:, cols] + kr_rot
        ss = jnp.sum(k * k, axis=-1, keepdims=True) * (1.0 / QK_DIM)
        ko_ref[:, cols] = (k * lax.rsqrt(ss + EPS) * knw_ref[...]).astype(BF16)


def _mla_prep(proj, qaw, kvw, wq, wk, wvt, qnw, knw, tabs, s, tm):
    t = proj.shape[0]
    npos = s // tm
    row = lambda i: (i, 0)
    const = lambda i: (0, 0)
    tab = pl.BlockSpec((tm, LANES), lambda i: (i % npos, 0))
    out = jax.ShapeDtypeStruct((t, N_HEADS * LANES), BF16)
    return pl.pallas_call(
        _mla_prep_kernel,
        grid=(t // tm,),
        in_specs=[
            pl.BlockSpec((tm, 2 * LANES), lambda i: (i, CB_CQ)),
            pl.BlockSpec((tm, LANES), lambda i: (i, CB_CKV)),
            pl.BlockSpec((tm, LANES), lambda i: (i, CB_KR)),
            pl.BlockSpec((1, 2 * LANES), const),
            pl.BlockSpec((1, LANES), const),
            pl.BlockSpec((2 * LANES, N_HEADS * LANES), const),
            pl.BlockSpec((LANES, N_HEADS * LANES), const),
            pl.BlockSpec((N_HEADS * DV_PAD, LANES), const),
            pl.BlockSpec((1, LANES), const),
            pl.BlockSpec((1, LANES), const),
            tab, tab, tab, tab, tab,
        ],
        out_specs=[pl.BlockSpec((tm, N_HEADS * LANES), row), pl.BlockSpec((tm, N_HEADS * LANES), row),
                   pl.BlockSpec((N_HEADS * DV_PAD, tm), lambda i: (0, i))],
        out_shape=[out, out, jax.ShapeDtypeStruct((N_HEADS * DV_PAD, t), BF16)],
        compiler_params=_params(("arbitrary",)),
        name="mla_prep",
    )(proj, proj, proj, qaw, kvw, wq, wk, wvt, qnw, knw, *tabs)


def _mla_attn_kernel(q_ref, k_ref, vt_ref, o_ref, m_s, acc_s, *, tq):
    i = pl.program_id(1)
    causal = _row((tq, tq)) <= _lane((tq, tq))
    for h in range(N_HEADS):
        m_s[h] = jnp.full((1, tq), NEG, F32)
        acc_s[h] = jnp.zeros((DV_PAD, tq), F32)

    def step(j, masked):
        rows = pl.ds(pl.multiple_of(j * tq, tq), tq)
        sts = []
        for h in range(N_HEADS):
            cols = slice(LANES * h, LANES * (h + 1))
            st = _nt(k_ref[rows, cols], q_ref[:, cols])
            sts.append(jnp.where(causal, st, NEG) if masked else st)
        vts = [vt_ref[DV_PAD * h:DV_PAD * (h + 1), _lane_tile(j, tq)] for h in range(N_HEADS)]
        _softmax_steps(sts, vts, [m_s.at[h] for h in range(N_HEADS)], [acc_s.at[h] for h in range(N_HEADS)])

    step(i, True)

    def body(jj, carry):
        step(i - jj, False)
        return carry

    lax.fori_loop(1, i + 1, body, 0)
    yt = jnp.concatenate([acc_s[h, 0:HEAD_DIM, :] / acc_s[h, HEAD_DIM:HEAD_DIM + 1, :] for h in range(N_HEADS)],
                         axis=0)
    o_ref[...] = yt.T


def _mla_attention(qm, km, vmt, b, s, tq):
    t = b * s
    nq = s // tq
    w = N_HEADS * LANES
    return pl.pallas_call(
        functools.partial(_mla_attn_kernel, tq=tq),
        grid=(b, nq),
        in_specs=[
            pl.BlockSpec((tq, w), lambda bi, i: (bi * nq + i, 0)),
            pl.BlockSpec((s, w), lambda bi, i: (bi, 0)),
            pl.BlockSpec((N_HEADS * DV_PAD, s), lambda bi, i: (0, bi)),
        ],
        out_specs=pl.BlockSpec((tq, 2 * LANES), lambda bi, i: (bi * nq + i, 0)),
        out_shape=jax.ShapeDtypeStruct((t, GROUP_WIDTH), F32),
        scratch_shapes=[pltpu.VMEM((N_HEADS, 1, tq), F32), pltpu.VMEM((N_HEADS, DV_PAD, tq), F32)],
        compiler_params=_params(("arbitrary", "arbitrary")),
        name="mla_attention",
    )(qm, km, vmt)


def _sb_kernel(q_ref, k_ref, v_ref, o_ref, kb_s, vt_s, q_s, r_s, acc_s, *, tq, s_len):
    i = pl.program_id(1)

    @pl.when(i == 0)
    def _cast_kv():
        ch = 256

        def body(t, carry):
            rows = pl.ds(pl.multiple_of(t * ch, ch), ch)
            kb_s[rows, :] = k_ref[rows, :].astype(BF16)
            vt_s[:, _lane_tile(t, ch)] = v_ref[rows, :].T.astype(BF16)
            return carry

        lax.fori_loop(0, s_len // ch, body, 0)

    key = _row((tq, tq))
    qry = _lane((tq, tq))
    strict = key < qry
    tri = jnp.where(key <= qry, 1.0, 0.0).astype(BF16)
    lane = _lane((tq, LANES))
    q = q_ref[...]
    for h in range(N_HEADS):
        mine = (lane < HEAD_DIM) if h % 2 == 0 else (lane >= HEAD_DIM)
        cols = slice(LANES * (h // 2), LANES * (h // 2 + 1))
        q_s[h] = jnp.where(mine, q[:, cols] * (HEAD_DIM ** -0.5 * LOG2E), 0.0).astype(BF16)
        r_s[h] = jnp.zeros((1, tq), F32)
        acc_s[h] = jnp.zeros((HEAD_DIM, tq), F32)

    heads = range(N_HEADS)

    def step(j, masked):
        rows = pl.ds(pl.multiple_of(j * tq, tq), tq)
        zs = [_nt(kb_s[rows, LANES * (h // 2):LANES * (h // 2 + 1)], q_s[h]) for h in heads]
        r_old = [r_s[h] for h in heads]
        acc_old = [acc_s[h] for h in heads]
        lgs = [-(jnp.maximum(z, 0.0) + jnp.log2(1.0 + jnp.exp2(-jnp.abs(z)))) for z in zs]
        if masked:
            lgs = [jnp.where(strict, lg, 0.0) for lg in lgs]
        his = [lg.astype(BF16) for lg in lgs]
        los = [(lg - hi.astype(F32)).astype(BF16) for lg, hi in zip(lgs, his)]
        csums = [_dot(tri, hi) + _dot(tri, lo) + r for hi, lo, r in zip(his, los, r_old)]
        als = [jnp.exp2(z + cs) for z, cs in zip(zs, csums)]
        if masked:
            als = [jnp.where(strict, a, 0.0) for a in als]
        acc_new = [acc + _dot(vt_s[HEAD_DIM * h:HEAD_DIM * (h + 1), _lane_tile(j, tq)], als[h].astype(BF16))
                   for h, acc in zip(heads, acc_old)]
        for h in heads:
            acc_s[h] = acc_new[h]
            r_s[h] = csums[h][0:1, :]

    step(i, True)

    def body(jj, carry):
        step(i - jj, False)
        return carry

    lax.fori_loop(1, i + 1, body, 0)
    o_ref[...] = jnp.concatenate([acc_s[h] for h in range(N_HEADS)], axis=0).T


def _sb_attention(proj, b, s, tq):
    t = b * s
    nq = s // tq
    w = 2 * LANES
    return pl.pallas_call(
        functools.partial(_sb_kernel, tq=tq, s_len=s),
        grid=(b, nq),
        in_specs=[
            pl.BlockSpec((tq, w), lambda bi, i: (bi * nq + i, CB_SQ)),
            pl.BlockSpec((s, w), lambda bi, i: (bi, CB_SK)),
            pl.BlockSpec((s, w), lambda bi, i: (bi, CB_SV)),
        ],
        out_specs=pl.BlockSpec((tq, w), lambda bi, i: (bi * nq + i, 0)),
        out_shape=jax.ShapeDtypeStruct((t, GROUP_WIDTH), F32),
        scratch_shapes=[pltpu.VMEM((s, w), BF16), pltpu.VMEM((w, s), BF16),
                        pltpu.VMEM((N_HEADS, tq, LANES), BF16),
                        pltpu.VMEM((N_HEADS, 1, tq), F32), pltpu.VMEM((N_HEADS, HEAD_DIM, tq), F32)],
        compiler_params=_params(("arbitrary", "arbitrary")),
        name="sb_attention",
    )(proj, proj, proj)


def _post_kernel(a_ref, ap_ref, yb_ref, yc_ref, yd_ref, x_ref, cw_ref, cb_ref, onw_ref, wo_ref, n2w_ref,
                 w1_ref, w2_ref, o_ref, *, tm, s_len, ffc):
    i = pl.program_id(0)
    a = a_ref[...]
    gw = GROUP_WIDTH
    v = a[:, gw:2 * gw] * a[:, 2 * gw:3 * gw]
    ap = ap_ref[...]
    first = (i * tm) % s_len == 0
    vp = jnp.where(first, 0.0, ap[:, gw:2 * gw] * ap[:, 2 * gw:3 * gw])
    row = _row(v.shape)
    v1 = jnp.where(row == 0, vp[7:8, :], pltpu.roll(v, 1, 0))
    v2 = jnp.where(row == 0, vp[6:7, :], jnp.where(row == 1, vp[7:8, :], pltpu.roll(v, 2, 0)))
    conv = cw_ref[0:1, :] * v2 + cw_ref[1:2, :] * v1 + cw_ref[2:3, :] * v
    ya = a[:, 0:gw] * (conv + cb_ref[...])

    mix = None
    for g, y in enumerate((ya, yb_ref[...], yc_ref[...], yd_ref[...])):
        ms = jnp.mean(y * y, axis=-1, keepdims=True)
        yn = (y * lax.rsqrt(ms + EPS) * onw_ref[:, gw * g:gw * (g + 1)]).astype(BF16)
        part = _dot(yn, wo_ref[gw * g:gw * (g + 1), :])
        mix = part if mix is None else mix + part
    x1 = x_ref[...] + mix

    ms = jnp.mean(x1 * x1, axis=-1, keepdims=True)
    h2 = (x1 * lax.rsqrt(ms + EPS) * n2w_ref[...]).astype(BF16)
    ff = None
    for cidx in range(D_FF // ffc):
        u = _dot(h2, w1_ref[:, ffc * cidx:ffc * (cidx + 1)])
        u = jnp.square(jnp.maximum(u, 0.0)).astype(BF16)
        part = _dot(u, w2_ref[ffc * cidx:ffc * (cidx + 1), :])
        ff = part if ff is None else ff + part
    o_ref[...] = x1 + ff


def _post(proj, yb, yc, yd, x2d, cw, cb, onw, wo, n2w, w1, w2, s, tm):
    t = x2d.shape[0]
    gw = GROUP_WIDTH
    row = lambda i: (i, 0)
    const = lambda i: (0, 0)
    once = pl.Buffered(1)
    kern = functools.partial(_post_kernel, tm=tm, s_len=s, ffc=1024)
    return pl.pallas_call(
        kern,
        grid=(t // tm,),
        in_specs=[
            pl.BlockSpec((tm, 3 * gw), row),
            pl.BlockSpec((8, 3 * gw), lambda i: (jnp.maximum(i * (tm // 8) - 1, 0), 0)),
            pl.BlockSpec((tm, gw), row),
            pl.BlockSpec((tm, gw), row),
            pl.BlockSpec((tm, gw), row),
            pl.BlockSpec((tm, D_MODEL), row),
            pl.BlockSpec((3, gw), const),
            pl.BlockSpec((1, gw), const),
            pl.BlockSpec((1, D_MODEL), const),
            pl.BlockSpec((D_MODEL, D_MODEL), const, pipeline_mode=once),
            pl.BlockSpec((1, D_MODEL), const),
            pl.BlockSpec((D_MODEL, D_FF), const, pipeline_mode=once),
            pl.BlockSpec((D_FF, D_MODEL), const, pipeline_mode=once),
        ],
        out_specs=pl.BlockSpec((tm, D_MODEL), row),
        out_shape=jax.ShapeDtypeStruct((t, D_MODEL), F32),
        compiler_params=_params(("arbitrary",)),
        name="post",
    )(proj, proj, yb, yc, yd, x2d, cw, cb, onw, wo, n2w, w1, w2)


def _t5_bucket(dist):
    max_exact = N_BUCKETS // 2
    d = jnp.maximum(dist, 0)
    large = max_exact + (jnp.log(jnp.maximum(d, 1).astype(F32) / max_exact)
                         / math.log(MAX_DISTANCE / max_exact) * (N_BUCKETS - max_exact)).astype(jnp.int32)
    large = jnp.minimum(large, N_BUCKETS - 1)
    return jnp.where(d < max_exact, d, large)


def _tables(s, tq_nsa):
    n_cmp = (s - CMP_LEN) // CMP_STRIDE + 1
    ng = s // CMP_STRIDE
    n_slc = s // SLC_LEN
    tpos = jnp.arange(s)[None, :]
    n = jnp.arange(ng)[:, None]
    dist_c = tpos - (n * CMP_STRIDE + CMP_LEN - 1)
    bidxct = jnp.where((dist_c >= 0) & (n < n_cmp), _t5_bucket(dist_c), -1).astype(jnp.int32)
    key = jnp.arange(tq_nsa)[:, None]
    qry = jnp.arange(tq_nsa)[None, :]
    bidx2t = jnp.stack([_t5_bucket(qry - key), _t5_bucket(tq_nsa + qry - key)]).astype(jnp.int32)
    starts = np.arange(n_cmp) * CMP_STRIDE
    ends = starts + CMP_LEN
    s0 = np.arange(n_slc) * SLC_LEN
    s1 = s0 + SLC_LEN
    ovl = np.clip(np.minimum(ends[:, None], s1[None]) - np.maximum(starts[:, None], s0[None]), 0, None) / CMP_LEN
    ovt = np.zeros((LANES, ng), np.float32)
    ovt[:n_slc, :n_cmp] = ovl.T
    key_blk = (np.arange(s) // SLC_LEN).reshape(s // tq_nsa, tq_nsa, 1)
    emt = (np.arange(LANES).reshape(1, 1, LANES) == key_blk).astype(np.float32)
    inv = 1.0 / (ROPE_THETA ** (jnp.arange(0, ROPE_DIM, 2, dtype=F32) / ROPE_DIM))
    ang = jnp.arange(s, dtype=F32)[:, None] * inv[None, :]
    cos, sin = jnp.cos(ang), jnp.sin(ang)
    z16 = jnp.zeros((s, 16), F32)
    z32 = jnp.zeros((s, 32), F32)
    z64 = jnp.zeros((s, 64), F32)
    one64 = jnp.ones((s, 64), F32)
    cq_t = jnp.concatenate([one64, cos, cos, z32], axis=1)
    s1_t = jnp.concatenate([z64, -sin, z16, z32], axis=1)
    s2_t = jnp.concatenate([z64, z16, sin, z32], axis=1)
    ck_t = jnp.concatenate([z64, cos, cos, z32], axis=1)
    sk_t = jnp.concatenate([z64, -sin, sin, z32], axis=1)
    return dict(bidxct=bidxct, bidx2t=bidx2t, ovt=jnp.asarray(ovt, BF16), emt=jnp.asarray(emt, BF16),
                rope=(cq_t, s1_t, s2_t, ck_t, sk_t))


def _pad_cols(w, width):
    return jnp.pad(w, ((0, 0), (0, width - w.shape[1])))


def _layer_weights(l, w_in, conv_w, conv_b, nsa_q_norm, nsa_k_norm, cmp_pos, cmp_w1, cmp_w2, mla_q_a_norm,
                   mla_kv_norm, mla_wq_b, mla_wkv_b, mla_q_norm, mla_k_norm, out_norm_w, w_out, norm2_w,
                   ffn_w1, ffn_w2):
    wi = w_in[l]
    kr = wi[:, 1740:1772]
    z32 = jnp.zeros((D_MODEL, 32), F32)
    w_in_p = jnp.concatenate([
        wi[:, 0:1408],
        _pad_cols(wi[:, 1408:1420], LANES),
        _pad_cols(wi[:, 1420:1612], 2 * LANES),
        wi[:, 1612:1740],
        kr[:, 16:32], kr[:, 0:16], z32, kr, z32,
        wi[:, 1772:2540],
    ], axis=1).astype(BF16)
    w1 = cmp_w1[l].reshape(2, CMP_LEN, HEAD_DIM, CMP_HIDDEN)
    zw = jnp.zeros((CMP_LEN, HEAD_DIM, CMP_HIDDEN), F32)
    cw1 = jnp.concatenate([jnp.concatenate([w1[0], zw], axis=2), jnp.concatenate([zw, w1[1]], axis=2)],
                          axis=1).astype(BF16)
    zc = jnp.zeros((CMP_HIDDEN, HEAD_DIM), F32)
    cw2 = jnp.concatenate([jnp.concatenate([cmp_w2[l, 0], zc], axis=1),
                           jnp.concatenate([zc, cmp_w2[l, 1]], axis=1)], axis=0).astype(BF16)
    cpos = jnp.concatenate([cmp_pos[l, 0], cmp_pos[l, 1]], axis=1)
    kn = nsa_k_norm[l]
    ones64 = jnp.ones((HEAD_DIM,), F32)
    knw_c = jnp.concatenate([kn[0], ones64])[None, :]
    knw_sw = jnp.stack([jnp.concatenate([kn[1], ones64]), jnp.concatenate([kn[2], ones64])])
    qnw = jnp.concatenate([nsa_q_norm[l], nsa_q_norm[l]])[None, :]
    wq = mla_wq_b[l].reshape(Q_LORA, N_HEADS, QK_DIM)
    wq = jnp.pad(wq, ((0, 2 * LANES - Q_LORA), (0, 0), (0, LANES - QK_DIM))).reshape(2 * LANES, N_HEADS * LANES)
    wkv = mla_wkv_b[l].reshape(KV_LORA, N_HEADS, 2 * HEAD_DIM)
    wk = jnp.pad(wkv[:, :, :HEAD_DIM], ((0, 0), (0, 0), (0, LANES - HEAD_DIM))).reshape(KV_LORA, N_HEADS * LANES)
    wvt = jnp.pad(wkv[:, :, HEAD_DIM:], ((0, 0), (0, 0), (0, DV_PAD - HEAD_DIM))).reshape(KV_LORA, -1).T
    return dict(
        w_in=w_in_p, cw1=cw1, cw2=cw2, cw2t=cw2.T, cpos=cpos, knw_c=knw_c, knw_sw=knw_sw, qnw=qnw,
        qaw=_pad_cols(mla_q_a_norm[l][None, :], 2 * LANES), kvw=mla_kv_norm[l][None, :],
        wq=wq.astype(BF16), wk=wk.astype(BF16), wvt=wvt.astype(BF16),
        mqn=_pad_cols(mla_q_norm[l][None, :], LANES), mkn=_pad_cols(mla_k_norm[l][None, :], LANES),
        cw=conv_w[l], cb=conv_b[l][None, :], onw=out_norm_w[l][None, :], wo=w_out[l].astype(BF16),
        n2w=norm2_w[l][None, :], w1=ffn_w1[l].astype(BF16), w2=ffn_w2[l].astype(BF16))


TM_PROJ = 512
TM_PREP = 512
TM_POST = 512
TQ_NSA = 128
TQ_MLA = 256
TQ_SB = 256


def kernel(x, rel_bias, norm1_w, w_in, conv_w, conv_b, nsa_q_norm, nsa_k_norm, cmp_pos, cmp_w1, cmp_w2,
           mla_q_a_norm, mla_kv_norm, mla_wq_b, mla_wkv_b, mla_q_norm, mla_k_norm, out_norm_w, w_out, norm2_w,
           ffn_w1, ffn_w2):
    b, s, d = x.shape
    depth = w_in.shape[0]
    tabs = _tables(s, TQ_NSA)
    x2d = x.reshape(b * s, d)
    for l in range(depth):
        w = _layer_weights(l, w_in, conv_w, conv_b, nsa_q_norm, nsa_k_norm, cmp_pos, cmp_w1, cmp_w2,
                           mla_q_a_norm, mla_kv_norm, mla_wq_b, mla_wkv_b, mla_q_norm, mla_k_norm, out_norm_w,
                           w_out, norm2_w, ffn_w1, ffn_w2)
        proj = _inproj(x2d, norm1_w[l][None, :], w["w_in"], TM_PROJ)
        kcvc, kcvct = _compress(proj, w["cpos"], w["cw1"], w["cw2"], w["cw2t"], w["knw_c"], b, s)
        yb = _nsa_attention(proj, kcvc, kcvct, rel_bias, tabs["bidxct"], tabs["bidx2t"], tabs["ovt"], tabs["emt"],
                            w["qnw"], w["knw_sw"], b, s, TQ_NSA)
        qm, km, vmt = _mla_prep(proj, w["qaw"], w["kvw"], w["wq"], w["wk"], w["wvt"], w["mqn"], w["mkn"],
                                tabs["rope"], s, TM_PREP)
        yc = _mla_attention(qm, km, vmt, b, s, TQ_MLA)
        yd = _sb_attention(proj, b, s, TQ_SB)
        x2d = _post(proj, yb, yc, yd, x2d, w["cw"], w["cb"], w["onw"], w["wo"], w["n2w"], w["w1"], w["w2"],
                    s, TM_POST)
    return x2d.reshape(b, s, d)
```

```python
import functools
import math

import jax
import jax.numpy as jnp
import numpy as np
from jax import lax
from jax.experimental import pallas as pl
from jax.experimental.pallas import tpu as pltpu

F32 = jnp.float32
BF16 = jnp.bfloat16

D_MODEL = 1024
GROUP_WIDTH = 256
HEAD_DIM = 64
N_HEADS = 4
LANES = 128
CMP_LEN = 32
CMP_STRIDE = 16
SLC_LEN = 64
N_SEL = 16
WINDOW = 512
CMP_HIDDEN = 256
Q_LORA = 192
KV_LORA = 128
ROPE_DIM = 32
QK_DIM = 96
ROPE_THETA = 10000.0
N_BUCKETS = 32
MAX_DISTANCE = 128
D_FF = 4096
EPS = 1e-6
NEG = -1e30
LOG2E = math.log2(math.e)
DV_PAD = 80

NP = 2816
CB_NQ = 3
CB_KCVC = 8
CB_KSVS = 9
CB_KWVW = 10
CB_GATE = 11
CB_CQ = 6
CB_CKV = 14
CB_KR = 15
CB_SQ = 8
CB_SK = 9
CB_SV = 10

VMEM_LIMIT = 56 * 1024 * 1024

NT_DIMS = (((1,), (1,)), ((), ()))


def _params(sem):
    return pltpu.CompilerParams(dimension_semantics=sem, vmem_limit_bytes=VMEM_LIMIT)


def _nt(a, b):
    return lax.dot_general(a, b, NT_DIMS, preferred_element_type=F32)


def _dot(a, b):
    return jnp.dot(a, b, preferred_element_type=F32)


def _lane(shape):
    return lax.broadcasted_iota(jnp.int32, shape, len(shape) - 1)


def _row(shape):
    return lax.broadcasted_iota(jnp.int32, shape, len(shape) - 2)


def _lane_tile(j, width):
    return pl.ds(pl.multiple_of(j * width, width), width)


def _inproj_kernel(x_ref, nw_ref, w_ref, o_ref):
    x = x_ref[...]
    ms = jnp.mean(x * x, axis=-1, keepdims=True)
    h = (x * lax.rsqrt(ms + EPS) * nw_ref[...]).astype(BF16)
    o_ref[...] = _dot(h, w_ref[...])


def _inproj(x2d, nw, w, tm):
    t = x2d.shape[0]
    return pl.pallas_call(
        _inproj_kernel,
        grid=(t // tm,),
        in_specs=[
            pl.BlockSpec((tm, D_MODEL), lambda i: (i, 0)),
            pl.BlockSpec((1, D_MODEL), lambda i: (0, 0)),
            pl.BlockSpec((D_MODEL, NP), lambda i: (0, 0), pipeline_mode=pl.Buffered(1)),
        ],
        out_specs=pl.BlockSpec((tm, NP), lambda i: (i, 0)),
        out_shape=jax.ShapeDtypeStruct((t, NP), F32),
        compiler_params=_params(("arbitrary",)),
        name="inproj",
    )(x2d, nw, w)


def _compress_kernel(x_ref, pos_ref, w1_ref, w2_ref, w2t_ref, knw_ref, o_ref, ot_ref):
    ng = x_ref.shape[1]
    acc_a = jnp.zeros((ng, 2 * CMP_HIDDEN), F32)
    acc_b = jnp.zeros((ng, 2 * CMP_HIDDEN), F32)
    for i in range(CMP_STRIDE):
        x = x_ref[0, :, i, :]
        xa = (x + pos_ref[i:i + 1, :]).astype(BF16)
        xb = (x + pos_ref[CMP_STRIDE + i:CMP_STRIDE + i + 1, :]).astype(BF16)
        acc_a += _dot(xa, w1_ref[i])
        acc_b += _dot(xb, w1_ref[CMP_STRIDE + i])
    pre = acc_a + pltpu.roll(acc_b, ng - 1, 0)
    hdn = (pre * jax.nn.sigmoid(pre)).astype(BF16)
    out = _dot(hdn, w2_ref[...])
    lane = _lane(out.shape)
    is_k = lane < HEAD_DIM
    ss = jnp.sum(jnp.where(is_k, out * out, 0.0), axis=-1, keepdims=True) * (1.0 / HEAD_DIM)
    o_ref[0] = jnp.where(is_k, out * lax.rsqrt(ss + EPS) * knw_ref[...], out)
    ot_ref[0] = _nt(w2t_ref[...], hdn)


def _compress(proj, pos, w1, w2, w2t, knw, b, s):
    ng = s // CMP_STRIDE
    x4 = proj.reshape(b, ng, CMP_STRIDE, NP)
    return pl.pallas_call(
        _compress_kernel,
        grid=(b,),
        in_specs=[
            pl.BlockSpec((1, ng, CMP_STRIDE, LANES), lambda i: (i, 0, 0, CB_KCVC)),
            pl.BlockSpec((CMP_LEN, LANES), lambda i: (0, 0)),
            pl.BlockSpec((CMP_LEN, LANES, 2 * CMP_HIDDEN), lambda i: (0, 0, 0)),
            pl.BlockSpec((2 * CMP_HIDDEN, LANES), lambda i: (0, 0)),
            pl.BlockSpec((LANES, 2 * CMP_HIDDEN), lambda i: (0, 0)),
            pl.BlockSpec((1, LANES), lambda i: (0, 0)),
        ],
        out_specs=[pl.BlockSpec((1, ng, LANES), lambda i: (i, 0, 0)),
                   pl.BlockSpec((1, LANES, ng), lambda i: (i, 0, 0))],
        out_shape=[jax.ShapeDtypeStruct((b, ng, LANES), F32), jax.ShapeDtypeStruct((b, LANES, ng), F32)],
        compiler_params=_params(("arbitrary",)),
        name="nsa_compress",
    )(x4, pos, w1, w2, w2t, knw)


def _softmax_steps(sts, vts, m_refs, acc_refs):
    m_old = [r[...] for r in m_refs]
    acc_old = [r[...] for r in acc_refs]
    m_new = [jnp.maximum(m, jnp.max(st, axis=0, keepdims=True)) for m, st in zip(m_old, sts)]
    ps = [jnp.exp2(st - m).astype(BF16) for st, m in zip(sts, m_new)]
    alphas = [jnp.exp2(mo - mn) for mo, mn in zip(m_old, m_new)]
    acc_new = [al * acc + _dot(vt, p) for al, acc, vt, p in zip(alphas, acc_old, vts, ps)]
    for r, v in zip(m_refs, m_new):
        r[...] = v
    for r, v in zip(acc_refs, acc_new):
        r[...] = v


def _with_ones_row(vt):
    pad = jnp.where(_row((DV_PAD - HEAD_DIM, vt.shape[1])) == 0, 1.0, 0.0).astype(vt.dtype)
    return jnp.concatenate([vt, pad], axis=0)


def _bucket_bias(bidx, relb_ref, h, fill):
    acc = jnp.full(bidx.shape, fill, F32)
    for bk in range(N_BUCKETS):
        acc = jnp.where(bidx == bk, relb_ref[bk, h] * LOG2E, acc)
    return acc


def _dup_low_half(x):
    y = jnp.where(_lane(x.shape) < HEAD_DIM, x, 0.0)
    return y + pltpu.roll(y, HEAD_DIM, 1)


def _nsa_kernel(relb_ref, q_ref, g_ref, kcvc_ref, kcvct_ref, ksvs_ref, kwvw_ref, bidxct_ref, bidx2t_ref, ovt_ref,
                emt_ref, qnw_ref, knw_ref, o_ref,
                biasc_s, bias2_s, ks_s, vst_s, kw_s, vwt_s, kc_s, vct_s, q_s, m_s, acc_s, *, tq, s_len):
    b = pl.program_id(0)
    i = pl.program_id(1)
    n_win = WINDOW // tq
    ng = s_len // CMP_STRIDE

    @pl.when((b == 0) & (i == 0))
    def _build_bias_tables():
        key = _row((tq, tq))
        qry = _lane((tq, tq))
        for h in range(N_HEADS):
            cols = slice(h * tq, (h + 1) * tq)
            far = jnp.full((tq, tq), relb_ref[N_BUCKETS - 1, h] * LOG2E, F32)
            bias2_s[0, :, cols] = jnp.where(key <= qry, _bucket_bias(bidx2t_ref[0], relb_ref, h, NEG), NEG)
            bias2_s[1, :, cols] = _bucket_bias(bidx2t_ref[1], relb_ref, h, NEG)
            bias2_s[2, :, cols] = far
            bias2_s[3, :, cols] = jnp.where(key > qry, far, NEG)

        def body(t, carry):
            bi = bidxct_ref[:, _lane_tile(t, tq)]
            for h in range(N_HEADS):
                biasc_s[t, :, h * tq:(h + 1) * tq] = _bucket_bias(bi, relb_ref, h, NEG)
            return carry

        lax.fori_loop(0, s_len // tq, body, 0)

    @pl.when(i == 0)
    def _prep_kv():
        ch = 256

        def body(t, carry):
            rows = pl.ds(pl.multiple_of(t * ch, ch), ch)
            for src, kdst, vdst, widx in ((ksvs_ref, ks_s, vst_s, 0), (kwvw_ref, kw_s, vwt_s, 1)):
                x = src[rows, :]
                is_k = _lane(x.shape) < HEAD_DIM
                ss = jnp.sum(jnp.where(is_k, x * x, 0.0), axis=-1, keepdims=True) * (1.0 / HEAD_DIM)
                kn = x * lax.rsqrt(ss + EPS) * knw_ref[widx:widx + 1, :]
                kdst[rows, :] = _dup_low_half(kn).astype(BF16)
                vdst[:, _lane_tile(t, ch)] = _with_ones_row(x.T[HEAD_DIM:, :]).astype(BF16)
            return carry

        lax.fori_loop(0, s_len // ch, body, 0)
        kc_s[...] = _dup_low_half(kcvc_ref[0]).astype(BF16)
        vct_s[...] = kcvct_ref[0][HEAD_DIM:, :].astype(BF16)

    q = q_ref[...]
    lane = _lane((tq, LANES))
    heads = range(N_HEADS)
    for h in heads:
        x = q[:, LANES * (h // 2):LANES * (h // 2 + 1)]
        mine = (lane < HEAD_DIM) if h % 2 == 0 else (lane >= HEAD_DIM)
        xm = jnp.where(mine, x, 0.0)
        ss = jnp.sum(xm * xm, axis=-1, keepdims=True) * (1.0 / HEAD_DIM)
        q_s[h] = (xm * lax.rsqrt(ss + EPS) * qnw_ref[...] * (HEAD_DIM ** -0.5 * LOG2E)).astype(BF16)

    o_cmp = []
    psum = None
    for h in heads:
        lc = _nt(kc_s[...], q_s[h]) + biasc_s[i, :, h * tq:(h + 1) * tq]
        mc = jnp.max(lc, axis=0, keepdims=True)
        pc = jnp.where(lc > 0.5 * NEG, jnp.exp2(lc - mc), 0.0)
        den = jnp.sum(pc, axis=0, keepdims=True)
        pc = pc / jnp.where(den > 0.0, den, 1.0)
        o_cmp.append(_dot(vct_s[...], pc.astype(BF16)))
        psum = pc if psum is None else psum + pc
    p_hi = psum.astype(BF16)
    p_lo = (psum - p_hi.astype(F32)).astype(BF16)
    score = _dot(ovt_ref[...], p_hi) + _dot(ovt_ref[...], p_lo)

    n_slc = s_len // SLC_LEN
    blk = _row((n_slc, tq))
    tpos = i * tq + _lane((n_slc, tq))
    tblk = tpos // SLC_LEN
    valid = blk * SLC_LEN <= tpos
    forced = (blk == 0) | (blk == tblk) | (blk == tblk - 1)
    sc = jnp.where(forced, jnp.inf, jnp.where(valid, score[0:n_slc], -jnp.inf))
    rank = jnp.zeros((n_slc, tq), F32)
    for k in range(n_slc):
        ck = sc[k:k + 1, :]
        beats = (ck > sc) | ((ck == sc) & (blk > k))
        rank += jnp.where(beats, 1.0, 0.0)
    pen = jnp.where((rank < float(min(N_SEL, n_slc))) & valid, 0.0, NEG)
    pen = jnp.concatenate([pen, jnp.zeros((LANES - n_slc, tq), F32)], axis=0).astype(BF16)

    for c in range(2 * N_HEADS):
        m_s[c] = jnp.full((1, tq), NEG, F32)
        acc_s[c] = jnp.zeros((DV_PAD, tq), F32)

    def step(jj, with_win):
        j = i - jj
        rows = pl.ds(pl.multiple_of(j * tq, tq), tq)
        cols = _lane_tile(j, tq)
        static = isinstance(jj, int)
        kind = min(jj, 2) if static else jnp.minimum(jj, 2)
        ks = ks_s[rows, :]
        masked = _dot(emt_ref[j], pen)
        sts = [_nt(ks, q_s[h]) + masked + bias2_s[kind, :, h * tq:(h + 1) * tq] for h in heads]
        vts = [vst_s[:, cols]] * N_HEADS
        chains = list(heads)
        if with_win:
            kw = kw_s[rows, :]
            kind_w = 3 if jj == n_win else kind
            sts += [_nt(kw, q_s[h]) + bias2_s[kind_w, :, h * tq:(h + 1) * tq] for h in heads]
            vts += [vwt_s[:, cols]] * N_HEADS
            chains += [N_HEADS + h for h in heads]
        _softmax_steps(sts, vts, [m_s.at[c] for c in chains], [acc_s.at[c] for c in chains])

    step(0, True)
    for jj in range(1, n_win + 1):
        pl.when(i >= jj)(functools.partial(step, jj, True))

    def sel_body(jj, carry):
        step(jj, False)
        return carry

    lax.fori_loop(n_win + 1, i + 1, sel_body, 0)

    gt = jax.nn.sigmoid(g_ref[...]).T
    ys = []
    for h in heads:
        o_slc = acc_s[h, 0:HEAD_DIM, :] / acc_s[h, HEAD_DIM:HEAD_DIM + 1, :]
        o_win = acc_s[N_HEADS + h, 0:HEAD_DIM, :] / acc_s[N_HEADS + h, HEAD_DIM:HEAD_DIM + 1, :]
        ys.append(gt[3 * h:3 * h + 1, :] * o_cmp[h] + gt[3 * h + 1:3 * h + 2, :] * o_slc
                  + gt[3 * h + 2:3 * h + 3, :] * o_win)
    o_ref[...] = jnp.concatenate(ys, axis=0).T


def _nsa_attention(proj, kcvc, kcvct, rel_bias, bidxct, bidx2t, ovt, emt, qnw, knw, b, s, tq):
    t = b * s
    nq = s // tq
    m_rows = N_HEADS * tq
    ng = s // CMP_STRIDE
    kern = functools.partial(_nsa_kernel, tq=tq, s_len=s)
    return pl.pallas_call(
        kern,
        grid=(b, nq),
        in_specs=[
            pl.BlockSpec(memory_space=pltpu.SMEM),
            pl.BlockSpec((tq, 2 * LANES), lambda bi, i: (bi * nq + i, CB_NQ)),
            pl.BlockSpec((tq, LANES), lambda bi, i: (bi * nq + i, CB_GATE)),
            pl.BlockSpec((1, ng, LANES), lambda bi, i: (bi, 0, 0)),
            pl.BlockSpec((1, LANES, ng), lambda bi, i: (bi, 0, 0)),
            pl.BlockSpec((s, LANES), lambda bi, i: (bi, CB_KSVS)),
            pl.BlockSpec((s, LANES), lambda bi, i: (bi, CB_KWVW)),
            pl.BlockSpec((ng, s), lambda bi, i: (0, 0)),
            pl.BlockSpec((2, tq, tq), lambda bi, i: (0, 0, 0)),
            pl.BlockSpec((LANES, ng), lambda bi, i: (0, 0)),
            pl.BlockSpec((nq, tq, LANES), lambda bi, i: (0, 0, 0)),
            pl.BlockSpec((1, LANES), lambda bi, i: (0, 0)),
            pl.BlockSpec((2, LANES), lambda bi, i: (0, 0)),
        ],
        out_specs=pl.BlockSpec((tq, 2 * LANES), lambda bi, i: (bi * nq + i, 0)),
        out_shape=jax.ShapeDtypeStruct((t, GROUP_WIDTH), F32),
        scratch_shapes=[
            pltpu.VMEM((nq, ng, m_rows), F32),
            pltpu.VMEM((4, tq, m_rows), F32),
            pltpu.VMEM((s, LANES), BF16),
            pltpu.VMEM((DV_PAD, s), BF16),
            pltpu.VMEM((s, LANES), BF16),
            pltpu.VMEM((DV_PAD, s), BF16),
            pltpu.VMEM((ng, LANES), BF16),
            pltpu.VMEM((HEAD_DIM, ng), BF16),
            pltpu.VMEM((N_HEADS, tq, LANES), BF16),
            pltpu.VMEM((2 * N_HEADS, 1, tq), F32),
            pltpu.VMEM((2 * N_HEADS, DV_PAD, tq), F32),
        ],
        compiler_params=_params(("arbitrary", "arbitrary")),
        name="nsa_attention",
    )(rel_bias, proj, proj, kcvc, kcvct, proj, proj, bidxct, bidx2t, ovt, emt, qnw, knw)


def _mla_prep_kernel(cq_ref, ckv_ref, kr_ref, qaw_ref, kvw_ref, wq_ref, wk_ref, wvt_ref, qnw_ref, knw_ref,
                     cq_t_ref, s1_t_ref, s2_t_ref, ck_t_ref, sk_t_ref, qo_ref, ko_ref, vto_ref):
    cq = cq_ref[...]
    ms = jnp.sum(cq * cq, axis=-1, keepdims=True) * (1.0 / Q_LORA)
    hq = (cq * lax.rsqrt(ms + EPS) * qaw_ref[...]).astype(BF16)
    qf = _dot(hq, wq_ref[...])
    ckv = ckv_ref[...]
    ms = jnp.mean(ckv * ckv, axis=-1, keepdims=True)
    hkv = (ckv * lax.rsqrt(ms + EPS) * kvw_ref[...]).astype(BF16)
    kf = _dot(hkv, wk_ref[...])
    vt = _nt(wvt_ref[...], hkv)
    vto_ref[...] = jnp.where(_row(vt.shape) % DV_PAD == HEAD_DIM, 1.0, vt).astype(BF16)
    krb = kr_ref[...]
    kr_rot = krb * ck_t_ref[...] + pltpu.roll(krb, HEAD_DIM, 1) * sk_t_ref[...]
    scale = QK_DIM ** -0.5 * LOG2E
    for h in range(N_HEADS):
        cols = slice(LANES * h, LANES * (h + 1))
        x = qf[:, cols]
        x = x * cq_t_ref[...] + pltpu.roll(x, LANES - 16, 1) * s1_t_ref[...] + pltpu.roll(x, 16, 1) * s2_t_ref[...]
        ss = jnp.sum(x * x, axis=-1, keepdims=True) * (1.0 / QK_DIM)
        qo_ref[:, cols] = (x * lax.rsqrt(ss + EPS) * qnw_ref[...] * scale).astype(BF16)
        k = kf[:, cols] + kr_rot
        ss = jnp.sum(k * k, axis=-1, keepdims=True) * (1.0 / QK_DIM)
        ko_ref[:, cols] = (k * lax.rsqrt(ss + EPS) * knw_ref[...]).astype(BF16)


def _mla_prep(proj, qaw, kvw, wq, wk, wvt, qnw, knw, tabs, s, tm):
    t = proj.shape[0]
    npos = s // tm
    row = lambda i: (i, 0)
    const = lambda i: (0, 0)
    tab = pl.BlockSpec((tm, LANES), lambda i: (i % npos, 0))
    out = jax.ShapeDtypeStruct((t, N_HEADS * LANES), BF16)
    return pl.pallas_call(
        _mla_prep_kernel,
        grid=(t // tm,),
        in_specs=[
            pl.BlockSpec((tm, 2 * LANES), lambda i: (i, CB_CQ)),
            pl.BlockSpec((tm, LANES), lambda i: (i, CB_CKV)),
            pl.BlockSpec((tm, LANES), lambda i: (i, CB_KR)),
            pl.BlockSpec((1, 2 * LANES), const),
            pl.BlockSpec((1, LANES), const),
            pl.BlockSpec((2 * LANES, N_HEADS * LANES), const),
            pl.BlockSpec((LANES, N_HEADS * LANES), const),
            pl.BlockSpec((N_HEADS * DV_PAD, LANES), const),
            pl.BlockSpec((1, LANES), const),
            pl.BlockSpec((1, LANES), const),
            tab, tab, tab, tab, tab,
        ],
        out_specs=[pl.BlockSpec((tm, N_HEADS * LANES), row), pl.BlockSpec((tm, N_HEADS * LANES), row),
                   pl.BlockSpec((N_HEADS * DV_PAD, tm), lambda i: (0, i))],
        out_shape=[out, out, jax.ShapeDtypeStruct((N_HEADS * DV_PAD, t), BF16)],
        compiler_params=_params(("arbitrary",)),
        name="mla_prep",
    )(proj, proj, proj, qaw, kvw, wq, wk, wvt, qnw, knw, *tabs)


def _mla_attn_kernel(q_ref, k_ref, vt_ref, o_ref, m_s, acc_s, *, tq):
    i = pl.program_id(1)
    causal = _row((tq, tq)) <= _lane((tq, tq))
    for h in range(N_HEADS):
        m_s[h] = jnp.full((1, tq), NEG, F32)
        acc_s[h] = jnp.zeros((DV_PAD, tq), F32)

    def step(j, masked):
        rows = pl.ds(pl.multiple_of(j * tq, tq), tq)
        sts = []
        for h in range(N_HEADS):
            cols = slice(LANES * h, LANES * (h + 1))
            st = _nt(k_ref[rows, cols], q_ref[:, cols])
            sts.append(jnp.where(causal, st, NEG) if masked else st)
        vts = [vt_ref[DV_PAD * h:DV_PAD * (h + 1), _lane_tile(j, tq)] for h in range(N_HEADS)]
        _softmax_steps(sts, vts, [m_s.at[h] for h in range(N_HEADS)], [acc_s.at[h] for h in range(N_HEADS)])

    step(i, True)

    def body(jj, carry):
        step(i - jj, False)
        return carry

    lax.fori_loop(1, i + 1, body, 0)
    yt = jnp.concatenate([acc_s[h, 0:HEAD_DIM, :] / acc_s[h, HEAD_DIM:HEAD_DIM + 1, :] for h in range(N_HEADS)],
                         axis=0)
    o_ref[...] = yt.T


def _mla_attention(qm, km, vmt, b, s, tq):
    t = b * s
    nq = s // tq
    w = N_HEADS * LANES
    return pl.pallas_call(
        functools.partial(_mla_attn_kernel, tq=tq),
        grid=(b, nq),
        in_specs=[
            pl.BlockSpec((tq, w), lambda bi, i: (bi * nq + i, 0)),
            pl.BlockSpec((s, w), lambda bi, i: (bi, 0)),
            pl.BlockSpec((N_HEADS * DV_PAD, s), lambda bi, i: (0, bi)),
        ],
        out_specs=pl.BlockSpec((tq, 2 * LANES), lambda bi, i: (bi * nq + i, 0)),
        out_shape=jax.ShapeDtypeStruct((t, GROUP_WIDTH), F32),
        scratch_shapes=[pltpu.VMEM((N_HEADS, 1, tq), F32), pltpu.VMEM((N_HEADS, DV_PAD, tq), F32)],
        compiler_params=_params(("arbitrary", "arbitrary")),
        name="mla_attention",
    )(qm, km, vmt)


def _sb_kernel(q_ref, k_ref, v_ref, o_ref, kb_s, vt_s, q_s, r_s, acc_s, *, tq, s_len):
    i = pl.program_id(1)

    @pl.when(i == 0)
    def _cast_kv():
        ch = 256

        def body(t, carry):
            rows = pl.ds(pl.multiple_of(t * ch, ch), ch)
            kb_s[rows, :] = k_ref[rows, :].astype(BF16)
            vt_s[:, _lane_tile(t, ch)] = v_ref[rows, :].T.astype(BF16)
            return carry

        lax.fori_loop(0, s_len // ch, body, 0)

    key = _row((tq, tq))
    qry = _lane((tq, tq))
    strict = key < qry
    tri = jnp.where(key <= qry, 1.0, 0.0).astype(BF16)
    lane = _lane((tq, LANES))
    q = q_ref[...]
    for h in range(N_HEADS):
        mine = (lane < HEAD_DIM) if h % 2 == 0 else (lane >= HEAD_DIM)
        cols = slice(LANES * (h // 2), LANES * (h // 2 + 1))
        q_s[h] = jnp.where(mine, q[:, cols] * (HEAD_DIM ** -0.5 * LOG2E), 0.0).astype(BF16)
        r_s[h] = jnp.zeros((1, tq), F32)
        acc_s[h] = jnp.zeros((HEAD_DIM, tq), F32)

    heads = range(N_HEADS)

    def step(j, masked):
        rows = pl.ds(pl.multiple_of(j * tq, tq), tq)
        zs = [_nt(kb_s[rows, LANES * (h // 2):LANES * (h // 2 + 1)], q_s[h]) for h in heads]
        r_old = [r_s[h] for h in heads]
        acc_old = [acc_s[h] for h in heads]
        lgs = [-(jnp.maximum(z, 0.0) + jnp.log2(1.0 + jnp.exp2(-jnp.abs(z)))) for z in zs]
        if masked:
            lgs = [jnp.where(strict, lg, 0.0) for lg in lgs]
        his = [lg.astype(BF16) for lg in lgs]
        los = [(lg - hi.astype(F32)).astype(BF16) for lg, hi in zip(lgs, his)]
        csums = [_dot(tri, hi) + _dot(tri, lo) + r for hi, lo, r in zip(his, los, r_old)]
        als = [jnp.exp2(z + cs) for z, cs in zip(zs, csums)]
        if masked:
            als = [jnp.where(strict, a, 0.0) for a in als]
        acc_new = [acc + _dot(vt_s[HEAD_DIM * h:HEAD_DIM * (h + 1), _lane_tile(j, tq)], als[h].astype(BF16))
                   for h, acc in zip(heads, acc_old)]
        for h in heads:
            acc_s[h] = acc_new[h]
            r_s[h] = csums[h][0:1, :]

    step(i, True)

    def body(jj, carry):
        step(i - jj, False)
        return carry

    lax.fori_loop(1, i + 1, body, 0)
    o_ref[...] = jnp.concatenate([acc_s[h] for h in range(N_HEADS)], axis=0).T


def _sb_attention(proj, b, s, tq):
    t = b * s
    nq = s // tq
    w = 2 * LANES
    return pl.pallas_call(
        functools.partial(_sb_kernel, tq=tq, s_len=s),
        grid=(b, nq),
        in_specs=[
            pl.BlockSpec((tq, w), lambda bi, i: (bi * nq + i, CB_SQ)),
            pl.BlockSpec((s, w), lambda bi, i: (bi, CB_SK)),
            pl.BlockSpec((s, w), lambda bi, i: (bi, CB_SV)),
        ],
        out_specs=pl.BlockSpec((tq, w), lambda bi, i: (bi * nq + i, 0)),
        out_shape=jax.ShapeDtypeStruct((t, GROUP_WIDTH), F32),
        scratch_shapes=[pltpu.VMEM((s, w), BF16), pltpu.VMEM((w, s), BF16),
                        pltpu.VMEM((N_HEADS, tq, LANES), BF16),
                        pltpu.VMEM((N_HEADS, 1, tq), F32), pltpu.VMEM((N_HEADS, HEAD_DIM, tq), F32)],
        compiler_params=_params(("arbitrary", "arbitrary")),
        name="sb_attention",
    )(proj, proj, proj)


def _post_kernel(a_ref, ap_ref, yb_ref, yc_ref, yd_ref, x_ref, cw_ref, cb_ref, onw_ref, wo_ref, n2w_ref,
                 w1_ref, w2_ref, o_ref, *, tm, s_len, ffc):
    i = pl.program_id(0)
    a = a_ref[...]
    gw = GROUP_WIDTH
    v = a[:, gw:2 * gw] * a[:, 2 * gw:3 * gw]
    ap = ap_ref[...]
    first = (i * tm) % s_len == 0
    vp = jnp.where(first, 0.0, ap[:, gw:2 * gw] * ap[:, 2 * gw:3 * gw])
    row = _row(v.shape)
    v1 = jnp.where(row == 0, vp[7:8, :], pltpu.roll(v, 1, 0))
    v2 = jnp.where(row == 0, vp[6:7, :], jnp.where(row == 1, vp[7:8, :], pltpu.roll(v, 2, 0)))
    conv = cw_ref[0:1, :] * v2 + cw_ref[1:2, :] * v1 + cw_ref[2:3, :] * v
    ya = a[:, 0:gw] * (conv + cb_ref[...])

    mix = None
    for g, y in enumerate((ya, yb_ref[...], yc_ref[...], yd_ref[...])):
        ms = jnp.mean(y * y, axis=-1, keepdims=True)
        yn = (y * lax.rsqrt(ms + EPS) * onw_ref[:, gw * g:gw * (g + 1)]).astype(BF16)
        part = _dot(yn, wo_ref[gw * g:gw * (g + 1), :])
        mix = part if mix is None else mix + part
    x1 = x_ref[...] + mix

    ms = jnp.mean(x1 * x1, axis=-1, keepdims=True)
    h2 = (x1 * lax.rsqrt(ms + EPS) * n2w_ref[...]).astype(BF16)
    ff = None
    for cidx in range(D_FF // ffc):
        u = _dot(h2, w1_ref[:, ffc * cidx:ffc * (cidx + 1)])
        u = jnp.square(jnp.maximum(u, 0.0)).astype(BF16)
        part = _dot(u, w2_ref[ffc * cidx:ffc * (cidx + 1), :])
        ff = part if ff is None else ff + part
    o_ref[...] = x1 + ff


def _post(proj, yb, yc, yd, x2d, cw, cb, onw, wo, n2w, w1, w2, s, tm):
    t = x2d.shape[0]
    gw = GROUP_WIDTH
    row = lambda i: (i, 0)
    const = lambda i: (0, 0)
    once = pl.Buffered(1)
    kern = functools.partial(_post_kernel, tm=tm, s_len=s, ffc=1024)
    return pl.pallas_call(
        kern,
        grid=(t // tm,),
        in_specs=[
            pl.BlockSpec((tm, 3 * gw), row),
            pl.BlockSpec((8, 3 * gw), lambda i: (jnp.maximum(i * (tm // 8) - 1, 0), 0)),
            pl.BlockSpec((tm, gw), row),
            pl.BlockSpec((tm, gw), row),
            pl.BlockSpec((tm, gw), row),
            pl.BlockSpec((tm, D_MODEL), row),
            pl.BlockSpec((3, gw), const),
            pl.BlockSpec((1, gw), const),
            pl.BlockSpec((1, D_MODEL), const),
            pl.BlockSpec((D_MODEL, D_MODEL), const, pipeline_mode=once),
            pl.BlockSpec((1, D_MODEL), const),
            pl.BlockSpec((D_MODEL, D_FF), const, pipeline_mode=once),
            pl.BlockSpec((D_FF, D_MODEL), const, pipeline_mode=once),
        ],
        out_specs=pl.BlockSpec((tm, D_MODEL), row),
        out_shape=jax.ShapeDtypeStruct((t, D_MODEL), F32),
        compiler_params=_params(("arbitrary",)),
        name="post",
    )(proj, proj, yb, yc, yd, x2d, cw, cb, onw, wo, n2w, w1, w2)


def _t5_bucket(dist):
    max_exact = N_BUCKETS // 2
    d = jnp.maximum(dist, 0)
    large = max_exact + (jnp.log(jnp.maximum(d, 1).astype(F32) / max_exact)
                         / math.log(MAX_DISTANCE / max_exact) * (N_BUCKETS - max_exact)).astype(jnp.int32)
    large = jnp.minimum(large, N_BUCKETS - 1)
    return jnp.where(d < max_exact, d, large)


def _tables(s, tq_nsa):
    n_cmp = (s - CMP_LEN) // CMP_STRIDE + 1
    ng = s // CMP_STRIDE
    n_slc = s // SLC_LEN
    tpos = jnp.arange(s)[None, :]
    n = jnp.arange(ng)[:, None]
    dist_c = tpos - (n * CMP_STRIDE + CMP_LEN - 1)
    bidxct = jnp.where((dist_c >= 0) & (n < n_cmp), _t5_bucket(dist_c), -1).astype(jnp.int32)
    key = jnp.arange(tq_nsa)[:, None]
    qry = jnp.arange(tq_nsa)[None, :]
    bidx2t = jnp.stack([_t5_bucket(qry - key), _t5_bucket(tq_nsa + qry - key)]).astype(jnp.int32)
    starts = np.arange(n_cmp) * CMP_STRIDE
    ends = starts + CMP_LEN
    s0 = np.arange(n_slc) * SLC_LEN
    s1 = s0 + SLC_LEN
    ovl = np.clip(np.minimum(ends[:, None], s1[None]) - np.maximum(starts[:, None], s0[None]), 0, None) / CMP_LEN
    ovt = np.zeros((LANES, ng), np.float32)
    ovt[:n_slc, :n_cmp] = ovl.T
    key_blk = (np.arange(s) // SLC_LEN).reshape(s // tq_nsa, tq_nsa, 1)
    emt = (np.arange(LANES).reshape(1, 1, LANES) == key_blk).astype(np.float32)
    inv = 1.0 / (ROPE_THETA ** (jnp.arange(0, ROPE_DIM, 2, dtype=F32) / ROPE_DIM))
    ang = jnp.arange(s, dtype=F32)[:, None] * inv[None, :]
    cos, sin = jnp.cos(ang), jnp.sin(ang)
    z16 = jnp.zeros((s, 16), F32)
    z32 = jnp.zeros((s, 32), F32)
    z64 = jnp.zeros((s, 64), F32)
    one64 = jnp.ones((s, 64), F32)
    cq_t = jnp.concatenate([one64, cos, cos, z32], axis=1)
    s1_t = jnp.concatenate([z64, -sin, z16, z32], axis=1)
    s2_t = jnp.concatenate([z64, z16, sin, z32], axis=1)
    ck_t = jnp.concatenate([z64, cos, cos, z32], axis=1)
    sk_t = jnp.concatenate([z64, -sin, sin, z32], axis=1)
    return dict(bidxct=bidxct, bidx2t=bidx2t, ovt=jnp.asarray(ovt, BF16), emt=jnp.asarray(emt, BF16),
                rope=(cq_t, s1_t, s2_t, ck_t, sk_t))


def _pad_cols(w, width):
    return jnp.pad(w, ((0, 0), (0, width - w.shape[1])))


def _layer_weights(l, w_in, conv_w, conv_b, nsa_q_norm, nsa_k_norm, cmp_pos, cmp_w1, cmp_w2, mla_q_a_norm,
                   mla_kv_norm, mla_wq_b, mla_wkv_b, mla_q_norm, mla_k_norm, out_norm_w, w_out, norm2_w,
                   ffn_w1, ffn_w2):
    wi = w_in[l]
    kr = wi[:, 1740:1772]
    z32 = jnp.zeros((D_MODEL, 32), F32)
    w_in_p = jnp.concatenate([
        wi[:, 0:1408],
        _pad_cols(wi[:, 1408:1420], LANES),
        _pad_cols(wi[:, 1420:1612], 2 * LANES),
        wi[:, 1612:1740],
        kr[:, 16:32], kr[:, 0:16], z32, kr, z32,
        wi[:, 1772:2540],
    ], axis=1).astype(BF16)
    w1 = cmp_w1[l].reshape(2, CMP_LEN, HEAD_DIM, CMP_HIDDEN)
    zw = jnp.zeros((CMP_LEN, HEAD_DIM, CMP_HIDDEN), F32)
    cw1 = jnp.concatenate([jnp.concatenate([w1[0], zw], axis=2), jnp.concatenate([zw, w1[1]], axis=2)],
                          axis=1).astype(BF16)
    zc = jnp.zeros((CMP_HIDDEN, HEAD_DIM), F32)
    cw2 = jnp.concatenate([jnp.concatenate([cmp_w2[l, 0], zc], axis=1),
                           jnp.concatenate([zc, cmp_w2[l, 1]], axis=1)], axis=0).astype(BF16)
    cpos = jnp.concatenate([cmp_pos[l, 0], cmp_pos[l, 1]], axis=1)
    kn = nsa_k_norm[l]
    ones64 = jnp.ones((HEAD_DIM,), F32)
    knw_c = jnp.concatenate([kn[0], ones64])[None, :]
    knw_sw = jnp.stack([jnp.concatenate([kn[1], ones64]), jnp.concatenate([kn[2], ones64])])
    qnw = jnp.concatenate([nsa_q_norm[l], nsa_q_norm[l]])[None, :]
    wq = mla_wq_b[l].reshape(Q_LORA, N_HEADS, QK_DIM)
    wq = jnp.pad(wq, ((0, 2 * LANES - Q_LORA), (0, 0), (0, LANES - QK_DIM))).reshape(2 * LANES, N_HEADS * LANES)
    wkv = mla_wkv_b[l].reshape(KV_LORA, N_HEADS, 2 * HEAD_DIM)
    wk = jnp.pad(wkv[:, :, :HEAD_DIM], ((0, 0), (0, 0), (0, LANES - HEAD_DIM))).reshape(KV_LORA, N_HEADS * LANES)
    wvt = jnp.pad(wkv[:, :, HEAD_DIM:], ((0, 0), (0, 0), (0, DV_PAD - HEAD_DIM))).reshape(KV_LORA, -1).T
    return dict(
        w_in=w_in_p, cw1=cw1, cw2=cw2, cw2t=cw2.T, cpos=cpos, knw_c=knw_c, knw_sw=knw_sw, qnw=qnw,
        qaw=_pad_cols(mla_q_a_norm[l][None, :], 2 * LANES), kvw=mla_kv_norm[l][None, :],
        wq=wq.astype(BF16), wk=wk.astype(BF16), wvt=wvt.astype(BF16),
        mqn=_pad_cols(mla_q_norm[l][None, :], LANES), mkn=_pad_cols(mla_k_norm[l][None, :], LANES),
        cw=conv_w[l], cb=conv_b[l][None, :], onw=out_norm_w[l][None, :], wo=w_out[l].astype(BF16),
        n2w=norm2_w[l][None, :], w1=ffn_w1[l].astype(BF16), w2=ffn_w2[l].astype(BF16))


TM_PROJ = 512
TM_PREP = 512
TM_POST = 512
TQ_NSA = 256
TQ_MLA = 256
TQ_SB = 256


def kernel(x, rel_bias, norm1_w, w_in, conv_w, conv_b, nsa_q_norm, nsa_k_norm, cmp_pos, cmp_w1, cmp_w2,
           mla_q_a_norm, mla_kv_norm, mla_wq_b, mla_wkv_b, mla_q_norm, mla_k_norm, out_norm_w, w_out, norm2_w,
           ffn_w1, ffn_w2):
    b, s, d = x.shape
    depth = w_in.shape[0]
    tabs = _tables(s, TQ_NSA)
    x2d = x.reshape(b * s, d)
    for l in range(depth):
        w = _layer_weights(l, w_in, conv_w, conv_b, nsa_q_norm, nsa_k_norm, cmp_pos, cmp_w1, cmp_w2,
                           mla_q_a_norm, mla_kv_norm, mla_wq_b, mla_wkv_b, mla_q_norm, mla_k_norm, out_norm_w,
                           w_out, norm2_w, ffn_w1, ffn_w2)
        proj = _inproj(x2d, norm1_w[l][None, :], w["w_in"], TM_PROJ)
        kcvc, kcvct = _compress(proj, w["cpos"], w["cw1"], w["cw2"], w["cw2t"], w["knw_c"], b, s)
        yb = _nsa_attention(proj, kcvc, kcvct, rel_bias, tabs["bidxct"], tabs["bidx2t"], tabs["ovt"], tabs["emt"],
                            w["qnw"], w["knw_sw"], b, s, TQ_NSA)
        qm, km, vmt = _mla_prep(proj, w["qaw"], w["kvw"], w["wq"], w["wk"], w["wvt"], w["mqn"], w["mkn"],
                                tabs["rope"], s, TM_PREP)
        yc = _mla_attention(qm, km, vmt, b, s, TQ_MLA)
        yd = _sb_attention(proj, b, s, TQ_SB)
        x2d = _post(proj, yb, yc, yd, x2d, w["cw"], w["cb"], w["onw"], w["wo"], w["n2w"], w["w1"], w["w2"],
                    s, TM_POST)
    return x2d.reshape(b, s, d)
```

```python
import functools
import math

import jax
import jax.numpy as jnp
import numpy as np
from jax import lax
from jax.experimental import pallas as pl
from jax.experimental.pallas import tpu as pltpu

F32 = jnp.float32
BF16 = jnp.bfloat16

D_MODEL = 1024
GROUP_WIDTH = 256
HEAD_DIM = 64
N_HEADS = 4
LANES = 128
CMP_LEN = 32
CMP_STRIDE = 16
SLC_LEN = 64
N_SEL = 16
WINDOW = 512
CMP_HIDDEN = 256
Q_LORA = 192
KV_LORA = 128
ROPE_DIM = 32
QK_DIM = 96
ROPE_THETA = 10000.0
N_BUCKETS = 32
MAX_DISTANCE = 128
D_FF = 4096
EPS = 1e-6
NEG = -1e30
LOG2E = math.log2(math.e)
DV_PAD = 80

NP = 2816
CB_NQ = 3
CB_KCVC = 8
CB_KSVS = 9
CB_KWVW = 10
CB_GATE = 11
CB_CQ = 6
CB_CKV = 14
CB_KR = 15
CB_SQ = 8
CB_SK = 9
CB_SV = 10

VMEM_LIMIT = 56 * 1024 * 1024

NT_DIMS = (((1,), (1,)), ((), ()))


def _params(sem):
    return pltpu.CompilerParams(dimension_semantics=sem, vmem_limit_bytes=VMEM_LIMIT)


def _nt(a, b):
    return lax.dot_general(a, b, NT_DIMS, preferred_element_type=F32)


def _dot(a, b):
    return jnp.dot(a, b, preferred_element_type=F32)


def _lane(shape):
    return lax.broadcasted_iota(jnp.int32, shape, len(shape) - 1)


def _row(shape):
    return lax.broadcasted_iota(jnp.int32, shape, len(shape) - 2)


def _lane_tile(j, width):
    return pl.ds(pl.multiple_of(j * width, width), width)


def _inproj_kernel(x_ref, nw_ref, w_ref, o_ref):
    x = x_ref[...]
    ms = jnp.mean(x * x, axis=-1, keepdims=True)
    h = (x * lax.rsqrt(ms + EPS) * nw_ref[...]).astype(BF16)
    o_ref[...] = _dot(h, w_ref[...])


def _inproj(x2d, nw, w, tm):
    t = x2d.shape[0]
    return pl.pallas_call(
        _inproj_kernel,
        grid=(t // tm,),
        in_specs=[
            pl.BlockSpec((tm, D_MODEL), lambda i: (i, 0)),
            pl.BlockSpec((1, D_MODEL), lambda i: (0, 0)),
            pl.BlockSpec((D_MODEL, NP), lambda i: (0, 0), pipeline_mode=pl.Buffered(1)),
        ],
        out_specs=pl.BlockSpec((tm, NP), lambda i: (i, 0)),
        out_shape=jax.ShapeDtypeStruct((t, NP), F32),
        compiler_params=_params(("arbitrary",)),
        name="inproj",
    )(x2d, nw, w)


def _compress_kernel(x_ref, pos_ref, w1_ref, w2_ref, w2t_ref, knw_ref, o_ref, ot_ref):
    ng = x_ref.shape[1]
    acc_a = jnp.zeros((ng, 2 * CMP_HIDDEN), F32)
    acc_b = jnp.zeros((ng, 2 * CMP_HIDDEN), F32)
    for i in range(CMP_STRIDE):
        x = x_ref[0, :, i, :]
        xa = (x + pos_ref[i:i + 1, :]).astype(BF16)
        xb = (x + pos_ref[CMP_STRIDE + i:CMP_STRIDE + i + 1, :]).astype(BF16)
        acc_a += _dot(xa, w1_ref[i])
        acc_b += _dot(xb, w1_ref[CMP_STRIDE + i])
    pre = acc_a + pltpu.roll(acc_b, ng - 1, 0)
    hdn = (pre * jax.nn.sigmoid(pre)).astype(BF16)
    out = _dot(hdn, w2_ref[...])
    lane = _lane(out.shape)
    is_k = lane < HEAD_DIM
    ss = jnp.sum(jnp.where(is_k, out * out, 0.0), axis=-1, keepdims=True) * (1.0 / HEAD_DIM)
    o_ref[0] = jnp.where(is_k, out * lax.rsqrt(ss + EPS) * knw_ref[...], out)
    ot_ref[0] = _nt(w2t_ref[...], hdn)


def _compress(proj, pos, w1, w2, w2t, knw, b, s):
    ng = s // CMP_STRIDE
    x4 = proj.reshape(b, ng, CMP_STRIDE, NP)
    return pl.pallas_call(
        _compress_kernel,
        grid=(b,),
        in_specs=[
            pl.BlockSpec((1, ng, CMP_STRIDE, LANES), lambda i: (i, 0, 0, CB_KCVC)),
            pl.BlockSpec((CMP_LEN, LANES), lambda i: (0, 0)),
            pl.BlockSpec((CMP_LEN, LANES, 2 * CMP_HIDDEN), lambda i: (0, 0, 0)),
            pl.BlockSpec((2 * CMP_HIDDEN, LANES), lambda i: (0, 0)),
            pl.BlockSpec((LANES, 2 * CMP_HIDDEN), lambda i: (0, 0)),
            pl.BlockSpec((1, LANES), lambda i: (0, 0)),
        ],
        out_specs=[pl.BlockSpec((1, ng, LANES), lambda i: (i, 0, 0)),
                   pl.BlockSpec((1, LANES, ng), lambda i: (i, 0, 0))],
        out_shape=[jax.ShapeDtypeStruct((b, ng, LANES), F32), jax.ShapeDtypeStruct((b, LANES, ng), F32)],
        compiler_params=_params(("arbitrary",)),
        name="nsa_compress",
    )(x4, pos, w1, w2, w2t, knw)


def _softmax_steps(sts, vts, m_refs, acc_refs):
    m_old = [r[...] for r in m_refs]
    acc_old = [r[...] for r in acc_refs]
    m_new = [jnp.maximum(m, jnp.max(st, axis=0, keepdims=True)) for m, st in zip(m_old, sts)]
    ps = [jnp.exp2(st - m).astype(BF16) for st, m in zip(sts, m_new)]
    alphas = [jnp.exp2(mo - mn) for mo, mn in zip(m_old, m_new)]
    acc_new = [al * acc + _dot(vt, p) for al, acc, vt, p in zip(alphas, acc_old, vts, ps)]
    for r, v in zip(m_refs, m_new):
        r[...] = v
    for r, v in zip(acc_refs, acc_new):
        r[...] = v


def _with_ones_row(vt):
    pad = jnp.where(_row((DV_PAD - HEAD_DIM, vt.shape[1])) == 0, 1.0, 0.0).astype(vt.dtype)
    return jnp.concatenate([vt, pad], axis=0)


def _bucket_bias(bidx, relb_ref, h, fill):
    acc = jnp.full(bidx.shape, fill, F32)
    for bk in range(N_BUCKETS):
        acc = jnp.where(bidx == bk, relb_ref[bk, h] * LOG2E, acc)
    return acc


def _dup_low_half(x):
    y = jnp.where(_lane(x.shape) < HEAD_DIM, x, 0.0)
    return y + pltpu.roll(y, HEAD_DIM, 1)


def _nsa_kernel(relb_ref, q_ref, g_ref, kcvc_ref, kcvct_ref, ksvs_ref, kwvw_ref, bidxct_ref, bidx2t_ref, ovt_ref,
                emt_ref, qnw_ref, knw_ref, o_ref,
                biasc_s, bias2_s, ks_s, vst_s, kw_s, vwt_s, kc_s, vct_s, q_s, m_s, acc_s, *, tq, s_len):
    b = pl.program_id(0)
    i = pl.program_id(1)
    n_win = WINDOW // tq
    ng = s_len // CMP_STRIDE

    @pl.when((b == 0) & (i == 0))
    def _build_bias_tables():
        key = _row((tq, tq))
        qry = _lane((tq, tq))
        for h in range(N_HEADS):
            cols = slice(h * tq, (h + 1) * tq)
            far = jnp.full((tq, tq), relb_ref[N_BUCKETS - 1, h] * LOG2E, F32)
            bias2_s[0, :, cols] = jnp.where(key <= qry, _bucket_bias(bidx2t_ref[0], relb_ref, h, NEG), NEG)
            bias2_s[1, :, cols] = _bucket_bias(bidx2t_ref[1], relb_ref, h, NEG)
            bias2_s[2, :, cols] = far
            bias2_s[3, :, cols] = jnp.where(key > qry, far, NEG)

        def body(t, carry):
            bi = bidxct_ref[:, _lane_tile(t, tq)]
            for h in range(N_HEADS):
                biasc_s[t, :, h * tq:(h + 1) * tq] = _bucket_bias(bi, relb_ref, h, NEG)
            return carry

        lax.fori_loop(0, s_len // tq, body, 0)

    @pl.when(i == 0)
    def _prep_kv():
        ch = 256

        def body(t, carry):
            rows = pl.ds(pl.multiple_of(t * ch, ch), ch)
            for src, kdst, vdst, widx in ((ksvs_ref, ks_s, vst_s, 0), (kwvw_ref, kw_s, vwt_s, 1)):
                x = src[rows, :]
                is_k = _lane(x.shape) < HEAD_DIM
                ss = jnp.sum(jnp.where(is_k, x * x, 0.0), axis=-1, keepdims=True) * (1.0 / HEAD_DIM)
                kn = x * lax.rsqrt(ss + EPS) * knw_ref[widx:widx + 1, :]
                kdst[rows, :] = _dup_low_half(kn).astype(BF16)
                vdst[:, _lane_tile(t, ch)] = _with_ones_row(x.T[HEAD_DIM:, :]).astype(BF16)
            return carry

        lax.fori_loop(0, s_len // ch, body, 0)
        kc_s[...] = _dup_low_half(kcvc_ref[0]).astype(BF16)
        vct_s[...] = kcvct_ref[0][HEAD_DIM:, :].astype(BF16)

    q = q_ref[...]
    lane = _lane((tq, LANES))
    heads = range(N_HEADS)
    for h in heads:
        x = q[:, LANES * (h // 2):LANES * (h // 2 + 1)]
        mine = (lane < HEAD_DIM) if h % 2 == 0 else (lane >= HEAD_DIM)
        xm = jnp.where(mine, x, 0.0)
        ss = jnp.sum(xm * xm, axis=-1, keepdims=True) * (1.0 / HEAD_DIM)
        q_s[h] = (xm * lax.rsqrt(ss + EPS) * qnw_ref[...] * (HEAD_DIM ** -0.5 * LOG2E)).astype(BF16)

    o_cmp = []
    psum = None
    for h in heads:
        lc = _nt(kc_s[...], q_s[h]) + biasc_s[i, :, h * tq:(h + 1) * tq]
        mc = jnp.max(lc, axis=0, keepdims=True)
        pc = jnp.where(lc > 0.5 * NEG, jnp.exp2(lc - mc), 0.0)
        den = jnp.sum(pc, axis=0, keepdims=True)
        pc = pc / jnp.where(den > 0.0, den, 1.0)
        o_cmp.append(_dot(vct_s[...], pc.astype(BF16)))
        psum = pc if psum is None else psum + pc
    p_hi = psum.astype(BF16)
    p_lo = (psum - p_hi.astype(F32)).astype(BF16)
    score = _dot(ovt_ref[...], p_hi) + _dot(ovt_ref[...], p_lo)

    n_slc = s_len // SLC_LEN
    blk = _row((n_slc, tq))
    tpos = i * tq + _lane((n_slc, tq))
    tblk = tpos // SLC_LEN
    valid = blk * SLC_LEN <= tpos
    forced = (blk == 0) | (blk == tblk) | (blk == tblk - 1)
    sc = jnp.where(forced, jnp.inf, jnp.where(valid, score[0:n_slc], -jnp.inf))
    rank = jnp.zeros((n_slc, tq), F32)
    for k in range(n_slc):
        ck = sc[k:k + 1, :]
        beats = (ck > sc) | ((ck == sc) & (blk > k))
        rank += jnp.where(beats, 1.0, 0.0)
    pen = jnp.where((rank < float(min(N_SEL, n_slc))) & valid, 0.0, NEG)
    pen = jnp.concatenate([pen, jnp.zeros((LANES - n_slc, tq), F32)], axis=0).astype(BF16)

    for c in range(2 * N_HEADS):
        m_s[c] = jnp.full((1, tq), NEG, F32)
        acc_s[c] = jnp.zeros((DV_PAD, tq), F32)

    def step(jj, with_win):
        j = i - jj
        rows = pl.ds(pl.multiple_of(j * tq, tq), tq)
        cols = _lane_tile(j, tq)
        static = isinstance(jj, int)
        kind = min(jj, 2) if static else jnp.minimum(jj, 2)
        ks = ks_s[rows, :]
        masked = _dot(emt_ref[j], pen)
        sts = [_nt(ks, q_s[h]) + masked + bias2_s[kind, :, h * tq:(h + 1) * tq] for h in heads]
        vts = [vst_s[:, cols]] * N_HEADS
        chains = list(heads)
        if with_win:
            kw = kw_s[rows, :]
            kind_w = 3 if jj == n_win else kind
            sts += [_nt(kw, q_s[h]) + bias2_s[kind_w, :, h * tq:(h + 1) * tq] for h in heads]
            vts += [vwt_s[:, cols]] * N_HEADS
            chains += [N_HEADS + h for h in heads]
        _softmax_steps(sts, vts, [m_s.at[c] for c in chains], [acc_s.at[c] for c in chains])

    step(0, True)
    for jj in range(1, n_win + 1):
        pl.when(i >= jj)(functools.partial(step, jj, True))

    def sel_body(jj, carry):
        step(jj, False)
        return carry

    lax.fori_loop(n_win + 1, i + 1, sel_body, 0)

    gt = jax.nn.sigmoid(g_ref[...]).T
    ys = []
    for h in heads:
        o_slc = acc_s[h, 0:HEAD_DIM, :] / acc_s[h, HEAD_DIM:HEAD_DIM + 1, :]
        o_win = acc_s[N_HEADS + h, 0:HEAD_DIM, :] / acc_s[N_HEADS + h, HEAD_DIM:HEAD_DIM + 1, :]
        ys.append(gt[3 * h:3 * h + 1, :] * o_cmp[h] + gt[3 * h + 1:3 * h + 2, :] * o_slc
                  + gt[3 * h + 2:3 * h + 3, :] * o_win)
    o_ref[...] = jnp.concatenate(ys, axis=0).T


def _nsa_attention(proj, kcvc, kcvct, rel_bias, bidxct, bidx2t, ovt, emt, qnw, knw, b, s, tq):
    t = b * s
    nq = s // tq
    m_rows = N_HEADS * tq
    ng = s // CMP_STRIDE
    kern = functools.partial(_nsa_kernel, tq=tq, s_len=s)
    return pl.pallas_call(
        kern,
        grid=(b, nq),
        in_specs=[
            pl.BlockSpec(memory_space=pltpu.SMEM),
            pl.BlockSpec((tq, 2 * LANES), lambda bi, i: (bi * nq + i, CB_NQ)),
            pl.BlockSpec((tq, LANES), lambda bi, i: (bi * nq + i, CB_GATE)),
            pl.BlockSpec((1, ng, LANES), lambda bi, i: (bi, 0, 0)),
            pl.BlockSpec((1, LANES, ng), lambda bi, i: (bi, 0, 0)),
            pl.BlockSpec((s, LANES), lambda bi, i: (bi, CB_KSVS)),
            pl.BlockSpec((s, LANES), lambda bi, i: (bi, CB_KWVW)),
            pl.BlockSpec((ng, s), lambda bi, i: (0, 0)),
            pl.BlockSpec((2, tq, tq), lambda bi, i: (0, 0, 0)),
            pl.BlockSpec((LANES, ng), lambda bi, i: (0, 0)),
            pl.BlockSpec((nq, tq, LANES), lambda bi, i: (0, 0, 0)),
            pl.BlockSpec((1, LANES), lambda bi, i: (0, 0)),
            pl.BlockSpec((2, LANES), lambda bi, i: (0, 0)),
        ],
        out_specs=pl.BlockSpec((tq, 2 * LANES), lambda bi, i: (bi * nq + i, 0)),
        out_shape=jax.ShapeDtypeStruct((t, GROUP_WIDTH), F32),
        scratch_shapes=[
            pltpu.VMEM((nq, ng, m_rows), F32),
            pltpu.VMEM((4, tq, m_rows), F32),
            pltpu.VMEM((s, LANES), BF16),
            pltpu.VMEM((DV_PAD, s), BF16),
            pltpu.VMEM((s, LANES), BF16),
            pltpu.VMEM((DV_PAD, s), BF16),
            pltpu.VMEM((ng, LANES), BF16),
            pltpu.VMEM((HEAD_DIM, ng), BF16),
            pltpu.VMEM((N_HEADS, tq, LANES), BF16),
            pltpu.VMEM((2 * N_HEADS, 1, tq), F32),
            pltpu.VMEM((2 * N_HEADS, DV_PAD, tq), F32),
        ],
        compiler_params=_params(("arbitrary", "arbitrary")),
        name="nsa_attention",
    )(rel_bias, proj, proj, kcvc, kcvct, proj, proj, bidxct, bidx2t, ovt, emt, qnw, knw)


def _mla_prep_kernel(cq_ref, ckv_ref, kr_ref, qaw_ref, kvw_ref, wq_ref, wk_ref, wvt_ref, qnw_ref, knw_ref,
                     cq_t_ref, s1_t_ref, s2_t_ref, ck_t_ref, sk_t_ref, qo_ref, ko_ref, vto_ref):
    cq = cq_ref[...]
    ms = jnp.sum(cq * cq, axis=-1, keepdims=True) * (1.0 / Q_LORA)
    hq = (cq * lax.rsqrt(ms + EPS) * qaw_ref[...]).astype(BF16)
    qf = _dot(hq, wq_ref[...])
    ckv = ckv_ref[...]
    ms = jnp.mean(ckv * ckv, axis=-1, keepdims=True)
    hkv = (ckv * lax.rsqrt(ms + EPS) * kvw_ref[...]).astype(BF16)
    kf = _dot(hkv, wk_ref[...])
    vt = _nt(wvt_ref[...], hkv)
    vto_ref[...] = jnp.where(_row(vt.shape) % DV_PAD == HEAD_DIM, 1.0, vt).astype(BF16)
    krb = kr_ref[...]
    kr_rot = krb * ck_t_ref[...] + pltpu.roll(krb, HEAD_DIM, 1) * sk_t_ref[...]
    scale = QK_DIM ** -0.5 * LOG2E
    for h in range(N_HEADS):
        cols = slice(LANES * h, LANES * (h + 1))
        x = qf[:, cols]
        x = x * cq_t_ref[...] + pltpu.roll(x, LANES - 16, 1) * s1_t_ref[...] + pltpu.roll(x, 16, 1) * s2_t_ref[...]
        ss = jnp.sum(x * x, axis=-1, keepdims=True) * (1.0 / QK_DIM)
        qo_ref[:, cols] = (x * lax.rsqrt(ss + EPS) * qnw_ref[...] * scale).astype(BF16)
        k = kf[:, cols] + kr_rot
        ss = jnp.sum(k * k, axis=-1, keepdims=True) * (1.0 / QK_DIM)
        ko_ref[:, cols] = (k * lax.rsqrt(ss + EPS) * knw_ref[...]).astype(BF16)


def _mla_prep(proj, qaw, kvw, wq, wk, wvt, qnw, knw, tabs, s, tm):
    t = proj.shape[0]
    npos = s // tm
    row = lambda i: (i, 0)
    const = lambda i: (0, 0)
    tab = pl.BlockSpec((tm, LANES), lambda i: (i % npos, 0))
    out = jax.ShapeDtypeStruct((t, N_HEADS * LANES), BF16)
    return pl.pallas_call(
        _mla_prep_kernel,
        grid=(t // tm,),
        in_specs=[
            pl.BlockSpec((tm, 2 * LANES), lambda i: (i, CB_CQ)),
            pl.BlockSpec((tm, LANES), lambda i: (i, CB_CKV)),
            pl.BlockSpec((tm, LANES), lambda i: (i, CB_KR)),
            pl.BlockSpec((1, 2 * LANES), const),
            pl.BlockSpec((1, LANES), const),
            pl.BlockSpec((2 * LANES, N_HEADS * LANES), const),
            pl.BlockSpec((LANES, N_HEADS * LANES), const),
            pl.BlockSpec((N_HEADS * DV_PAD, LANES), const),
            pl.BlockSpec((1, LANES), const),
            pl.BlockSpec((1, LANES), const),
            tab, tab, tab, tab, tab,
        ],
        out_specs=[pl.BlockSpec((tm, N_HEADS * LANES), row), pl.BlockSpec((tm, N_HEADS * LANES), row),
                   pl.BlockSpec((N_HEADS * DV_PAD, tm), lambda i: (0, i))],
        out_shape=[out, out, jax.ShapeDtypeStruct((N_HEADS * DV_PAD, t), BF16)],
        compiler_params=_params(("arbitrary",)),
        name="mla_prep",
    )(proj, proj, proj, qaw, kvw, wq, wk, wvt, qnw, knw, *tabs)


def _mla_attn_kernel(q_ref, k_ref, vt_ref, o_ref, m_s, acc_s, *, tq):
    i = pl.program_id(1)
    causal = _row((tq, tq)) <= _lane((tq, tq))
    for h in range(N_HEADS):
        m_s[h] = jnp.full((1, tq), NEG, F32)
        acc_s[h] = jnp.zeros((DV_PAD, tq), F32)

    def step(j, width, masked):
        tk = width * tq
        rows = pl.ds(j * tq if isinstance(j, int) else pl.multiple_of(j * tq, tq), tk)
        sts = []
        for h in range(N_HEADS):
            cols = slice(LANES * h, LANES * (h + 1))
            st = _nt(k_ref[rows, cols], q_ref[:, cols])
            sts.append(jnp.where(causal, st, NEG) if masked else st)
        vts = [vt_ref[DV_PAD * h:DV_PAD * (h + 1), rows] for h in range(N_HEADS)]
        _softmax_steps(sts, vts, [m_s.at[h] for h in range(N_HEADS)], [acc_s.at[h] for h in range(N_HEADS)])

    step(i, 1, True)

    def body(jj, carry):
        step(i - 2 * jj, 2, False)
        return carry

    lax.fori_loop(1, i // 2 + 1, body, 0)
    pl.when(i % 2 == 1)(functools.partial(step, 0, 1, False))
    yt = jnp.concatenate([acc_s[h, 0:HEAD_DIM, :] / acc_s[h, HEAD_DIM:HEAD_DIM + 1, :] for h in range(N_HEADS)],
                         axis=0)
    o_ref[...] = yt.T


def _mla_attention(qm, km, vmt, b, s, tq):
    t = b * s
    nq = s // tq
    w = N_HEADS * LANES
    return pl.pallas_call(
        functools.partial(_mla_attn_kernel, tq=tq),
        grid=(b, nq),
        in_specs=[
            pl.BlockSpec((tq, w), lambda bi, i: (bi * nq + i, 0)),
            pl.BlockSpec((s, w), lambda bi, i: (bi, 0)),
            pl.BlockSpec((N_HEADS * DV_PAD, s), lambda bi, i: (0, bi)),
        ],
        out_specs=pl.BlockSpec((tq, 2 * LANES), lambda bi, i: (bi * nq + i, 0)),
        out_shape=jax.ShapeDtypeStruct((t, GROUP_WIDTH), F32),
        scratch_shapes=[pltpu.VMEM((N_HEADS, 1, tq), F32), pltpu.VMEM((N_HEADS, DV_PAD, tq), F32)],
        compiler_params=_params(("arbitrary", "arbitrary")),
        name="mla_attention",
    )(qm, km, vmt)


def _sb_kernel(q_ref, k_ref, v_ref, o_ref, kb_s, vt_s, q_s, r_s, acc_s, *, tq, s_len):
    i = pl.program_id(1)

    @pl.when(i == 0)
    def _cast_kv():
        ch = 256

        def body(t, carry):
            rows = pl.ds(pl.multiple_of(t * ch, ch), ch)
            kb_s[rows, :] = k_ref[rows, :].astype(BF16)
            vt_s[:, _lane_tile(t, ch)] = v_ref[rows, :].T.astype(BF16)
            return carry

        lax.fori_loop(0, s_len // ch, body, 0)

    key = _row((tq, tq))
    qry = _lane((tq, tq))
    strict = key < qry
    tri = jnp.where(key <= qry, 1.0, 0.0).astype(BF16)
    lane = _lane((tq, LANES))
    q = q_ref[...]
    for h in range(N_HEADS):
        mine = (lane < HEAD_DIM) if h % 2 == 0 else (lane >= HEAD_DIM)
        cols = slice(LANES * (h // 2), LANES * (h // 2 + 1))
        q_s[h] = jnp.where(mine, q[:, cols] * (HEAD_DIM ** -0.5 * LOG2E), 0.0).astype(BF16)
        r_s[h] = jnp.zeros((1, tq), F32)
        acc_s[h] = jnp.zeros((HEAD_DIM, tq), F32)

    heads = range(N_HEADS)

    def step(j, width, masked):
        rs = [r_s[h] for h in heads]
        accs = [acc_s[h] for h in heads]
        tiles = [j + width - 1 - w for w in range(width)]
        rows = [pl.ds(t * tq if isinstance(t, int) else pl.multiple_of(t * tq, tq), tq) for t in tiles]
        zs = [[_nt(kb_s[r, LANES * (h // 2):LANES * (h // 2 + 1)], q_s[h]) for h in heads] for r in rows]
        part = []
        for zt in zs:
            lgs = [jnp.minimum(-z, 0.0) - jnp.log2(1.0 + jnp.exp2(jnp.minimum(z, -z))) for z in zt]
            if masked:
                lgs = [jnp.where(strict, lg, 0.0) for lg in lgs]
            his = [lg.astype(BF16) for lg in lgs]
            los = [(lg - hi.astype(F32)).astype(BF16) for lg, hi in zip(lgs, his)]
            part.append([_dot(tri, hi) + _dot(tri, lo) for hi, lo in zip(his, los)])
        for r, zt, pt in zip(rows, zs, part):
            csums = [p + rc for p, rc in zip(pt, rs)]
            als = [jnp.exp2(z + cs) for z, cs in zip(zt, csums)]
            if masked:
                als = [jnp.where(strict, a, 0.0) for a in als]
            accs = [acc + _dot(vt_s[HEAD_DIM * h:HEAD_DIM * (h + 1), r], als[h].astype(BF16))
                    for h, acc in zip(heads, accs)]
            rs = [cs[0:1, :] for cs in csums]
        for h in heads:
            acc_s[h] = accs[h]
            r_s[h] = rs[h]

    step(i, 1, True)

    def body(jj, carry):
        step(i - 2 * jj, 2, False)
        return carry

    lax.fori_loop(1, i // 2 + 1, body, 0)
    pl.when(i % 2 == 1)(functools.partial(step, 0, 1, False))
    o_ref[...] = jnp.concatenate([acc_s[h] for h in range(N_HEADS)], axis=0).T


def _sb_attention(proj, b, s, tq):
    t = b * s
    nq = s // tq
    w = 2 * LANES
    return pl.pallas_call(
        functools.partial(_sb_kernel, tq=tq, s_len=s),
        grid=(b, nq),
        in_specs=[
            pl.BlockSpec((tq, w), lambda bi, i: (bi * nq + i, CB_SQ)),
            pl.BlockSpec((s, w), lambda bi, i: (bi, CB_SK)),
            pl.BlockSpec((s, w), lambda bi, i: (bi, CB_SV)),
        ],
        out_specs=pl.BlockSpec((tq, w), lambda bi, i: (bi * nq + i, 0)),
        out_shape=jax.ShapeDtypeStruct((t, GROUP_WIDTH), F32),
        scratch_shapes=[pltpu.VMEM((s, w), BF16), pltpu.VMEM((w, s), BF16),
                        pltpu.VMEM((N_HEADS, tq, LANES), BF16),
                        pltpu.VMEM((N_HEADS, 1, tq), F32), pltpu.VMEM((N_HEADS, HEAD_DIM, tq), F32)],
        compiler_params=_params(("arbitrary", "arbitrary")),
        name="sb_attention",
    )(proj, proj, proj)


def _post_kernel(a_ref, ap_ref, yb_ref, yc_ref, yd_ref, x_ref, cw_ref, cb_ref, onw_ref, wo_ref, n2w_ref,
                 w1_ref, w2_ref, o_ref, *, tm, s_len, ffc):
    i = pl.program_id(0)
    a = a_ref[...]
    gw = GROUP_WIDTH
    v = a[:, gw:2 * gw] * a[:, 2 * gw:3 * gw]
    ap = ap_ref[...]
    first = (i * tm) % s_len == 0
    vp = jnp.where(first, 0.0, ap[:, gw:2 * gw] * ap[:, 2 * gw:3 * gw])
    row = _row(v.shape)
    v1 = jnp.where(row == 0, vp[7:8, :], pltpu.roll(v, 1, 0))
    v2 = jnp.where(row == 0, vp[6:7, :], jnp.where(row == 1, vp[7:8, :], pltpu.roll(v, 2, 0)))
    conv = cw_ref[0:1, :] * v2 + cw_ref[1:2, :] * v1 + cw_ref[2:3, :] * v
    ya = a[:, 0:gw] * (conv + cb_ref[...])

    mix = None
    for g, y in enumerate((ya, yb_ref[...], yc_ref[...], yd_ref[...])):
        ms = jnp.mean(y * y, axis=-1, keepdims=True)
        yn = (y * lax.rsqrt(ms + EPS) * onw_ref[:, gw * g:gw * (g + 1)]).astype(BF16)
        part = _dot(yn, wo_ref[gw * g:gw * (g + 1), :])
        mix = part if mix is None else mix + part
    x1 = x_ref[...] + mix

    ms = jnp.mean(x1 * x1, axis=-1, keepdims=True)
    h2 = (x1 * lax.rsqrt(ms + EPS) * n2w_ref[...]).astype(BF16)
    ff = None
    for cidx in range(D_FF // ffc):
        u = _dot(h2, w1_ref[:, ffc * cidx:ffc * (cidx + 1)])
        u = jnp.square(jnp.maximum(u, 0.0)).astype(BF16)
        part = _dot(u, w2_ref[ffc * cidx:ffc * (cidx + 1), :])
        ff = part if ff is None else ff + part
    o_ref[...] = x1 + ff


def _post(proj, yb, yc, yd, x2d, cw, cb, onw, wo, n2w, w1, w2, s, tm):
    t = x2d.shape[0]
    gw = GROUP_WIDTH
    row = lambda i: (i, 0)
    const = lambda i: (0, 0)
    once = pl.Buffered(1)
    kern = functools.partial(_post_kernel, tm=tm, s_len=s, ffc=1024)
    return pl.pallas_call(
        kern,
        grid=(t // tm,),
        in_specs=[
            pl.BlockSpec((tm, 3 * gw), row),
            pl.BlockSpec((8, 3 * gw), lambda i: (jnp.maximum(i * (tm // 8) - 1, 0), 0)),
            pl.BlockSpec((tm, gw), row),
            pl.BlockSpec((tm, gw), row),
            pl.BlockSpec((tm, gw), row),
            pl.BlockSpec((tm, D_MODEL), row),
            pl.BlockSpec((3, gw), const),
            pl.BlockSpec((1, gw), const),
            pl.BlockSpec((1, D_MODEL), const),
            pl.BlockSpec((D_MODEL, D_MODEL), const, pipeline_mode=once),
            pl.BlockSpec((1, D_MODEL), const),
            pl.BlockSpec((D_MODEL, D_FF), const, pipeline_mode=once),
            pl.BlockSpec((D_FF, D_MODEL), const, pipeline_mode=once),
        ],
        out_specs=pl.BlockSpec((tm, D_MODEL), row),
        out_shape=jax.ShapeDtypeStruct((t, D_MODEL), F32),
        compiler_params=_params(("arbitrary",)),
        name="post",
    )(proj, proj, yb, yc, yd, x2d, cw, cb, onw, wo, n2w, w1, w2)


def _t5_bucket(dist):
    max_exact = N_BUCKETS // 2
    d = jnp.maximum(dist, 0)
    large = max_exact + (jnp.log(jnp.maximum(d, 1).astype(F32) / max_exact)
                         / math.log(MAX_DISTANCE / max_exact) * (N_BUCKETS - max_exact)).astype(jnp.int32)
    large = jnp.minimum(large, N_BUCKETS - 1)
    return jnp.where(d < max_exact, d, large)


def _tables(s, tq_nsa):
    n_cmp = (s - CMP_LEN) // CMP_STRIDE + 1
    ng = s // CMP_STRIDE
    n_slc = s // SLC_LEN
    tpos = jnp.arange(s)[None, :]
    n = jnp.arange(ng)[:, None]
    dist_c = tpos - (n * CMP_STRIDE + CMP_LEN - 1)
    bidxct = jnp.where((dist_c >= 0) & (n < n_cmp), _t5_bucket(dist_c), -1).astype(jnp.int32)
    key = jnp.arange(tq_nsa)[:, None]
    qry = jnp.arange(tq_nsa)[None, :]
    bidx2t = jnp.stack([_t5_bucket(qry - key), _t5_bucket(tq_nsa + qry - key)]).astype(jnp.int32)
    starts = np.arange(n_cmp) * CMP_STRIDE
    ends = starts + CMP_LEN
    s0 = np.arange(n_slc) * SLC_LEN
    s1 = s0 + SLC_LEN
    ovl = np.clip(np.minimum(ends[:, None], s1[None]) - np.maximum(starts[:, None], s0[None]), 0, None) / CMP_LEN
    ovt = np.zeros((LANES, ng), np.float32)
    ovt[:n_slc, :n_cmp] = ovl.T
    key_blk = (np.arange(s) // SLC_LEN).reshape(s // tq_nsa, tq_nsa, 1)
    emt = (np.arange(LANES).reshape(1, 1, LANES) == key_blk).astype(np.float32)
    inv = 1.0 / (ROPE_THETA ** (jnp.arange(0, ROPE_DIM, 2, dtype=F32) / ROPE_DIM))
    ang = jnp.arange(s, dtype=F32)[:, None] * inv[None, :]
    cos, sin = jnp.cos(ang), jnp.sin(ang)
    z16 = jnp.zeros((s, 16), F32)
    z32 = jnp.zeros((s, 32), F32)
    z64 = jnp.zeros((s, 64), F32)
    one64 = jnp.ones((s, 64), F32)
    cq_t = jnp.concatenate([one64, cos, cos, z32], axis=1)
    s1_t = jnp.concatenate([z64, -sin, z16, z32], axis=1)
    s2_t = jnp.concatenate([z64, z16, sin, z32], axis=1)
    ck_t = jnp.concatenate([z64, cos, cos, z32], axis=1)
    sk_t = jnp.concatenate([z64, -sin, sin, z32], axis=1)
    return dict(bidxct=bidxct, bidx2t=bidx2t, ovt=jnp.asarray(ovt, BF16), emt=jnp.asarray(emt, BF16),
                rope=(cq_t, s1_t, s2_t, ck_t, sk_t))


def _pad_cols(w, width):
    return jnp.pad(w, ((0, 0), (0, width - w.shape[1])))


def _layer_weights(l, w_in, conv_w, conv_b, nsa_q_norm, nsa_k_norm, cmp_pos, cmp_w1, cmp_w2, mla_q_a_norm,
                   mla_kv_norm, mla_wq_b, mla_wkv_b, mla_q_norm, mla_k_norm, out_norm_w, w_out, norm2_w,
                   ffn_w1, ffn_w2):
    wi = w_in[l]
    kr = wi[:, 1740:1772]
    z32 = jnp.zeros((D_MODEL, 32), F32)
    w_in_p = jnp.concatenate([
        wi[:, 0:1408],
        _pad_cols(wi[:, 1408:1420], LANES),
        _pad_cols(wi[:, 1420:1612], 2 * LANES),
        wi[:, 1612:1740],
        kr[:, 16:32], kr[:, 0:16], z32, kr, z32,
        wi[:, 1772:2540],
    ], axis=1).astype(BF16)
    w1 = cmp_w1[l].reshape(2, CMP_LEN, HEAD_DIM, CMP_HIDDEN)
    zw = jnp.zeros((CMP_LEN, HEAD_DIM, CMP_HIDDEN), F32)
    cw1 = jnp.concatenate([jnp.concatenate([w1[0], zw], axis=2), jnp.concatenate([zw, w1[1]], axis=2)],
                          axis=1).astype(BF16)
    zc = jnp.zeros((CMP_HIDDEN, HEAD_DIM), F32)
    cw2 = jnp.concatenate([jnp.concatenate([cmp_w2[l, 0], zc], axis=1),
                           jnp.concatenate([zc, cmp_w2[l, 1]], axis=1)], axis=0).astype(BF16)
    cpos = jnp.concatenate([cmp_pos[l, 0], cmp_pos[l, 1]], axis=1)
    kn = nsa_k_norm[l]
    ones64 = jnp.ones((HEAD_DIM,), F32)
    knw_c = jnp.concatenate([kn[0], ones64])[None, :]
    knw_sw = jnp.stack([jnp.concatenate([kn[1], ones64]), jnp.concatenate([kn[2], ones64])])
    qnw = jnp.concatenate([nsa_q_norm[l], nsa_q_norm[l]])[None, :]
    wq = mla_wq_b[l].reshape(Q_LORA, N_HEADS, QK_DIM)
    wq = jnp.pad(wq, ((0, 2 * LANES - Q_LORA), (0, 0), (0, LANES - QK_DIM))).reshape(2 * LANES, N_HEADS * LANES)
    wkv = mla_wkv_b[l].reshape(KV_LORA, N_HEADS, 2 * HEAD_DIM)
    wk = jnp.pad(wkv[:, :, :HEAD_DIM], ((0, 0), (0, 0), (0, LANES - HEAD_DIM))).reshape(KV_LORA, N_HEADS * LANES)
    wvt = jnp.pad(wkv[:, :, HEAD_DIM:], ((0, 0), (0, 0), (0, DV_PAD - HEAD_DIM))).reshape(KV_LORA, -1).T
    return dict(
        w_in=w_in_p, cw1=cw1, cw2=cw2, cw2t=cw2.T, cpos=cpos, knw_c=knw_c, knw_sw=knw_sw, qnw=qnw,
        qaw=_pad_cols(mla_q_a_norm[l][None, :], 2 * LANES), kvw=mla_kv_norm[l][None, :],
        wq=wq.astype(BF16), wk=wk.astype(BF16), wvt=wvt.astype(BF16),
        mqn=_pad_cols(mla_q_norm[l][None, :], LANES), mkn=_pad_cols(mla_k_norm[l][None, :], LANES),
        cw=conv_w[l], cb=conv_b[l][None, :], onw=out_norm_w[l][None, :], wo=w_out[l].astype(BF16),
        n2w=norm2_w[l][None, :], w1=ffn_w1[l].astype(BF16), w2=ffn_w2[l].astype(BF16))


TM_PROJ = 512
TM_PREP = 512
TM_POST = 512
TQ_NSA = 256
TQ_MLA = 256
TQ_SB = 256


def kernel(x, rel_bias, norm1_w, w_in, conv_w, conv_b, nsa_q_norm, nsa_k_norm, cmp_pos, cmp_w1, cmp_w2,
           mla_q_a_norm, mla_kv_norm, mla_wq_b, mla_wkv_b, mla_q_norm, mla_k_norm, out_norm_w, w_out, norm2_w,
           ffn_w1, ffn_w2):
    b, s, d = x.shape
    depth = w_in.shape[0]
    tabs = _tables(s, TQ_NSA)
    x2d = x.reshape(b * s, d)
    for l in range(depth):
        w = _layer_weights(l, w_in, conv_w, conv_b, nsa_q_norm, nsa_k_norm, cmp_pos, cmp_w1, cmp_w2,
                           mla_q_a_norm, mla_kv_norm, mla_wq_b, mla_wkv_b, mla_q_norm, mla_k_norm, out_norm_w,
                           w_out, norm2_w, ffn_w1, ffn_w2)
        proj = _inproj(x2d, norm1_w[l][None, :], w["w_in"], TM_PROJ)
        kcvc, kcvct = _compress(proj, w["cpos"], w["cw1"], w["cw2"], w["cw2t"], w["knw_c"], b, s)
        yb = _nsa_attention(proj, kcvc, kcvct, rel_bias, tabs["bidxct"], tabs["bidx2t"], tabs["ovt"], tabs["emt"],
                            w["qnw"], w["knw_sw"], b, s, TQ_NSA)
        qm, km, vmt = _mla_prep(proj, w["qaw"], w["kvw"], w["wq"], w["wk"], w["wvt"], w["mqn"], w["mkn"],
                                tabs["rope"], s, TM_PREP)
        yc = _mla_attention(qm, km, vmt, b, s, TQ_MLA)
        yd = _sb_attention(proj, b, s, TQ_SB)
        x2d = _post(proj, yb, yc, yd, x2d, w["cw"], w["cb"], w["onw"], w["wo"], w["n2w"], w["w1"], w["w2"],
                    s, TM_POST)
    return x2d.reshape(b, s, d)
```

```python
import functools
import math

import jax
import jax.numpy as jnp
import numpy as np
from jax import lax
from jax.experimental import pallas as pl
from jax.experimental.pallas import tpu as pltpu

F32 = jnp.float32
BF16 = jnp.bfloat16

D_MODEL = 1024
GROUP_WIDTH = 256
HEAD_DIM = 64
N_HEADS = 4
LANES = 128
CMP_LEN = 32
CMP_STRIDE = 16
SLC_LEN = 64
N_SEL = 16
WINDOW = 512
CMP_HIDDEN = 256
Q_LORA = 192
KV_LORA = 128
ROPE_DIM = 32
QK_DIM = 96
ROPE_THETA = 10000.0
N_BUCKETS = 32
MAX_DISTANCE = 128
D_FF = 4096
EPS = 1e-6
NEG = -1e30
LOG2E = math.log2(math.e)
DV_PAD = 80

NP = 2816
CB_NQ = 3
CB_KCVC = 8
CB_KSVS = 9
CB_KWVW = 10
CB_GATE = 11
CB_CQ = 6
CB_CKV = 14
CB_KR = 15
CB_SQ = 8
CB_SK = 9
CB_SV = 10

VMEM_LIMIT = 56 * 1024 * 1024

NT_DIMS = (((1,), (1,)), ((), ()))


def _params(sem):
    return pltpu.CompilerParams(dimension_semantics=sem, vmem_limit_bytes=VMEM_LIMIT)


def _nt(a, b):
    return lax.dot_general(a, b, NT_DIMS, preferred_element_type=F32)


def _dot(a, b):
    return jnp.dot(a, b, preferred_element_type=F32)


def _lane(shape):
    return lax.broadcasted_iota(jnp.int32, shape, len(shape) - 1)


def _row(shape):
    return lax.broadcasted_iota(jnp.int32, shape, len(shape) - 2)


def _lane_tile(j, width):
    return pl.ds(pl.multiple_of(j * width, width), width)


def _inproj_kernel(x_ref, nw_ref, w_ref, o_ref):
    x = x_ref[...]
    ms = jnp.mean(x * x, axis=-1, keepdims=True)
    h = (x * lax.rsqrt(ms + EPS) * nw_ref[...]).astype(BF16)
    o_ref[...] = _dot(h, w_ref[...])


def _inproj(x2d, nw, w, tm):
    t = x2d.shape[0]
    return pl.pallas_call(
        _inproj_kernel,
        grid=(t // tm,),
        in_specs=[
            pl.BlockSpec((tm, D_MODEL), lambda i: (i, 0)),
            pl.BlockSpec((1, D_MODEL), lambda i: (0, 0)),
            pl.BlockSpec((D_MODEL, NP), lambda i: (0, 0), pipeline_mode=pl.Buffered(1)),
        ],
        out_specs=pl.BlockSpec((tm, NP), lambda i: (i, 0)),
        out_shape=jax.ShapeDtypeStruct((t, NP), F32),
        compiler_params=_params(("arbitrary",)),
        name="inproj",
    )(x2d, nw, w)


def _compress_kernel(x_ref, pos_ref, w1_ref, w2_ref, w2t_ref, knw_ref, o_ref, ot_ref):
    ng = x_ref.shape[1]
    acc_a = jnp.zeros((ng, 2 * CMP_HIDDEN), F32)
    acc_b = jnp.zeros((ng, 2 * CMP_HIDDEN), F32)
    for i in range(CMP_STRIDE):
        x = x_ref[0, :, i, :]
        xa = (x + pos_ref[i:i + 1, :]).astype(BF16)
        xb = (x + pos_ref[CMP_STRIDE + i:CMP_STRIDE + i + 1, :]).astype(BF16)
        acc_a += _dot(xa, w1_ref[i])
        acc_b += _dot(xb, w1_ref[CMP_STRIDE + i])
    pre = acc_a + pltpu.roll(acc_b, ng - 1, 0)
    hdn = (pre * jax.nn.sigmoid(pre)).astype(BF16)
    out = _dot(hdn, w2_ref[...])
    lane = _lane(out.shape)
    is_k = lane < HEAD_DIM
    ss = jnp.sum(jnp.where(is_k, out * out, 0.0), axis=-1, keepdims=True) * (1.0 / HEAD_DIM)
    o_ref[0] = jnp.where(is_k, out * lax.rsqrt(ss + EPS) * knw_ref[...], out)
    ot_ref[0] = _nt(w2t_ref[...], hdn)


def _compress(proj, pos, w1, w2, w2t, knw, b, s):
    ng = s // CMP_STRIDE
    x4 = proj.reshape(b, ng, CMP_STRIDE, NP)
    return pl.pallas_call(
        _compress_kernel,
        grid=(b,),
        in_specs=[
            pl.BlockSpec((1, ng, CMP_STRIDE, LANES), lambda i: (i, 0, 0, CB_KCVC)),
            pl.BlockSpec((CMP_LEN, LANES), lambda i: (0, 0)),
            pl.BlockSpec((CMP_LEN, LANES, 2 * CMP_HIDDEN), lambda i: (0, 0, 0)),
            pl.BlockSpec((2 * CMP_HIDDEN, LANES), lambda i: (0, 0)),
            pl.BlockSpec((LANES, 2 * CMP_HIDDEN), lambda i: (0, 0)),
            pl.BlockSpec((1, LANES), lambda i: (0, 0)),
        ],
        out_specs=[pl.BlockSpec((1, ng, LANES), lambda i: (i, 0, 0)),
                   pl.BlockSpec((1, LANES, ng), lambda i: (i, 0, 0))],
        out_shape=[jax.ShapeDtypeStruct((b, ng, LANES), F32), jax.ShapeDtypeStruct((b, LANES, ng), F32)],
        compiler_params=_params(("arbitrary",)),
        name="nsa_compress",
    )(x4, pos, w1, w2, w2t, knw)


def _softmax_update(sts, vts, m_old, acc_old):
    m_new = [jnp.maximum(m, jnp.max(st, axis=0, keepdims=True)) for m, st in zip(m_old, sts)]
    ps = [jnp.exp2(st - m).astype(BF16) for st, m in zip(sts, m_new)]
    alphas = [jnp.exp2(mo - mn) for mo, mn in zip(m_old, m_new)]
    acc_new = [al * acc + _dot(vt, p) for al, acc, vt, p in zip(alphas, acc_old, vts, ps)]
    return m_new, acc_new


def _softmax_steps(sts, vts, m_refs, acc_refs):
    m_new, acc_new = _softmax_update(sts, vts, [r[...] for r in m_refs], [r[...] for r in acc_refs])
    for r, v in zip(m_refs, m_new):
        r[...] = v
    for r, v in zip(acc_refs, acc_new):
        r[...] = v


def _with_ones_row(vt):
    pad = jnp.where(_row((DV_PAD - HEAD_DIM, vt.shape[1])) == 0, 1.0, 0.0).astype(vt.dtype)
    return jnp.concatenate([vt, pad], axis=0)


def _bucket_bias(bidx, relb_ref, h, fill):
    acc = jnp.full(bidx.shape, fill, F32)
    for bk in range(N_BUCKETS):
        acc = jnp.where(bidx == bk, relb_ref[bk, h] * LOG2E, acc)
    return acc


def _group_sums(x, member):
    g = jnp.where(member, 1.0, 0.0).astype(BF16)
    hi = x.astype(BF16)
    lo = (x - hi.astype(F32)).astype(BF16)
    return _dot(hi, g) + _dot(lo, g)


def _dup_low_half(x):
    y = jnp.where(_lane(x.shape) < HEAD_DIM, x, 0.0)
    return y + pltpu.roll(y, HEAD_DIM, 1)


def _nsa_kernel(relb_ref, q_ref, g_ref, kcvc_ref, kcvct_ref, ksvs_ref, kwvw_ref, bidxct_ref, bidx2t_ref, ovt_ref,
                emt_ref, qnw_ref, knw_ref, o_ref,
                biasc_s, bias2_s, ks_s, vst_s, kw_s, vwt_s, kc_s, vct_s, q_s, m_s, acc_s, *, tq, s_len):
    b = pl.program_id(0)
    i = pl.program_id(1)
    n_win = WINDOW // tq
    ng = s_len // CMP_STRIDE

    @pl.when((b == 0) & (i == 0))
    def _build_bias_tables():
        key = _row((tq, tq))
        qry = _lane((tq, tq))
        for h in range(N_HEADS):
            cols = slice(h * tq, (h + 1) * tq)
            far = jnp.full((tq, tq), relb_ref[N_BUCKETS - 1, h] * LOG2E, F32)
            bias2_s[0, :, cols] = jnp.where(key <= qry, _bucket_bias(bidx2t_ref[0], relb_ref, h, NEG), NEG)
            bias2_s[1, :, cols] = _bucket_bias(bidx2t_ref[1], relb_ref, h, NEG)
            bias2_s[2, :, cols] = far
            bias2_s[3, :, cols] = jnp.where(key > qry, far, NEG)
            bias2_s[4, :, cols] = jnp.full((tq, tq), NEG, F32)

        def body(t, carry):
            bi = bidxct_ref[:, _lane_tile(t, tq)]
            for h in range(N_HEADS):
                biasc_s[t, :, h * tq:(h + 1) * tq] = _bucket_bias(bi, relb_ref, h, NEG)
            return carry

        lax.fori_loop(0, s_len // tq, body, 0)

    @pl.when(i == 0)
    def _prep_kv():
        ch = 256

        def body(t, carry):
            rows = pl.ds(pl.multiple_of(t * ch, ch), ch)
            for src, kdst, vdst, widx in ((ksvs_ref, ks_s, vst_s, 0), (kwvw_ref, kw_s, vwt_s, 1)):
                x = src[rows, :]
                ss = _group_sums(x * x, _row((LANES, LANES)) < HEAD_DIM) * (1.0 / HEAD_DIM)
                kn = x * lax.rsqrt(ss + EPS) * knw_ref[widx:widx + 1, :]
                kdst[rows, :] = _dup_low_half(kn).astype(BF16)
                vdst[:, _lane_tile(t, ch)] = _with_ones_row(x.T[HEAD_DIM:, :]).astype(BF16)
            return carry

        lax.fori_loop(0, s_len // ch, body, 0)
        kc_s[...] = _dup_low_half(kcvc_ref[0]).astype(BF16)
        vct_s[...] = kcvct_ref[0][HEAD_DIM:, :].astype(BF16)

    q = q_ref[...]
    lane = _lane((tq, LANES))
    heads = range(N_HEADS)
    gw = 2 * LANES
    head_shift = HEAD_DIM.bit_length() - 1
    same_head = (lax.shift_right_logical(_row((gw, gw)), head_shift)
                 == lax.shift_right_logical(_lane((gw, gw)), head_shift))
    ss = _group_sums(q * q, same_head) * (1.0 / HEAD_DIM)
    qn = q * lax.rsqrt(ss + EPS) * qnw_ref[...] * (HEAD_DIM ** -0.5 * LOG2E)
    for h in heads:
        mine = (lane < HEAD_DIM) if h % 2 == 0 else (lane >= HEAD_DIM)
        q_s[h] = jnp.where(mine, qn[:, LANES * (h // 2):LANES * (h // 2 + 1)], 0.0).astype(BF16)

    lcs = [_nt(kc_s[...], q_s[h]) + biasc_s[i, :, h * tq:(h + 1) * tq] for h in heads]
    pcs = [jnp.where(lc > 0.5 * NEG, jnp.exp2(lc - jnp.max(lc, axis=0, keepdims=True)), 0.0) for lc in lcs]
    dens = [jnp.sum(pc, axis=0, keepdims=True) for pc in pcs]
    pcs = [pc / jnp.where(den > 0.0, den, 1.0) for pc, den in zip(pcs, dens)]
    o_cmp = [_dot(vct_s[...], pc.astype(BF16)) for pc in pcs]
    psum = (pcs[0] + pcs[1]) + (pcs[2] + pcs[3])
    p_hi = psum.astype(BF16)
    p_lo = (psum - p_hi.astype(F32)).astype(BF16)
    score = _dot(ovt_ref[...], p_hi) + _dot(ovt_ref[...], p_lo)

    n_slc = s_len // SLC_LEN
    blk = _row((n_slc, tq))
    tpos = i * tq + _lane((n_slc, tq))
    tblk = tpos // SLC_LEN
    valid = blk * SLC_LEN <= tpos
    forced = (blk == 0) | (blk == tblk) | (blk == tblk - 1)
    sc = jnp.where(forced, jnp.inf, jnp.where(valid, score[0:n_slc], -jnp.inf))
    rank = jnp.zeros((n_slc, tq), F32)
    for k in range(n_slc):
        ck = sc[k:k + 1, :]
        beats = (ck > sc) | ((ck == sc) & (blk > k))
        rank += jnp.where(beats, 1.0, 0.0)
    pen = jnp.where((rank < float(min(N_SEL, n_slc))) & valid, 0.0, NEG)
    pen = jnp.concatenate([pen, jnp.zeros((LANES - n_slc, tq), F32)], axis=0).astype(BF16)

    ms = [jnp.full((1, tq), NEG, F32)] * (2 * N_HEADS)
    accs = [jnp.zeros((DV_PAD, tq), F32)] * (2 * N_HEADS)
    for jj in range(n_win + 1):
        exists = i >= jj
        rows = pl.ds(pl.multiple_of(jnp.maximum(i - jj, 0) * tq, tq), tq)
        kind = jnp.where(exists, min(jj, 2), 4)
        kind_w = jnp.where(exists, 3 if jj == n_win else min(jj, 2), 4)
        ks = ks_s[rows, :]
        kw = kw_s[rows, :]
        masked = _dot(emt_ref[rows, :], pen)
        sts = [_nt(ks, q_s[h]) + masked + bias2_s[kind, :, h * tq:(h + 1) * tq] for h in heads]
        sts += [_nt(kw, q_s[h]) + bias2_s[kind_w, :, h * tq:(h + 1) * tq] for h in heads]
        vts = [vst_s[:, rows]] * N_HEADS + [vwt_s[:, rows]] * N_HEADS
        ms, accs = _softmax_update(sts, vts, ms, accs)
    for h in heads:
        m_s[h] = ms[h]
        acc_s[h] = accs[h]
    o_win = [accs[N_HEADS + h][0:HEAD_DIM, :] / accs[N_HEADS + h][HEAD_DIM:HEAD_DIM + 1, :] for h in heads]

    def far_step(j, width):
        rows = pl.ds(j * tq if isinstance(j, int) else pl.multiple_of(j * tq, tq), width * tq)
        ks = ks_s[rows, :]
        masked = _dot(emt_ref[rows, :], pen)
        sts = [_nt(ks, q_s[h]) + (masked + relb_ref[N_BUCKETS - 1, h] * LOG2E) for h in heads]
        _softmax_steps(sts, [vst_s[:, rows]] * N_HEADS, [m_s.at[h] for h in heads], [acc_s.at[h] for h in heads])

    n_far = jnp.maximum(i - n_win, 0)

    def far_body(p, carry):
        far_step(n_far - 2 * (p + 1), 2)
        return carry

    lax.fori_loop(0, n_far // 2, far_body, 0)
    pl.when(n_far % 2 == 1)(functools.partial(far_step, 0, 1))

    gt = jax.nn.sigmoid(g_ref[...]).T
    ys = []
    for h in heads:
        o_slc = acc_s[h, 0:HEAD_DIM, :] / acc_s[h, HEAD_DIM:HEAD_DIM + 1, :]
        ys.append(gt[3 * h:3 * h + 1, :] * o_cmp[h] + gt[3 * h + 1:3 * h + 2, :] * o_slc
                  + gt[3 * h + 2:3 * h + 3, :] * o_win[h])
    o_ref[...] = jnp.concatenate(ys, axis=0).T


def _nsa_attention(proj, kcvc, kcvct, rel_bias, bidxct, bidx2t, ovt, emt, qnw, knw, b, s, tq):
    t = b * s
    nq = s // tq
    m_rows = N_HEADS * tq
    ng = s // CMP_STRIDE
    kern = functools.partial(_nsa_kernel, tq=tq, s_len=s)
    return pl.pallas_call(
        kern,
        grid=(b, nq),
        in_specs=[
            pl.BlockSpec(memory_space=pltpu.SMEM),
            pl.BlockSpec((tq, 2 * LANES), lambda bi, i: (bi * nq + i, CB_NQ)),
            pl.BlockSpec((tq, LANES), lambda bi, i: (bi * nq + i, CB_GATE)),
            pl.BlockSpec((1, ng, LANES), lambda bi, i: (bi, 0, 0)),
            pl.BlockSpec((1, LANES, ng), lambda bi, i: (bi, 0, 0)),
            pl.BlockSpec((s, LANES), lambda bi, i: (bi, CB_KSVS)),
            pl.BlockSpec((s, LANES), lambda bi, i: (bi, CB_KWVW)),
            pl.BlockSpec((ng, s), lambda bi, i: (0, 0)),
            pl.BlockSpec((2, tq, tq), lambda bi, i: (0, 0, 0)),
            pl.BlockSpec((LANES, ng), lambda bi, i: (0, 0)),
            pl.BlockSpec((s, LANES), lambda bi, i: (0, 0)),
            pl.BlockSpec((1, 2 * LANES), lambda bi, i: (0, 0)),
            pl.BlockSpec((2, LANES), lambda bi, i: (0, 0)),
        ],
        out_specs=pl.BlockSpec((tq, 2 * LANES), lambda bi, i: (bi * nq + i, 0)),
        out_shape=jax.ShapeDtypeStruct((t, GROUP_WIDTH), F32),
        scratch_shapes=[
            pltpu.VMEM((nq, ng, m_rows), F32),
            pltpu.VMEM((5, tq, m_rows), F32),
            pltpu.VMEM((s, LANES), BF16),
            pltpu.VMEM((DV_PAD, s), BF16),
            pltpu.VMEM((s, LANES), BF16),
            pltpu.VMEM((DV_PAD, s), BF16),
            pltpu.VMEM((ng, LANES), BF16),
            pltpu.VMEM((HEAD_DIM, ng), BF16),
            pltpu.VMEM((N_HEADS, tq, LANES), BF16),
            pltpu.VMEM((N_HEADS, 1, tq), F32),
            pltpu.VMEM((N_HEADS, DV_PAD, tq), F32),
        ],
        compiler_params=_params(("arbitrary", "arbitrary")),
        name="nsa_attention",
    )(rel_bias, proj, proj, kcvc, kcvct, proj, proj, bidxct, bidx2t, ovt, emt, qnw, knw)


def _mla_prep_kernel(cq_ref, ckv_ref, kr_ref, qaw_ref, kvw_ref, wq_ref, wk_ref, wvt_ref, qnw_ref, knw_ref,
                     cq_t_ref, s1_t_ref, s2_t_ref, ck_t_ref, sk_t_ref, qo_ref, ko_ref, vto_ref):
    cq = cq_ref[...]
    ms = jnp.sum(cq * cq, axis=-1, keepdims=True) * (1.0 / Q_LORA)
    hq = (cq * lax.rsqrt(ms + EPS) * qaw_ref[...]).astype(BF16)
    qf = _dot(hq, wq_ref[...])
    ckv = ckv_ref[...]
    ms = jnp.mean(ckv * ckv, axis=-1, keepdims=True)
    hkv = (ckv * lax.rsqrt(ms + EPS) * kvw_ref[...]).astype(BF16)
    kf = _dot(hkv, wk_ref[...])
    vt = _nt(wvt_ref[...], hkv)
    vto_ref[...] = jnp.where(_row(vt.shape) % DV_PAD == HEAD_DIM, 1.0, vt).astype(BF16)
    krb = kr_ref[...]
    kr_rot = krb * ck_t_ref[...] + pltpu.roll(krb, HEAD_DIM, 1) * sk_t_ref[...]
    scale = QK_DIM ** -0.5 * LOG2E
    for h in range(N_HEADS):
        cols = slice(LANES * h, LANES * (h + 1))
        x = qf[:, cols]
        x = x * cq_t_ref[...] + pltpu.roll(x, LANES - 16, 1) * s1_t_ref[...] + pltpu.roll(x, 16, 1) * s2_t_ref[...]
        ss = jnp.sum(x * x, axis=-1, keepdims=True) * (1.0 / QK_DIM)
        qo_ref[:, cols] = (x * lax.rsqrt(ss + EPS) * qnw_ref[...] * scale).astype(BF16)
        k = kf[:, cols] + kr_rot
        ss = jnp.sum(k * k, axis=-1, keepdims=True) * (1.0 / QK_DIM)
        ko_ref[:, cols] = (k * lax.rsqrt(ss + EPS) * knw_ref[...]).astype(BF16)


def _mla_prep(proj, qaw, kvw, wq, wk, wvt, qnw, knw, tabs, s, tm):
    t = proj.shape[0]
    npos = s // tm
    row = lambda i: (i, 0)
    const = lambda i: (0, 0)
    tab = pl.BlockSpec((tm, LANES), lambda i: (i % npos, 0))
    out = jax.ShapeDtypeStruct((t, N_HEADS * LANES), BF16)
    return pl.pallas_call(
        _mla_prep_kernel,
        grid=(t // tm,),
        in_specs=[
            pl.BlockSpec((tm, 2 * LANES), lambda i: (i, CB_CQ)),
            pl.BlockSpec((tm, LANES), lambda i: (i, CB_CKV)),
            pl.BlockSpec((tm, LANES), lambda i: (i, CB_KR)),
            pl.BlockSpec((1, 2 * LANES), const),
            pl.BlockSpec((1, LANES), const),
            pl.BlockSpec((2 * LANES, N_HEADS * LANES), const),
            pl.BlockSpec((LANES, N_HEADS * LANES), const),
            pl.BlockSpec((N_HEADS * DV_PAD, LANES), const),
            pl.BlockSpec((1, LANES), const),
            pl.BlockSpec((1, LANES), const),
            tab, tab, tab, tab, tab,
        ],
        out_specs=[pl.BlockSpec((tm, N_HEADS * LANES), row), pl.BlockSpec((tm, N_HEADS * LANES), row),
                   pl.BlockSpec((N_HEADS * DV_PAD, tm), lambda i: (0, i))],
        out_shape=[out, out, jax.ShapeDtypeStruct((N_HEADS * DV_PAD, t), BF16)],
        compiler_params=_params(("arbitrary",)),
        name="mla_prep",
    )(proj, proj, proj, qaw, kvw, wq, wk, wvt, qnw, knw, *tabs)


def _mla_attn_kernel(q_ref, k_ref, vt_ref, o_ref, m_s, acc_s, *, tq):
    i = pl.program_id(1)
    causal = _row((tq, tq)) <= _lane((tq, tq))
    for h in range(N_HEADS):
        m_s[h] = jnp.full((1, tq), NEG, F32)
        acc_s[h] = jnp.zeros((DV_PAD, tq), F32)

    def step(j, width, masked):
        tk = width * tq
        rows = pl.ds(j * tq if isinstance(j, int) else pl.multiple_of(j * tq, tq), tk)
        sts = []
        for h in range(N_HEADS):
            cols = slice(LANES * h, LANES * (h + 1))
            st = _nt(k_ref[rows, cols], q_ref[:, cols])
            sts.append(jnp.where(causal, st, NEG) if masked else st)
        vts = [vt_ref[DV_PAD * h:DV_PAD * (h + 1), rows] for h in range(N_HEADS)]
        _softmax_steps(sts, vts, [m_s.at[h] for h in range(N_HEADS)], [acc_s.at[h] for h in range(N_HEADS)])

    step(i, 1, True)

    def body(jj, carry):
        step(i - 2 * jj, 2, False)
        return carry

    lax.fori_loop(1, i // 2 + 1, body, 0)
    pl.when(i % 2 == 1)(functools.partial(step, 0, 1, False))
    yt = jnp.concatenate([acc_s[h, 0:HEAD_DIM, :] / acc_s[h, HEAD_DIM:HEAD_DIM + 1, :] for h in range(N_HEADS)],
                         axis=0)
    o_ref[...] = yt.T


def _mla_attention(qm, km, vmt, b, s, tq):
    t = b * s
    nq = s // tq
    w = N_HEADS * LANES
    return pl.pallas_call(
        functools.partial(_mla_attn_kernel, tq=tq),
        grid=(b, nq),
        in_specs=[
            pl.BlockSpec((tq, w), lambda bi, i: (bi * nq + i, 0)),
            pl.BlockSpec((s, w), lambda bi, i: (bi, 0)),
            pl.BlockSpec((N_HEADS * DV_PAD, s), lambda bi, i: (0, bi)),
        ],
        out_specs=pl.BlockSpec((tq, 2 * LANES), lambda bi, i: (bi * nq + i, 0)),
        out_shape=jax.ShapeDtypeStruct((t, GROUP_WIDTH), F32),
        scratch_shapes=[pltpu.VMEM((N_HEADS, 1, tq), F32), pltpu.VMEM((N_HEADS, DV_PAD, tq), F32)],
        compiler_params=_params(("arbitrary", "arbitrary")),
        name="mla_attention",
    )(qm, km, vmt)


def _sb_kernel(q_ref, k_ref, v_ref, o_ref, kb_s, vt_s, q_s, r_s, acc_s, *, tq, s_len):
    i = pl.program_id(1)

    @pl.when(i == 0)
    def _cast_kv():
        ch = 256

        def body(t, carry):
            rows = pl.ds(pl.multiple_of(t * ch, ch), ch)
            kb_s[rows, :] = k_ref[rows, :].astype(BF16)
            vt_s[:, _lane_tile(t, ch)] = v_ref[rows, :].T.astype(BF16)
            return carry

        lax.fori_loop(0, s_len // ch, body, 0)

    key = _row((tq, tq))
    qry = _lane((tq, tq))
    strict = key < qry
    tri = jnp.where(key <= qry, 1.0, 0.0).astype(BF16)
    lane = _lane((tq, LANES))
    q = q_ref[...]
    for h in range(N_HEADS):
        mine = (lane < HEAD_DIM) if h % 2 == 0 else (lane >= HEAD_DIM)
        cols = slice(LANES * (h // 2), LANES * (h // 2 + 1))
        q_s[h] = jnp.where(mine, q[:, cols] * (HEAD_DIM ** -0.5 * LOG2E), 0.0).astype(BF16)
        r_s[h] = jnp.zeros((1, tq), F32)
        acc_s[h] = jnp.zeros((HEAD_DIM, tq), F32)

    heads = range(N_HEADS)

    def step(j, width, masked):
        rs = [r_s[h] for h in heads]
        accs = [acc_s[h] for h in heads]
        tiles = [j + width - 1 - w for w in range(width)]
        rows = [pl.ds(t * tq if isinstance(t, int) else pl.multiple_of(t * tq, tq), tq) for t in tiles]
        zs = [[_nt(kb_s[r, LANES * (h // 2):LANES * (h // 2 + 1)], q_s[h]) for h in heads] for r in rows]
        part = []
        for zt in zs:
            lgs = [jnp.minimum(-z, 0.0) - jnp.log2(1.0 + jnp.exp2(jnp.minimum(z, -z))) for z in zt]
            if masked:
                lgs = [jnp.where(strict, lg, 0.0) for lg in lgs]
            his = [lg.astype(BF16) for lg in lgs]
            los = [(lg - hi.astype(F32)).astype(BF16) for lg, hi in zip(lgs, his)]
            part.append([_dot(tri, hi) + _dot(tri, lo) for hi, lo in zip(his, los)])
        for r, zt, pt in zip(rows, zs, part):
            csums = [p + rc for p, rc in zip(pt, rs)]
            als = [jnp.exp2(z + cs) for z, cs in zip(zt, csums)]
            if masked:
                als = [jnp.where(strict, a, 0.0) for a in als]
            accs = [acc + _dot(vt_s[HEAD_DIM * h:HEAD_DIM * (h + 1), r], als[h].astype(BF16))
                    for h, acc in zip(heads, accs)]
            rs = [cs[0:1, :] for cs in csums]
        for h in heads:
            acc_s[h] = accs[h]
            r_s[h] = rs[h]

    step(i, 1, True)

    def body(jj, carry):
        step(i - 2 * jj, 2, False)
        return carry

    lax.fori_loop(1, i // 2 + 1, body, 0)
    pl.when(i % 2 == 1)(functools.partial(step, 0, 1, False))
    o_ref[...] = jnp.concatenate([acc_s[h] for h in range(N_HEADS)], axis=0).T


def _sb_attention(proj, b, s, tq):
    t = b * s
    nq = s // tq
    w = 2 * LANES
    return pl.pallas_call(
        functools.partial(_sb_kernel, tq=tq, s_len=s),
        grid=(b, nq),
        in_specs=[
            pl.BlockSpec((tq, w), lambda bi, i: (bi * nq + i, CB_SQ)),
            pl.BlockSpec((s, w), lambda bi, i: (bi, CB_SK)),
            pl.BlockSpec((s, w), lambda bi, i: (bi, CB_SV)),
        ],
        out_specs=pl.BlockSpec((tq, w), lambda bi, i: (bi * nq + i, 0)),
        out_shape=jax.ShapeDtypeStruct((t, GROUP_WIDTH), F32),
        scratch_shapes=[pltpu.VMEM((s, w), BF16), pltpu.VMEM((w, s), BF16),
                        pltpu.VMEM((N_HEADS, tq, LANES), BF16),
                        pltpu.VMEM((N_HEADS, 1, tq), F32), pltpu.VMEM((N_HEADS, HEAD_DIM, tq), F32)],
        compiler_params=_params(("arbitrary", "arbitrary")),
        name="sb_attention",
    )(proj, proj, proj)


def _post_kernel(a_ref, ap_ref, yb_ref, yc_ref, yd_ref, x_ref, cw_ref, cb_ref, onw_ref, wo_ref, n2w_ref,
                 w1_ref, w2_ref, o_ref, *, tm, s_len, ffc):
    i = pl.program_id(0)
    a = a_ref[...]
    gw = GROUP_WIDTH
    v = a[:, gw:2 * gw] * a[:, 2 * gw:3 * gw]
    ap = ap_ref[...]
    first = (i * tm) % s_len == 0
    vp = jnp.where(first, 0.0, ap[:, gw:2 * gw] * ap[:, 2 * gw:3 * gw])
    row = _row(v.shape)
    v1 = jnp.where(row == 0, vp[7:8, :], pltpu.roll(v, 1, 0))
    v2 = jnp.where(row == 0, vp[6:7, :], jnp.where(row == 1, vp[7:8, :], pltpu.roll(v, 2, 0)))
    conv = cw_ref[0:1, :] * v2 + cw_ref[1:2, :] * v1 + cw_ref[2:3, :] * v
    ya = a[:, 0:gw] * (conv + cb_ref[...])

    mix = None
    for g, y in enumerate((ya, yb_ref[...], yc_ref[...], yd_ref[...])):
        ms = jnp.mean(y * y, axis=-1, keepdims=True)
        yn = (y * lax.rsqrt(ms + EPS) * onw_ref[:, gw * g:gw * (g + 1)]).astype(BF16)
        part = _dot(yn, wo_ref[gw * g:gw * (g + 1), :])
        mix = part if mix is None else mix + part
    x1 = x_ref[...] + mix

    ms = jnp.mean(x1 * x1, axis=-1, keepdims=True)
    h2 = (x1 * lax.rsqrt(ms + EPS) * n2w_ref[...]).astype(BF16)
    ff = None
    for cidx in range(D_FF // ffc):
        u = _dot(h2, w1_ref[:, ffc * cidx:ffc * (cidx + 1)])
        u = jnp.square(jnp.maximum(u, 0.0)).astype(BF16)
        part = _dot(u, w2_ref[ffc * cidx:ffc * (cidx + 1), :])
        ff = part if ff is None else ff + part
    o_ref[...] = x1 + ff


def _post(proj, yb, yc, yd, x2d, cw, cb, onw, wo, n2w, w1, w2, s, tm):
    t = x2d.shape[0]
    gw = GROUP_WIDTH
    row = lambda i: (i, 0)
    const = lambda i: (0, 0)
    once = pl.Buffered(1)
    kern = functools.partial(_post_kernel, tm=tm, s_len=s, ffc=1024)
    return pl.pallas_call(
        kern,
        grid=(t // tm,),
        in_specs=[
            pl.BlockSpec((tm, 3 * gw), row),
            pl.BlockSpec((8, 3 * gw), lambda i: (jnp.maximum(i * (tm // 8) - 1, 0), 0)),
            pl.BlockSpec((tm, gw), row),
            pl.BlockSpec((tm, gw), row),
            pl.BlockSpec((tm, gw), row),
            pl.BlockSpec((tm, D_MODEL), row),
            pl.BlockSpec((3, gw), const),
            pl.BlockSpec((1, gw), const),
            pl.BlockSpec((1, D_MODEL), const),
            pl.BlockSpec((D_MODEL, D_MODEL), const, pipeline_mode=once),
            pl.BlockSpec((1, D_MODEL), const),
            pl.BlockSpec((D_MODEL, D_FF), const, pipeline_mode=once),
            pl.BlockSpec((D_FF, D_MODEL), const, pipeline_mode=once),
        ],
        out_specs=pl.BlockSpec((tm, D_MODEL), row),
        out_shape=jax.ShapeDtypeStruct((t, D_MODEL), F32),
        compiler_params=_params(("arbitrary",)),
        name="post",
    )(proj, proj, yb, yc, yd, x2d, cw, cb, onw, wo, n2w, w1, w2)


def _t5_bucket(dist):
    max_exact = N_BUCKETS // 2
    d = jnp.maximum(dist, 0)
    large = max_exact + (jnp.log(jnp.maximum(d, 1).astype(F32) / max_exact)
                         / math.log(MAX_DISTANCE / max_exact) * (N_BUCKETS - max_exact)).astype(jnp.int32)
    large = jnp.minimum(large, N_BUCKETS - 1)
    return jnp.where(d < max_exact, d, large)


def _tables(s, tq_nsa):
    n_cmp = (s - CMP_LEN) // CMP_STRIDE + 1
    ng = s // CMP_STRIDE
    n_slc = s // SLC_LEN
    tpos = jnp.arange(s)[None, :]
    n = jnp.arange(ng)[:, None]
    dist_c = tpos - (n * CMP_STRIDE + CMP_LEN - 1)
    bidxct = jnp.where((dist_c >= 0) & (n < n_cmp), _t5_bucket(dist_c), -1).astype(jnp.int32)
    key = jnp.arange(tq_nsa)[:, None]
    qry = jnp.arange(tq_nsa)[None, :]
    bidx2t = jnp.stack([_t5_bucket(qry - key), _t5_bucket(tq_nsa + qry - key)]).astype(jnp.int32)
    starts = np.arange(n_cmp) * CMP_STRIDE
    ends = starts + CMP_LEN
    s0 = np.arange(n_slc) * SLC_LEN
    s1 = s0 + SLC_LEN
    ovl = np.clip(np.minimum(ends[:, None], s1[None]) - np.maximum(starts[:, None], s0[None]), 0, None) / CMP_LEN
    ovt = np.zeros((LANES, ng), np.float32)
    ovt[:n_slc, :n_cmp] = ovl.T
    emt = (np.arange(LANES)[None, :] == (np.arange(s) // SLC_LEN)[:, None]).astype(np.float32)
    inv = 1.0 / (ROPE_THETA ** (jnp.arange(0, ROPE_DIM, 2, dtype=F32) / ROPE_DIM))
    ang = jnp.arange(s, dtype=F32)[:, None] * inv[None, :]
    cos, sin = jnp.cos(ang), jnp.sin(ang)
    z16 = jnp.zeros((s, 16), F32)
    z32 = jnp.zeros((s, 32), F32)
    z64 = jnp.zeros((s, 64), F32)
    one64 = jnp.ones((s, 64), F32)
    cq_t = jnp.concatenate([one64, cos, cos, z32], axis=1)
    s1_t = jnp.concatenate([z64, -sin, z16, z32], axis=1)
    s2_t = jnp.concatenate([z64, z16, sin, z32], axis=1)
    ck_t = jnp.concatenate([z64, cos, cos, z32], axis=1)
    sk_t = jnp.concatenate([z64, -sin, sin, z32], axis=1)
    return dict(bidxct=bidxct, bidx2t=bidx2t, ovt=jnp.asarray(ovt, BF16), emt=jnp.asarray(emt, BF16),
                rope=(cq_t, s1_t, s2_t, ck_t, sk_t))


def _pad_cols(w, width):
    return jnp.pad(w, ((0, 0), (0, width - w.shape[1])))


def _layer_weights(l, w_in, conv_w, conv_b, nsa_q_norm, nsa_k_norm, cmp_pos, cmp_w1, cmp_w2, mla_q_a_norm,
                   mla_kv_norm, mla_wq_b, mla_wkv_b, mla_q_norm, mla_k_norm, out_norm_w, w_out, norm2_w,
                   ffn_w1, ffn_w2):
    wi = w_in[l]
    kr = wi[:, 1740:1772]
    z32 = jnp.zeros((D_MODEL, 32), F32)
    w_in_p = jnp.concatenate([
        wi[:, 0:1408],
        _pad_cols(wi[:, 1408:1420], LANES),
        _pad_cols(wi[:, 1420:1612], 2 * LANES),
        wi[:, 1612:1740],
        kr[:, 16:32], kr[:, 0:16], z32, kr, z32,
        wi[:, 1772:2540],
    ], axis=1).astype(BF16)
    w1 = cmp_w1[l].reshape(2, CMP_LEN, HEAD_DIM, CMP_HIDDEN)
    zw = jnp.zeros((CMP_LEN, HEAD_DIM, CMP_HIDDEN), F32)
    cw1 = jnp.concatenate([jnp.concatenate([w1[0], zw], axis=2), jnp.concatenate([zw, w1[1]], axis=2)],
                          axis=1).astype(BF16)
    zc = jnp.zeros((CMP_HIDDEN, HEAD_DIM), F32)
    cw2 = jnp.concatenate([jnp.concatenate([cmp_w2[l, 0], zc], axis=1),
                           jnp.concatenate([zc, cmp_w2[l, 1]], axis=1)], axis=0).astype(BF16)
    cpos = jnp.concatenate([cmp_pos[l, 0], cmp_pos[l, 1]], axis=1)
    kn = nsa_k_norm[l]
    ones64 = jnp.ones((HEAD_DIM,), F32)
    knw_c = jnp.concatenate([kn[0], ones64])[None, :]
    knw_sw = jnp.stack([jnp.concatenate([kn[1], ones64]), jnp.concatenate([kn[2], ones64])])
    qnw = jnp.tile(nsa_q_norm[l], N_HEADS)[None, :]
    wq = mla_wq_b[l].reshape(Q_LORA, N_HEADS, QK_DIM)
    wq = jnp.pad(wq, ((0, 2 * LANES - Q_LORA), (0, 0), (0, LANES - QK_DIM))).reshape(2 * LANES, N_HEADS * LANES)
    wkv = mla_wkv_b[l].reshape(KV_LORA, N_HEADS, 2 * HEAD_DIM)
    wk = jnp.pad(wkv[:, :, :HEAD_DIM], ((0, 0), (0, 0), (0, LANES - HEAD_DIM))).reshape(KV_LORA, N_HEADS * LANES)
    wvt = jnp.pad(wkv[:, :, HEAD_DIM:], ((0, 0), (0, 0), (0, DV_PAD - HEAD_DIM))).reshape(KV_LORA, -1).T
    return dict(
        w_in=w_in_p, cw1=cw1, cw2=cw2, cw2t=cw2.T, cpos=cpos, knw_c=knw_c, knw_sw=knw_sw, qnw=qnw,
        qaw=_pad_cols(mla_q_a_norm[l][None, :], 2 * LANES), kvw=mla_kv_norm[l][None, :],
        wq=wq.astype(BF16), wk=wk.astype(BF16), wvt=wvt.astype(BF16),
        mqn=_pad_cols(mla_q_norm[l][None, :], LANES), mkn=_pad_cols(mla_k_norm[l][None, :], LANES),
        cw=conv_w[l], cb=conv_b[l][None, :], onw=out_norm_w[l][None, :], wo=w_out[l].astype(BF16),
        n2w=norm2_w[l][None, :], w1=ffn_w1[l].astype(BF16), w2=ffn_w2[l].astype(BF16))


TM_PROJ = 512
TM_PREP = 512
TM_POST = 512
TQ_NSA = 256
TQ_MLA = 256
TQ_SB = 256


def kernel(x, rel_bias, norm1_w, w_in, conv_w, conv_b, nsa_q_norm, nsa_k_norm, cmp_pos, cmp_w1, cmp_w2,
           mla_q_a_norm, mla_kv_norm, mla_wq_b, mla_wkv_b, mla_q_norm, mla_k_norm, out_norm_w, w_out, norm2_w,
           ffn_w1, ffn_w2):
    b, s, d = x.shape
    depth = w_in.shape[0]
    tabs = _tables(s, TQ_NSA)
    x2d = x.reshape(b * s, d)
    for l in range(depth):
        w = _layer_weights(l, w_in, conv_w, conv_b, nsa_q_norm, nsa_k_norm, cmp_pos, cmp_w1, cmp_w2,
                           mla_q_a_norm, mla_kv_norm, mla_wq_b, mla_wkv_b, mla_q_norm, mla_k_norm, out_norm_w,
                           w_out, norm2_w, ffn_w1, ffn_w2)
        proj = _inproj(x2d, norm1_w[l][None, :], w["w_in"], TM_PROJ)
        kcvc, kcvct = _compress(proj, w["cpos"], w["cw1"], w["cw2"], w["cw2t"], w["knw_c"], b, s)
        yb = _nsa_attention(proj, kcvc, kcvct, rel_bias, tabs["bidxct"], tabs["bidx2t"], tabs["ovt"], tabs["emt"],
                            w["qnw"], w["knw_sw"], b, s, TQ_NSA)
        qm, km, vmt = _mla_prep(proj, w["qaw"], w["kvw"], w["wq"], w["wk"], w["wvt"], w["mqn"], w["mkn"],
                                tabs["rope"], s, TM_PREP)
        yc = _mla_attention(qm, km, vmt, b, s, TQ_MLA)
        yd = _sb_attention(proj, b, s, TQ_SB)
        x2d = _post(proj, yb, yc, yd, x2d, w["cw"], w["cb"], w["onw"], w["wo"], w["n2w"], w["w1"], w["w2"],
                    s, TM_POST)
    return x2d.reshape(b, s, d)
```

```python
import functools
import math

import jax
import jax.numpy as jnp
import numpy as np
from jax import lax
from jax.experimental import pallas as pl
from jax.experimental.pallas import tpu as pltpu

F32 = jnp.float32
BF16 = jnp.bfloat16

D_MODEL = 1024
GROUP_WIDTH = 256
HEAD_DIM = 64
N_HEADS = 4
LANES = 128
CMP_LEN = 32
CMP_STRIDE = 16
SLC_LEN = 64
N_SEL = 16
WINDOW = 512
CMP_HIDDEN = 256
Q_LORA = 192
KV_LORA = 128
ROPE_DIM = 32
QK_DIM = 96
ROPE_THETA = 10000.0
N_BUCKETS = 32
MAX_DISTANCE = 128
D_FF = 4096
EPS = 1e-6
NEG = -1e30
LOG2E = math.log2(math.e)
DV_PAD = 80

NP = 2816
CB_NQ = 3
CB_KCVC = 8
CB_KSVS = 9
CB_KWVW = 10
CB_GATE = 11
CB_CQ = 6
CB_CKV = 14
CB_KR = 15
CB_SQ = 8
CB_SK = 9
CB_SV = 10

VMEM_LIMIT = 56 * 1024 * 1024

NT_DIMS = (((1,), (1,)), ((), ()))


def _params(sem):
    return pltpu.CompilerParams(dimension_semantics=sem, vmem_limit_bytes=VMEM_LIMIT)


def _nt(a, b):
    return lax.dot_general(a, b, NT_DIMS, preferred_element_type=F32)


def _dot(a, b):
    return jnp.dot(a, b, preferred_element_type=F32)


def _lane(shape):
    return lax.broadcasted_iota(jnp.int32, shape, len(shape) - 1)


def _row(shape):
    return lax.broadcasted_iota(jnp.int32, shape, len(shape) - 2)


def _lane_tile(j, width):
    return pl.ds(pl.multiple_of(j * width, width), width)


def _inproj_kernel(x_ref, nw_ref, w_ref, o_ref):
    x = x_ref[...]
    ms = jnp.mean(x * x, axis=-1, keepdims=True)
    h = (x * lax.rsqrt(ms + EPS) * nw_ref[...]).astype(BF16)
    o_ref[...] = _dot(h, w_ref[...])


def _inproj(x2d, nw, w, tm):
    t = x2d.shape[0]
    return pl.pallas_call(
        _inproj_kernel,
        grid=(t // tm,),
        in_specs=[
            pl.BlockSpec((tm, D_MODEL), lambda i: (i, 0)),
            pl.BlockSpec((1, D_MODEL), lambda i: (0, 0)),
            pl.BlockSpec((D_MODEL, NP), lambda i: (0, 0), pipeline_mode=pl.Buffered(1)),
        ],
        out_specs=pl.BlockSpec((tm, NP), lambda i: (i, 0)),
        out_shape=jax.ShapeDtypeStruct((t, NP), F32),
        compiler_params=_params(("arbitrary",)),
        name="inproj",
    )(x2d, nw, w)


def _compress_kernel(x_ref, pos_ref, w1_ref, w2_ref, w2t_ref, knw_ref, o_ref, ot_ref):
    ng = x_ref.shape[1]
    acc_a = jnp.zeros((ng, 2 * CMP_HIDDEN), F32)
    acc_b = jnp.zeros((ng, 2 * CMP_HIDDEN), F32)
    for i in range(CMP_STRIDE):
        x = x_ref[0, :, i, :]
        xa = (x + pos_ref[i:i + 1, :]).astype(BF16)
        xb = (x + pos_ref[CMP_STRIDE + i:CMP_STRIDE + i + 1, :]).astype(BF16)
        acc_a += _dot(xa, w1_ref[i])
        acc_b += _dot(xb, w1_ref[CMP_STRIDE + i])
    pre = acc_a + pltpu.roll(acc_b, ng - 1, 0)
    hdn = (pre * jax.nn.sigmoid(pre)).astype(BF16)
    out = _dot(hdn, w2_ref[...])
    lane = _lane(out.shape)
    is_k = lane < HEAD_DIM
    ss = jnp.sum(jnp.where(is_k, out * out, 0.0), axis=-1, keepdims=True) * (1.0 / HEAD_DIM)
    o_ref[0] = jnp.where(is_k, out * lax.rsqrt(ss + EPS) * knw_ref[...], out)
    ot_ref[0] = _nt(w2t_ref[...], hdn)


def _compress(proj, pos, w1, w2, w2t, knw, b, s):
    ng = s // CMP_STRIDE
    x4 = proj.reshape(b, ng, CMP_STRIDE, NP)
    return pl.pallas_call(
        _compress_kernel,
        grid=(b,),
        in_specs=[
            pl.BlockSpec((1, ng, CMP_STRIDE, LANES), lambda i: (i, 0, 0, CB_KCVC)),
            pl.BlockSpec((CMP_LEN, LANES), lambda i: (0, 0)),
            pl.BlockSpec((CMP_LEN, LANES, 2 * CMP_HIDDEN), lambda i: (0, 0, 0)),
            pl.BlockSpec((2 * CMP_HIDDEN, LANES), lambda i: (0, 0)),
            pl.BlockSpec((LANES, 2 * CMP_HIDDEN), lambda i: (0, 0)),
            pl.BlockSpec((1, LANES), lambda i: (0, 0)),
        ],
        out_specs=[pl.BlockSpec((1, ng, LANES), lambda i: (i, 0, 0)),
                   pl.BlockSpec((1, LANES, ng), lambda i: (i, 0, 0))],
        out_shape=[jax.ShapeDtypeStruct((b, ng, LANES), F32), jax.ShapeDtypeStruct((b, LANES, ng), F32)],
        compiler_params=_params(("arbitrary",)),
        name="nsa_compress",
    )(x4, pos, w1, w2, w2t, knw)


def _softmax_update(sts, vts, m_old, acc_old):
    m_new = [jnp.maximum(m, jnp.max(st, axis=0, keepdims=True)) for m, st in zip(m_old, sts)]
    ps = [jnp.exp2(st - m).astype(BF16) for st, m in zip(sts, m_new)]
    alphas = [jnp.exp2(mo - mn) for mo, mn in zip(m_old, m_new)]
    acc_new = [al * acc + _dot(vt, p) for al, acc, vt, p in zip(alphas, acc_old, vts, ps)]
    return m_new, acc_new


def _softmax_steps(sts, vts, m_refs, acc_refs):
    m_new, acc_new = _softmax_update(sts, vts, [r[...] for r in m_refs], [r[...] for r in acc_refs])
    for r, v in zip(m_refs, m_new):
        r[...] = v
    for r, v in zip(acc_refs, acc_new):
        r[...] = v


def _with_ones_row(vt):
    pad = jnp.where(_row((DV_PAD - HEAD_DIM, vt.shape[1])) == 0, 1.0, 0.0).astype(vt.dtype)
    return jnp.concatenate([vt, pad], axis=0)


def _bucket_bias(bidx, relb_ref, h, fill):
    acc = jnp.full(bidx.shape, fill, F32)
    for bk in range(N_BUCKETS):
        acc = jnp.where(bidx == bk, relb_ref[bk, h] * LOG2E, acc)
    return acc


def _group_sums(x, member):
    g = jnp.where(member, 1.0, 0.0).astype(BF16)
    hi = x.astype(BF16)
    lo = (x - hi.astype(F32)).astype(BF16)
    return _dot(hi, g) + _dot(lo, g)


def _dup_low_half(x):
    y = jnp.where(_lane(x.shape) < HEAD_DIM, x, 0.0)
    return y + pltpu.roll(y, HEAD_DIM, 1)


def _nsa_kernel(relb_ref, q_ref, g_ref, kcvc_ref, kcvct_ref, ksvs_ref, kwvw_ref, bidxct_ref, bidx2t_ref, ovt_ref,
                emt_ref, qnw_ref, knw_ref, o_ref,
                biasc_s, bias2_s, ks_s, vst_s, kw_s, vwt_s, kc_s, vct_s, q_s, m_s, acc_s, *, tq, s_len):
    b = pl.program_id(0)
    i = pl.program_id(1)
    n_win = WINDOW // tq
    ng = s_len // CMP_STRIDE

    @pl.when((b == 0) & (i == 0))
    def _build_bias_tables():
        key = _row((tq, tq))
        qry = _lane((tq, tq))
        for h in range(N_HEADS):
            cols = slice(h * tq, (h + 1) * tq)
            far = jnp.full((tq, tq), relb_ref[N_BUCKETS - 1, h] * LOG2E, F32)
            bias2_s[0, :, cols] = jnp.where(key <= qry, _bucket_bias(bidx2t_ref[0], relb_ref, h, NEG), NEG)
            bias2_s[1, :, cols] = _bucket_bias(bidx2t_ref[1], relb_ref, h, NEG)
            bias2_s[2, :, cols] = far
            bias2_s[3, :, cols] = jnp.where(key > qry, far, NEG)
            bias2_s[4, :, cols] = jnp.full((tq, tq), NEG, F32)

        def body(t, carry):
            bi = bidxct_ref[:, _lane_tile(t, tq)]
            for h in range(N_HEADS):
                biasc_s[t, :, h * tq:(h + 1) * tq] = _bucket_bias(bi, relb_ref, h, NEG)
            return carry

        lax.fori_loop(0, s_len // tq, body, 0)

    @pl.when(i == 0)
    def _prep_kv():
        ch = 256

        def body(t, carry):
            rows = pl.ds(pl.multiple_of(t * ch, ch), ch)
            for src, kdst, vdst, widx in ((ksvs_ref, ks_s, vst_s, 0), (kwvw_ref, kw_s, vwt_s, 1)):
                x = src[rows, :]
                ss = _group_sums(x * x, _row((LANES, LANES)) < HEAD_DIM) * (1.0 / HEAD_DIM)
                kn = x * lax.rsqrt(ss + EPS) * knw_ref[widx:widx + 1, :]
                kdst[rows, :] = _dup_low_half(kn).astype(BF16)
                vdst[:, _lane_tile(t, ch)] = _with_ones_row(x.T[HEAD_DIM:, :]).astype(BF16)
            return carry

        lax.fori_loop(0, s_len // ch, body, 0)
        kc_s[...] = _dup_low_half(kcvc_ref[0]).astype(BF16)
        vct_s[...] = kcvct_ref[0][HEAD_DIM:, :].astype(BF16)

    q = q_ref[...]
    lane = _lane((tq, LANES))
    heads = range(N_HEADS)
    gw = 2 * LANES
    head_shift = HEAD_DIM.bit_length() - 1
    same_head = (lax.shift_right_logical(_row((gw, gw)), head_shift)
                 == lax.shift_right_logical(_lane((gw, gw)), head_shift))
    ss = _group_sums(q * q, same_head) * (1.0 / HEAD_DIM)
    qn = q * lax.rsqrt(ss + EPS) * qnw_ref[...] * (HEAD_DIM ** -0.5 * LOG2E)
    for h in heads:
        mine = (lane < HEAD_DIM) if h % 2 == 0 else (lane >= HEAD_DIM)
        q_s[h] = jnp.where(mine, qn[:, LANES * (h // 2):LANES * (h // 2 + 1)], 0.0).astype(BF16)

    lcs = [_nt(kc_s[...], q_s[h]) + biasc_s[i, :, h * tq:(h + 1) * tq] for h in heads]
    pcs = [jnp.where(lc > 0.5 * NEG, jnp.exp2(lc - jnp.max(lc, axis=0, keepdims=True)), 0.0) for lc in lcs]
    dens = [jnp.sum(pc, axis=0, keepdims=True) for pc in pcs]
    pcs = [pc / jnp.where(den > 0.0, den, 1.0) for pc, den in zip(pcs, dens)]
    o_cmp = [_dot(vct_s[...], pc.astype(BF16)) for pc in pcs]
    psum = (pcs[0] + pcs[1]) + (pcs[2] + pcs[3])
    p_hi = psum.astype(BF16)
    p_lo = (psum - p_hi.astype(F32)).astype(BF16)
    score = _dot(ovt_ref[...], p_hi) + _dot(ovt_ref[...], p_lo)

    n_slc = s_len // SLC_LEN
    blk = _row((n_slc, tq))
    tpos = i * tq + _lane((n_slc, tq))
    tblk = tpos // SLC_LEN
    valid = blk * SLC_LEN <= tpos
    forced = (blk == 0) | (blk == tblk) | (blk == tblk - 1)
    sc = jnp.where(forced, jnp.inf, jnp.where(valid, score[0:n_slc], -jnp.inf))
    rank = jnp.zeros((n_slc, tq), F32)
    for k in range(n_slc):
        ck = sc[k:k + 1, :]
        beats = (ck > sc) | ((ck == sc) & (blk > k))
        rank += jnp.where(beats, 1.0, 0.0)
    pen = jnp.where((rank < float(min(N_SEL, n_slc))) & valid, 0.0, NEG)
    pen = jnp.concatenate([pen, jnp.zeros((LANES - n_slc, tq), F32)], axis=0).astype(BF16)

    ms = [jnp.full((1, tq), NEG, F32)] * (2 * N_HEADS)
    accs = [jnp.zeros((DV_PAD, tq), F32)] * (2 * N_HEADS)
    for jj in range(n_win + 1):
        exists = i >= jj
        rows = pl.ds(pl.multiple_of(jnp.maximum(i - jj, 0) * tq, tq), tq)
        kind = jnp.where(exists, min(jj, 2), 4)
        kind_w = jnp.where(exists, 3 if jj == n_win else min(jj, 2), 4)
        ks = ks_s[rows, :]
        kw = kw_s[rows, :]
        masked = _dot(emt_ref[rows, :], pen)
        sts = [_nt(ks, q_s[h]) + masked + bias2_s[kind, :, h * tq:(h + 1) * tq] for h in heads]
        sts += [_nt(kw, q_s[h]) + bias2_s[kind_w, :, h * tq:(h + 1) * tq] for h in heads]
        vts = [vst_s[:, rows]] * N_HEADS + [vwt_s[:, rows]] * N_HEADS
        ms, accs = _softmax_update(sts, vts, ms, accs)
    for h in heads:
        m_s[h] = ms[h]
        acc_s[h] = accs[h]
    o_win = [accs[N_HEADS + h][0:HEAD_DIM, :] / accs[N_HEADS + h][HEAD_DIM:HEAD_DIM + 1, :] for h in heads]

    def far_step(j, width):
        rows = pl.ds(j * tq if isinstance(j, int) else pl.multiple_of(j * tq, tq), width * tq)
        ks = ks_s[rows, :]
        masked = _dot(emt_ref[rows, :], pen)
        sts = [_nt(ks, q_s[h]) + (masked + relb_ref[N_BUCKETS - 1, h] * LOG2E) for h in heads]
        _softmax_steps(sts, [vst_s[:, rows]] * N_HEADS, [m_s.at[h] for h in heads], [acc_s.at[h] for h in heads])

    n_far = jnp.maximum(i - n_win, 0)

    def far_body(p, carry):
        far_step(n_far - 2 * (p + 1), 2)
        return carry

    lax.fori_loop(0, n_far // 2, far_body, 0)
    pl.when(n_far % 2 == 1)(functools.partial(far_step, 0, 1))

    gt = jax.nn.sigmoid(g_ref[...]).T
    ys = []
    for h in heads:
        o_slc = acc_s[h, 0:HEAD_DIM, :] / acc_s[h, HEAD_DIM:HEAD_DIM + 1, :]
        ys.append(gt[3 * h:3 * h + 1, :] * o_cmp[h] + gt[3 * h + 1:3 * h + 2, :] * o_slc
                  + gt[3 * h + 2:3 * h + 3, :] * o_win[h])
    o_ref[...] = jnp.concatenate(ys, axis=0).T


def _nsa_attention(proj, kcvc, kcvct, rel_bias, bidxct, bidx2t, ovt, emt, qnw, knw, b, s, tq):
    t = b * s
    nq = s // tq
    m_rows = N_HEADS * tq
    ng = s // CMP_STRIDE
    kern = functools.partial(_nsa_kernel, tq=tq, s_len=s)
    return pl.pallas_call(
        kern,
        grid=(b, nq),
        in_specs=[
            pl.BlockSpec(memory_space=pltpu.SMEM),
            pl.BlockSpec((tq, 2 * LANES), lambda bi, i: (bi * nq + i, CB_NQ)),
            pl.BlockSpec((tq, LANES), lambda bi, i: (bi * nq + i, CB_GATE)),
            pl.BlockSpec((1, ng, LANES), lambda bi, i: (bi, 0, 0)),
            pl.BlockSpec((1, LANES, ng), lambda bi, i: (bi, 0, 0)),
            pl.BlockSpec((s, LANES), lambda bi, i: (bi, CB_KSVS)),
            pl.BlockSpec((s, LANES), lambda bi, i: (bi, CB_KWVW)),
            pl.BlockSpec((ng, s), lambda bi, i: (0, 0)),
            pl.BlockSpec((2, tq, tq), lambda bi, i: (0, 0, 0)),
            pl.BlockSpec((LANES, ng), lambda bi, i: (0, 0)),
            pl.BlockSpec((s, LANES), lambda bi, i: (0, 0)),
            pl.BlockSpec((1, 2 * LANES), lambda bi, i: (0, 0)),
            pl.BlockSpec((2, LANES), lambda bi, i: (0, 0)),
        ],
        out_specs=pl.BlockSpec((tq, 2 * LANES), lambda bi, i: (bi * nq + i, 0)),
        out_shape=jax.ShapeDtypeStruct((t, GROUP_WIDTH), F32),
        scratch_shapes=[
            pltpu.VMEM((nq, ng, m_rows), F32),
            pltpu.VMEM((5, tq, m_rows), F32),
            pltpu.VMEM((s, LANES), BF16),
            pltpu.VMEM((DV_PAD, s), BF16),
            pltpu.VMEM((s, LANES), BF16),
            pltpu.VMEM((DV_PAD, s), BF16),
            pltpu.VMEM((ng, LANES), BF16),
            pltpu.VMEM((HEAD_DIM, ng), BF16),
            pltpu.VMEM((N_HEADS, tq, LANES), BF16),
            pltpu.VMEM((N_HEADS, 1, tq), F32),
            pltpu.VMEM((N_HEADS, DV_PAD, tq), F32),
        ],
        compiler_params=_params(("arbitrary", "arbitrary")),
        name="nsa_attention",
    )(rel_bias, proj, proj, kcvc, kcvct, proj, proj, bidxct, bidx2t, ovt, emt, qnw, knw)


def _mla_prep_kernel(cq_ref, ckv_ref, kr_ref, qaw_ref, kvw_ref, wq_ref, wk_ref, wvt_ref, qnw_ref, knw_ref,
                     cq_t_ref, s1_t_ref, s2_t_ref, ck_t_ref, sk_t_ref, qo_ref, ko_ref, vto_ref):
    cq = cq_ref[...]
    ms = jnp.sum(cq * cq, axis=-1, keepdims=True) * (1.0 / Q_LORA)
    hq = (cq * lax.rsqrt(ms + EPS) * qaw_ref[...]).astype(BF16)
    qf = _dot(hq, wq_ref[...])
    ckv = ckv_ref[...]
    ms = jnp.mean(ckv * ckv, axis=-1, keepdims=True)
    hkv = (ckv * lax.rsqrt(ms + EPS) * kvw_ref[...]).astype(BF16)
    kf = _dot(hkv, wk_ref[...])
    vt = _nt(wvt_ref[...], hkv)
    vto_ref[...] = jnp.where(_row(vt.shape) % DV_PAD == HEAD_DIM, 1.0, vt).astype(BF16)
    krb = kr_ref[...]
    kr_rot = krb * ck_t_ref[...] + pltpu.roll(krb, HEAD_DIM, 1) * sk_t_ref[...]
    scale = QK_DIM ** -0.5 * LOG2E
    for h in range(N_HEADS):
        cols = slice(LANES * h, LANES * (h + 1))
        x = qf[:, cols]
        x = x * cq_t_ref[...] + pltpu.roll(x, LANES - 16, 1) * s1_t_ref[...] + pltpu.roll(x, 16, 1) * s2_t_ref[...]
        ss = jnp.sum(x * x, axis=-1, keepdims=True) * (1.0 / QK_DIM)
        qo_ref[:, cols] = (x * lax.rsqrt(ss + EPS) * qnw_ref[...] * scale).astype(BF16)
        k = kf[:, cols] + kr_rot
        ss = jnp.sum(k * k, axis=-1, keepdims=True) * (1.0 / QK_DIM)
        ko_ref[:, cols] = (k * lax.rsqrt(ss + EPS) * knw_ref[...]).astype(BF16)


def _mla_prep(proj, qaw, kvw, wq, wk, wvt, qnw, knw, tabs, s, tm):
    t = proj.shape[0]
    npos = s // tm
    row = lambda i: (i, 0)
    const = lambda i: (0, 0)
    tab = pl.BlockSpec((tm, LANES), lambda i: (i % npos, 0))
    out = jax.ShapeDtypeStruct((t, N_HEADS * LANES), BF16)
    return pl.pallas_call(
        _mla_prep_kernel,
        grid=(t // tm,),
        in_specs=[
            pl.BlockSpec((tm, 2 * LANES), lambda i: (i, CB_CQ)),
            pl.BlockSpec((tm, LANES), lambda i: (i, CB_CKV)),
            pl.BlockSpec((tm, LANES), lambda i: (i, CB_KR)),
            pl.BlockSpec((1, 2 * LANES), const),
            pl.BlockSpec((1, LANES), const),
            pl.BlockSpec((2 * LANES, N_HEADS * LANES), const),
            pl.BlockSpec((LANES, N_HEADS * LANES), const),
            pl.BlockSpec((N_HEADS * DV_PAD, LANES), const),
            pl.BlockSpec((1, LANES), const),
            pl.BlockSpec((1, LANES), const),
            tab, tab, tab, tab, tab,
        ],
        out_specs=[pl.BlockSpec((tm, N_HEADS * LANES), row), pl.BlockSpec((tm, N_HEADS * LANES), row),
                   pl.BlockSpec((N_HEADS * DV_PAD, tm), lambda i: (0, i))],
        out_shape=[out, out, jax.ShapeDtypeStruct((N_HEADS * DV_PAD, t), BF16)],
        compiler_params=_params(("arbitrary",)),
        name="mla_prep",
    )(proj, proj, proj, qaw, kvw, wq, wk, wvt, qnw, knw, *tabs)


def _mla_attn_kernel(q_ref, k_ref, vt_ref, o_ref, m_s, acc_s, *, tq):
    i = pl.program_id(1)
    causal = _row((tq, tq)) <= _lane((tq, tq))
    for h in range(N_HEADS):
        m_s[h] = jnp.full((1, tq), NEG, F32)
        acc_s[h] = jnp.zeros((DV_PAD, tq), F32)

    def step(j, width, masked):
        tk = width * tq
        rows = pl.ds(j * tq if isinstance(j, int) else pl.multiple_of(j * tq, tq), tk)
        sts = []
        for h in range(N_HEADS):
            cols = slice(LANES * h, LANES * (h + 1))
            st = _nt(k_ref[rows, cols], q_ref[:, cols])
            sts.append(jnp.where(causal, st, NEG) if masked else st)
        vts = [vt_ref[DV_PAD * h:DV_PAD * (h + 1), rows] for h in range(N_HEADS)]
        _softmax_steps(sts, vts, [m_s.at[h] for h in range(N_HEADS)], [acc_s.at[h] for h in range(N_HEADS)])

    step(i, 1, True)

    def body(jj, carry):
        step(i - 2 * jj, 2, False)
        return carry

    lax.fori_loop(1, i // 2 + 1, body, 0)
    pl.when(i % 2 == 1)(functools.partial(step, 0, 1, False))
    yt = jnp.concatenate([acc_s[h, 0:HEAD_DIM, :] / acc_s[h, HEAD_DIM:HEAD_DIM + 1, :] for h in range(N_HEADS)],
                         axis=0)
    o_ref[...] = yt.T


def _mla_attention(qm, km, vmt, b, s, tq):
    t = b * s
    nq = s // tq
    w = N_HEADS * LANES
    return pl.pallas_call(
        functools.partial(_mla_attn_kernel, tq=tq),
        grid=(b, nq),
        in_specs=[
            pl.BlockSpec((tq, w), lambda bi, i: (bi * nq + i, 0)),
            pl.BlockSpec((s, w), lambda bi, i: (bi, 0)),
            pl.BlockSpec((N_HEADS * DV_PAD, s), lambda bi, i: (0, bi)),
        ],
        out_specs=pl.BlockSpec((tq, 2 * LANES), lambda bi, i: (bi * nq + i, 0)),
        out_shape=jax.ShapeDtypeStruct((t, GROUP_WIDTH), F32),
        scratch_shapes=[pltpu.VMEM((N_HEADS, 1, tq), F32), pltpu.VMEM((N_HEADS, DV_PAD, tq), F32)],
        compiler_params=_params(("arbitrary", "arbitrary")),
        name="mla_attention",
    )(qm, km, vmt)


def _sb_kernel(q_ref, k_ref, v_ref, o_ref, kb_s, vt_s, q_s, r_s, acc_s, kmax_s, *, tq, s_len):
    i = pl.program_id(1)
    gw = 2 * LANES
    head_shift = HEAD_DIM.bit_length() - 1
    same_head = (lax.shift_right_logical(_row((gw, gw)), head_shift)
                 == lax.shift_right_logical(_lane((gw, gw)), head_shift))

    @pl.when(i == 0)
    def _cast_kv():
        ch = 256

        def body(t, kmax):
            rows = pl.ds(pl.multiple_of(t * ch, ch), ch)
            k = k_ref[rows, :]
            kb_s[rows, :] = k.astype(BF16)
            vt_s[:, _lane_tile(t, ch)] = v_ref[rows, :].T.astype(BF16)
            return jnp.maximum(kmax, _group_sums(k * k, same_head))

        kmax = lax.fori_loop(0, s_len // ch, body, jnp.zeros((ch, gw), F32))
        kmax_s[0] = jnp.max(kmax)

    key = _row((tq, tq))
    qry = _lane((tq, tq))
    strict = key < qry
    tri = jnp.where(key <= qry, 1.0, 0.0).astype(BF16)
    tri2 = jnp.concatenate([tri, tri], axis=1)
    lane = _lane((tq, LANES))
    q = q_ref[...] * (HEAD_DIM ** -0.5 * LOG2E)
    for h in range(N_HEADS):
        mine = (lane < HEAD_DIM) if h % 2 == 0 else (lane >= HEAD_DIM)
        q_s[h] = jnp.where(mine, q[:, LANES * (h // 2):LANES * (h // 2 + 1)], 0.0).astype(BF16)
        r_s[h] = jnp.zeros((1, tq), F32)
        acc_s[h] = jnp.zeros((HEAD_DIM, tq), F32)
    qmax = jnp.max(_group_sums(q * q, same_head))
    z_bound = jnp.sqrt(jnp.full((1, tq), qmax * kmax_s[0], F32)) * 1.01 + 1.0

    heads = range(N_HEADS)

    def step(j, width, masked):
        rs = [r_s[h] for h in heads]
        accs = [acc_s[h] for h in heads]
        tiles = [j + width - 1 - w for w in range(width)]
        rows = [pl.ds(t * tq if isinstance(t, int) else pl.multiple_of(t * tq, tq), tq) for t in tiles]
        zs = [[_nt(kb_s[r, LANES * (h // 2):LANES * (h // 2 + 1)], q_s[h]) for h in heads] for r in rows]
        part = []
        for zt in zs:
            negabs = [pltpu.bitcast(pltpu.bitcast(z, jnp.uint32) | jnp.uint32(0x80000000), F32) for z in zt]
            sps = [jnp.maximum(z, 0.0) + jnp.log2(1.0 + jnp.exp2(na)) for z, na in zip(zt, negabs)]
            if masked:
                sps = [jnp.where(strict, sp, 0.0) for sp in sps]
            his = [sp.astype(BF16) for sp in sps]
            los = [(sp - hi.astype(F32)).astype(BF16) for sp, hi in zip(sps, his)]
            part.append([_dot(tri2, jnp.concatenate([hi, lo], axis=0)) for hi, lo in zip(his, los)])
        for r, zt, pt in zip(rows, zs, part):
            csums = [p + rc for p, rc in zip(pt, rs)]
            als = [jnp.exp2(z - cs) for z, cs in zip(zt, csums)]
            if masked:
                als = [jnp.where(strict, a, 0.0) for a in als]
            accs = [acc + _dot(vt_s[HEAD_DIM * h:HEAD_DIM * (h + 1), r], als[h].astype(BF16))
                    for h, acc in zip(heads, accs)]
            rs = [cs[0:1, :] for cs in csums]
        for h in heads:
            acc_s[h] = accs[h]
            r_s[h] = rs[h]

    def live():
        r_min = jnp.minimum(jnp.minimum(r_s[0], r_s[1]), jnp.minimum(r_s[2], r_s[3]))
        return jnp.max(z_bound - r_min) >= -150.0

    step(i, 1, True)
    pl.when(i >= 1)(functools.partial(step, i - 1, 1, False))

    n_rest = jnp.maximum(i - 1, 0)
    n_pairs = n_rest // 2

    def cond(carry):
        p, alive = carry
        return (p < n_pairs) & alive

    def body(carry):
        p, _ = carry
        step(i - 1 - 2 * (p + 1), 2, False)
        return p + 1, live()

    _, alive = lax.while_loop(cond, body, (jnp.int32(0), live()))
    pl.when((n_rest % 2 == 1) & alive)(functools.partial(step, 0, 1, False))
    o_ref[...] = jnp.concatenate([acc_s[h] for h in range(N_HEADS)], axis=0).T


def _sb_attention(proj, b, s, tq):
    t = b * s
    nq = s // tq
    w = 2 * LANES
    return pl.pallas_call(
        functools.partial(_sb_kernel, tq=tq, s_len=s),
        grid=(b, nq),
        in_specs=[
            pl.BlockSpec((tq, w), lambda bi, i: (bi * nq + i, CB_SQ)),
            pl.BlockSpec((s, w), lambda bi, i: (bi, CB_SK)),
            pl.BlockSpec((s, w), lambda bi, i: (bi, CB_SV)),
        ],
        out_specs=pl.BlockSpec((tq, w), lambda bi, i: (bi * nq + i, 0)),
        out_shape=jax.ShapeDtypeStruct((t, GROUP_WIDTH), F32),
        scratch_shapes=[pltpu.VMEM((s, w), BF16), pltpu.VMEM((w, s), BF16),
                        pltpu.VMEM((N_HEADS, tq, LANES), BF16),
                        pltpu.VMEM((N_HEADS, 1, tq), F32), pltpu.VMEM((N_HEADS, HEAD_DIM, tq), F32),
                        pltpu.SMEM((1,), F32)],
        compiler_params=_params(("arbitrary", "arbitrary")),
        name="sb_attention",
    )(proj, proj, proj)


def _post_kernel(a_ref, ap_ref, yb_ref, yc_ref, yd_ref, x_ref, cw_ref, cb_ref, onw_ref, wo_ref, n2w_ref,
                 w1_ref, w2_ref, o_ref, *, tm, s_len, ffc):
    i = pl.program_id(0)
    a = a_ref[...]
    gw = GROUP_WIDTH
    v = a[:, gw:2 * gw] * a[:, 2 * gw:3 * gw]
    ap = ap_ref[...]
    first = (i * tm) % s_len == 0
    vp = jnp.where(first, 0.0, ap[:, gw:2 * gw] * ap[:, 2 * gw:3 * gw])
    row = _row(v.shape)
    v1 = jnp.where(row == 0, vp[7:8, :], pltpu.roll(v, 1, 0))
    v2 = jnp.where(row == 0, vp[6:7, :], jnp.where(row == 1, vp[7:8, :], pltpu.roll(v, 2, 0)))
    conv = cw_ref[0:1, :] * v2 + cw_ref[1:2, :] * v1 + cw_ref[2:3, :] * v
    ya = a[:, 0:gw] * (conv + cb_ref[...])

    mix = None
    for g, y in enumerate((ya, yb_ref[...], yc_ref[...], yd_ref[...])):
        ms = jnp.mean(y * y, axis=-1, keepdims=True)
        yn = (y * lax.rsqrt(ms + EPS) * onw_ref[:, gw * g:gw * (g + 1)]).astype(BF16)
        part = _dot(yn, wo_ref[gw * g:gw * (g + 1), :])
        mix = part if mix is None else mix + part
    x1 = x_ref[...] + mix

    ms = jnp.mean(x1 * x1, axis=-1, keepdims=True)
    h2 = (x1 * lax.rsqrt(ms + EPS) * n2w_ref[...]).astype(BF16)
    ff = None
    for cidx in range(D_FF // ffc):
        u = _dot(h2, w1_ref[:, ffc * cidx:ffc * (cidx + 1)])
        u = jnp.square(jnp.maximum(u, 0.0)).astype(BF16)
        part = _dot(u, w2_ref[ffc * cidx:ffc * (cidx + 1), :])
        ff = part if ff is None else ff + part
    o_ref[...] = x1 + ff


def _post(proj, yb, yc, yd, x2d, cw, cb, onw, wo, n2w, w1, w2, s, tm):
    t = x2d.shape[0]
    gw = GROUP_WIDTH
    row = lambda i: (i, 0)
    const = lambda i: (0, 0)
    once = pl.Buffered(1)
    kern = functools.partial(_post_kernel, tm=tm, s_len=s, ffc=1024)
    return pl.pallas_call(
        kern,
        grid=(t // tm,),
        in_specs=[
            pl.BlockSpec((tm, 3 * gw), row),
            pl.BlockSpec((8, 3 * gw), lambda i: (jnp.maximum(i * (tm // 8) - 1, 0), 0)),
            pl.BlockSpec((tm, gw), row),
            pl.BlockSpec((tm, gw), row),
            pl.BlockSpec((tm, gw), row),
            pl.BlockSpec((tm, D_MODEL), row),
            pl.BlockSpec((3, gw), const),
            pl.BlockSpec((1, gw), const),
            pl.BlockSpec((1, D_MODEL), const),
            pl.BlockSpec((D_MODEL, D_MODEL), const, pipeline_mode=once),
            pl.BlockSpec((1, D_MODEL), const),
            pl.BlockSpec((D_MODEL, D_FF), const, pipeline_mode=once),
            pl.BlockSpec((D_FF, D_MODEL), const, pipeline_mode=once),
        ],
        out_specs=pl.BlockSpec((tm, D_MODEL), row),
        out_shape=jax.ShapeDtypeStruct((t, D_MODEL), F32),
        compiler_params=_params(("arbitrary",)),
        name="post",
    )(proj, proj, yb, yc, yd, x2d, cw, cb, onw, wo, n2w, w1, w2)


def _t5_bucket(dist):
    max_exact = N_BUCKETS // 2
    d = jnp.maximum(dist, 0)
    large = max_exact + (jnp.log(jnp.maximum(d, 1).astype(F32) / max_exact)
                         / math.log(MAX_DISTANCE / max_exact) * (N_BUCKETS - max_exact)).astype(jnp.int32)
    large = jnp.minimum(large, N_BUCKETS - 1)
    return jnp.where(d < max_exact, d, large)


def _tables(s, tq_nsa):
    n_cmp = (s - CMP_LEN) // CMP_STRIDE + 1
    ng = s // CMP_STRIDE
    n_slc = s // SLC_LEN
    tpos = jnp.arange(s)[None, :]
    n = jnp.arange(ng)[:, None]
    dist_c = tpos - (n * CMP_STRIDE + CMP_LEN - 1)
    bidxct = jnp.where((dist_c >= 0) & (n < n_cmp), _t5_bucket(dist_c), -1).astype(jnp.int32)
    key = jnp.arange(tq_nsa)[:, None]
    qry = jnp.arange(tq_nsa)[None, :]
    bidx2t = jnp.stack([_t5_bucket(qry - key), _t5_bucket(tq_nsa + qry - key)]).astype(jnp.int32)
    starts = np.arange(n_cmp) * CMP_STRIDE
    ends = starts + CMP_LEN
    s0 = np.arange(n_slc) * SLC_LEN
    s1 = s0 + SLC_LEN
    ovl = np.clip(np.minimum(ends[:, None], s1[None]) - np.maximum(starts[:, None], s0[None]), 0, None) / CMP_LEN
    ovt = np.zeros((LANES, ng), np.float32)
    ovt[:n_slc, :n_cmp] = ovl.T
    emt = (np.arange(LANES)[None, :] == (np.arange(s) // SLC_LEN)[:, None]).astype(np.float32)
    inv = 1.0 / (ROPE_THETA ** (jnp.arange(0, ROPE_DIM, 2, dtype=F32) / ROPE_DIM))
    ang = jnp.arange(s, dtype=F32)[:, None] * inv[None, :]
    cos, sin = jnp.cos(ang), jnp.sin(ang)
    z16 = jnp.zeros((s, 16), F32)
    z32 = jnp.zeros((s, 32), F32)
    z64 = jnp.zeros((s, 64), F32)
    one64 = jnp.ones((s, 64), F32)
    cq_t = jnp.concatenate([one64, cos, cos, z32], axis=1)
    s1_t = jnp.concatenate([z64, -sin, z16, z32], axis=1)
    s2_t = jnp.concatenate([z64, z16, sin, z32], axis=1)
    ck_t = jnp.concatenate([z64, cos, cos, z32], axis=1)
    sk_t = jnp.concatenate([z64, -sin, sin, z32], axis=1)
    return dict(bidxct=bidxct, bidx2t=bidx2t, ovt=jnp.asarray(ovt, BF16), emt=jnp.asarray(emt, BF16),
                rope=(cq_t, s1_t, s2_t, ck_t, sk_t))


def _pad_cols(w, width):
    return jnp.pad(w, ((0, 0), (0, width - w.shape[1])))


def _layer_weights(l, w_in, conv_w, conv_b, nsa_q_norm, nsa_k_norm, cmp_pos, cmp_w1, cmp_w2, mla_q_a_norm,
                   mla_kv_norm, mla_wq_b, mla_wkv_b, mla_q_norm, mla_k_norm, out_norm_w, w_out, norm2_w,
                   ffn_w1, ffn_w2):
    wi = w_in[l]
    kr = wi[:, 1740:1772]
    z32 = jnp.zeros((D_MODEL, 32), F32)
    w_in_p = jnp.concatenate([
        wi[:, 0:1408],
        _pad_cols(wi[:, 1408:1420], LANES),
        _pad_cols(wi[:, 1420:1612], 2 * LANES),
        wi[:, 1612:1740],
        kr[:, 16:32], kr[:, 0:16], z32, kr, z32,
        wi[:, 1772:2540],
    ], axis=1).astype(BF16)
    w1 = cmp_w1[l].reshape(2, CMP_LEN, HEAD_DIM, CMP_HIDDEN)
    zw = jnp.zeros((CMP_LEN, HEAD_DIM, CMP_HIDDEN), F32)
    cw1 = jnp.concatenate([jnp.concatenate([w1[0], zw], axis=2), jnp.concatenate([zw, w1[1]], axis=2)],
                          axis=1).astype(BF16)
    zc = jnp.zeros((CMP_HIDDEN, HEAD_DIM), F32)
    cw2 = jnp.concatenate([jnp.concatenate([cmp_w2[l, 0], zc], axis=1),
                           jnp.concatenate([zc, cmp_w2[l, 1]], axis=1)], axis=0).astype(BF16)
    cpos = jnp.concatenate([cmp_pos[l, 0], cmp_pos[l, 1]], axis=1)
    kn = nsa_k_norm[l]
    ones64 = jnp.ones((HEAD_DIM,), F32)
    knw_c = jnp.concatenate([kn[0], ones64])[None, :]
    knw_sw = jnp.stack([jnp.concatenate([kn[1], ones64]), jnp.concatenate([kn[2], ones64])])
    qnw = jnp.tile(nsa_q_norm[l], N_HEADS)[None, :]
    wq = mla_wq_b[l].reshape(Q_LORA, N_HEADS, QK_DIM)
    wq = jnp.pad(wq, ((0, 2 * LANES - Q_LORA), (0, 0), (0, LANES - QK_DIM))).reshape(2 * LANES, N_HEADS * LANES)
    wkv = mla_wkv_b[l].reshape(KV_LORA, N_HEADS, 2 * HEAD_DIM)
    wk = jnp.pad(wkv[:, :, :HEAD_DIM], ((0, 0), (0, 0), (0, LANES - HEAD_DIM))).reshape(KV_LORA, N_HEADS * LANES)
    wvt = jnp.pad(wkv[:, :, HEAD_DIM:], ((0, 0), (0, 0), (0, DV_PAD - HEAD_DIM))).reshape(KV_LORA, -1).T
    return dict(
        w_in=w_in_p, cw1=cw1, cw2=cw2, cw2t=cw2.T, cpos=cpos, knw_c=knw_c, knw_sw=knw_sw, qnw=qnw,
        qaw=_pad_cols(mla_q_a_norm[l][None, :], 2 * LANES), kvw=mla_kv_norm[l][None, :],
        wq=wq.astype(BF16), wk=wk.astype(BF16), wvt=wvt.astype(BF16),
        mqn=_pad_cols(mla_q_norm[l][None, :], LANES), mkn=_pad_cols(mla_k_norm[l][None, :], LANES),
        cw=conv_w[l], cb=conv_b[l][None, :], onw=out_norm_w[l][None, :], wo=w_out[l].astype(BF16),
        n2w=norm2_w[l][None, :], w1=ffn_w1[l].astype(BF16), w2=ffn_w2[l].astype(BF16))


TM_PROJ = 512
TM_PREP = 512
TM_POST = 512
TQ_NSA = 256
TQ_MLA = 256
TQ_SB = 256


def kernel(x, rel_bias, norm1_w, w_in, conv_w, conv_b, nsa_q_norm, nsa_k_norm, cmp_pos, cmp_w1, cmp_w2,
           mla_q_a_norm, mla_kv_norm, mla_wq_b, mla_wkv_b, mla_q_norm, mla_k_norm, out_norm_w, w_out, norm2_w,
           ffn_w1, ffn_w2):
    b, s, d = x.shape
    depth = w_in.shape[0]
    tabs = _tables(s, TQ_NSA)
    x2d = x.reshape(b * s, d)
    for l in range(depth):
        w = _layer_weights(l, w_in, conv_w, conv_b, nsa_q_norm, nsa_k_norm, cmp_pos, cmp_w1, cmp_w2,
                           mla_q_a_norm, mla_kv_norm, mla_wq_b, mla_wkv_b, mla_q_norm, mla_k_norm, out_norm_w,
                           w_out, norm2_w, ffn_w1, ffn_w2)
        proj = _inproj(x2d, norm1_w[l][None, :], w["w_in"], TM_PROJ)
        kcvc, kcvct = _compress(proj, w["cpos"], w["cw1"], w["cw2"], w["cw2t"], w["knw_c"], b, s)
        yb = _nsa_attention(proj, kcvc, kcvct, rel_bias, tabs["bidxct"], tabs["bidx2t"], tabs["ovt"], tabs["emt"],
                            w["qnw"], w["knw_sw"], b, s, TQ_NSA)
        qm, km, vmt = _mla_prep(proj, w["qaw"], w["kvw"], w["wq"], w["wk"], w["wvt"], w["mqn"], w["mkn"],
                                tabs["rope"], s, TM_PREP)
        yc = _mla_attention(qm, km, vmt, b, s, TQ_MLA)
        yd = _sb_attention(proj, b, s, TQ_SB)
        x2d = _post(proj, yb, yc, yd, x2d, w["cw"], w["cb"], w["onw"], w["wo"], w["n2w"], w["w1"], w["w2"],
                    s, TM_POST)
    return x2d.reshape(b, s, d)
```

```python
import functools
import math

import jax
import jax.numpy as jnp
import numpy as np
from jax import lax
from jax.experimental import pallas as pl
from jax.experimental.pallas import tpu as pltpu

F32 = jnp.float32
BF16 = jnp.bfloat16

D_MODEL = 1024
GROUP_WIDTH = 256
HEAD_DIM = 64
N_HEADS = 4
LANES = 128
CMP_LEN = 32
CMP_STRIDE = 16
SLC_LEN = 64
N_SEL = 16
WINDOW = 512
CMP_HIDDEN = 256
Q_LORA = 192
KV_LORA = 128
ROPE_DIM = 32
QK_DIM = 96
ROPE_THETA = 10000.0
N_BUCKETS = 32
MAX_DISTANCE = 128
D_FF = 4096
EPS = 1e-6
NEG = -1e30
LOG2E = math.log2(math.e)
DV_PAD = 80

NP = 2816
CB_NQ = 3
CB_KCVC = 8
CB_KSVS = 9
CB_KWVW = 10
CB_GATE = 11
CB_CQ = 6
CB_CKV = 14
CB_KR = 15
CB_SQ = 8
CB_SK = 9
CB_SV = 10

VMEM_LIMIT = 56 * 1024 * 1024

NT_DIMS = (((1,), (1,)), ((), ()))


def _params(sem):
    return pltpu.CompilerParams(dimension_semantics=sem, vmem_limit_bytes=VMEM_LIMIT)


def _nt(a, b):
    return lax.dot_general(a, b, NT_DIMS, preferred_element_type=F32)


def _dot(a, b):
    return jnp.dot(a, b, preferred_element_type=F32)


def _lane(shape):
    return lax.broadcasted_iota(jnp.int32, shape, len(shape) - 1)


def _row(shape):
    return lax.broadcasted_iota(jnp.int32, shape, len(shape) - 2)


def _lane_tile(j, width):
    return pl.ds(pl.multiple_of(j * width, width), width)


_IN_SEGMENTS = (
    ((0, 1408), 0),
    ((1408, 1420), 1408),
    ((1420, 1612), 1536),
    ((1612, 1740), 1792),
    ((1756, 1772), 1920),
    ((1740, 1756), 1936),
    ((1740, 1772), 1984),
    ((1772, 2540), 2048),
)
IN_COLS = 2540


def _inproj_kernel(x_ref, nw_ref, w_ref, o_ref, w_s):
    @pl.when(pl.program_id(0) == 0)
    def _relayout_weights():
        end = 0
        for (a, b), dst in _IN_SEGMENTS:
            if dst > end:
                w_s[:, end:dst] = jnp.zeros((D_MODEL, dst - end), BF16)
            w_s[:, dst:dst + b - a] = w_ref[:, a:b].astype(BF16)
            end = dst + b - a
        assert end == NP

    x = x_ref[...]
    ms = jnp.mean(x * x, axis=-1, keepdims=True)
    h = (x * lax.rsqrt(ms + EPS) * nw_ref[...]).astype(BF16)
    o_ref[...] = _dot(h, w_s[...])


def _inproj(x2d, nw, w_in, l, tm):
    t = x2d.shape[0]
    return pl.pallas_call(
        _inproj_kernel,
        grid=(t // tm,),
        in_specs=[
            pl.BlockSpec((tm, D_MODEL), lambda i: (i, 0)),
            pl.BlockSpec((1, D_MODEL), lambda i: (0, 0)),
            pl.BlockSpec((None, D_MODEL, IN_COLS), lambda i: (l, 0, 0), pipeline_mode=pl.Buffered(1)),
        ],
        out_specs=pl.BlockSpec((tm, NP), lambda i: (i, 0)),
        out_shape=jax.ShapeDtypeStruct((t, NP), F32),
        scratch_shapes=[pltpu.VMEM((D_MODEL, NP), BF16)],
        compiler_params=_params(("arbitrary",)),
        name="inproj",
    )(x2d, nw, w_in)


def _compress_kernel(x_ref, pos_ref, w1_ref, w2_ref, w2t_ref, knw_ref, o_ref, ot_ref):
    ng = x_ref.shape[1]
    acc_a = jnp.zeros((ng, 2 * CMP_HIDDEN), F32)
    acc_b = jnp.zeros((ng, 2 * CMP_HIDDEN), F32)
    for i in range(CMP_STRIDE):
        x = x_ref[0, :, i, :]
        xa = (x + pos_ref[i:i + 1, :]).astype(BF16)
        xb = (x + pos_ref[CMP_STRIDE + i:CMP_STRIDE + i + 1, :]).astype(BF16)
        acc_a += _dot(xa, w1_ref[i])
        acc_b += _dot(xb, w1_ref[CMP_STRIDE + i])
    pre = acc_a + pltpu.roll(acc_b, ng - 1, 0)
    hdn = (pre * jax.nn.sigmoid(pre)).astype(BF16)
    out = _dot(hdn, w2_ref[...])
    lane = _lane(out.shape)
    is_k = lane < HEAD_DIM
    ss = jnp.sum(jnp.where(is_k, out * out, 0.0), axis=-1, keepdims=True) * (1.0 / HEAD_DIM)
    o_ref[0] = jnp.where(is_k, out * lax.rsqrt(ss + EPS) * knw_ref[...], out)
    ot_ref[0] = _nt(w2t_ref[...], hdn)


def _compress(proj, pos, w1, w2, w2t, knw, b, s):
    ng = s // CMP_STRIDE
    x4 = proj.reshape(b, ng, CMP_STRIDE, NP)
    return pl.pallas_call(
        _compress_kernel,
        grid=(b,),
        in_specs=[
            pl.BlockSpec((1, ng, CMP_STRIDE, LANES), lambda i: (i, 0, 0, CB_KCVC)),
            pl.BlockSpec((CMP_LEN, LANES), lambda i: (0, 0)),
            pl.BlockSpec((CMP_LEN, LANES, 2 * CMP_HIDDEN), lambda i: (0, 0, 0)),
            pl.BlockSpec((2 * CMP_HIDDEN, LANES), lambda i: (0, 0)),
            pl.BlockSpec((LANES, 2 * CMP_HIDDEN), lambda i: (0, 0)),
            pl.BlockSpec((1, LANES), lambda i: (0, 0)),
        ],
        out_specs=[pl.BlockSpec((1, ng, LANES), lambda i: (i, 0, 0)),
                   pl.BlockSpec((1, LANES, ng), lambda i: (i, 0, 0))],
        out_shape=[jax.ShapeDtypeStruct((b, ng, LANES), F32), jax.ShapeDtypeStruct((b, LANES, ng), F32)],
        compiler_params=_params(("arbitrary",)),
        name="nsa_compress",
    )(x4, pos, w1, w2, w2t, knw)


def _softmax_update(sts, vts, m_old, acc_old):
    m_new = [jnp.maximum(m, jnp.max(st, axis=0, keepdims=True)) for m, st in zip(m_old, sts)]
    ps = [jnp.exp2(st - m).astype(BF16) for st, m in zip(sts, m_new)]
    alphas = [jnp.exp2(mo - mn) for mo, mn in zip(m_old, m_new)]
    acc_new = [al * acc + _dot(vt, p) for al, acc, vt, p in zip(alphas, acc_old, vts, ps)]
    return m_new, acc_new


def _softmax_steps(sts, vts, m_refs, acc_refs):
    m_new, acc_new = _softmax_update(sts, vts, [r[...] for r in m_refs], [r[...] for r in acc_refs])
    for r, v in zip(m_refs, m_new):
        r[...] = v
    for r, v in zip(acc_refs, acc_new):
        r[...] = v


def _with_ones_row(vt):
    pad = jnp.where(_row((DV_PAD - HEAD_DIM, vt.shape[1])) == 0, 1.0, 0.0).astype(vt.dtype)
    return jnp.concatenate([vt, pad], axis=0)


def _bucket_bias(bidx, relb_ref, h, fill):
    acc = jnp.full(bidx.shape, fill, F32)
    for bk in range(N_BUCKETS):
        acc = jnp.where(bidx == bk, relb_ref[bk, h] * LOG2E, acc)
    return acc


def _group_sums(x, member):
    g = jnp.where(member, 1.0, 0.0).astype(BF16)
    hi = x.astype(BF16)
    lo = (x - hi.astype(F32)).astype(BF16)
    return _dot(hi, g) + _dot(lo, g)


def _dup_low_half(x):
    y = jnp.where(_lane(x.shape) < HEAD_DIM, x, 0.0)
    return y + pltpu.roll(y, HEAD_DIM, 1)


def _nsa_kernel(relb_ref, q_ref, g_ref, kcvc_ref, kcvct_ref, ksvs_ref, kwvw_ref, bidxct_ref, bidx2t_ref, ovt_ref,
                emt_ref, qnw_ref, knw_ref, o_ref,
                biasc_s, bias2_s, ks_s, vst_s, kw_s, vwt_s, kc_s, vct_s, qx_s, m_s, acc_s, *, tq, s_len):
    b = pl.program_id(0)
    i = pl.program_id(1)
    n_win = WINDOW // tq
    ng = s_len // CMP_STRIDE

    @pl.when((b == 0) & (i == 0))
    def _build_bias_tables():
        key = _row((tq, tq))
        qry = _lane((tq, tq))
        for h in range(N_HEADS):
            cols = slice(h * tq, (h + 1) * tq)
            far = jnp.full((tq, tq), relb_ref[N_BUCKETS - 1, h] * LOG2E, F32)
            bias2_s[0, :, cols] = jnp.where(key <= qry, _bucket_bias(bidx2t_ref[0], relb_ref, h, NEG), NEG)
            bias2_s[1, :, cols] = _bucket_bias(bidx2t_ref[1], relb_ref, h, NEG)
            bias2_s[2, :, cols] = far
            bias2_s[3, :, cols] = jnp.where(key > qry, far, NEG)
            bias2_s[4, :, cols] = jnp.full((tq, tq), NEG, F32)

        def body(t, carry):
            bi = bidxct_ref[:, _lane_tile(t, tq)]
            for h in range(N_HEADS):
                biasc_s[t, :, h * tq:(h + 1) * tq] = _bucket_bias(bi, relb_ref, h, NEG)
            return carry

        lax.fori_loop(0, s_len // tq, body, 0)

    @pl.when(i == 0)
    def _prep_kv():
        ch = 256

        def body(t, carry):
            rows = pl.ds(pl.multiple_of(t * ch, ch), ch)
            for src, kdst, vdst, widx in ((ksvs_ref, ks_s, vst_s, 0), (kwvw_ref, kw_s, vwt_s, 1)):
                x = src[rows, :]
                ss = _group_sums(x * x, _row((LANES, LANES)) < HEAD_DIM) * (1.0 / HEAD_DIM)
                kn = x * lax.rsqrt(ss + EPS) * knw_ref[widx:widx + 1, :]
                kdst[rows, 0:LANES] = _dup_low_half(kn).astype(BF16)
                vdst[:, _lane_tile(t, ch)] = _with_ones_row(x.T[HEAD_DIM:, :]).astype(BF16)
            ks_s[rows, LANES:] = emt_ref[rows, :]
            return carry

        lax.fori_loop(0, s_len // ch, body, 0)
        kc_s[...] = _dup_low_half(kcvc_ref[0]).astype(BF16)
        vct_s[...] = kcvct_ref[0][HEAD_DIM:, :].astype(BF16)

    q = q_ref[...]
    lane = _lane((tq, LANES))
    heads = range(N_HEADS)
    gw = 2 * LANES
    head_shift = HEAD_DIM.bit_length() - 1
    same_head = (lax.shift_right_logical(_row((gw, gw)), head_shift)
                 == lax.shift_right_logical(_lane((gw, gw)), head_shift))
    ss = _group_sums(q * q, same_head) * (1.0 / HEAD_DIM)
    qn = q * lax.rsqrt(ss + EPS) * qnw_ref[...] * (HEAD_DIM ** -0.5 * LOG2E)
    for h in heads:
        mine = (lane < HEAD_DIM) if h % 2 == 0 else (lane >= HEAD_DIM)
        qx_s[h, :, 0:LANES] = jnp.where(mine, qn[:, LANES * (h // 2):LANES * (h // 2 + 1)], 0.0).astype(BF16)

    lcs = [_nt(kc_s[...], qx_s[h, :, 0:LANES]) + biasc_s[i, :, h * tq:(h + 1) * tq] for h in heads]
    pcs = [jnp.where(lc > 0.5 * NEG, jnp.exp2(lc - jnp.max(lc, axis=0, keepdims=True)), 0.0) for lc in lcs]
    dens = [jnp.sum(pc, axis=0, keepdims=True) for pc in pcs]
    pcs = [pc / jnp.where(den > 0.0, den, 1.0) for pc, den in zip(pcs, dens)]
    o_cmp = [_dot(vct_s[...], pc.astype(BF16)) for pc in pcs]
    psum = (pcs[0] + pcs[1]) + (pcs[2] + pcs[3])

    n_slc = s_len // SLC_LEN
    n_sel = min(N_SEL, n_slc)
    blk = _row((n_slc, tq))
    tpos = i * tq + _lane((n_slc, tq))
    tblk = tpos // SLC_LEN
    valid = blk * SLC_LEN <= tpos

    p_hi = psum.astype(BF16)
    p_lo = (psum - p_hi.astype(F32)).astype(BF16)
    score = _dot(ovt_ref[...], p_hi) + _dot(ovt_ref[...], p_lo)
    forced = (blk == 0) | (blk == tblk) | (blk == tblk - 1)
    sc = jnp.where(forced, jnp.inf, jnp.where(valid, score[0:n_slc], -jnp.inf))
    rank = jnp.zeros((n_slc, tq), F32)
    for k in range(n_slc):
        ck = sc[k:k + 1, :]
        beats = (ck > sc) | ((ck == sc) & (blk > k))
        rank += jnp.where(beats, 1.0, 0.0)
    pen = jnp.where((rank < float(n_sel)) & valid, 0.0, NEG)
    pen = jnp.concatenate([pen, jnp.zeros((LANES - n_slc, tq), F32)], axis=0)
    pen_t = pen.T.astype(BF16)
    pen = pen.astype(BF16)
    for h in heads:
        qx_s[h, :, LANES:] = pen_t

    ms = [jnp.full((1, tq), NEG, F32)] * (2 * N_HEADS)
    accs = [jnp.zeros((DV_PAD, tq), F32)] * (2 * N_HEADS)
    for jj in range(n_win + 1):
        exists = i >= jj
        rows = pl.ds(pl.multiple_of(jnp.maximum(i - jj, 0) * tq, tq), tq)
        kind = jnp.where(exists, min(jj, 2), 4)
        kind_w = jnp.where(exists, 3 if jj == n_win else min(jj, 2), 4)
        ks = ks_s[rows, 0:LANES]
        kw = kw_s[rows, :]
        masked = _dot(ks_s[rows, LANES:], pen)
        sts = [_nt(ks, qx_s[h, :, 0:LANES]) + masked + bias2_s[kind, :, h * tq:(h + 1) * tq] for h in heads]
        sts += [_nt(kw, qx_s[h, :, 0:LANES]) + bias2_s[kind_w, :, h * tq:(h + 1) * tq] for h in heads]
        vts = [vst_s[:, rows]] * N_HEADS + [vwt_s[:, rows]] * N_HEADS
        ms, accs = _softmax_update(sts, vts, ms, accs)
    for h in heads:
        m_s[h] = ms[h]
        acc_s[h] = accs[h]
    o_win = [accs[N_HEADS + h][0:HEAD_DIM, :] / accs[N_HEADS + h][HEAD_DIM:HEAD_DIM + 1, :] for h in heads]

    def far_step(j, width):
        rows = pl.ds(j * tq if isinstance(j, int) else pl.multiple_of(j * tq, tq), width * tq)
        ks = ks_s[rows, :]
        sts = [_nt(ks, qx_s[h]) + relb_ref[N_BUCKETS - 1, h] * LOG2E for h in heads]
        _softmax_steps(sts, [vst_s[:, rows]] * N_HEADS, [m_s.at[h] for h in heads], [acc_s.at[h] for h in heads])

    n_far = jnp.maximum(i - n_win, 0)

    def far_body(p, carry):
        far_step(n_far - 2 * (p + 1), 2)
        return carry

    lax.fori_loop(0, n_far // 2, far_body, 0)
    pl.when(n_far % 2 == 1)(functools.partial(far_step, 0, 1))

    gt = jax.nn.sigmoid(g_ref[...]).T
    ys = []
    for h in heads:
        o_slc = acc_s[h, 0:HEAD_DIM, :] / acc_s[h, HEAD_DIM:HEAD_DIM + 1, :]
        ys.append(gt[3 * h:3 * h + 1, :] * o_cmp[h] + gt[3 * h + 1:3 * h + 2, :] * o_slc
                  + gt[3 * h + 2:3 * h + 3, :] * o_win[h])
    o_ref[...] = jnp.concatenate(ys, axis=0).T


def _nsa_attention(proj, kcvc, kcvct, rel_bias, bidxct, bidx2t, ovt, emt, qnw, knw, b, s, tq):
    t = b * s
    nq = s // tq
    m_rows = N_HEADS * tq
    ng = s // CMP_STRIDE
    kern = functools.partial(_nsa_kernel, tq=tq, s_len=s)
    return pl.pallas_call(
        kern,
        grid=(b, nq),
        in_specs=[
            pl.BlockSpec(memory_space=pltpu.SMEM),
            pl.BlockSpec((tq, 2 * LANES), lambda bi, i: (bi * nq + i, CB_NQ)),
            pl.BlockSpec((tq, LANES), lambda bi, i: (bi * nq + i, CB_GATE)),
            pl.BlockSpec((1, ng, LANES), lambda bi, i: (bi, 0, 0)),
            pl.BlockSpec((1, LANES, ng), lambda bi, i: (bi, 0, 0)),
            pl.BlockSpec((s, LANES), lambda bi, i: (bi, CB_KSVS)),
            pl.BlockSpec((s, LANES), lambda bi, i: (bi, CB_KWVW)),
            pl.BlockSpec((ng, s), lambda bi, i: (0, 0)),
            pl.BlockSpec((2, tq, tq), lambda bi, i: (0, 0, 0)),
            pl.BlockSpec((LANES, ng), lambda bi, i: (0, 0)),
            pl.BlockSpec((s, LANES), lambda bi, i: (0, 0)),
            pl.BlockSpec((1, 2 * LANES), lambda bi, i: (0, 0)),
            pl.BlockSpec((2, LANES), lambda bi, i: (0, 0)),
        ],
        out_specs=pl.BlockSpec((tq, 2 * LANES), lambda bi, i: (bi * nq + i, 0)),
        out_shape=jax.ShapeDtypeStruct((t, GROUP_WIDTH), F32),
        scratch_shapes=[
            pltpu.VMEM((nq, ng, m_rows), F32),
            pltpu.VMEM((5, tq, m_rows), F32),
            pltpu.VMEM((s, 2 * LANES), BF16),
            pltpu.VMEM((DV_PAD, s), BF16),
            pltpu.VMEM((s, LANES), BF16),
            pltpu.VMEM((DV_PAD, s), BF16),
            pltpu.VMEM((ng, LANES), BF16),
            pltpu.VMEM((HEAD_DIM, ng), BF16),
            pltpu.VMEM((N_HEADS, tq, 2 * LANES), BF16),
            pltpu.VMEM((N_HEADS, 1, tq), F32),
            pltpu.VMEM((N_HEADS, DV_PAD, tq), F32),
        ],
        compiler_params=_params(("arbitrary", "arbitrary")),
        name="nsa_attention",
    )(rel_bias, proj, proj, kcvc, kcvct, proj, proj, bidxct, bidx2t, ovt, emt, qnw, knw)


def _mla_prep_kernel(cq_ref, ckv_ref, kr_ref, qaw_ref, kvw_ref, wq_ref, wk_ref, wvt_ref, qnw_ref, knw_ref,
                     cq_t_ref, s1_t_ref, s2_t_ref, ck_t_ref, sk_t_ref, qo_ref, ko_ref, vto_ref):
    cq = cq_ref[...]
    ms = jnp.sum(cq * cq, axis=-1, keepdims=True) * (1.0 / Q_LORA)
    hq = (cq * lax.rsqrt(ms + EPS) * qaw_ref[...]).astype(BF16)
    qf = _dot(hq, wq_ref[...])
    ckv = ckv_ref[...]
    ms = jnp.mean(ckv * ckv, axis=-1, keepdims=True)
    hkv = (ckv * lax.rsqrt(ms + EPS) * kvw_ref[...]).astype(BF16)
    kf = _dot(hkv, wk_ref[...])
    vt = _nt(wvt_ref[...], hkv)
    vto_ref[...] = jnp.where(_row(vt.shape) % DV_PAD == HEAD_DIM, 1.0, vt).astype(BF16)
    krb = kr_ref[...]
    kr_rot = krb * ck_t_ref[...] + pltpu.roll(krb, HEAD_DIM, 1) * sk_t_ref[...]
    scale = QK_DIM ** -0.5 * LOG2E
    for h in range(N_HEADS):
        cols = slice(LANES * h, LANES * (h + 1))
        x = qf[:, cols]
        x = x * cq_t_ref[...] + pltpu.roll(x, LANES - 16, 1) * s1_t_ref[...] + pltpu.roll(x, 16, 1) * s2_t_ref[...]
        ss = jnp.sum(x * x, axis=-1, keepdims=True) * (1.0 / QK_DIM)
        qo_ref[:, cols] = (x * lax.rsqrt(ss + EPS) * qnw_ref[...] * scale).astype(BF16)
        k = kf[:, cols] + kr_rot
        ss = jnp.sum(k * k, axis=-1, keepdims=True) * (1.0 / QK_DIM)
        ko_ref[:, cols] = (k * lax.rsqrt(ss + EPS) * knw_ref[...]).astype(BF16)


def _mla_prep(proj, qaw, kvw, wq, wk, wvt, qnw, knw, tabs, s, tm):
    t = proj.shape[0]
    npos = s // tm
    row = lambda i: (i, 0)
    const = lambda i: (0, 0)
    tab = pl.BlockSpec((tm, LANES), lambda i: (i % npos, 0))
    out = jax.ShapeDtypeStruct((t, N_HEADS * LANES), BF16)
    return pl.pallas_call(
        _mla_prep_kernel,
        grid=(t // tm,),
        in_specs=[
            pl.BlockSpec((tm, 2 * LANES), lambda i: (i, CB_CQ)),
            pl.BlockSpec((tm, LANES), lambda i: (i, CB_CKV)),
            pl.BlockSpec((tm, LANES), lambda i: (i, CB_KR)),
            pl.BlockSpec((1, 2 * LANES), const),
            pl.BlockSpec((1, LANES), const),
            pl.BlockSpec((2 * LANES, N_HEADS * LANES), const),
            pl.BlockSpec((LANES, N_HEADS * LANES), const),
            pl.BlockSpec((N_HEADS * DV_PAD, LANES), const),
            pl.BlockSpec((1, LANES), const),
            pl.BlockSpec((1, LANES), const),
            tab, tab, tab, tab, tab,
        ],
        out_specs=[pl.BlockSpec((tm, N_HEADS * LANES), row), pl.BlockSpec((tm, N_HEADS * LANES), row),
                   pl.BlockSpec((N_HEADS * DV_PAD, tm), lambda i: (0, i))],
        out_shape=[out, out, jax.ShapeDtypeStruct((N_HEADS * DV_PAD, t), BF16)],
        compiler_params=_params(("arbitrary",)),
        name="mla_prep",
    )(proj, proj, proj, qaw, kvw, wq, wk, wvt, qnw, knw, *tabs)


def _mla_attn_kernel(q_ref, k_ref, vt_ref, o_ref, m_s, acc_s, *, tq):
    i = pl.program_id(1)
    causal = _row((tq, tq)) <= _lane((tq, tq))
    for h in range(N_HEADS):
        m_s[h] = jnp.full((1, tq), NEG, F32)
        acc_s[h] = jnp.zeros((DV_PAD, tq), F32)

    def step(j, width, masked):
        tk = width * tq
        rows = pl.ds(j * tq if isinstance(j, int) else pl.multiple_of(j * tq, tq), tk)
        sts = []
        for h in range(N_HEADS):
            cols = slice(LANES * h, LANES * (h + 1))
            st = _nt(k_ref[rows, cols], q_ref[:, cols])
            sts.append(jnp.where(causal, st, NEG) if masked else st)
        vts = [vt_ref[DV_PAD * h:DV_PAD * (h + 1), rows] for h in range(N_HEADS)]
        _softmax_steps(sts, vts, [m_s.at[h] for h in range(N_HEADS)], [acc_s.at[h] for h in range(N_HEADS)])

    step(i, 1, True)

    def body(jj, carry):
        step(i - 2 * jj, 2, False)
        return carry

    lax.fori_loop(1, i // 2 + 1, body, 0)
    pl.when(i % 2 == 1)(functools.partial(step, 0, 1, False))
    yt = jnp.concatenate([acc_s[h, 0:HEAD_DIM, :] / acc_s[h, HEAD_DIM:HEAD_DIM + 1, :] for h in range(N_HEADS)],
                         axis=0)
    o_ref[...] = yt.T


def _mla_attention(qm, km, vmt, b, s, tq):
    t = b * s
    nq = s // tq
    w = N_HEADS * LANES
    return pl.pallas_call(
        functools.partial(_mla_attn_kernel, tq=tq),
        grid=(b, nq),
        in_specs=[
            pl.BlockSpec((tq, w), lambda bi, i: (bi * nq + i, 0)),
            pl.BlockSpec((s, w), lambda bi, i: (bi, 0)),
            pl.BlockSpec((N_HEADS * DV_PAD, s), lambda bi, i: (0, bi)),
        ],
        out_specs=pl.BlockSpec((tq, 2 * LANES), lambda bi, i: (bi * nq + i, 0)),
        out_shape=jax.ShapeDtypeStruct((t, GROUP_WIDTH), F32),
        scratch_shapes=[pltpu.VMEM((N_HEADS, 1, tq), F32), pltpu.VMEM((N_HEADS, DV_PAD, tq), F32)],
        compiler_params=_params(("arbitrary", "arbitrary")),
        name="mla_attention",
    )(qm, km, vmt)


def _sb_kernel(q_ref, k_ref, v_ref, o_ref, kb_s, vt_s, q_s, r_s, acc_s, kmax_s, *, tq, s_len):
    i = pl.program_id(1)
    gw = 2 * LANES
    head_shift = HEAD_DIM.bit_length() - 1
    same_head = (lax.shift_right_logical(_row((gw, gw)), head_shift)
                 == lax.shift_right_logical(_lane((gw, gw)), head_shift))

    @pl.when(i == 0)
    def _cast_kv():
        ch = 256

        def body(t, kmax):
            rows = pl.ds(pl.multiple_of(t * ch, ch), ch)
            k = k_ref[rows, :]
            kb_s[rows, :] = k.astype(BF16)
            vt_s[:, _lane_tile(t, ch)] = v_ref[rows, :].T.astype(BF16)
            return jnp.maximum(kmax, _group_sums(k * k, same_head))

        kmax = lax.fori_loop(0, s_len // ch, body, jnp.zeros((ch, gw), F32))
        kmax_s[0] = jnp.max(kmax)

    key = _row((tq, tq))
    qry = _lane((tq, tq))
    strict = key < qry
    tri = jnp.where(key <= qry, 1.0, 0.0).astype(BF16)
    tri2 = jnp.concatenate([tri, tri], axis=1)
    lane = _lane((tq, LANES))
    q = q_ref[...] * (HEAD_DIM ** -0.5 * LOG2E)
    for h in range(N_HEADS):
        mine = (lane < HEAD_DIM) if h % 2 == 0 else (lane >= HEAD_DIM)
        q_s[h] = jnp.where(mine, q[:, LANES * (h // 2):LANES * (h // 2 + 1)], 0.0).astype(BF16)
        r_s[h] = jnp.zeros((1, tq), F32)
        acc_s[h] = jnp.zeros((HEAD_DIM, tq), F32)
    qmax = jnp.max(_group_sums(q * q, same_head))
    z_bound = jnp.sqrt(jnp.full((1, tq), qmax * kmax_s[0], F32)) * 1.01 + 1.0

    heads = range(N_HEADS)

    def step(j, width, masked):
        rs = [r_s[h] for h in heads]
        accs = [acc_s[h] for h in heads]
        tiles = [j + width - 1 - w for w in range(width)]
        rows = [pl.ds(t * tq if isinstance(t, int) else pl.multiple_of(t * tq, tq), tq) for t in tiles]
        zs = [[_nt(kb_s[r, LANES * (h // 2):LANES * (h // 2 + 1)], q_s[h]) for h in heads] for r in rows]
        part = []
        for zt in zs:
            negabs = [pltpu.bitcast(pltpu.bitcast(z, jnp.uint32) | jnp.uint32(0x80000000), F32) for z in zt]
            sps = [jnp.maximum(z, 0.0) + jnp.log2(1.0 + jnp.exp2(na)) for z, na in zip(zt, negabs)]
            if masked:
                sps = [jnp.where(strict, sp, 0.0) for sp in sps]
            his = [sp.astype(BF16) for sp in sps]
            los = [(sp - hi.astype(F32)).astype(BF16) for sp, hi in zip(sps, his)]
            part.append([_dot(tri2, jnp.concatenate([hi, lo], axis=0)) for hi, lo in zip(his, los)])
        for r, zt, pt in zip(rows, zs, part):
            csums = [p + rc for p, rc in zip(pt, rs)]
            als = [jnp.exp2(z - cs) for z, cs in zip(zt, csums)]
            if masked:
                als = [jnp.where(strict, a, 0.0) for a in als]
            accs = [acc + _dot(vt_s[HEAD_DIM * h:HEAD_DIM * (h + 1), r], als[h].astype(BF16))
                    for h, acc in zip(heads, accs)]
            rs = [cs[0:1, :] for cs in csums]
        for h in heads:
            acc_s[h] = accs[h]
            r_s[h] = rs[h]

    def live():
        r_min = jnp.minimum(jnp.minimum(r_s[0], r_s[1]), jnp.minimum(r_s[2], r_s[3]))
        return jnp.max(z_bound - r_min) >= -150.0

    step(i, 1, True)
    pl.when(i >= 1)(functools.partial(step, i - 1, 1, False))

    n_rest = jnp.maximum(i - 1, 0)
    n_pairs = n_rest // 2

    def cond(carry):
        p, alive = carry
        return (p < n_pairs) & alive

    def body(carry):
        p, _ = carry
        step(i - 1 - 2 * (p + 1), 2, False)
        return p + 1, live()

    _, alive = lax.while_loop(cond, body, (jnp.int32(0), live()))
    pl.when((n_rest % 2 == 1) & alive)(functools.partial(step, 0, 1, False))
    o_ref[...] = jnp.concatenate([acc_s[h] for h in range(N_HEADS)], axis=0).T


def _sb_attention(proj, b, s, tq):
    t = b * s
    nq = s // tq
    w = 2 * LANES
    return pl.pallas_call(
        functools.partial(_sb_kernel, tq=tq, s_len=s),
        grid=(b, nq),
        in_specs=[
            pl.BlockSpec((tq, w), lambda bi, i: (bi * nq + i, CB_SQ)),
            pl.BlockSpec((s, w), lambda bi, i: (bi, CB_SK)),
            pl.BlockSpec((s, w), lambda bi, i: (bi, CB_SV)),
        ],
        out_specs=pl.BlockSpec((tq, w), lambda bi, i: (bi * nq + i, 0)),
        out_shape=jax.ShapeDtypeStruct((t, GROUP_WIDTH), F32),
        scratch_shapes=[pltpu.VMEM((s, w), BF16), pltpu.VMEM((w, s), BF16),
                        pltpu.VMEM((N_HEADS, tq, LANES), BF16),
                        pltpu.VMEM((N_HEADS, 1, tq), F32), pltpu.VMEM((N_HEADS, HEAD_DIM, tq), F32),
                        pltpu.SMEM((1,), F32)],
        compiler_params=_params(("arbitrary", "arbitrary")),
        name="sb_attention",
    )(proj, proj, proj)


def _post_kernel(a_ref, ap_ref, yb_ref, yc_ref, yd_ref, x_ref, cw_ref, cb_ref, onw_ref, wo_ref, n2w_ref,
                 w1_ref, w2_ref, o_ref, *, tm, s_len, ffc):
    i = pl.program_id(0)
    a = a_ref[...]
    gw = GROUP_WIDTH
    v = a[:, gw:2 * gw] * a[:, 2 * gw:3 * gw]
    ap = ap_ref[...]
    first = (i * tm) % s_len == 0
    vp = jnp.where(first, 0.0, ap[:, gw:2 * gw] * ap[:, 2 * gw:3 * gw])
    row = _row(v.shape)
    v1 = jnp.where(row == 0, vp[7:8, :], pltpu.roll(v, 1, 0))
    v2 = jnp.where(row == 0, vp[6:7, :], jnp.where(row == 1, vp[7:8, :], pltpu.roll(v, 2, 0)))
    conv = cw_ref[0:1, :] * v2 + cw_ref[1:2, :] * v1 + cw_ref[2:3, :] * v
    ya = a[:, 0:gw] * (conv + cb_ref[...])

    mix = None
    for g, y in enumerate((ya, yb_ref[...], yc_ref[...], yd_ref[...])):
        ms = jnp.mean(y * y, axis=-1, keepdims=True)
        yn = (y * lax.rsqrt(ms + EPS) * onw_ref[:, gw * g:gw * (g + 1)]).astype(BF16)
        part = _dot(yn, wo_ref[gw * g:gw * (g + 1), :])
        mix = part if mix is None else mix + part
    x1 = x_ref[...] + mix

    ms = jnp.mean(x1 * x1, axis=-1, keepdims=True)
    h2 = (x1 * lax.rsqrt(ms + EPS) * n2w_ref[...]).astype(BF16)
    ff = None
    for cidx in range(D_FF // ffc):
        u = _dot(h2, w1_ref[:, ffc * cidx:ffc * (cidx + 1)])
        u = jnp.square(jnp.maximum(u, 0.0)).astype(BF16)
        part = _dot(u, w2_ref[ffc * cidx:ffc * (cidx + 1), :])
        ff = part if ff is None else ff + part
    o_ref[...] = x1 + ff


def _post(proj, yb, yc, yd, x2d, cw, cb, onw, wo, n2w, w1, w2, l, s, tm):
    t = x2d.shape[0]
    gw = GROUP_WIDTH
    row = lambda i: (i, 0)
    const = lambda i: (0, 0)
    layer = lambda i: (l, 0, 0)
    once = pl.Buffered(1)
    kern = functools.partial(_post_kernel, tm=tm, s_len=s, ffc=1024)
    return pl.pallas_call(
        kern,
        grid=(t // tm,),
        in_specs=[
            pl.BlockSpec((tm, 3 * gw), row),
            pl.BlockSpec((8, 3 * gw), lambda i: (jnp.maximum(i * (tm // 8) - 1, 0), 0)),
            pl.BlockSpec((tm, gw), row),
            pl.BlockSpec((tm, gw), row),
            pl.BlockSpec((tm, gw), row),
            pl.BlockSpec((tm, D_MODEL), row),
            pl.BlockSpec((3, gw), const),
            pl.BlockSpec((1, gw), const),
            pl.BlockSpec((1, D_MODEL), const),
            pl.BlockSpec((None, D_MODEL, D_MODEL), layer, pipeline_mode=once),
            pl.BlockSpec((1, D_MODEL), const),
            pl.BlockSpec((None, D_MODEL, D_FF), layer, pipeline_mode=once),
            pl.BlockSpec((None, D_FF, D_MODEL), layer, pipeline_mode=once),
        ],
        out_specs=pl.BlockSpec((tm, D_MODEL), row),
        out_shape=jax.ShapeDtypeStruct((t, D_MODEL), F32),
        compiler_params=_params(("arbitrary",)),
        name="post",
    )(proj, proj, yb, yc, yd, x2d, cw, cb, onw, wo, n2w, w1, w2)


def _t5_bucket(dist):
    max_exact = N_BUCKETS // 2
    d = jnp.maximum(dist, 0)
    large = max_exact + (jnp.log(jnp.maximum(d, 1).astype(F32) / max_exact)
                         / math.log(MAX_DISTANCE / max_exact) * (N_BUCKETS - max_exact)).astype(jnp.int32)
    large = jnp.minimum(large, N_BUCKETS - 1)
    return jnp.where(d < max_exact, d, large)


def _tables(s, tq_nsa):
    n_cmp = (s - CMP_LEN) // CMP_STRIDE + 1
    ng = s // CMP_STRIDE
    n_slc = s // SLC_LEN
    tpos = jnp.arange(s)[None, :]
    n = jnp.arange(ng)[:, None]
    dist_c = tpos - (n * CMP_STRIDE + CMP_LEN - 1)
    bidxct = jnp.where((dist_c >= 0) & (n < n_cmp), _t5_bucket(dist_c), -1).astype(jnp.int32)
    key = jnp.arange(tq_nsa)[:, None]
    qry = jnp.arange(tq_nsa)[None, :]
    bidx2t = jnp.stack([_t5_bucket(qry - key), _t5_bucket(tq_nsa + qry - key)]).astype(jnp.int32)
    starts = np.arange(n_cmp) * CMP_STRIDE
    ends = starts + CMP_LEN
    s0 = np.arange(n_slc) * SLC_LEN
    s1 = s0 + SLC_LEN
    ovl = np.clip(np.minimum(ends[:, None], s1[None]) - np.maximum(starts[:, None], s0[None]), 0, None) / CMP_LEN
    ovt = np.zeros((LANES, ng), np.float32)
    ovt[:n_slc, :n_cmp] = ovl.T
    emt = (np.arange(LANES)[None, :] == (np.arange(s) // SLC_LEN)[:, None]).astype(np.float32)
    inv = 1.0 / (ROPE_THETA ** (jnp.arange(0, ROPE_DIM, 2, dtype=F32) / ROPE_DIM))
    ang = jnp.arange(s, dtype=F32)[:, None] * inv[None, :]
    cos, sin = jnp.cos(ang), jnp.sin(ang)
    z16 = jnp.zeros((s, 16), F32)
    z32 = jnp.zeros((s, 32), F32)
    z64 = jnp.zeros((s, 64), F32)
    one64 = jnp.ones((s, 64), F32)
    cq_t = jnp.concatenate([one64, cos, cos, z32], axis=1)
    s1_t = jnp.concatenate([z64, -sin, z16, z32], axis=1)
    s2_t = jnp.concatenate([z64, z16, sin, z32], axis=1)
    ck_t = jnp.concatenate([z64, cos, cos, z32], axis=1)
    sk_t = jnp.concatenate([z64, -sin, sin, z32], axis=1)
    return dict(bidxct=bidxct, bidx2t=bidx2t, ovt=jnp.asarray(ovt, BF16), emt=jnp.asarray(emt, BF16),
                rope=(cq_t, s1_t, s2_t, ck_t, sk_t))


def _pad_cols(w, width):
    return jnp.pad(w, ((0, 0), (0, width - w.shape[1])))


def _layer_weights(l, conv_w, conv_b, nsa_q_norm, nsa_k_norm, cmp_pos, cmp_w1, cmp_w2, mla_q_a_norm,
                   mla_kv_norm, mla_wq_b, mla_wkv_b, mla_q_norm, mla_k_norm, out_norm_w, norm2_w):
    w1 = cmp_w1[l].reshape(2, CMP_LEN, HEAD_DIM, CMP_HIDDEN)
    zw = jnp.zeros((CMP_LEN, HEAD_DIM, CMP_HIDDEN), F32)
    cw1 = jnp.concatenate([jnp.concatenate([w1[0], zw], axis=2), jnp.concatenate([zw, w1[1]], axis=2)],
                          axis=1).astype(BF16)
    zc = jnp.zeros((CMP_HIDDEN, HEAD_DIM), F32)
    cw2 = jnp.concatenate([jnp.concatenate([cmp_w2[l, 0], zc], axis=1),
                           jnp.concatenate([zc, cmp_w2[l, 1]], axis=1)], axis=0).astype(BF16)
    cpos = jnp.concatenate([cmp_pos[l, 0], cmp_pos[l, 1]], axis=1)
    kn = nsa_k_norm[l]
    ones64 = jnp.ones((HEAD_DIM,), F32)
    knw_c = jnp.concatenate([kn[0], ones64])[None, :]
    knw_sw = jnp.stack([jnp.concatenate([kn[1], ones64]), jnp.concatenate([kn[2], ones64])])
    qnw = jnp.tile(nsa_q_norm[l], N_HEADS)[None, :]
    wq = mla_wq_b[l].reshape(Q_LORA, N_HEADS, QK_DIM)
    wq = jnp.pad(wq, ((0, 2 * LANES - Q_LORA), (0, 0), (0, LANES - QK_DIM))).reshape(2 * LANES, N_HEADS * LANES)
    wkv = mla_wkv_b[l].reshape(KV_LORA, N_HEADS, 2 * HEAD_DIM)
    wk = jnp.pad(wkv[:, :, :HEAD_DIM], ((0, 0), (0, 0), (0, LANES - HEAD_DIM))).reshape(KV_LORA, N_HEADS * LANES)
    wvt = jnp.pad(wkv[:, :, HEAD_DIM:], ((0, 0), (0, 0), (0, DV_PAD - HEAD_DIM))).reshape(KV_LORA, -1).T
    return dict(
        cw1=cw1, cw2=cw2, cw2t=cw2.T, cpos=cpos, knw_c=knw_c, knw_sw=knw_sw, qnw=qnw,
        qaw=_pad_cols(mla_q_a_norm[l][None, :], 2 * LANES), kvw=mla_kv_norm[l][None, :],
        wq=wq.astype(BF16), wk=wk.astype(BF16), wvt=wvt.astype(BF16),
        mqn=_pad_cols(mla_q_norm[l][None, :], LANES), mkn=_pad_cols(mla_k_norm[l][None, :], LANES),
        cw=conv_w[l], cb=conv_b[l][None, :], onw=out_norm_w[l][None, :], n2w=norm2_w[l][None, :])


TM_PROJ = 512
TM_PREP = 512
TM_POST = 512
TQ_NSA = 256
TQ_MLA = 256
TQ_SB = 256


def kernel(x, rel_bias, norm1_w, w_in, conv_w, conv_b, nsa_q_norm, nsa_k_norm, cmp_pos, cmp_w1, cmp_w2,
           mla_q_a_norm, mla_kv_norm, mla_wq_b, mla_wkv_b, mla_q_norm, mla_k_norm, out_norm_w, w_out, norm2_w,
           ffn_w1, ffn_w2):
    b, s, d = x.shape
    depth = w_in.shape[0]
    tabs = _tables(s, TQ_NSA)
    x2d = x.reshape(b * s, d)
    wo_b, w1_b, w2_b = w_out.astype(BF16), ffn_w1.astype(BF16), ffn_w2.astype(BF16)
    for l in range(depth):
        w = _layer_weights(l, conv_w, conv_b, nsa_q_norm, nsa_k_norm, cmp_pos, cmp_w1, cmp_w2,
                           mla_q_a_norm, mla_kv_norm, mla_wq_b, mla_wkv_b, mla_q_norm, mla_k_norm, out_norm_w,
                           norm2_w)
        proj = _inproj(x2d, norm1_w[l][None, :], w_in, l, TM_PROJ)
        kcvc, kcvct = _compress(proj, w["cpos"], w["cw1"], w["cw2"], w["cw2t"], w["knw_c"], b, s)
        yb = _nsa_attention(proj, kcvc, kcvct, rel_bias, tabs["bidxct"], tabs["bidx2t"], tabs["ovt"], tabs["emt"],
                            w["qnw"], w["knw_sw"], b, s, TQ_NSA)
        qm, km, vmt = _mla_prep(proj, w["qaw"], w["kvw"], w["wq"], w["wk"], w["wvt"], w["mqn"], w["mkn"],
                                tabs["rope"], s, TM_PREP)
        yc = _mla_attention(qm, km, vmt, b, s, TQ_MLA)
        yd = _sb_attention(proj, b, s, TQ_SB)
        x2d = _post(proj, yb, yc, yd, x2d, w["cw"], w["cb"], w["onw"], wo_b, w["n2w"], w1_b, w2_b, l, s, TM_POST)
    return x2d.reshape(b, s, d)
```

```python
import functools
import math

import jax
import jax.numpy as jnp
import numpy as np
from jax import lax
from jax.experimental import pallas as pl
from jax.experimental.pallas import tpu as pltpu

F32 = jnp.float32
BF16 = jnp.bfloat16

D_MODEL = 1024
GROUP_WIDTH = 256
HEAD_DIM = 64
N_HEADS = 4
LANES = 128
CMP_LEN = 32
CMP_STRIDE = 16
SLC_LEN = 64
N_SEL = 16
WINDOW = 512
CMP_HIDDEN = 256
Q_LORA = 192
KV_LORA = 128
ROPE_DIM = 32
QK_DIM = 96
ROPE_THETA = 10000.0
N_BUCKETS = 32
MAX_DISTANCE = 128
D_FF = 4096
EPS = 1e-6
NEG = -1e30
LOG2E = math.log2(math.e)
DV_PAD = 80

NP = 2816
CB_NQ = 3
CB_KCVC = 8
CB_KSVS = 9
CB_KWVW = 10
CB_GATE = 11
CB_CQ = 6
CB_CKV = 14
CB_KR = 15
CB_SQ = 8
CB_SK = 9
CB_SV = 10

VMEM_LIMIT = 56 * 1024 * 1024

NT_DIMS = (((1,), (1,)), ((), ()))


def _params(sem):
    return pltpu.CompilerParams(dimension_semantics=sem, vmem_limit_bytes=VMEM_LIMIT)


def _nt(a, b):
    return lax.dot_general(a, b, NT_DIMS, preferred_element_type=F32)


def _dot(a, b):
    return jnp.dot(a, b, preferred_element_type=F32)


def _lane(shape):
    return lax.broadcasted_iota(jnp.int32, shape, len(shape) - 1)


def _row(shape):
    return lax.broadcasted_iota(jnp.int32, shape, len(shape) - 2)


def _lane_tile(j, width):
    return pl.ds(pl.multiple_of(j * width, width), width)


_IN_SEGMENTS = (
    ((0, 1408), 0),
    ((1408, 1420), 1408),
    ((1420, 1612), 1536),
    ((1612, 1740), 1792),
    ((1756, 1772), 1920),
    ((1740, 1756), 1936),
    ((1740, 1772), 1984),
    ((1772, 2540), 2048),
)
IN_COLS = 2540


def _inproj_kernel(x_ref, nw_ref, w_ref, o_ref, w_s):
    @pl.when(pl.program_id(0) == 0)
    def _relayout_weights():
        end = 0
        for (a, b), dst in _IN_SEGMENTS:
            if dst > end:
                w_s[:, end:dst] = jnp.zeros((D_MODEL, dst - end), BF16)
            w_s[:, dst:dst + b - a] = w_ref[:, a:b].astype(BF16)
            end = dst + b - a
        assert end == NP

    x = x_ref[...]
    ms = jnp.mean(x * x, axis=-1, keepdims=True)
    h = (x * lax.rsqrt(ms + EPS) * nw_ref[...]).astype(BF16)
    o_ref[...] = _dot(h, w_s[...])


def _inproj(x2d, nw, w_in, l, tm):
    t = x2d.shape[0]
    return pl.pallas_call(
        _inproj_kernel,
        grid=(t // tm,),
        in_specs=[
            pl.BlockSpec((tm, D_MODEL), lambda i: (i, 0)),
            pl.BlockSpec((1, D_MODEL), lambda i: (0, 0)),
            pl.BlockSpec((None, D_MODEL, IN_COLS), lambda i: (l, 0, 0), pipeline_mode=pl.Buffered(1)),
        ],
        out_specs=pl.BlockSpec((tm, NP), lambda i: (i, 0)),
        out_shape=jax.ShapeDtypeStruct((t, NP), F32),
        scratch_shapes=[pltpu.VMEM((D_MODEL, NP), BF16)],
        compiler_params=_params(("arbitrary",)),
        name="inproj",
    )(x2d, nw, w_in)


def _compress_kernel(x_ref, pos_ref, w1_ref, w2_ref, w2t_ref, knw_ref, o_ref, ot_ref):
    ng = x_ref.shape[1]
    acc_a = jnp.zeros((ng, 2 * CMP_HIDDEN), F32)
    acc_b = jnp.zeros((ng, 2 * CMP_HIDDEN), F32)
    for i in range(CMP_STRIDE):
        x = x_ref[0, :, i, :]
        xa = (x + pos_ref[i:i + 1, :]).astype(BF16)
        xb = (x + pos_ref[CMP_STRIDE + i:CMP_STRIDE + i + 1, :]).astype(BF16)
        acc_a += _dot(xa, w1_ref[i])
        acc_b += _dot(xb, w1_ref[CMP_STRIDE + i])
    pre = acc_a + pltpu.roll(acc_b, ng - 1, 0)
    hdn = (pre * jax.nn.sigmoid(pre)).astype(BF16)
    out = _dot(hdn, w2_ref[...])
    lane = _lane(out.shape)
    is_k = lane < HEAD_DIM
    ss = jnp.sum(jnp.where(is_k, out * out, 0.0), axis=-1, keepdims=True) * (1.0 / HEAD_DIM)
    o_ref[0] = jnp.where(is_k, out * lax.rsqrt(ss + EPS) * knw_ref[...], out)
    ot_ref[0] = _nt(w2t_ref[...], hdn)


def _compress(proj, pos, w1, w2, w2t, knw, b, s):
    ng = s // CMP_STRIDE
    x4 = proj.reshape(b, ng, CMP_STRIDE, NP)
    return pl.pallas_call(
        _compress_kernel,
        grid=(b,),
        in_specs=[
            pl.BlockSpec((1, ng, CMP_STRIDE, LANES), lambda i: (i, 0, 0, CB_KCVC)),
            pl.BlockSpec((CMP_LEN, LANES), lambda i: (0, 0)),
            pl.BlockSpec((CMP_LEN, LANES, 2 * CMP_HIDDEN), lambda i: (0, 0, 0)),
            pl.BlockSpec((2 * CMP_HIDDEN, LANES), lambda i: (0, 0)),
            pl.BlockSpec((LANES, 2 * CMP_HIDDEN), lambda i: (0, 0)),
            pl.BlockSpec((1, LANES), lambda i: (0, 0)),
        ],
        out_specs=[pl.BlockSpec((1, ng, LANES), lambda i: (i, 0, 0)),
                   pl.BlockSpec((1, LANES, ng), lambda i: (i, 0, 0))],
        out_shape=[jax.ShapeDtypeStruct((b, ng, LANES), F32), jax.ShapeDtypeStruct((b, LANES, ng), F32)],
        compiler_params=_params(("arbitrary",)),
        name="nsa_compress",
    )(x4, pos, w1, w2, w2t, knw)


def _softmax_update(sts, vts, m_old, acc_old):
    m_new = [jnp.maximum(m, jnp.max(st, axis=0, keepdims=True)) for m, st in zip(m_old, sts)]
    ps = [jnp.exp2(st - m).astype(BF16) for st, m in zip(sts, m_new)]
    alphas = [jnp.exp2(mo - mn) for mo, mn in zip(m_old, m_new)]
    acc_new = [al * acc + _dot(vt, p) for al, acc, vt, p in zip(alphas, acc_old, vts, ps)]
    return m_new, acc_new


def _softmax_steps(sts, vts, m_refs, acc_refs):
    m_new, acc_new = _softmax_update(sts, vts, [r[...] for r in m_refs], [r[...] for r in acc_refs])
    for r, v in zip(m_refs, m_new):
        r[...] = v
    for r, v in zip(acc_refs, acc_new):
        r[...] = v


def _with_ones_row(vt):
    pad = jnp.where(_row((DV_PAD - HEAD_DIM, vt.shape[1])) == 0, 1.0, 0.0).astype(vt.dtype)
    return jnp.concatenate([vt, pad], axis=0)


def _bucket_bias(bidx, relb_ref, h, fill):
    acc = jnp.full(bidx.shape, fill, F32)
    for bk in range(N_BUCKETS):
        acc = jnp.where(bidx == bk, relb_ref[bk, h] * LOG2E, acc)
    return acc


def _group_sums(x, member):
    g = jnp.where(member, 1.0, 0.0).astype(BF16)
    hi = x.astype(BF16)
    lo = (x - hi.astype(F32)).astype(BF16)
    return _dot(hi, g) + _dot(lo, g)


def _dup_low_half(x):
    y = jnp.where(_lane(x.shape) < HEAD_DIM, x, 0.0)
    return y + pltpu.roll(y, HEAD_DIM, 1)


def _nsa_kernel(relb_ref, q_ref, g_ref, kcvc_ref, kcvct_ref, ksvs_ref, kwvw_ref, bidxct_ref, bidx2t_ref, ovt_ref,
                emt_ref, qnw_ref, knw_ref, o_ref,
                biasc_s, bias2_s, ks_s, vst_s, kw_s, vwt_s, kc_s, vct_s, qx_s, m_s, acc_s, *, tq, s_len):
    b = pl.program_id(0)
    i = pl.program_id(1)
    n_win = WINDOW // tq
    ng = s_len // CMP_STRIDE

    @pl.when((b == 0) & (i == 0))
    def _build_bias_tables():
        key = _row((tq, tq))
        qry = _lane((tq, tq))
        for h in range(N_HEADS):
            cols = slice(h * tq, (h + 1) * tq)
            far = jnp.full((tq, tq), relb_ref[N_BUCKETS - 1, h] * LOG2E, F32)
            bias2_s[0, :, cols] = jnp.where(key <= qry, _bucket_bias(bidx2t_ref[0], relb_ref, h, NEG), NEG)
            bias2_s[1, :, cols] = _bucket_bias(bidx2t_ref[1], relb_ref, h, NEG)
            bias2_s[2, :, cols] = far
            bias2_s[3, :, cols] = jnp.where(key > qry, far, NEG)
            bias2_s[4, :, cols] = jnp.full((tq, tq), NEG, F32)

        def body(t, carry):
            bi = bidxct_ref[:, _lane_tile(t, tq)]
            for h in range(N_HEADS):
                biasc_s[t, :, h * tq:(h + 1) * tq] = _bucket_bias(bi, relb_ref, h, NEG)
            return carry

        lax.fori_loop(0, s_len // tq, body, 0)

    @pl.when(i == 0)
    def _prep_kv():
        ch = 256

        def body(t, carry):
            rows = pl.ds(pl.multiple_of(t * ch, ch), ch)
            for src, kdst, vdst, widx in ((ksvs_ref, ks_s, vst_s, 0), (kwvw_ref, kw_s, vwt_s, 1)):
                x = src[rows, :]
                ss = _group_sums(x * x, _row((LANES, LANES)) < HEAD_DIM) * (1.0 / HEAD_DIM)
                kn = x * lax.rsqrt(ss + EPS) * knw_ref[widx:widx + 1, :]
                kdst[rows, 0:LANES] = _dup_low_half(kn).astype(BF16)
                vdst[:, _lane_tile(t, ch)] = _with_ones_row(x.T[HEAD_DIM:, :]).astype(BF16)
            ks_s[rows, LANES:] = emt_ref[rows, :]
            return carry

        lax.fori_loop(0, s_len // ch, body, 0)
        kc_s[...] = _dup_low_half(kcvc_ref[0]).astype(BF16)
        vct_s[...] = kcvct_ref[0][HEAD_DIM:, :].astype(BF16)

    q = q_ref[...]
    lane = _lane((tq, LANES))
    heads = range(N_HEADS)
    gw = 2 * LANES
    head_shift = HEAD_DIM.bit_length() - 1
    same_head = (lax.shift_right_logical(_row((gw, gw)), head_shift)
                 == lax.shift_right_logical(_lane((gw, gw)), head_shift))
    ss = _group_sums(q * q, same_head) * (1.0 / HEAD_DIM)
    qn = q * lax.rsqrt(ss + EPS) * qnw_ref[...] * (HEAD_DIM ** -0.5 * LOG2E)
    for h in heads:
        mine = (lane < HEAD_DIM) if h % 2 == 0 else (lane >= HEAD_DIM)
        qx_s[h, :, 0:LANES] = jnp.where(mine, qn[:, LANES * (h // 2):LANES * (h // 2 + 1)], 0.0).astype(BF16)

    lcs = [_nt(kc_s[...], qx_s[h, :, 0:LANES]) + biasc_s[i, :, h * tq:(h + 1) * tq] for h in heads]
    pcs = [jnp.where(lc > 0.5 * NEG, jnp.exp2(lc - jnp.max(lc, axis=0, keepdims=True)), 0.0) for lc in lcs]
    dens = [jnp.sum(pc, axis=0, keepdims=True) for pc in pcs]
    pcs = [pc / jnp.where(den > 0.0, den, 1.0) for pc, den in zip(pcs, dens)]
    o_cmp = [_dot(vct_s[...], pc.astype(BF16)) for pc in pcs]
    psum = (pcs[0] + pcs[1]) + (pcs[2] + pcs[3])

    n_slc = s_len // SLC_LEN
    n_sel = min(N_SEL, n_slc)
    blk = _row((n_slc, tq))
    tpos = i * tq + _lane((n_slc, tq))
    tblk = tpos // SLC_LEN
    valid = blk * SLC_LEN <= tpos

    p_hi = psum.astype(BF16)
    p_lo = (psum - p_hi.astype(F32)).astype(BF16)
    score = _dot(ovt_ref[...], p_hi) + _dot(ovt_ref[...], p_lo)
    forced = (blk == 0) | (blk == tblk) | (blk == tblk - 1)
    sc = jnp.where(forced, jnp.inf, jnp.where(valid, score[0:n_slc], -jnp.inf))
    rank = jnp.zeros((n_slc, tq), F32)
    for k in range(n_slc):
        ck = sc[k:k + 1, :]
        beats = (ck > sc) | ((ck == sc) & (blk > k))
        rank += jnp.where(beats, 1.0, 0.0)
    pen = jnp.where((rank < float(n_sel)) & valid, 0.0, NEG)
    pen = jnp.concatenate([pen, jnp.zeros((LANES - n_slc, tq), F32)], axis=0)
    pen_t = pen.T.astype(BF16)
    pen = pen.astype(BF16)
    for h in heads:
        qx_s[h, :, LANES:] = pen_t

    ms = [jnp.full((1, tq), NEG, F32)] * (2 * N_HEADS)
    accs = [jnp.zeros((DV_PAD, tq), F32)] * (2 * N_HEADS)
    for jj in range(n_win + 1):
        exists = i >= jj
        rows = pl.ds(pl.multiple_of(jnp.maximum(i - jj, 0) * tq, tq), tq)
        kind = jnp.where(exists, min(jj, 2), 4)
        kind_w = jnp.where(exists, 3 if jj == n_win else min(jj, 2), 4)
        ks = ks_s[rows, 0:LANES]
        kw = kw_s[rows, :]
        masked = _dot(ks_s[rows, LANES:], pen)
        sts = [_nt(ks, qx_s[h, :, 0:LANES]) + masked + bias2_s[kind, :, h * tq:(h + 1) * tq] for h in heads]
        sts += [_nt(kw, qx_s[h, :, 0:LANES]) + bias2_s[kind_w, :, h * tq:(h + 1) * tq] for h in heads]
        vts = [vst_s[:, rows]] * N_HEADS + [vwt_s[:, rows]] * N_HEADS
        ms, accs = _softmax_update(sts, vts, ms, accs)
    for h in heads:
        m_s[h] = ms[h]
        acc_s[h] = accs[h]
    o_win = [accs[N_HEADS + h][0:HEAD_DIM, :] / accs[N_HEADS + h][HEAD_DIM:HEAD_DIM + 1, :] for h in heads]

    def far_step(j, width):
        rows = pl.ds(j * tq if isinstance(j, int) else pl.multiple_of(j * tq, tq), width * tq)
        ks = ks_s[rows, :]
        sts = [_nt(ks, qx_s[h]) + relb_ref[N_BUCKETS - 1, h] * LOG2E for h in heads]
        _softmax_steps(sts, [vst_s[:, rows]] * N_HEADS, [m_s.at[h] for h in heads], [acc_s.at[h] for h in heads])

    n_far = jnp.maximum(i - n_win, 0)

    def far_body(p, carry):
        far_step(n_far - 2 * (p + 1), 2)
        return carry

    lax.fori_loop(0, n_far // 2, far_body, 0)
    pl.when(n_far % 2 == 1)(functools.partial(far_step, 0, 1))

    gt = jax.nn.sigmoid(g_ref[...]).T
    ys = []
    for h in heads:
        o_slc = acc_s[h, 0:HEAD_DIM, :] / acc_s[h, HEAD_DIM:HEAD_DIM + 1, :]
        ys.append(gt[3 * h:3 * h + 1, :] * o_cmp[h] + gt[3 * h + 1:3 * h + 2, :] * o_slc
                  + gt[3 * h + 2:3 * h + 3, :] * o_win[h])
    o_ref[...] = jnp.concatenate(ys, axis=0).T


def _nsa_attention(proj, kcvc, kcvct, rel_bias, bidxct, bidx2t, ovt, emt, qnw, knw, b, s, tq):
    t = b * s
    nq = s // tq
    m_rows = N_HEADS * tq
    ng = s // CMP_STRIDE
    kern = functools.partial(_nsa_kernel, tq=tq, s_len=s)
    return pl.pallas_call(
        kern,
        grid=(b, nq),
        in_specs=[
            pl.BlockSpec(memory_space=pltpu.SMEM),
            pl.BlockSpec((tq, 2 * LANES), lambda bi, i: (bi * nq + i, CB_NQ)),
            pl.BlockSpec((tq, LANES), lambda bi, i: (bi * nq + i, CB_GATE)),
            pl.BlockSpec((1, ng, LANES), lambda bi, i: (bi, 0, 0)),
            pl.BlockSpec((1, LANES, ng), lambda bi, i: (bi, 0, 0)),
            pl.BlockSpec((s, LANES), lambda bi, i: (bi, CB_KSVS)),
            pl.BlockSpec((s, LANES), lambda bi, i: (bi, CB_KWVW)),
            pl.BlockSpec((ng, s), lambda bi, i: (0, 0)),
            pl.BlockSpec((2, tq, tq), lambda bi, i: (0, 0, 0)),
            pl.BlockSpec((LANES, ng), lambda bi, i: (0, 0)),
            pl.BlockSpec((s, LANES), lambda bi, i: (0, 0)),
            pl.BlockSpec((1, 2 * LANES), lambda bi, i: (0, 0)),
            pl.BlockSpec((2, LANES), lambda bi, i: (0, 0)),
        ],
        out_specs=pl.BlockSpec((tq, 2 * LANES), lambda bi, i: (bi * nq + i, 0)),
        out_shape=jax.ShapeDtypeStruct((t, GROUP_WIDTH), F32),
        scratch_shapes=[
            pltpu.VMEM((nq, ng, m_rows), F32),
            pltpu.VMEM((5, tq, m_rows), F32),
            pltpu.VMEM((s, 2 * LANES), BF16),
            pltpu.VMEM((DV_PAD, s), BF16),
            pltpu.VMEM((s, LANES), BF16),
            pltpu.VMEM((DV_PAD, s), BF16),
            pltpu.VMEM((ng, LANES), BF16),
            pltpu.VMEM((HEAD_DIM, ng), BF16),
            pltpu.VMEM((N_HEADS, tq, 2 * LANES), BF16),
            pltpu.VMEM((N_HEADS, 1, tq), F32),
            pltpu.VMEM((N_HEADS, DV_PAD, tq), F32),
        ],
        compiler_params=_params(("arbitrary", "arbitrary")),
        name="nsa_attention",
    )(rel_bias, proj, proj, kcvc, kcvct, proj, proj, bidxct, bidx2t, ovt, emt, qnw, knw)


def _mla_prep_kernel(cq_ref, ckv_ref, kr_ref, qaw_ref, kvw_ref, wq_ref, wqs_ref, wk_ref, wvt_ref, qnw_ref, knw_ref,
                     cq_t_ref, ck_t_ref, sk_t_ref, qo_ref, ko_ref, vto_ref):
    cq = cq_ref[...]
    ms = jnp.sum(cq * cq, axis=-1, keepdims=True) * (1.0 / Q_LORA)
    hq = (cq * lax.rsqrt(ms + EPS) * qaw_ref[...]).astype(BF16)
    qf = _dot(hq, wq_ref[...])
    qsw = _dot(hq, wqs_ref[...])
    ckv = ckv_ref[...]
    ms = jnp.mean(ckv * ckv, axis=-1, keepdims=True)
    hkv = (ckv * lax.rsqrt(ms + EPS) * kvw_ref[...]).astype(BF16)
    kf = _dot(hkv, wk_ref[...])
    vt = _nt(wvt_ref[...], hkv)
    vto_ref[...] = jnp.where(_row(vt.shape) % DV_PAD == HEAD_DIM, 1.0, vt).astype(BF16)
    krb = kr_ref[...]
    kr_rot = krb * ck_t_ref[...] + pltpu.roll(krb, HEAD_DIM, 1) * sk_t_ref[...]
    for h in range(N_HEADS):
        cols = slice(LANES * h, LANES * (h + 1))
        x = qf[:, cols] * cq_t_ref[...] + qsw[:, cols] * sk_t_ref[...]
        ss = jnp.sum(x * x, axis=-1, keepdims=True) * (1.0 / QK_DIM)
        qo_ref[:, cols] = (x * lax.rsqrt(ss + EPS) * qnw_ref[...]).astype(BF16)
        k = kf[:, cols] + kr_rot
        ss = jnp.sum(k * k, axis=-1, keepdims=True) * (1.0 / QK_DIM)
        ko_ref[:, cols] = (k * lax.rsqrt(ss + EPS) * knw_ref[...]).astype(BF16)


def _mla_prep(proj, qaw, kvw, wq, wqs, wk, wvt, qnw, knw, tabs, s, tm):
    t = proj.shape[0]
    npos = s // tm
    row = lambda i: (i, 0)
    const = lambda i: (0, 0)
    tab = pl.BlockSpec((tm, LANES), lambda i: (i % npos, 0))
    out = jax.ShapeDtypeStruct((t, N_HEADS * LANES), BF16)
    return pl.pallas_call(
        _mla_prep_kernel,
        grid=(t // tm,),
        in_specs=[
            pl.BlockSpec((tm, 2 * LANES), lambda i: (i, CB_CQ)),
            pl.BlockSpec((tm, LANES), lambda i: (i, CB_CKV)),
            pl.BlockSpec((tm, LANES), lambda i: (i, CB_KR)),
            pl.BlockSpec((1, 2 * LANES), const),
            pl.BlockSpec((1, LANES), const),
            pl.BlockSpec((2 * LANES, N_HEADS * LANES), const),
            pl.BlockSpec((2 * LANES, N_HEADS * LANES), const),
            pl.BlockSpec((LANES, N_HEADS * LANES), const),
            pl.BlockSpec((N_HEADS * DV_PAD, LANES), const),
            pl.BlockSpec((1, LANES), const),
            pl.BlockSpec((1, LANES), const),
            tab, tab, tab,
        ],
        out_specs=[pl.BlockSpec((tm, N_HEADS * LANES), row), pl.BlockSpec((tm, N_HEADS * LANES), row),
                   pl.BlockSpec((N_HEADS * DV_PAD, tm), lambda i: (0, i))],
        out_shape=[out, out, jax.ShapeDtypeStruct((N_HEADS * DV_PAD, t), BF16)],
        compiler_params=_params(("arbitrary",)),
        name="mla_prep",
    )(proj, proj, proj, qaw, kvw, wq, wqs, wk, wvt, qnw, knw, *tabs)


def _mla_attn_kernel(q_ref, k_ref, vt_ref, o_ref, m_s, acc_s, *, tq):
    i = pl.program_id(1)
    causal = _row((tq, tq)) <= _lane((tq, tq))
    for h in range(N_HEADS):
        m_s[h] = jnp.full((1, tq), NEG, F32)
        acc_s[h] = jnp.zeros((DV_PAD, tq), F32)

    def step(j, width, masked):
        tk = width * tq
        rows = pl.ds(j * tq if isinstance(j, int) else pl.multiple_of(j * tq, tq), tk)
        sts = []
        for h in range(N_HEADS):
            cols = slice(LANES * h, LANES * (h + 1))
            st = _nt(k_ref[rows, cols], q_ref[:, cols])
            sts.append(jnp.where(causal, st, NEG) if masked else st)
        vts = [vt_ref[DV_PAD * h:DV_PAD * (h + 1), rows] for h in range(N_HEADS)]
        _softmax_steps(sts, vts, [m_s.at[h] for h in range(N_HEADS)], [acc_s.at[h] for h in range(N_HEADS)])

    step(i, 1, True)

    def body(jj, carry):
        step(i - 2 * jj, 2, False)
        return carry

    lax.fori_loop(1, i // 2 + 1, body, 0)
    pl.when(i % 2 == 1)(functools.partial(step, 0, 1, False))
    yt = jnp.concatenate([acc_s[h, 0:HEAD_DIM, :] / acc_s[h, HEAD_DIM:HEAD_DIM + 1, :] for h in range(N_HEADS)],
                         axis=0)
    o_ref[...] = yt.T


def _mla_attention(qm, km, vmt, b, s, tq):
    t = b * s
    nq = s // tq
    w = N_HEADS * LANES
    return pl.pallas_call(
        functools.partial(_mla_attn_kernel, tq=tq),
        grid=(b, nq),
        in_specs=[
            pl.BlockSpec((tq, w), lambda bi, i: (bi * nq + i, 0)),
            pl.BlockSpec((s, w), lambda bi, i: (bi, 0)),
            pl.BlockSpec((N_HEADS * DV_PAD, s), lambda bi, i: (0, bi)),
        ],
        out_specs=pl.BlockSpec((tq, 2 * LANES), lambda bi, i: (bi * nq + i, 0)),
        out_shape=jax.ShapeDtypeStruct((t, GROUP_WIDTH), F32),
        scratch_shapes=[pltpu.VMEM((N_HEADS, 1, tq), F32), pltpu.VMEM((N_HEADS, DV_PAD, tq), F32)],
        compiler_params=_params(("arbitrary", "arbitrary")),
        name="mla_attention",
    )(qm, km, vmt)


def _sb_kernel(q_ref, k_ref, v_ref, o_ref, kb_s, vt_s, q_s, r_s, acc_s, kmax_s, *, tq, s_len):
    i = pl.program_id(1)
    gw = 2 * LANES
    head_shift = HEAD_DIM.bit_length() - 1
    same_head = (lax.shift_right_logical(_row((gw, gw)), head_shift)
                 == lax.shift_right_logical(_lane((gw, gw)), head_shift))

    @pl.when(i == 0)
    def _cast_kv():
        ch = 256

        def body(t, kmax):
            rows = pl.ds(pl.multiple_of(t * ch, ch), ch)
            k = k_ref[rows, :]
            kb_s[rows, :] = k.astype(BF16)
            vt_s[:, _lane_tile(t, ch)] = v_ref[rows, :].T.astype(BF16)
            return jnp.maximum(kmax, _group_sums(k * k, same_head))

        kmax = lax.fori_loop(0, s_len // ch, body, jnp.zeros((ch, gw), F32))
        kmax_s[0] = jnp.max(kmax)

    key = _row((tq, tq))
    qry = _lane((tq, tq))
    strict = key < qry
    tri = jnp.where(key <= qry, 1.0, 0.0).astype(BF16)
    tri2 = jnp.concatenate([tri, tri], axis=1)
    lane = _lane((tq, LANES))
    q = q_ref[...] * (HEAD_DIM ** -0.5 * LOG2E)
    for h in range(N_HEADS):
        mine = (lane < HEAD_DIM) if h % 2 == 0 else (lane >= HEAD_DIM)
        q_s[h] = jnp.where(mine, q[:, LANES * (h // 2):LANES * (h // 2 + 1)], 0.0).astype(BF16)
        r_s[h] = jnp.zeros((1, tq), F32)
        acc_s[h] = jnp.zeros((HEAD_DIM, tq), F32)
    qmax = jnp.max(_group_sums(q * q, same_head))
    z_bound = jnp.sqrt(jnp.full((1, tq), qmax * kmax_s[0], F32)) * 1.01 + 1.0

    heads = range(N_HEADS)

    def step(j, width, masked):
        rs = [r_s[h] for h in heads]
        accs = [acc_s[h] for h in heads]
        tiles = [j + width - 1 - w for w in range(width)]
        rows = [pl.ds(t * tq if isinstance(t, int) else pl.multiple_of(t * tq, tq), tq) for t in tiles]
        zs = [[_nt(kb_s[r, LANES * (h // 2):LANES * (h // 2 + 1)], q_s[h]) for h in heads] for r in rows]
        part = []
        for zt in zs:
            negabs = [pltpu.bitcast(pltpu.bitcast(z, jnp.uint32) | jnp.uint32(0x80000000), F32) for z in zt]
            sps = [jnp.maximum(z, 0.0) + jnp.log2(1.0 + jnp.exp2(na)) for z, na in zip(zt, negabs)]
            if masked:
                sps = [jnp.where(strict, sp, 0.0) for sp in sps]
            his = [sp.astype(BF16) for sp in sps]
            los = [(sp - hi.astype(F32)).astype(BF16) for sp, hi in zip(sps, his)]
            part.append([_dot(tri2, jnp.concatenate([hi, lo], axis=0)) for hi, lo in zip(his, los)])
        for r, zt, pt in zip(rows, zs, part):
            csums = [p + rc for p, rc in zip(pt, rs)]
            als = [jnp.exp2(z - cs) for z, cs in zip(zt, csums)]
            if masked:
                als = [jnp.where(strict, a, 0.0) for a in als]
            accs = [acc + _dot(vt_s[HEAD_DIM * h:HEAD_DIM * (h + 1), r], als[h].astype(BF16))
                    for h, acc in zip(heads, accs)]
            rs = [cs[0:1, :] for cs in csums]
        for h in heads:
            acc_s[h] = accs[h]
            r_s[h] = rs[h]

    def live():
        r_min = jnp.minimum(jnp.minimum(r_s[0], r_s[1]), jnp.minimum(r_s[2], r_s[3]))
        return jnp.max(z_bound - r_min) >= -150.0

    step(i, 1, True)
    pl.when(i >= 1)(functools.partial(step, i - 1, 1, False))

    n_rest = jnp.maximum(i - 1, 0)
    n_pairs = n_rest // 2

    def cond(carry):
        p, alive = carry
        return (p < n_pairs) & alive

    def body(carry):
        p, _ = carry
        step(i - 1 - 2 * (p + 1), 2, False)
        return p + 1, live()

    _, alive = lax.while_loop(cond, body, (jnp.int32(0), live()))
    pl.when((n_rest % 2 == 1) & alive)(functools.partial(step, 0, 1, False))
    o_ref[...] = jnp.concatenate([acc_s[h] for h in range(N_HEADS)], axis=0).T


def _sb_attention(proj, b, s, tq):
    t = b * s
    nq = s // tq
    w = 2 * LANES
    return pl.pallas_call(
        functools.partial(_sb_kernel, tq=tq, s_len=s),
        grid=(b, nq),
        in_specs=[
            pl.BlockSpec((tq, w), lambda bi, i: (bi * nq + i, CB_SQ)),
            pl.BlockSpec((s, w), lambda bi, i: (bi, CB_SK)),
            pl.BlockSpec((s, w), lambda bi, i: (bi, CB_SV)),
        ],
        out_specs=pl.BlockSpec((tq, w), lambda bi, i: (bi * nq + i, 0)),
        out_shape=jax.ShapeDtypeStruct((t, GROUP_WIDTH), F32),
        scratch_shapes=[pltpu.VMEM((s, w), BF16), pltpu.VMEM((w, s), BF16),
                        pltpu.VMEM((N_HEADS, tq, LANES), BF16),
                        pltpu.VMEM((N_HEADS, 1, tq), F32), pltpu.VMEM((N_HEADS, HEAD_DIM, tq), F32),
                        pltpu.SMEM((1,), F32)],
        compiler_params=_params(("arbitrary", "arbitrary")),
        name="sb_attention",
    )(proj, proj, proj)


def _post_kernel(a_ref, ap_ref, yb_ref, yc_ref, yd_ref, x_ref, cw_ref, cb_ref, onw_ref, wo_ref, n2w_ref,
                 w1_ref, w2_ref, o_ref, *, tm, s_len, ffc):
    i = pl.program_id(0)
    a = a_ref[...]
    gw = GROUP_WIDTH
    v = a[:, gw:2 * gw] * a[:, 2 * gw:3 * gw]
    ap = ap_ref[...]
    first = (i * tm) % s_len == 0
    vp = jnp.where(first, 0.0, ap[:, gw:2 * gw] * ap[:, 2 * gw:3 * gw])
    row = _row(v.shape)
    v1 = jnp.where(row == 0, vp[7:8, :], pltpu.roll(v, 1, 0))
    v2 = jnp.where(row == 0, vp[6:7, :], jnp.where(row == 1, vp[7:8, :], pltpu.roll(v, 2, 0)))
    conv = cw_ref[0:1, :] * v2 + cw_ref[1:2, :] * v1 + cw_ref[2:3, :] * v
    ya = a[:, 0:gw] * (conv + cb_ref[...])

    mix = None
    for g, y in enumerate((ya, yb_ref[...], yc_ref[...], yd_ref[...])):
        ms = jnp.mean(y * y, axis=-1, keepdims=True)
        yn = (y * lax.rsqrt(ms + EPS) * onw_ref[:, gw * g:gw * (g + 1)]).astype(BF16)
        part = _dot(yn, wo_ref[gw * g:gw * (g + 1), :])
        mix = part if mix is None else mix + part
    x1 = x_ref[...] + mix

    ms = jnp.mean(x1 * x1, axis=-1, keepdims=True)
    h2 = (x1 * lax.rsqrt(ms + EPS) * n2w_ref[...]).astype(BF16)
    ff = None
    for cidx in range(D_FF // ffc):
        u = _dot(h2, w1_ref[:, ffc * cidx:ffc * (cidx + 1)])
        u = jnp.square(jnp.maximum(u, 0.0)).astype(BF16)
        part = _dot(u, w2_ref[ffc * cidx:ffc * (cidx + 1), :])
        ff = part if ff is None else ff + part
    o_ref[...] = x1 + ff


def _post(proj, yb, yc, yd, x2d, cw, cb, onw, wo, n2w, w1, w2, l, s, tm):
    t = x2d.shape[0]
    gw = GROUP_WIDTH
    row = lambda i: (i, 0)
    const = lambda i: (0, 0)
    layer = lambda i: (l, 0, 0)
    once = pl.Buffered(1)
    kern = functools.partial(_post_kernel, tm=tm, s_len=s, ffc=1024)
    return pl.pallas_call(
        kern,
        grid=(t // tm,),
        in_specs=[
            pl.BlockSpec((tm, 3 * gw), row),
            pl.BlockSpec((8, 3 * gw), lambda i: (jnp.maximum(i * (tm // 8) - 1, 0), 0)),
            pl.BlockSpec((tm, gw), row),
            pl.BlockSpec((tm, gw), row),
            pl.BlockSpec((tm, gw), row),
            pl.BlockSpec((tm, D_MODEL), row),
            pl.BlockSpec((3, gw), const),
            pl.BlockSpec((1, gw), const),
            pl.BlockSpec((1, D_MODEL), const),
            pl.BlockSpec((None, D_MODEL, D_MODEL), layer, pipeline_mode=once),
            pl.BlockSpec((1, D_MODEL), const),
            pl.BlockSpec((None, D_MODEL, D_FF), layer, pipeline_mode=once),
            pl.BlockSpec((None, D_FF, D_MODEL), layer, pipeline_mode=once),
        ],
        out_specs=pl.BlockSpec((tm, D_MODEL), row),
        out_shape=jax.ShapeDtypeStruct((t, D_MODEL), F32),
        compiler_params=_params(("arbitrary",)),
        name="post",
    )(proj, proj, yb, yc, yd, x2d, cw, cb, onw, wo, n2w, w1, w2)


def _t5_bucket(dist):
    max_exact = N_BUCKETS // 2
    d = jnp.maximum(dist, 0)
    large = max_exact + (jnp.log(jnp.maximum(d, 1).astype(F32) / max_exact)
                         / math.log(MAX_DISTANCE / max_exact) * (N_BUCKETS - max_exact)).astype(jnp.int32)
    large = jnp.minimum(large, N_BUCKETS - 1)
    return jnp.where(d < max_exact, d, large)


def _tables(s, tq_nsa):
    n_cmp = (s - CMP_LEN) // CMP_STRIDE + 1
    ng = s // CMP_STRIDE
    n_slc = s // SLC_LEN
    tpos = jnp.arange(s)[None, :]
    n = jnp.arange(ng)[:, None]
    dist_c = tpos - (n * CMP_STRIDE + CMP_LEN - 1)
    bidxct = jnp.where((dist_c >= 0) & (n < n_cmp), _t5_bucket(dist_c), -1).astype(jnp.int32)
    key = jnp.arange(tq_nsa)[:, None]
    qry = jnp.arange(tq_nsa)[None, :]
    bidx2t = jnp.stack([_t5_bucket(qry - key), _t5_bucket(tq_nsa + qry - key)]).astype(jnp.int32)
    starts = np.arange(n_cmp) * CMP_STRIDE
    ends = starts + CMP_LEN
    s0 = np.arange(n_slc) * SLC_LEN
    s1 = s0 + SLC_LEN
    ovl = np.clip(np.minimum(ends[:, None], s1[None]) - np.maximum(starts[:, None], s0[None]), 0, None) / CMP_LEN
    ovt = np.zeros((LANES, ng), np.float32)
    ovt[:n_slc, :n_cmp] = ovl.T
    emt = (np.arange(LANES)[None, :] == (np.arange(s) // SLC_LEN)[:, None]).astype(np.float32)
    inv = 1.0 / (ROPE_THETA ** (jnp.arange(0, ROPE_DIM, 2, dtype=F32) / ROPE_DIM))
    ang = jnp.arange(s, dtype=F32)[:, None] * inv[None, :]
    cos, sin = jnp.cos(ang), jnp.sin(ang)
    z16 = jnp.zeros((s, 16), F32)
    z32 = jnp.zeros((s, 32), F32)
    z64 = jnp.zeros((s, 64), F32)
    one64 = jnp.ones((s, 64), F32)
    cq_t = jnp.concatenate([one64, cos, cos, z32], axis=1)
    ck_t = jnp.concatenate([z64, cos, cos, z32], axis=1)
    sk_t = jnp.concatenate([z64, -sin, sin, z32], axis=1)
    return dict(bidxct=bidxct, bidx2t=bidx2t, ovt=jnp.asarray(ovt, BF16), emt=jnp.asarray(emt, BF16),
                rope=(cq_t, ck_t, sk_t))


def _pad_cols(w, width):
    return jnp.pad(w, ((0, 0), (0, width - w.shape[1])))


def _layer_weights(l, conv_w, conv_b, nsa_q_norm, nsa_k_norm, cmp_pos, cmp_w1, cmp_w2, mla_q_a_norm,
                   mla_kv_norm, mla_wq_b, mla_wkv_b, mla_q_norm, mla_k_norm, out_norm_w, norm2_w):
    w1 = cmp_w1[l].reshape(2, CMP_LEN, HEAD_DIM, CMP_HIDDEN)
    zw = jnp.zeros((CMP_LEN, HEAD_DIM, CMP_HIDDEN), F32)
    cw1 = jnp.concatenate([jnp.concatenate([w1[0], zw], axis=2), jnp.concatenate([zw, w1[1]], axis=2)],
                          axis=1).astype(BF16)
    zc = jnp.zeros((CMP_HIDDEN, HEAD_DIM), F32)
    cw2 = jnp.concatenate([jnp.concatenate([cmp_w2[l, 0], zc], axis=1),
                           jnp.concatenate([zc, cmp_w2[l, 1]], axis=1)], axis=0).astype(BF16)
    cpos = jnp.concatenate([cmp_pos[l, 0], cmp_pos[l, 1]], axis=1)
    kn = nsa_k_norm[l]
    ones64 = jnp.ones((HEAD_DIM,), F32)
    knw_c = jnp.concatenate([kn[0], ones64])[None, :]
    knw_sw = jnp.stack([jnp.concatenate([kn[1], ones64]), jnp.concatenate([kn[2], ones64])])
    qnw = jnp.tile(nsa_q_norm[l], N_HEADS)[None, :]
    wq = mla_wq_b[l].reshape(Q_LORA, N_HEADS, QK_DIM)
    half = ROPE_DIM // 2
    wqs = jnp.concatenate([jnp.zeros((Q_LORA, N_HEADS, HEAD_DIM), F32), wq[:, :, HEAD_DIM + half:],
                           wq[:, :, HEAD_DIM:HEAD_DIM + half]], axis=2)
    pad_q = lambda w: jnp.pad(w, ((0, 2 * LANES - Q_LORA), (0, 0), (0, LANES - QK_DIM))).reshape(
        2 * LANES, N_HEADS * LANES).astype(BF16)
    wkv = mla_wkv_b[l].reshape(KV_LORA, N_HEADS, 2 * HEAD_DIM)
    wk = jnp.pad(wkv[:, :, :HEAD_DIM], ((0, 0), (0, 0), (0, LANES - HEAD_DIM))).reshape(KV_LORA, N_HEADS * LANES)
    wvt = jnp.pad(wkv[:, :, HEAD_DIM:], ((0, 0), (0, 0), (0, DV_PAD - HEAD_DIM))).reshape(KV_LORA, -1).T
    return dict(
        cw1=cw1, cw2=cw2, cw2t=cw2.T, cpos=cpos, knw_c=knw_c, knw_sw=knw_sw, qnw=qnw,
        qaw=_pad_cols(mla_q_a_norm[l][None, :], 2 * LANES), kvw=mla_kv_norm[l][None, :],
        wq=pad_q(wq), wqs=pad_q(wqs), wk=wk.astype(BF16), wvt=wvt.astype(BF16),
        mqn=_pad_cols(mla_q_norm[l][None, :] * (QK_DIM ** -0.5 * LOG2E), LANES),
        mkn=_pad_cols(mla_k_norm[l][None, :], LANES),
        cw=conv_w[l], cb=conv_b[l][None, :], onw=out_norm_w[l][None, :], n2w=norm2_w[l][None, :])


TM_PROJ = 512
TM_PREP = 512
TM_POST = 512
TQ_NSA = 256
TQ_MLA = 256
TQ_SB = 256


def kernel(x, rel_bias, norm1_w, w_in, conv_w, conv_b, nsa_q_norm, nsa_k_norm, cmp_pos, cmp_w1, cmp_w2,
           mla_q_a_norm, mla_kv_norm, mla_wq_b, mla_wkv_b, mla_q_norm, mla_k_norm, out_norm_w, w_out, norm2_w,
           ffn_w1, ffn_w2):
    b, s, d = x.shape
    depth = w_in.shape[0]
    tabs = _tables(s, TQ_NSA)
    x2d = x.reshape(b * s, d)
    wo_b, w1_b, w2_b = w_out.astype(BF16), ffn_w1.astype(BF16), ffn_w2.astype(BF16)
    for l in range(depth):
        w = _layer_weights(l, conv_w, conv_b, nsa_q_norm, nsa_k_norm, cmp_pos, cmp_w1, cmp_w2,
                           mla_q_a_norm, mla_kv_norm, mla_wq_b, mla_wkv_b, mla_q_norm, mla_k_norm, out_norm_w,
                           norm2_w)
        proj = _inproj(x2d, norm1_w[l][None, :], w_in, l, TM_PROJ)
        kcvc, kcvct = _compress(proj, w["cpos"], w["cw1"], w["cw2"], w["cw2t"], w["knw_c"], b, s)
        yb = _nsa_attention(proj, kcvc, kcvct, rel_bias, tabs["bidxct"], tabs["bidx2t"], tabs["ovt"], tabs["emt"],
                            w["qnw"], w["knw_sw"], b, s, TQ_NSA)
        qm, km, vmt = _mla_prep(proj, w["qaw"], w["kvw"], w["wq"], w["wqs"], w["wk"], w["wvt"], w["mqn"], w["mkn"],
                                tabs["rope"], s, TM_PREP)
        yc = _mla_attention(qm, km, vmt, b, s, TQ_MLA)
        yd = _sb_attention(proj, b, s, TQ_SB)
        x2d = _post(proj, yb, yc, yd, x2d, w["cw"], w["cb"], w["onw"], wo_b, w["n2w"], w1_b, w2_b, l, s, TM_POST)
    return x2d.reshape(b, s, d)
```

```python
import functools
import math

import jax
import jax.numpy as jnp
import numpy as np
from jax import lax
from jax.experimental import pallas as pl
from jax.experimental.pallas import tpu as pltpu

F32 = jnp.float32
BF16 = jnp.bfloat16

D_MODEL = 1024
GROUP_WIDTH = 256
HEAD_DIM = 64
N_HEADS = 4
LANES = 128
CMP_LEN = 32
CMP_STRIDE = 16
SLC_LEN = 64
N_SEL = 16
WINDOW = 512
CMP_HIDDEN = 256
Q_LORA = 192
KV_LORA = 128
ROPE_DIM = 32
QK_DIM = 96
ROPE_THETA = 10000.0
N_BUCKETS = 32
MAX_DISTANCE = 128
D_FF = 4096
EPS = 1e-6
NEG = -1e30
LOG2E = math.log2(math.e)
DV_PAD = 80

NP = 2816
CB_NQ = 3
CB_KCVC = 8
CB_KSVS = 9
CB_KWVW = 10
CB_GATE = 11
CB_CQ = 6
CB_CKV = 14
CB_KR = 15
CB_SQ = 8
CB_SK = 9
CB_SV = 10

VMEM_LIMIT = 56 * 1024 * 1024

NT_DIMS = (((1,), (1,)), ((), ()))


def _params(sem):
    return pltpu.CompilerParams(dimension_semantics=sem, vmem_limit_bytes=VMEM_LIMIT)


def _nt(a, b):
    return lax.dot_general(a, b, NT_DIMS, preferred_element_type=F32)


def _dot(a, b):
    return jnp.dot(a, b, preferred_element_type=F32)


def _lane(shape):
    return lax.broadcasted_iota(jnp.int32, shape, len(shape) - 1)


def _row(shape):
    return lax.broadcasted_iota(jnp.int32, shape, len(shape) - 2)


def _lane_tile(j, width):
    return pl.ds(pl.multiple_of(j * width, width), width)


_IN_SEGMENTS = (
    ((0, 1408), 0),
    ((1408, 1420), 1408),
    ((1420, 1612), 1536),
    ((1612, 1740), 1792),
    ((1756, 1772), 1920),
    ((1740, 1756), 1936),
    ((1740, 1772), 1984),
    ((1772, 2540), 2048),
)
IN_COLS = 2540


def _inproj_kernel(x_ref, nw_ref, w_ref, o_ref, w_s):
    @pl.when(pl.program_id(0) == 0)
    def _relayout_weights():
        end = 0
        for (a, b), dst in _IN_SEGMENTS:
            if dst > end:
                w_s[:, end:dst] = jnp.zeros((D_MODEL, dst - end), BF16)
            w_s[:, dst:dst + b - a] = w_ref[:, a:b].astype(BF16)
            end = dst + b - a
        assert end == NP

    x = x_ref[...]
    ms = jnp.mean(x * x, axis=-1, keepdims=True)
    h = (x * lax.rsqrt(ms + EPS) * nw_ref[...]).astype(BF16)
    o_ref[...] = _dot(h, w_s[...])


def _inproj(x2d, nw, w_in, l, tm):
    t = x2d.shape[0]
    return pl.pallas_call(
        _inproj_kernel,
        grid=(t // tm,),
        in_specs=[
            pl.BlockSpec((tm, D_MODEL), lambda i: (i, 0)),
            pl.BlockSpec((1, D_MODEL), lambda i: (0, 0)),
            pl.BlockSpec((None, D_MODEL, IN_COLS), lambda i: (l, 0, 0), pipeline_mode=pl.Buffered(1)),
        ],
        out_specs=pl.BlockSpec((tm, NP), lambda i: (i, 0)),
        out_shape=jax.ShapeDtypeStruct((t, NP), F32),
        scratch_shapes=[pltpu.VMEM((D_MODEL, NP), BF16)],
        compiler_params=_params(("arbitrary",)),
        name="inproj",
    )(x2d, nw, w_in)


def _compress_kernel(x_ref, pos_ref, w1_ref, w2_ref, w2t_ref, knw_ref, o_ref, ot_ref):
    ng = x_ref.shape[1]
    acc_a = jnp.zeros((ng, 2 * CMP_HIDDEN), F32)
    acc_b = jnp.zeros((ng, 2 * CMP_HIDDEN), F32)
    for i in range(CMP_STRIDE):
        x = x_ref[0, :, i, :]
        xa = (x + pos_ref[i:i + 1, :]).astype(BF16)
        xb = (x + pos_ref[CMP_STRIDE + i:CMP_STRIDE + i + 1, :]).astype(BF16)
        acc_a += _dot(xa, w1_ref[i])
        acc_b += _dot(xb, w1_ref[CMP_STRIDE + i])
    pre = acc_a + pltpu.roll(acc_b, ng - 1, 0)
    hdn = (pre * jax.nn.sigmoid(pre)).astype(BF16)
    out = _dot(hdn, w2_ref[...])
    lane = _lane(out.shape)
    is_k = lane < HEAD_DIM
    ss = jnp.sum(jnp.where(is_k, out * out, 0.0), axis=-1, keepdims=True) * (1.0 / HEAD_DIM)
    o_ref[0] = jnp.where(is_k, out * lax.rsqrt(ss + EPS) * knw_ref[...], out)
    ot_ref[0] = _nt(w2t_ref[...], hdn)


def _compress(proj, pos, w1, w2, w2t, knw, b, s):
    ng = s // CMP_STRIDE
    x4 = proj.reshape(b, ng, CMP_STRIDE, NP)
    return pl.pallas_call(
        _compress_kernel,
        grid=(b,),
        in_specs=[
            pl.BlockSpec((1, ng, CMP_STRIDE, LANES), lambda i: (i, 0, 0, CB_KCVC)),
            pl.BlockSpec((CMP_LEN, LANES), lambda i: (0, 0)),
            pl.BlockSpec((CMP_LEN, LANES, 2 * CMP_HIDDEN), lambda i: (0, 0, 0)),
            pl.BlockSpec((2 * CMP_HIDDEN, LANES), lambda i: (0, 0)),
            pl.BlockSpec((LANES, 2 * CMP_HIDDEN), lambda i: (0, 0)),
            pl.BlockSpec((1, LANES), lambda i: (0, 0)),
        ],
        out_specs=[pl.BlockSpec((1, ng, LANES), lambda i: (i, 0, 0)),
                   pl.BlockSpec((1, LANES, ng), lambda i: (i, 0, 0))],
        out_shape=[jax.ShapeDtypeStruct((b, ng, LANES), F32), jax.ShapeDtypeStruct((b, LANES, ng), F32)],
        compiler_params=_params(("arbitrary",)),
        name="nsa_compress",
    )(x4, pos, w1, w2, w2t, knw)


def _softmax_update(sts, vts, m_old, acc_old, fixed=False):
    if fixed:
        return m_old, [acc + _dot(vt, jnp.exp2(st).astype(BF16)) for acc, vt, st in zip(acc_old, vts, sts)]
    m_new = [jnp.maximum(m, jnp.max(st, axis=0, keepdims=True)) for m, st in zip(m_old, sts)]
    ps = [jnp.exp2(st - m).astype(BF16) for st, m in zip(sts, m_new)]
    alphas = [jnp.exp2(mo - mn) for mo, mn in zip(m_old, m_new)]
    acc_new = [al * acc + _dot(vt, p) for al, acc, vt, p in zip(alphas, acc_old, vts, ps)]
    return m_new, acc_new


def _softmax_steps(sts, vts, m_refs, acc_refs, fixed=False):
    m_old = [None] * len(sts) if fixed else [r[...] for r in m_refs]
    m_new, acc_new = _softmax_update(sts, vts, m_old, [r[...] for r in acc_refs], fixed)
    for r, v in zip(() if fixed else m_refs, m_new):
        r[...] = v
    for r, v in zip(acc_refs, acc_new):
        r[...] = v


def _with_ones_row(vt):
    pad = jnp.where(_row((DV_PAD - HEAD_DIM, vt.shape[1])) == 0, 1.0, 0.0).astype(vt.dtype)
    return jnp.concatenate([vt, pad], axis=0)


def _bucket_bias(bidx, relb_ref, h, fill):
    acc = jnp.full(bidx.shape, fill, F32)
    for bk in range(N_BUCKETS):
        acc = jnp.where(bidx == bk, relb_ref[bk, h] * LOG2E, acc)
    return acc


def _group_sums(x, member):
    g = jnp.where(member, 1.0, 0.0).astype(BF16)
    hi = x.astype(BF16)
    lo = (x - hi.astype(F32)).astype(BF16)
    return _dot(hi, g) + _dot(lo, g)


def _dup_low_half(x):
    y = jnp.where(_lane(x.shape) < HEAD_DIM, x, 0.0)
    return y + pltpu.roll(y, HEAD_DIM, 1)


def _nsa_kernel(relb_ref, shift_ref, q_ref, g_ref, kcvc_ref, kcvct_ref, ksvs_ref, kwvw_ref, bidxct_ref, bidx2t_ref, ovt_ref,
                emt_ref, qnw_ref, knw_ref, o_ref,
                biasc_s, bias2_s, ks_s, vst_s, kw_s, vwt_s, kc_s, vct_s, qx_s, m_s, acc_s, *, tq, s_len, fixed):
    b = pl.program_id(0)
    shift = shift_ref[0] if fixed else 0.0
    i = pl.program_id(1)
    n_win = WINDOW // tq
    ng = s_len // CMP_STRIDE

    @pl.when((b == 0) & (i == 0))
    def _build_bias_tables():
        key = _row((tq, tq))
        qry = _lane((tq, tq))
        for h in range(N_HEADS):
            cols = slice(h * tq, (h + 1) * tq)
            far = jnp.full((tq, tq), relb_ref[N_BUCKETS - 1, h] * LOG2E - shift, F32)
            bias2_s[0, :, cols] = jnp.where(key <= qry, _bucket_bias(bidx2t_ref[0], relb_ref, h, NEG) - shift, NEG)
            bias2_s[1, :, cols] = _bucket_bias(bidx2t_ref[1], relb_ref, h, NEG) - shift
            bias2_s[2, :, cols] = far
            bias2_s[3, :, cols] = jnp.where(key > qry, far, NEG)
            bias2_s[4, :, cols] = jnp.full((tq, tq), NEG, F32)

        def body(t, carry):
            bi = bidxct_ref[:, _lane_tile(t, tq)]
            for h in range(N_HEADS):
                biasc_s[t, :, h * tq:(h + 1) * tq] = _bucket_bias(bi, relb_ref, h, NEG)
            return carry

        lax.fori_loop(0, s_len // tq, body, 0)

    @pl.when(i == 0)
    def _prep_kv():
        ch = 256

        def body(t, carry):
            rows = pl.ds(pl.multiple_of(t * ch, ch), ch)
            for src, kdst, vdst, widx in ((ksvs_ref, ks_s, vst_s, 0), (kwvw_ref, kw_s, vwt_s, 1)):
                x = src[rows, :]
                ss = _group_sums(x * x, _row((LANES, LANES)) < HEAD_DIM) * (1.0 / HEAD_DIM)
                kn = x * lax.rsqrt(ss + EPS) * knw_ref[widx:widx + 1, :]
                kdst[rows, 0:LANES] = _dup_low_half(kn).astype(BF16)
                vdst[:, _lane_tile(t, ch)] = _with_ones_row(x.T[HEAD_DIM:, :]).astype(BF16)
            ks_s[rows, LANES:] = emt_ref[rows, :]
            return carry

        lax.fori_loop(0, s_len // ch, body, 0)
        kc_s[...] = _dup_low_half(kcvc_ref[0]).astype(BF16)
        vct_s[...] = kcvct_ref[0][HEAD_DIM:, :].astype(BF16)

    q = q_ref[...]
    lane = _lane((tq, LANES))
    heads = range(N_HEADS)
    gw = 2 * LANES
    head_shift = HEAD_DIM.bit_length() - 1
    same_head = (lax.shift_right_logical(_row((gw, gw)), head_shift)
                 == lax.shift_right_logical(_lane((gw, gw)), head_shift))
    ss = _group_sums(q * q, same_head) * (1.0 / HEAD_DIM)
    qn = q * lax.rsqrt(ss + EPS) * qnw_ref[...] * (HEAD_DIM ** -0.5 * LOG2E)
    for h in heads:
        mine = (lane < HEAD_DIM) if h % 2 == 0 else (lane >= HEAD_DIM)
        qx_s[h, :, 0:LANES] = jnp.where(mine, qn[:, LANES * (h // 2):LANES * (h // 2 + 1)], 0.0).astype(BF16)

    lcs = [_nt(kc_s[...], qx_s[h, :, 0:LANES]) + biasc_s[i, :, h * tq:(h + 1) * tq] for h in heads]
    pcs = [jnp.where(lc > 0.5 * NEG, jnp.exp2(lc - jnp.max(lc, axis=0, keepdims=True)), 0.0) for lc in lcs]
    dens = [jnp.sum(pc, axis=0, keepdims=True) for pc in pcs]
    pcs = [pc / jnp.where(den > 0.0, den, 1.0) for pc, den in zip(pcs, dens)]
    o_cmp = [_dot(vct_s[...], pc.astype(BF16)) for pc in pcs]
    psum = (pcs[0] + pcs[1]) + (pcs[2] + pcs[3])

    n_slc = s_len // SLC_LEN
    n_sel = min(N_SEL, n_slc)
    blk = _row((n_slc, tq))
    tpos = i * tq + _lane((n_slc, tq))
    tblk = tpos // SLC_LEN
    valid = blk * SLC_LEN <= tpos

    p_hi = psum.astype(BF16)
    p_lo = (psum - p_hi.astype(F32)).astype(BF16)
    score = _dot(ovt_ref[...], p_hi) + _dot(ovt_ref[...], p_lo)
    forced = (blk == 0) | (blk == tblk) | (blk == tblk - 1)
    sc = jnp.where(forced, jnp.inf, jnp.where(valid, score[0:n_slc], -jnp.inf))
    rank = jnp.zeros((n_slc, tq), F32)
    for k in range(n_slc):
        ck = sc[k:k + 1, :]
        beats = (ck > sc) | ((ck == sc) & (blk > k))
        rank += jnp.where(beats, 1.0, 0.0)
    pen = jnp.where((rank < float(n_sel)) & valid, 0.0, NEG)
    pen = jnp.concatenate([pen, jnp.zeros((LANES - n_slc, tq), F32)], axis=0)
    pen_t = pen.T.astype(BF16)
    pen = pen.astype(BF16)
    for h in heads:
        qx_s[h, :, LANES:] = pen_t

    ms = [jnp.full((1, tq), NEG, F32)] * (2 * N_HEADS)
    accs = [jnp.zeros((DV_PAD, tq), F32)] * (2 * N_HEADS)
    for jj in range(n_win + 1):
        exists = i >= jj
        rows = pl.ds(pl.multiple_of(jnp.maximum(i - jj, 0) * tq, tq), tq)
        kind = jnp.where(exists, min(jj, 2), 4)
        kind_w = jnp.where(exists, 3 if jj == n_win else min(jj, 2), 4)
        ks = ks_s[rows, 0:LANES]
        kw = kw_s[rows, :]
        masked = _dot(ks_s[rows, LANES:], pen)
        sts = [_nt(ks, qx_s[h, :, 0:LANES]) + masked + bias2_s[kind, :, h * tq:(h + 1) * tq] for h in heads]
        sts += [_nt(kw, qx_s[h, :, 0:LANES]) + bias2_s[kind_w, :, h * tq:(h + 1) * tq] for h in heads]
        vts = [vst_s[:, rows]] * N_HEADS + [vwt_s[:, rows]] * N_HEADS
        ms, accs = _softmax_update(sts, vts, ms, accs, fixed)
    for h in heads:
        if not fixed:
            m_s[h] = ms[h]
        acc_s[h] = accs[h]
    o_win = [accs[N_HEADS + h][0:HEAD_DIM, :] / accs[N_HEADS + h][HEAD_DIM:HEAD_DIM + 1, :] for h in heads]

    def far_step(j, width):
        rows = pl.ds(j * tq if isinstance(j, int) else pl.multiple_of(j * tq, tq), width * tq)
        ks = ks_s[rows, :]
        sts = [_nt(ks, qx_s[h]) + (relb_ref[N_BUCKETS - 1, h] * LOG2E - shift) for h in heads]
        _softmax_steps(sts, [vst_s[:, rows]] * N_HEADS, [m_s.at[h] for h in heads], [acc_s.at[h] for h in heads],
                       fixed)

    n_far = jnp.maximum(i - n_win, 0)

    def far_body(p, carry):
        far_step(n_far - 2 * (p + 1), 2)
        return carry

    lax.fori_loop(0, n_far // 2, far_body, 0)
    pl.when(n_far % 2 == 1)(functools.partial(far_step, 0, 1))

    gt = jax.nn.sigmoid(g_ref[...]).T
    ys = []
    for h in heads:
        o_slc = acc_s[h, 0:HEAD_DIM, :] / acc_s[h, HEAD_DIM:HEAD_DIM + 1, :]
        ys.append(gt[3 * h:3 * h + 1, :] * o_cmp[h] + gt[3 * h + 1:3 * h + 2, :] * o_slc
                  + gt[3 * h + 2:3 * h + 3, :] * o_win[h])
    o_ref[...] = jnp.concatenate(ys, axis=0).T


def _nsa_attention(proj, kcvc, kcvct, rel_bias, shift, bidxct, bidx2t, ovt, emt, qnw, knw, b, s, tq, fixed):
    t = b * s
    nq = s // tq
    m_rows = N_HEADS * tq
    ng = s // CMP_STRIDE
    kern = functools.partial(_nsa_kernel, tq=tq, s_len=s, fixed=fixed)
    return pl.pallas_call(
        kern,
        grid=(b, nq),
        in_specs=[
            pl.BlockSpec(memory_space=pltpu.SMEM),
            pl.BlockSpec(memory_space=pltpu.SMEM),
            pl.BlockSpec((tq, 2 * LANES), lambda bi, i: (bi * nq + i, CB_NQ)),
            pl.BlockSpec((tq, LANES), lambda bi, i: (bi * nq + i, CB_GATE)),
            pl.BlockSpec((1, ng, LANES), lambda bi, i: (bi, 0, 0)),
            pl.BlockSpec((1, LANES, ng), lambda bi, i: (bi, 0, 0)),
            pl.BlockSpec((s, LANES), lambda bi, i: (bi, CB_KSVS)),
            pl.BlockSpec((s, LANES), lambda bi, i: (bi, CB_KWVW)),
            pl.BlockSpec((ng, s), lambda bi, i: (0, 0)),
            pl.BlockSpec((2, tq, tq), lambda bi, i: (0, 0, 0)),
            pl.BlockSpec((LANES, ng), lambda bi, i: (0, 0)),
            pl.BlockSpec((s, LANES), lambda bi, i: (0, 0)),
            pl.BlockSpec((1, 2 * LANES), lambda bi, i: (0, 0)),
            pl.BlockSpec((2, LANES), lambda bi, i: (0, 0)),
        ],
        out_specs=pl.BlockSpec((tq, 2 * LANES), lambda bi, i: (bi * nq + i, 0)),
        out_shape=jax.ShapeDtypeStruct((t, GROUP_WIDTH), F32),
        scratch_shapes=[
            pltpu.VMEM((nq, ng, m_rows), F32),
            pltpu.VMEM((5, tq, m_rows), F32),
            pltpu.VMEM((s, 2 * LANES), BF16),
            pltpu.VMEM((DV_PAD, s), BF16),
            pltpu.VMEM((s, LANES), BF16),
            pltpu.VMEM((DV_PAD, s), BF16),
            pltpu.VMEM((ng, LANES), BF16),
            pltpu.VMEM((HEAD_DIM, ng), BF16),
            pltpu.VMEM((N_HEADS, tq, 2 * LANES), BF16),
            pltpu.VMEM((N_HEADS, 1, tq), F32),
            pltpu.VMEM((N_HEADS, DV_PAD, tq), F32),
        ],
        compiler_params=_params(("arbitrary", "arbitrary")),
        name="nsa_attention",
    )(rel_bias, shift, proj, proj, kcvc, kcvct, proj, proj, bidxct, bidx2t, ovt, emt, qnw, knw)


def _mla_prep_kernel(shift_ref, cq_ref, ckv_ref, kr_ref, qaw_ref, kvw_ref, wq_ref, wqs_ref, wk_ref, wvt_ref, qnw_ref, knw_ref,
                     cq_t_ref, ck_t_ref, sk_t_ref, qo_ref, ko_ref, vto_ref):
    cq = cq_ref[...]
    ms = jnp.sum(cq * cq, axis=-1, keepdims=True) * (1.0 / Q_LORA)
    hq = (cq * lax.rsqrt(ms + EPS) * qaw_ref[...]).astype(BF16)
    qf = _dot(hq, wq_ref[...])
    qsw = _dot(hq, wqs_ref[...])
    ckv = ckv_ref[...]
    ms = jnp.mean(ckv * ckv, axis=-1, keepdims=True)
    hkv = (ckv * lax.rsqrt(ms + EPS) * kvw_ref[...]).astype(BF16)
    kf = _dot(hkv, wk_ref[...])
    vt = _nt(wvt_ref[...], hkv)
    vto_ref[...] = jnp.where(_row(vt.shape) % DV_PAD == HEAD_DIM, 1.0, vt).astype(BF16)
    krb = kr_ref[...]
    kr_rot = krb * ck_t_ref[...] + pltpu.roll(krb, HEAD_DIM, 1) * sk_t_ref[...]
    is_shift = _lane((cq.shape[0], LANES)) == QK_DIM
    for h in range(N_HEADS):
        cols = slice(LANES * h, LANES * (h + 1))
        x = qf[:, cols] * cq_t_ref[...] + qsw[:, cols] * sk_t_ref[...]
        ss = jnp.sum(x * x, axis=-1, keepdims=True) * (1.0 / QK_DIM)
        qn = x * lax.rsqrt(ss + EPS) * qnw_ref[...]
        qo_ref[:, cols] = jnp.where(is_shift, -shift_ref[0], qn).astype(BF16)
        k = kf[:, cols] + kr_rot
        ss = jnp.sum(k * k, axis=-1, keepdims=True) * (1.0 / QK_DIM)
        ko_ref[:, cols] = jnp.where(is_shift, 1.0, k * lax.rsqrt(ss + EPS) * knw_ref[...]).astype(BF16)


def _mla_prep(proj, shift, qaw, kvw, wq, wqs, wk, wvt, qnw, knw, tabs, s, tm):
    t = proj.shape[0]
    npos = s // tm
    row = lambda i: (i, 0)
    const = lambda i: (0, 0)
    tab = pl.BlockSpec((tm, LANES), lambda i: (i % npos, 0))
    out = jax.ShapeDtypeStruct((t, N_HEADS * LANES), BF16)
    return pl.pallas_call(
        _mla_prep_kernel,
        grid=(t // tm,),
        in_specs=[
            pl.BlockSpec(memory_space=pltpu.SMEM),
            pl.BlockSpec((tm, 2 * LANES), lambda i: (i, CB_CQ)),
            pl.BlockSpec((tm, LANES), lambda i: (i, CB_CKV)),
            pl.BlockSpec((tm, LANES), lambda i: (i, CB_KR)),
            pl.BlockSpec((1, 2 * LANES), const),
            pl.BlockSpec((1, LANES), const),
            pl.BlockSpec((2 * LANES, N_HEADS * LANES), const),
            pl.BlockSpec((2 * LANES, N_HEADS * LANES), const),
            pl.BlockSpec((LANES, N_HEADS * LANES), const),
            pl.BlockSpec((N_HEADS * DV_PAD, LANES), const),
            pl.BlockSpec((1, LANES), const),
            pl.BlockSpec((1, LANES), const),
            tab, tab, tab,
        ],
        out_specs=[pl.BlockSpec((tm, N_HEADS * LANES), row), pl.BlockSpec((tm, N_HEADS * LANES), row),
                   pl.BlockSpec((N_HEADS * DV_PAD, tm), lambda i: (0, i))],
        out_shape=[out, out, jax.ShapeDtypeStruct((N_HEADS * DV_PAD, t), BF16)],
        compiler_params=_params(("arbitrary",)),
        name="mla_prep",
    )(shift, proj, proj, proj, qaw, kvw, wq, wqs, wk, wvt, qnw, knw, *tabs)


def _mla_attn_kernel(q_ref, k_ref, vt_ref, o_ref, m_s, acc_s, *, tq, fixed):
    i = pl.program_id(1)
    causal = _row((tq, tq)) <= _lane((tq, tq))
    for h in range(N_HEADS):
        if not fixed:
            m_s[h] = jnp.full((1, tq), NEG, F32)
        acc_s[h] = jnp.zeros((DV_PAD, tq), F32)

    def step(j, width, masked):
        tk = width * tq
        rows = pl.ds(j * tq if isinstance(j, int) else pl.multiple_of(j * tq, tq), tk)
        sts = []
        for h in range(N_HEADS):
            cols = slice(LANES * h, LANES * (h + 1))
            st = _nt(k_ref[rows, cols], q_ref[:, cols])
            sts.append(jnp.where(causal, st, NEG) if masked else st)
        vts = [vt_ref[DV_PAD * h:DV_PAD * (h + 1), rows] for h in range(N_HEADS)]
        _softmax_steps(sts, vts, [m_s.at[h] for h in range(N_HEADS)], [acc_s.at[h] for h in range(N_HEADS)], fixed)

    step(i, 1, True)

    def body(jj, carry):
        step(i - 2 * jj, 2, False)
        return carry

    lax.fori_loop(1, i // 2 + 1, body, 0)
    pl.when(i % 2 == 1)(functools.partial(step, 0, 1, False))
    yt = jnp.concatenate([acc_s[h, 0:HEAD_DIM, :] / acc_s[h, HEAD_DIM:HEAD_DIM + 1, :] for h in range(N_HEADS)],
                         axis=0)
    o_ref[...] = yt.T


def _mla_attention(qm, km, vmt, b, s, tq, fixed):
    t = b * s
    nq = s // tq
    w = N_HEADS * LANES
    return pl.pallas_call(
        functools.partial(_mla_attn_kernel, tq=tq, fixed=fixed),
        grid=(b, nq),
        in_specs=[
            pl.BlockSpec((tq, w), lambda bi, i: (bi * nq + i, 0)),
            pl.BlockSpec((s, w), lambda bi, i: (bi, 0)),
            pl.BlockSpec((N_HEADS * DV_PAD, s), lambda bi, i: (0, bi)),
        ],
        out_specs=pl.BlockSpec((tq, 2 * LANES), lambda bi, i: (bi * nq + i, 0)),
        out_shape=jax.ShapeDtypeStruct((t, GROUP_WIDTH), F32),
        scratch_shapes=[pltpu.VMEM((N_HEADS, 1, tq), F32), pltpu.VMEM((N_HEADS, DV_PAD, tq), F32)],
        compiler_params=_params(("arbitrary", "arbitrary")),
        name="mla_attention",
    )(qm, km, vmt)


def _sb_kernel(q_ref, k_ref, v_ref, o_ref, kb_s, vt_s, q_s, r_s, acc_s, kmax_s, *, tq, s_len):
    i = pl.program_id(1)
    gw = 2 * LANES
    head_shift = HEAD_DIM.bit_length() - 1
    same_head = (lax.shift_right_logical(_row((gw, gw)), head_shift)
                 == lax.shift_right_logical(_lane((gw, gw)), head_shift))

    @pl.when(i == 0)
    def _cast_kv():
        ch = 256

        def body(t, kmax):
            rows = pl.ds(pl.multiple_of(t * ch, ch), ch)
            k = k_ref[rows, :]
            kb_s[rows, :] = k.astype(BF16)
            vt_s[:, _lane_tile(t, ch)] = v_ref[rows, :].T.astype(BF16)
            return jnp.maximum(kmax, _group_sums(k * k, same_head))

        kmax = lax.fori_loop(0, s_len // ch, body, jnp.zeros((ch, gw), F32))
        kmax_s[0] = jnp.max(kmax)

    key = _row((tq, tq))
    qry = _lane((tq, tq))
    strict = key < qry
    tri = jnp.where(key <= qry, 1.0, 0.0).astype(BF16)
    tri2 = jnp.concatenate([tri, tri], axis=1)
    lane = _lane((tq, LANES))
    q = q_ref[...] * (HEAD_DIM ** -0.5 * LOG2E)
    for h in range(N_HEADS):
        mine = (lane < HEAD_DIM) if h % 2 == 0 else (lane >= HEAD_DIM)
        q_s[h] = jnp.where(mine, q[:, LANES * (h // 2):LANES * (h // 2 + 1)], 0.0).astype(BF16)
        r_s[h] = jnp.zeros((1, tq), F32)
        acc_s[h] = jnp.zeros((HEAD_DIM, tq), F32)
    qmax = jnp.max(_group_sums(q * q, same_head))
    z_bound = jnp.sqrt(jnp.full((1, tq), qmax * kmax_s[0], F32)) * 1.01 + 1.0

    heads = range(N_HEADS)

    def step(j, width, masked):
        rs = [r_s[h] for h in heads]
        accs = [acc_s[h] for h in heads]
        tiles = [j + width - 1 - w for w in range(width)]
        rows = [pl.ds(t * tq if isinstance(t, int) else pl.multiple_of(t * tq, tq), tq) for t in tiles]
        zs = [[_nt(kb_s[r, LANES * (h // 2):LANES * (h // 2 + 1)], q_s[h]) for h in heads] for r in rows]
        part = []
        for zt in zs:
            negabs = [pltpu.bitcast(pltpu.bitcast(z, jnp.uint32) | jnp.uint32(0x80000000), F32) for z in zt]
            sps = [jnp.maximum(z, 0.0) + jnp.log2(1.0 + jnp.exp2(na)) for z, na in zip(zt, negabs)]
            if masked:
                sps = [jnp.where(strict, sp, 0.0) for sp in sps]
            his = [sp.astype(BF16) for sp in sps]
            los = [(sp - hi.astype(F32)).astype(BF16) for sp, hi in zip(sps, his)]
            part.append([_dot(tri2, jnp.concatenate([hi, lo], axis=0)) for hi, lo in zip(his, los)])
        for r, zt, pt in zip(rows, zs, part):
            csums = [p + rc for p, rc in zip(pt, rs)]
            als = [jnp.exp2(z - cs) for z, cs in zip(zt, csums)]
            if masked:
                als = [jnp.where(strict, a, 0.0) for a in als]
            accs = [acc + _dot(vt_s[HEAD_DIM * h:HEAD_DIM * (h + 1), r], als[h].astype(BF16))
                    for h, acc in zip(heads, accs)]
            rs = [cs[0:1, :] for cs in csums]
        for h in heads:
            acc_s[h] = accs[h]
            r_s[h] = rs[h]

    def live():
        r_min = jnp.minimum(jnp.minimum(r_s[0], r_s[1]), jnp.minimum(r_s[2], r_s[3]))
        return jnp.max(z_bound - r_min) >= -150.0

    step(i, 1, True)
    pl.when(i >= 1)(functools.partial(step, i - 1, 1, False))

    n_rest = jnp.maximum(i - 1, 0)
    n_pairs = n_rest // 2

    def cond(carry):
        p, alive = carry
        return (p < n_pairs) & alive

    def body(carry):
        p, _ = carry
        step(i - 1 - 2 * (p + 1), 2, False)
        return p + 1, live()

    _, alive = lax.while_loop(cond, body, (jnp.int32(0), live()))
    pl.when((n_rest % 2 == 1) & alive)(functools.partial(step, 0, 1, False))
    o_ref[...] = jnp.concatenate([acc_s[h] for h in range(N_HEADS)], axis=0).T


def _sb_attention(proj, b, s, tq):
    t = b * s
    nq = s // tq
    w = 2 * LANES
    return pl.pallas_call(
        functools.partial(_sb_kernel, tq=tq, s_len=s),
        grid=(b, nq),
        in_specs=[
            pl.BlockSpec((tq, w), lambda bi, i: (bi * nq + i, CB_SQ)),
            pl.BlockSpec((s, w), lambda bi, i: (bi, CB_SK)),
            pl.BlockSpec((s, w), lambda bi, i: (bi, CB_SV)),
        ],
        out_specs=pl.BlockSpec((tq, w), lambda bi, i: (bi * nq + i, 0)),
        out_shape=jax.ShapeDtypeStruct((t, GROUP_WIDTH), F32),
        scratch_shapes=[pltpu.VMEM((s, w), BF16), pltpu.VMEM((w, s), BF16),
                        pltpu.VMEM((N_HEADS, tq, LANES), BF16),
                        pltpu.VMEM((N_HEADS, 1, tq), F32), pltpu.VMEM((N_HEADS, HEAD_DIM, tq), F32),
                        pltpu.SMEM((1,), F32)],
        compiler_params=_params(("arbitrary", "arbitrary")),
        name="sb_attention",
    )(proj, proj, proj)


def _post_kernel(a_ref, ap_ref, yb_ref, yc_ref, yd_ref, x_ref, cw_ref, cb_ref, onw_ref, wo_ref, n2w_ref,
                 w1_ref, w2_ref, o_ref, *, tm, s_len, ffc):
    i = pl.program_id(0)
    a = a_ref[...]
    gw = GROUP_WIDTH
    v = a[:, gw:2 * gw] * a[:, 2 * gw:3 * gw]
    ap = ap_ref[...]
    first = (i * tm) % s_len == 0
    vp = jnp.where(first, 0.0, ap[:, gw:2 * gw] * ap[:, 2 * gw:3 * gw])
    row = _row(v.shape)
    v1 = jnp.where(row == 0, vp[7:8, :], pltpu.roll(v, 1, 0))
    v2 = jnp.where(row == 0, vp[6:7, :], jnp.where(row == 1, vp[7:8, :], pltpu.roll(v, 2, 0)))
    conv = cw_ref[0:1, :] * v2 + cw_ref[1:2, :] * v1 + cw_ref[2:3, :] * v
    ya = a[:, 0:gw] * (conv + cb_ref[...])

    mix = None
    for g, y in enumerate((ya, yb_ref[...], yc_ref[...], yd_ref[...])):
        ms = jnp.mean(y * y, axis=-1, keepdims=True)
        yn = (y * lax.rsqrt(ms + EPS) * onw_ref[:, gw * g:gw * (g + 1)]).astype(BF16)
        part = _dot(yn, wo_ref[gw * g:gw * (g + 1), :])
        mix = part if mix is None else mix + part
    x1 = x_ref[...] + mix

    ms = jnp.mean(x1 * x1, axis=-1, keepdims=True)
    h2 = (x1 * lax.rsqrt(ms + EPS) * n2w_ref[...]).astype(BF16)
    ff = None
    for cidx in range(D_FF // ffc):
        u = _dot(h2, w1_ref[:, ffc * cidx:ffc * (cidx + 1)])
        u = jnp.square(jnp.maximum(u, 0.0)).astype(BF16)
        part = _dot(u, w2_ref[ffc * cidx:ffc * (cidx + 1), :])
        ff = part if ff is None else ff + part
    o_ref[...] = x1 + ff


def _post(proj, yb, yc, yd, x2d, cw, cb, onw, wo, n2w, w1, w2, l, s, tm):
    t = x2d.shape[0]
    gw = GROUP_WIDTH
    row = lambda i: (i, 0)
    const = lambda i: (0, 0)
    layer = lambda i: (l, 0, 0)
    once = pl.Buffered(1)
    kern = functools.partial(_post_kernel, tm=tm, s_len=s, ffc=1024)
    return pl.pallas_call(
        kern,
        grid=(t // tm,),
        in_specs=[
            pl.BlockSpec((tm, 3 * gw), row),
            pl.BlockSpec((8, 3 * gw), lambda i: (jnp.maximum(i * (tm // 8) - 1, 0), 0)),
            pl.BlockSpec((tm, gw), row),
            pl.BlockSpec((tm, gw), row),
            pl.BlockSpec((tm, gw), row),
            pl.BlockSpec((tm, D_MODEL), row),
            pl.BlockSpec((3, gw), const),
            pl.BlockSpec((1, gw), const),
            pl.BlockSpec((1, D_MODEL), const),
            pl.BlockSpec((None, D_MODEL, D_MODEL), layer, pipeline_mode=once),
            pl.BlockSpec((1, D_MODEL), const),
            pl.BlockSpec((None, D_MODEL, D_FF), layer, pipeline_mode=once),
            pl.BlockSpec((None, D_FF, D_MODEL), layer, pipeline_mode=once),
        ],
        out_specs=pl.BlockSpec((tm, D_MODEL), row),
        out_shape=jax.ShapeDtypeStruct((t, D_MODEL), F32),
        compiler_params=_params(("arbitrary",)),
        name="post",
    )(proj, proj, yb, yc, yd, x2d, cw, cb, onw, wo, n2w, w1, w2)


def _t5_bucket(dist):
    max_exact = N_BUCKETS // 2
    d = jnp.maximum(dist, 0)
    large = max_exact + (jnp.log(jnp.maximum(d, 1).astype(F32) / max_exact)
                         / math.log(MAX_DISTANCE / max_exact) * (N_BUCKETS - max_exact)).astype(jnp.int32)
    large = jnp.minimum(large, N_BUCKETS - 1)
    return jnp.where(d < max_exact, d, large)


def _tables(s, tq_nsa):
    n_cmp = (s - CMP_LEN) // CMP_STRIDE + 1
    ng = s // CMP_STRIDE
    n_slc = s // SLC_LEN
    tpos = jnp.arange(s)[None, :]
    n = jnp.arange(ng)[:, None]
    dist_c = tpos - (n * CMP_STRIDE + CMP_LEN - 1)
    bidxct = jnp.where((dist_c >= 0) & (n < n_cmp), _t5_bucket(dist_c), -1).astype(jnp.int32)
    key = jnp.arange(tq_nsa)[:, None]
    qry = jnp.arange(tq_nsa)[None, :]
    bidx2t = jnp.stack([_t5_bucket(qry - key), _t5_bucket(tq_nsa + qry - key)]).astype(jnp.int32)
    starts = np.arange(n_cmp) * CMP_STRIDE
    ends = starts + CMP_LEN
    s0 = np.arange(n_slc) * SLC_LEN
    s1 = s0 + SLC_LEN
    ovl = np.clip(np.minimum(ends[:, None], s1[None]) - np.maximum(starts[:, None], s0[None]), 0, None) / CMP_LEN
    ovt = np.zeros((LANES, ng), np.float32)
    ovt[:n_slc, :n_cmp] = ovl.T
    emt = (np.arange(LANES)[None, :] == (np.arange(s) // SLC_LEN)[:, None]).astype(np.float32)
    inv = 1.0 / (ROPE_THETA ** (jnp.arange(0, ROPE_DIM, 2, dtype=F32) / ROPE_DIM))
    ang = jnp.arange(s, dtype=F32)[:, None] * inv[None, :]
    cos, sin = jnp.cos(ang), jnp.sin(ang)
    z16 = jnp.zeros((s, 16), F32)
    z32 = jnp.zeros((s, 32), F32)
    z64 = jnp.zeros((s, 64), F32)
    one64 = jnp.ones((s, 64), F32)
    cq_t = jnp.concatenate([one64, cos, cos, z32], axis=1)
    ck_t = jnp.concatenate([z64, cos, cos, z32], axis=1)
    sk_t = jnp.concatenate([z64, -sin, sin, z32], axis=1)
    return dict(bidxct=bidxct, bidx2t=bidx2t, ovt=jnp.asarray(ovt, BF16), emt=jnp.asarray(emt, BF16),
                rope=(cq_t, ck_t, sk_t))


def _shift_bound(d, q_gain, k_gain, extra=0.0):
    return (d * jnp.max(jnp.abs(q_gain)) * jnp.max(jnp.abs(k_gain)) * 1.01 + extra + 0.1).reshape(1)


MAX_SHIFT = 60.0


def _pad_cols(w, width):
    return jnp.pad(w, ((0, 0), (0, width - w.shape[1])))


def _layer_weights(l, conv_w, conv_b, nsa_q_norm, nsa_k_norm, cmp_pos, cmp_w1, cmp_w2, mla_q_a_norm,
                   mla_kv_norm, mla_wq_b, mla_wkv_b, mla_q_norm, mla_k_norm, out_norm_w, norm2_w):
    w1 = cmp_w1[l].reshape(2, CMP_LEN, HEAD_DIM, CMP_HIDDEN)
    zw = jnp.zeros((CMP_LEN, HEAD_DIM, CMP_HIDDEN), F32)
    cw1 = jnp.concatenate([jnp.concatenate([w1[0], zw], axis=2), jnp.concatenate([zw, w1[1]], axis=2)],
                          axis=1).astype(BF16)
    zc = jnp.zeros((CMP_HIDDEN, HEAD_DIM), F32)
    cw2 = jnp.concatenate([jnp.concatenate([cmp_w2[l, 0], zc], axis=1),
                           jnp.concatenate([zc, cmp_w2[l, 1]], axis=1)], axis=0).astype(BF16)
    cpos = jnp.concatenate([cmp_pos[l, 0], cmp_pos[l, 1]], axis=1)
    kn = nsa_k_norm[l]
    ones64 = jnp.ones((HEAD_DIM,), F32)
    knw_c = jnp.concatenate([kn[0], ones64])[None, :]
    knw_sw = jnp.stack([jnp.concatenate([kn[1], ones64]), jnp.concatenate([kn[2], ones64])])
    qnw = jnp.tile(nsa_q_norm[l], N_HEADS)[None, :]
    wq = mla_wq_b[l].reshape(Q_LORA, N_HEADS, QK_DIM)
    half = ROPE_DIM // 2
    wqs = jnp.concatenate([jnp.zeros((Q_LORA, N_HEADS, HEAD_DIM), F32), wq[:, :, HEAD_DIM + half:],
                           wq[:, :, HEAD_DIM:HEAD_DIM + half]], axis=2)
    pad_q = lambda w: jnp.pad(w, ((0, 2 * LANES - Q_LORA), (0, 0), (0, LANES - QK_DIM))).reshape(
        2 * LANES, N_HEADS * LANES).astype(BF16)
    wkv = mla_wkv_b[l].reshape(KV_LORA, N_HEADS, 2 * HEAD_DIM)
    wk = jnp.pad(wkv[:, :, :HEAD_DIM], ((0, 0), (0, 0), (0, LANES - HEAD_DIM))).reshape(KV_LORA, N_HEADS * LANES)
    wvt = jnp.pad(wkv[:, :, HEAD_DIM:], ((0, 0), (0, 0), (0, DV_PAD - HEAD_DIM))).reshape(KV_LORA, -1).T
    return dict(
        cw1=cw1, cw2=cw2, cw2t=cw2.T, cpos=cpos, knw_c=knw_c, knw_sw=knw_sw, qnw=qnw,
        qaw=_pad_cols(mla_q_a_norm[l][None, :], 2 * LANES), kvw=mla_kv_norm[l][None, :],
        wq=pad_q(wq), wqs=pad_q(wqs), wk=wk.astype(BF16), wvt=wvt.astype(BF16),
        mqn=_pad_cols(mla_q_norm[l][None, :] * (QK_DIM ** -0.5 * LOG2E), LANES),
        mkn=_pad_cols(mla_k_norm[l][None, :], LANES),
        cw=conv_w[l], cb=conv_b[l][None, :], onw=out_norm_w[l][None, :], n2w=norm2_w[l][None, :])


TM_PROJ = 512
TM_PREP = 512
TM_POST = 512
TQ_NSA = 256
TQ_MLA = 256
TQ_SB = 256


def kernel(x, rel_bias, norm1_w, w_in, conv_w, conv_b, nsa_q_norm, nsa_k_norm, cmp_pos, cmp_w1, cmp_w2,
           mla_q_a_norm, mla_kv_norm, mla_wq_b, mla_wkv_b, mla_q_norm, mla_k_norm, out_norm_w, w_out, norm2_w,
           ffn_w1, ffn_w2):
    b, s, d = x.shape
    depth = w_in.shape[0]
    tabs = _tables(s, TQ_NSA)
    x2d = x.reshape(b * s, d)
    wo_b, w1_b, w2_b = w_out.astype(BF16), ffn_w1.astype(BF16), ffn_w2.astype(BF16)
    for l in range(depth):
        w = _layer_weights(l, conv_w, conv_b, nsa_q_norm, nsa_k_norm, cmp_pos, cmp_w1, cmp_w2,
                           mla_q_a_norm, mla_kv_norm, mla_wq_b, mla_wkv_b, mla_q_norm, mla_k_norm, out_norm_w,
                           norm2_w)
        proj = _inproj(x2d, norm1_w[l][None, :], w_in, l, TM_PROJ)
        kcvc, kcvct = _compress(proj, w["cpos"], w["cw1"], w["cw2"], w["cw2t"], w["knw_c"], b, s)
        nsa_shift = _shift_bound(HEAD_DIM, w["qnw"] * (HEAD_DIM ** -0.5 * LOG2E), w["knw_sw"][:, :HEAD_DIM],
                                 jnp.max(jnp.abs(rel_bias)) * LOG2E)
        nsa_args = (proj, kcvc, kcvct, rel_bias, nsa_shift, tabs["bidxct"], tabs["bidx2t"], tabs["ovt"], tabs["emt"],
                    w["qnw"], w["knw_sw"])
        yb = lax.cond(nsa_shift[0] <= MAX_SHIFT,
                      lambda *a: _nsa_attention(*a, b, s, TQ_NSA, True),
                      lambda *a: _nsa_attention(*a, b, s, TQ_NSA, False), *nsa_args)
        mla_shift = _shift_bound(QK_DIM, w["mqn"], w["mkn"])
        qm, km, vmt = _mla_prep(proj, mla_shift, w["qaw"], w["kvw"], w["wq"], w["wqs"], w["wk"], w["wvt"], w["mqn"],
                                w["mkn"], tabs["rope"], s, TM_PREP)
        yc = lax.cond(mla_shift[0] <= MAX_SHIFT,
                      lambda *a: _mla_attention(*a, b, s, TQ_MLA, True),
                      lambda *a: _mla_attention(*a, b, s, TQ_MLA, False), qm, km, vmt)
        yd = _sb_attention(proj, b, s, TQ_SB)
        x2d = _post(proj, yb, yc, yd, x2d, w["cw"], w["cb"], w["onw"], wo_b, w["n2w"], w1_b, w2_b, l, s, TM_POST)
    return x2d.reshape(b, s, d)
```

```python
import functools
import math

import jax
import jax.numpy as jnp
import numpy as np
from jax import lax
from jax.experimental import pallas as pl
from jax.experimental.pallas import tpu as pltpu

F32 = jnp.float32
BF16 = jnp.bfloat16

D_MODEL = 1024
GROUP_WIDTH = 256
HEAD_DIM = 64
N_HEADS = 4
LANES = 128
CMP_LEN = 32
CMP_STRIDE = 16
SLC_LEN = 64
N_SEL = 16
WINDOW = 512
CMP_HIDDEN = 256
Q_LORA = 192
KV_LORA = 128
ROPE_DIM = 32
QK_DIM = 96
ROPE_THETA = 10000.0
N_BUCKETS = 32
MAX_DISTANCE = 128
D_FF = 4096
EPS = 1e-6
NEG = -1e30
LOG2E = math.log2(math.e)
DV_PAD = 80

NP = 2816
CB_NQ = 3
CB_KCVC = 8
CB_KSVS = 9
CB_KWVW = 10
CB_GATE = 11
CB_CQ = 6
CB_CKV = 14
CB_KR = 15
CB_SQ = 8
CB_SK = 9
CB_SV = 10

VMEM_LIMIT = 56 * 1024 * 1024

NT_DIMS = (((1,), (1,)), ((), ()))


def _params(sem):
    return pltpu.CompilerParams(dimension_semantics=sem, vmem_limit_bytes=VMEM_LIMIT)


def _nt(a, b):
    return lax.dot_general(a, b, NT_DIMS, preferred_element_type=F32)


def _dot(a, b):
    return jnp.dot(a, b, preferred_element_type=F32)


def _lane(shape):
    return lax.broadcasted_iota(jnp.int32, shape, len(shape) - 1)


def _row(shape):
    return lax.broadcasted_iota(jnp.int32, shape, len(shape) - 2)


def _lane_tile(j, width):
    return pl.ds(pl.multiple_of(j * width, width), width)


_IN_SEGMENTS = (
    ((0, 1408), 0),
    ((1408, 1420), 1408),
    ((1420, 1612), 1536),
    ((1612, 1740), 1792),
    ((1756, 1772), 1920),
    ((1740, 1756), 1936),
    ((1740, 1772), 1984),
    ((1772, 2540), 2048),
)
IN_COLS = 2540


def _inproj_kernel(x_ref, nw_ref, w_ref, o_ref, w_s):
    @pl.when(pl.program_id(0) == 0)
    def _relayout_weights():
        end = 0
        for (a, b), dst in _IN_SEGMENTS:
            if dst > end:
                w_s[:, end:dst] = jnp.zeros((D_MODEL, dst - end), BF16)
            w_s[:, dst:dst + b - a] = w_ref[:, a:b].astype(BF16)
            end = dst + b - a
        assert end == NP

    x = x_ref[...]
    ms = jnp.mean(x * x, axis=-1, keepdims=True)
    h = (x * lax.rsqrt(ms + EPS) * nw_ref[...]).astype(BF16)
    o_ref[...] = _dot(h, w_s[...])


def _inproj(x2d, nw, w_in, l, tm):
    t = x2d.shape[0]
    return pl.pallas_call(
        _inproj_kernel,
        grid=(t // tm,),
        in_specs=[
            pl.BlockSpec((tm, D_MODEL), lambda i: (i, 0)),
            pl.BlockSpec((1, D_MODEL), lambda i: (0, 0)),
            pl.BlockSpec((None, D_MODEL, IN_COLS), lambda i: (l, 0, 0), pipeline_mode=pl.Buffered(1)),
        ],
        out_specs=pl.BlockSpec((tm, NP), lambda i: (i, 0)),
        out_shape=jax.ShapeDtypeStruct((t, NP), F32),
        scratch_shapes=[pltpu.VMEM((D_MODEL, NP), BF16)],
        compiler_params=_params(("arbitrary",)),
        name="inproj",
    )(x2d, nw, w_in)


def _compress_kernel(x_ref, pos_ref, w1_ref, w2_ref, w2t_ref, knw_ref, o_ref, ot_ref):
    ng = x_ref.shape[1]
    acc_a = jnp.zeros((ng, 2 * CMP_HIDDEN), F32)
    acc_b = jnp.zeros((ng, 2 * CMP_HIDDEN), F32)
    for i in range(CMP_STRIDE):
        x = x_ref[0, :, i, :]
        xa = (x + pos_ref[i:i + 1, :]).astype(BF16)
        xb = (x + pos_ref[CMP_STRIDE + i:CMP_STRIDE + i + 1, :]).astype(BF16)
        acc_a += _dot(xa, w1_ref[i])
        acc_b += _dot(xb, w1_ref[CMP_STRIDE + i])
    pre = acc_a + pltpu.roll(acc_b, ng - 1, 0)
    hdn = (pre * jax.nn.sigmoid(pre)).astype(BF16)
    out = _dot(hdn, w2_ref[...])
    lane = _lane(out.shape)
    is_k = lane < HEAD_DIM
    ss = jnp.sum(jnp.where(is_k, out * out, 0.0), axis=-1, keepdims=True) * (1.0 / HEAD_DIM)
    o_ref[0] = jnp.where(is_k, out * lax.rsqrt(ss + EPS) * knw_ref[...], out)
    ot_ref[0] = _nt(w2t_ref[...], hdn)


def _compress(proj, pos, w1, w2, w2t, knw, b, s):
    ng = s // CMP_STRIDE
    x4 = proj.reshape(b, ng, CMP_STRIDE, NP)
    return pl.pallas_call(
        _compress_kernel,
        grid=(b,),
        in_specs=[
            pl.BlockSpec((1, ng, CMP_STRIDE, LANES), lambda i: (i, 0, 0, CB_KCVC)),
            pl.BlockSpec((CMP_LEN, LANES), lambda i: (0, 0)),
            pl.BlockSpec((CMP_LEN, LANES, 2 * CMP_HIDDEN), lambda i: (0, 0, 0)),
            pl.BlockSpec((2 * CMP_HIDDEN, LANES), lambda i: (0, 0)),
            pl.BlockSpec((LANES, 2 * CMP_HIDDEN), lambda i: (0, 0)),
            pl.BlockSpec((1, LANES), lambda i: (0, 0)),
        ],
        out_specs=[pl.BlockSpec((1, ng, LANES), lambda i: (i, 0, 0)),
                   pl.BlockSpec((1, LANES, ng), lambda i: (i, 0, 0))],
        out_shape=[jax.ShapeDtypeStruct((b, ng, LANES), F32), jax.ShapeDtypeStruct((b, LANES, ng), F32)],
        compiler_params=_params(("arbitrary",)),
        name="nsa_compress",
    )(x4, pos, w1, w2, w2t, knw)


def _softmax_update(sts, vts, m_old, acc_old, fixed=False):
    if fixed:
        return m_old, [acc + _dot(vt, jnp.exp2(st).astype(BF16)) for acc, vt, st in zip(acc_old, vts, sts)]
    m_new = [jnp.maximum(m, jnp.max(st, axis=0, keepdims=True)) for m, st in zip(m_old, sts)]
    ps = [jnp.exp2(st - m).astype(BF16) for st, m in zip(sts, m_new)]
    alphas = [jnp.exp2(mo - mn) for mo, mn in zip(m_old, m_new)]
    acc_new = [al * acc + _dot(vt, p) for al, acc, vt, p in zip(alphas, acc_old, vts, ps)]
    return m_new, acc_new


def _softmax_steps(sts, vts, m_refs, acc_refs, fixed=False):
    m_old = [None] * len(sts) if fixed else [r[...] for r in m_refs]
    m_new, acc_new = _softmax_update(sts, vts, m_old, [r[...] for r in acc_refs], fixed)
    for r, v in zip(() if fixed else m_refs, m_new):
        r[...] = v
    for r, v in zip(acc_refs, acc_new):
        r[...] = v


def _with_ones_row(vt):
    pad = jnp.where(_row((DV_PAD - HEAD_DIM, vt.shape[1])) == 0, 1.0, 0.0).astype(vt.dtype)
    return jnp.concatenate([vt, pad], axis=0)


def _bucket_bias(bidx, relb_ref, h, fill):
    acc = jnp.full(bidx.shape, fill, F32)
    for bk in range(N_BUCKETS):
        acc = jnp.where(bidx == bk, relb_ref[bk, h] * LOG2E, acc)
    return acc


def _group_sums(x, member):
    g = jnp.where(member, 1.0, 0.0).astype(BF16)
    hi = x.astype(BF16)
    lo = (x - hi.astype(F32)).astype(BF16)
    return _dot(hi, g) + _dot(lo, g)


def _dup_low_half(x):
    y = jnp.where(_lane(x.shape) < HEAD_DIM, x, 0.0)
    return y + pltpu.roll(y, HEAD_DIM, 1)


def _nsa_kernel(relb_ref, shift_ref, q_ref, g_ref, kcvc_ref, kcvct_ref, ksvs_ref, kwvw_ref, bidxct_ref, bidx2t_ref, ovt_ref,
                emt_ref, qnw_ref, knw_ref, o_ref,
                biasc_s, bias2_s, ks_s, vst_s, kw_s, vwt_s, kc_s, vct_s, qx_s, m_s, acc_s, *, tq, s_len, fixed):
    b = pl.program_id(0)
    shift = shift_ref[0] if fixed else 0.0
    i = pl.program_id(1)
    n_win = WINDOW // tq
    ng = s_len // CMP_STRIDE

    @pl.when((b == 0) & (i == 0))
    def _build_bias_tables():
        key = _row((tq, tq))
        qry = _lane((tq, tq))
        for h in range(N_HEADS):
            cols = slice(h * tq, (h + 1) * tq)
            far = jnp.full((tq, tq), relb_ref[N_BUCKETS - 1, h] * LOG2E - shift, F32)
            bias2_s[0, :, cols] = jnp.where(key <= qry, _bucket_bias(bidx2t_ref[0], relb_ref, h, NEG) - shift, NEG)
            bias2_s[1, :, cols] = _bucket_bias(bidx2t_ref[1], relb_ref, h, NEG) - shift
            bias2_s[2, :, cols] = far
            bias2_s[3, :, cols] = jnp.where(key > qry, far, NEG)
            bias2_s[4, :, cols] = jnp.full((tq, tq), NEG, F32)

        def body(t, carry):
            bi = bidxct_ref[:, _lane_tile(t, tq)]
            for h in range(N_HEADS):
                biasc_s[t, :, h * tq:(h + 1) * tq] = _bucket_bias(bi, relb_ref, h, NEG)
            return carry

        lax.fori_loop(0, s_len // tq, body, 0)

    @pl.when(i == 0)
    def _prep_kv():
        ch = 256

        def body(t, carry):
            rows = pl.ds(pl.multiple_of(t * ch, ch), ch)
            for src, kdst, vdst, widx in ((ksvs_ref, ks_s, vst_s, 0), (kwvw_ref, kw_s, vwt_s, 1)):
                x = src[rows, :]
                ss = _group_sums(x * x, _row((LANES, LANES)) < HEAD_DIM) * (1.0 / HEAD_DIM)
                kn = x * lax.rsqrt(ss + EPS) * knw_ref[widx:widx + 1, :]
                kdst[rows, 0:LANES] = _dup_low_half(kn).astype(BF16)
                vdst[:, _lane_tile(t, ch)] = _with_ones_row(x.T[HEAD_DIM:, :]).astype(BF16)
            ks_s[rows, LANES:] = emt_ref[rows, :]
            return carry

        lax.fori_loop(0, s_len // ch, body, 0)
        kc_s[...] = _dup_low_half(kcvc_ref[0]).astype(BF16)
        vct_s[...] = kcvct_ref[0][HEAD_DIM:, :].astype(BF16)

    q = q_ref[...]
    lane = _lane((tq, LANES))
    heads = range(N_HEADS)
    gw = 2 * LANES
    head_shift = HEAD_DIM.bit_length() - 1
    same_head = (lax.shift_right_logical(_row((gw, gw)), head_shift)
                 == lax.shift_right_logical(_lane((gw, gw)), head_shift))
    ss = _group_sums(q * q, same_head) * (1.0 / HEAD_DIM)
    qn = q * lax.rsqrt(ss + EPS) * qnw_ref[...] * (HEAD_DIM ** -0.5 * LOG2E)
    for h in heads:
        mine = (lane < HEAD_DIM) if h % 2 == 0 else (lane >= HEAD_DIM)
        qx_s[h, :, 0:LANES] = jnp.where(mine, qn[:, LANES * (h // 2):LANES * (h // 2 + 1)], 0.0).astype(BF16)

    lcs = [_nt(kc_s[...], qx_s[h, :, 0:LANES]) + biasc_s[i, :, h * tq:(h + 1) * tq] for h in heads]
    pcs = [jnp.where(lc > 0.5 * NEG, jnp.exp2(lc - jnp.max(lc, axis=0, keepdims=True)), 0.0) for lc in lcs]
    dens = [jnp.sum(pc, axis=0, keepdims=True) for pc in pcs]
    pcs = [pc / jnp.where(den > 0.0, den, 1.0) for pc, den in zip(pcs, dens)]
    o_cmp = [_dot(vct_s[...], pc.astype(BF16)) for pc in pcs]
    psum = (pcs[0] + pcs[1]) + (pcs[2] + pcs[3])

    n_slc = s_len // SLC_LEN
    n_sel = min(N_SEL, n_slc)
    blk = _row((n_slc, tq))
    tpos = i * tq + _lane((n_slc, tq))
    tblk = tpos // SLC_LEN
    valid = blk * SLC_LEN <= tpos

    p_hi = psum.astype(BF16)
    p_lo = (psum - p_hi.astype(F32)).astype(BF16)
    score = _dot(ovt_ref[...], p_hi) + _dot(ovt_ref[...], p_lo)
    forced = (blk == 0) | (blk == tblk) | (blk == tblk - 1)
    sc = jnp.where(forced, jnp.inf, jnp.where(valid, score[0:n_slc], -jnp.inf))
    rank = jnp.zeros((n_slc, tq), F32)
    for k in range(n_slc):
        ck = sc[k:k + 1, :]
        beats = (ck > sc) | ((ck == sc) & (blk > k))
        rank += jnp.where(beats, 1.0, 0.0)
    pen = jnp.where((rank < float(n_sel)) & valid, 0.0, NEG)
    pen = jnp.concatenate([pen, jnp.zeros((LANES - n_slc, tq), F32)], axis=0)
    pen_t = pen.T.astype(BF16)
    pen = pen.astype(BF16)
    for h in heads:
        qx_s[h, :, LANES:] = pen_t

    ms = [jnp.full((1, tq), NEG, F32)] * (2 * N_HEADS)
    accs = [jnp.zeros((DV_PAD, tq), F32)] * (2 * N_HEADS)
    for jj in range(n_win + 1):
        exists = i >= jj
        rows = pl.ds(pl.multiple_of(jnp.maximum(i - jj, 0) * tq, tq), tq)
        kind = jnp.where(exists, min(jj, 2), 4)
        kind_w = jnp.where(exists, 3 if jj == n_win else min(jj, 2), 4)
        ks = ks_s[rows, 0:LANES]
        kw = kw_s[rows, :]
        masked = _dot(ks_s[rows, LANES:], pen)
        sts = [_nt(ks, qx_s[h, :, 0:LANES]) + masked + bias2_s[kind, :, h * tq:(h + 1) * tq] for h in heads]
        sts += [_nt(kw, qx_s[h, :, 0:LANES]) + bias2_s[kind_w, :, h * tq:(h + 1) * tq] for h in heads]
        vts = [vst_s[:, rows]] * N_HEADS + [vwt_s[:, rows]] * N_HEADS
        ms, accs = _softmax_update(sts, vts, ms, accs, fixed)
    for h in heads:
        if not fixed:
            m_s[h] = ms[h]
        acc_s[h] = accs[h]
    o_win = [accs[N_HEADS + h][0:HEAD_DIM, :] / accs[N_HEADS + h][HEAD_DIM:HEAD_DIM + 1, :] for h in heads]

    def far_step(j, width):
        rows = pl.ds(j * tq if isinstance(j, int) else pl.multiple_of(j * tq, tq), width * tq)
        ks = ks_s[rows, :]
        sts = [_nt(ks, qx_s[h]) + (relb_ref[N_BUCKETS - 1, h] * LOG2E - shift) for h in heads]
        _softmax_steps(sts, [vst_s[:, rows]] * N_HEADS, [m_s.at[h] for h in heads], [acc_s.at[h] for h in heads],
                       fixed)

    n_far = jnp.maximum(i - n_win, 0)
    if fixed:
        def far_sweep(n):
            accs = [acc_s[h] for h in heads]
            for start in range(0, n, 2):
                rows = pl.ds(start * tq, min(2, n - start) * tq)
                ks = ks_s[rows, :]
                sts = [_nt(ks, qx_s[h]) + (relb_ref[N_BUCKETS - 1, h] * LOG2E - shift) for h in heads]
                _, accs = _softmax_update(sts, [vst_s[:, rows]] * N_HEADS, None, accs, True)
            for h in heads:
                acc_s[h] = accs[h]

        for n in range(1, s_len // tq - n_win):
            pl.when(n_far == n)(functools.partial(far_sweep, n))
    else:
        def far_body(p, carry):
            far_step(n_far - 2 * (p + 1), 2)
            return carry

        lax.fori_loop(0, n_far // 2, far_body, 0)
        pl.when(n_far % 2 == 1)(functools.partial(far_step, 0, 1))

    gt = jax.nn.sigmoid(g_ref[...]).T
    ys = []
    for h in heads:
        o_slc = acc_s[h, 0:HEAD_DIM, :] / acc_s[h, HEAD_DIM:HEAD_DIM + 1, :]
        ys.append(gt[3 * h:3 * h + 1, :] * o_cmp[h] + gt[3 * h + 1:3 * h + 2, :] * o_slc
                  + gt[3 * h + 2:3 * h + 3, :] * o_win[h])
    o_ref[...] = jnp.concatenate(ys, axis=0).T


def _nsa_attention(proj, kcvc, kcvct, rel_bias, shift, bidxct, bidx2t, ovt, emt, qnw, knw, b, s, tq, fixed):
    t = b * s
    nq = s // tq
    m_rows = N_HEADS * tq
    ng = s // CMP_STRIDE
    kern = functools.partial(_nsa_kernel, tq=tq, s_len=s, fixed=fixed)
    return pl.pallas_call(
        kern,
        grid=(b, nq),
        in_specs=[
            pl.BlockSpec(memory_space=pltpu.SMEM),
            pl.BlockSpec(memory_space=pltpu.SMEM),
            pl.BlockSpec((tq, 2 * LANES), lambda bi, i: (bi * nq + i, CB_NQ)),
            pl.BlockSpec((tq, LANES), lambda bi, i: (bi * nq + i, CB_GATE)),
            pl.BlockSpec((1, ng, LANES), lambda bi, i: (bi, 0, 0)),
            pl.BlockSpec((1, LANES, ng), lambda bi, i: (bi, 0, 0)),
            pl.BlockSpec((s, LANES), lambda bi, i: (bi, CB_KSVS)),
            pl.BlockSpec((s, LANES), lambda bi, i: (bi, CB_KWVW)),
            pl.BlockSpec((ng, s), lambda bi, i: (0, 0)),
            pl.BlockSpec((2, tq, tq), lambda bi, i: (0, 0, 0)),
            pl.BlockSpec((LANES, ng), lambda bi, i: (0, 0)),
            pl.BlockSpec((s, LANES), lambda bi, i: (0, 0)),
            pl.BlockSpec((1, 2 * LANES), lambda bi, i: (0, 0)),
            pl.BlockSpec((2, LANES), lambda bi, i: (0, 0)),
        ],
        out_specs=pl.BlockSpec((tq, 2 * LANES), lambda bi, i: (bi * nq + i, 0)),
        out_shape=jax.ShapeDtypeStruct((t, GROUP_WIDTH), F32),
        scratch_shapes=[
            pltpu.VMEM((nq, ng, m_rows), F32),
            pltpu.VMEM((5, tq, m_rows), F32),
            pltpu.VMEM((s, 2 * LANES), BF16),
            pltpu.VMEM((DV_PAD, s), BF16),
            pltpu.VMEM((s, LANES), BF16),
            pltpu.VMEM((DV_PAD, s), BF16),
            pltpu.VMEM((ng, LANES), BF16),
            pltpu.VMEM((HEAD_DIM, ng), BF16),
            pltpu.VMEM((N_HEADS, tq, 2 * LANES), BF16),
            pltpu.VMEM((N_HEADS, 1, tq), F32),
            pltpu.VMEM((N_HEADS, DV_PAD, tq), F32),
        ],
        compiler_params=_params(("arbitrary", "arbitrary")),
        name="nsa_attention",
    )(rel_bias, shift, proj, proj, kcvc, kcvct, proj, proj, bidxct, bidx2t, ovt, emt, qnw, knw)


def _mla_prep_kernel(shift_ref, cq_ref, ckv_ref, kr_ref, qaw_ref, kvw_ref, wq_ref, wqs_ref, wk_ref, wvt_ref, qnw_ref, knw_ref,
                     cq_t_ref, ck_t_ref, sk_t_ref, qo_ref, ko_ref, vto_ref):
    cq = cq_ref[...]
    ms = jnp.sum(cq * cq, axis=-1, keepdims=True) * (1.0 / Q_LORA)
    hq = (cq * lax.rsqrt(ms + EPS) * qaw_ref[...]).astype(BF16)
    qf = _dot(hq, wq_ref[...])
    qsw = _dot(hq, wqs_ref[...])
    ckv = ckv_ref[...]
    ms = jnp.mean(ckv * ckv, axis=-1, keepdims=True)
    hkv = (ckv * lax.rsqrt(ms + EPS) * kvw_ref[...]).astype(BF16)
    kf = _dot(hkv, wk_ref[...])
    vt = _nt(wvt_ref[...], hkv)
    vto_ref[...] = jnp.where(_row(vt.shape) % DV_PAD == HEAD_DIM, 1.0, vt).astype(BF16)
    krb = kr_ref[...]
    kr_rot = krb * ck_t_ref[...] + pltpu.roll(krb, HEAD_DIM, 1) * sk_t_ref[...]
    is_shift = _lane((cq.shape[0], LANES)) == QK_DIM
    for h in range(N_HEADS):
        cols = slice(LANES * h, LANES * (h + 1))
        x = qf[:, cols] * cq_t_ref[...] + qsw[:, cols] * sk_t_ref[...]
        ss = jnp.sum(x * x, axis=-1, keepdims=True) * (1.0 / QK_DIM)
        qn = x * lax.rsqrt(ss + EPS) * qnw_ref[...]
        qo_ref[:, cols] = jnp.where(is_shift, -shift_ref[0], qn).astype(BF16)
        k = kf[:, cols] + kr_rot
        ss = jnp.sum(k * k, axis=-1, keepdims=True) * (1.0 / QK_DIM)
        ko_ref[:, cols] = jnp.where(is_shift, 1.0, k * lax.rsqrt(ss + EPS) * knw_ref[...]).astype(BF16)


def _mla_prep(proj, shift, qaw, kvw, wq, wqs, wk, wvt, qnw, knw, tabs, s, tm):
    t = proj.shape[0]
    npos = s // tm
    row = lambda i: (i, 0)
    const = lambda i: (0, 0)
    tab = pl.BlockSpec((tm, LANES), lambda i: (i % npos, 0))
    out = jax.ShapeDtypeStruct((t, N_HEADS * LANES), BF16)
    return pl.pallas_call(
        _mla_prep_kernel,
        grid=(t // tm,),
        in_specs=[
            pl.BlockSpec(memory_space=pltpu.SMEM),
            pl.BlockSpec((tm, 2 * LANES), lambda i: (i, CB_CQ)),
            pl.BlockSpec((tm, LANES), lambda i: (i, CB_CKV)),
            pl.BlockSpec((tm, LANES), lambda i: (i, CB_KR)),
            pl.BlockSpec((1, 2 * LANES), const),
            pl.BlockSpec((1, LANES), const),
            pl.BlockSpec((2 * LANES, N_HEADS * LANES), const),
            pl.BlockSpec((2 * LANES, N_HEADS * LANES), const),
            pl.BlockSpec((LANES, N_HEADS * LANES), const),
            pl.BlockSpec((N_HEADS * DV_PAD, LANES), const),
            pl.BlockSpec((1, LANES), const),
            pl.BlockSpec((1, LANES), const),
            tab, tab, tab,
        ],
        out_specs=[pl.BlockSpec((tm, N_HEADS * LANES), row), pl.BlockSpec((tm, N_HEADS * LANES), row),
                   pl.BlockSpec((N_HEADS * DV_PAD, tm), lambda i: (0, i))],
        out_shape=[out, out, jax.ShapeDtypeStruct((N_HEADS * DV_PAD, t), BF16)],
        compiler_params=_params(("arbitrary",)),
        name="mla_prep",
    )(shift, proj, proj, proj, qaw, kvw, wq, wqs, wk, wvt, qnw, knw, *tabs)


def _mla_attn_kernel(q_ref, k_ref, vt_ref, o_ref, m_s, acc_s, *, tq, nq, fixed):
    i = pl.program_id(1)
    heads = range(N_HEADS)
    causal = _row((tq, tq)) <= _lane((tq, tq))

    def logits(rows, masked):
        sts = [_nt(k_ref[rows, LANES * h:LANES * (h + 1)], q_ref[:, LANES * h:LANES * (h + 1)]) for h in heads]
        return [jnp.where(causal, st, NEG) for st in sts] if masked else sts

    def values(rows):
        return [vt_ref[DV_PAD * h:DV_PAD * (h + 1), rows] for h in heads]

    if fixed:
        def sweep(n):
            accs = [jnp.zeros((DV_PAD, tq), F32)] * N_HEADS
            for start in range(0, n, 2):
                rows = pl.ds(start * tq, min(2, n - start) * tq)
                _, accs = _softmax_update(logits(rows, False), values(rows), None, accs, True)
            rows = pl.ds(n * tq, tq)
            _, accs = _softmax_update(logits(rows, True), values(rows), None, accs, True)
            for h in heads:
                acc_s[h] = accs[h]

        for n in range(nq):
            pl.when(i == n)(functools.partial(sweep, n))
    else:
        for h in heads:
            m_s[h] = jnp.full((1, tq), NEG, F32)
            acc_s[h] = jnp.zeros((DV_PAD, tq), F32)

        def step(j, width, masked):
            rows = pl.ds(j * tq if isinstance(j, int) else pl.multiple_of(j * tq, tq), width * tq)
            _softmax_steps(logits(rows, masked), values(rows), [m_s.at[h] for h in heads],
                           [acc_s.at[h] for h in heads])

        step(i, 1, True)

        def body(jj, carry):
            step(i - 2 * jj, 2, False)
            return carry

        lax.fori_loop(1, i // 2 + 1, body, 0)
        pl.when(i % 2 == 1)(functools.partial(step, 0, 1, False))
    yt = jnp.concatenate([acc_s[h, 0:HEAD_DIM, :] / acc_s[h, HEAD_DIM:HEAD_DIM + 1, :] for h in range(N_HEADS)],
                         axis=0)
    o_ref[...] = yt.T


def _mla_attention(qm, km, vmt, b, s, tq, fixed):
    t = b * s
    nq = s // tq
    w = N_HEADS * LANES
    return pl.pallas_call(
        functools.partial(_mla_attn_kernel, tq=tq, nq=nq, fixed=fixed),
        grid=(b, nq),
        in_specs=[
            pl.BlockSpec((tq, w), lambda bi, i: (bi * nq + i, 0)),
            pl.BlockSpec((s, w), lambda bi, i: (bi, 0)),
            pl.BlockSpec((N_HEADS * DV_PAD, s), lambda bi, i: (0, bi)),
        ],
        out_specs=pl.BlockSpec((tq, 2 * LANES), lambda bi, i: (bi * nq + i, 0)),
        out_shape=jax.ShapeDtypeStruct((t, GROUP_WIDTH), F32),
        scratch_shapes=[pltpu.VMEM((N_HEADS, 1, tq), F32), pltpu.VMEM((N_HEADS, DV_PAD, tq), F32)],
        compiler_params=_params(("arbitrary", "arbitrary")),
        name="mla_attention",
    )(qm, km, vmt)


def _sb_kernel(q_ref, k_ref, v_ref, o_ref, kb_s, vt_s, q_s, r_s, acc_s, kmax_s, *, tq, s_len):
    i = pl.program_id(1)
    gw = 2 * LANES
    head_shift = HEAD_DIM.bit_length() - 1
    same_head = (lax.shift_right_logical(_row((gw, gw)), head_shift)
                 == lax.shift_right_logical(_lane((gw, gw)), head_shift))

    @pl.when(i == 0)
    def _cast_kv():
        ch = 256

        def body(t, kmax):
            rows = pl.ds(pl.multiple_of(t * ch, ch), ch)
            k = k_ref[rows, :]
            kb_s[rows, :] = k.astype(BF16)
            vt_s[:, _lane_tile(t, ch)] = v_ref[rows, :].T.astype(BF16)
            return jnp.maximum(kmax, _group_sums(k * k, same_head))

        kmax = lax.fori_loop(0, s_len // ch, body, jnp.zeros((ch, gw), F32))
        kmax_s[0] = jnp.max(kmax)

    key = _row((tq, tq))
    qry = _lane((tq, tq))
    strict = key < qry
    tri = jnp.where(key <= qry, 1.0, 0.0).astype(BF16)
    tri2 = jnp.concatenate([tri, tri], axis=1)
    lane = _lane((tq, LANES))
    q = q_ref[...] * (HEAD_DIM ** -0.5 * LOG2E)
    for h in range(N_HEADS):
        mine = (lane < HEAD_DIM) if h % 2 == 0 else (lane >= HEAD_DIM)
        q_s[h] = jnp.where(mine, q[:, LANES * (h // 2):LANES * (h // 2 + 1)], 0.0).astype(BF16)
        r_s[h] = jnp.zeros((1, tq), F32)
        acc_s[h] = jnp.zeros((HEAD_DIM, tq), F32)
    qmax = jnp.max(_group_sums(q * q, same_head))
    z_bound = jnp.sqrt(jnp.full((1, tq), qmax * kmax_s[0], F32)) * 1.01 + 1.0

    heads = range(N_HEADS)

    def step(j, width, masked):
        rs = [r_s[h] for h in heads]
        accs = [acc_s[h] for h in heads]
        tiles = [j + width - 1 - w for w in range(width)]
        rows = [pl.ds(t * tq if isinstance(t, int) else pl.multiple_of(t * tq, tq), tq) for t in tiles]
        zs = [[_nt(kb_s[r, LANES * (h // 2):LANES * (h // 2 + 1)], q_s[h]) for h in heads] for r in rows]
        part = []
        for zt in zs:
            negabs = [pltpu.bitcast(pltpu.bitcast(z, jnp.uint32) | jnp.uint32(0x80000000), F32) for z in zt]
            sps = [jnp.maximum(z, 0.0) + jnp.log2(1.0 + jnp.exp2(na)) for z, na in zip(zt, negabs)]
            if masked:
                sps = [jnp.where(strict, sp, 0.0) for sp in sps]
            his = [sp.astype(BF16) for sp in sps]
            los = [(sp - hi.astype(F32)).astype(BF16) for sp, hi in zip(sps, his)]
            part.append([_dot(tri2, jnp.concatenate([hi, lo], axis=0)) for hi, lo in zip(his, los)])
        for r, zt, pt in zip(rows, zs, part):
            csums = [p + rc for p, rc in zip(pt, rs)]
            als = [jnp.exp2(z - cs) for z, cs in zip(zt, csums)]
            if masked:
                als = [jnp.where(strict, a, 0.0) for a in als]
            accs = [acc + _dot(vt_s[HEAD_DIM * h:HEAD_DIM * (h + 1), r], als[h].astype(BF16))
                    for h, acc in zip(heads, accs)]
            rs = [cs[0:1, :] for cs in csums]
        for h in heads:
            acc_s[h] = accs[h]
            r_s[h] = rs[h]

    def live():
        r_min = jnp.minimum(jnp.minimum(r_s[0], r_s[1]), jnp.minimum(r_s[2], r_s[3]))
        return jnp.max(z_bound - r_min) >= -150.0

    step(i, 1, True)
    pl.when(i >= 1)(functools.partial(step, i - 1, 1, False))

    n_rest = jnp.maximum(i - 1, 0)
    n_pairs = n_rest // 2

    def cond(carry):
        p, alive = carry
        return (p < n_pairs) & alive

    def body(carry):
        p, _ = carry
        step(i - 1 - 2 * (p + 1), 2, False)
        return p + 1, live()

    _, alive = lax.while_loop(cond, body, (jnp.int32(0), live()))
    pl.when((n_rest % 2 == 1) & alive)(functools.partial(step, 0, 1, False))
    o_ref[...] = jnp.concatenate([acc_s[h] for h in range(N_HEADS)], axis=0).T


def _sb_attention(proj, b, s, tq):
    t = b * s
    nq = s // tq
    w = 2 * LANES
    return pl.pallas_call(
        functools.partial(_sb_kernel, tq=tq, s_len=s),
        grid=(b, nq),
        in_specs=[
            pl.BlockSpec((tq, w), lambda bi, i: (bi * nq + i, CB_SQ)),
            pl.BlockSpec((s, w), lambda bi, i: (bi, CB_SK)),
            pl.BlockSpec((s, w), lambda bi, i: (bi, CB_SV)),
        ],
        out_specs=pl.BlockSpec((tq, w), lambda bi, i: (bi * nq + i, 0)),
        out_shape=jax.ShapeDtypeStruct((t, GROUP_WIDTH), F32),
        scratch_shapes=[pltpu.VMEM((s, w), BF16), pltpu.VMEM((w, s), BF16),
                        pltpu.VMEM((N_HEADS, tq, LANES), BF16),
                        pltpu.VMEM((N_HEADS, 1, tq), F32), pltpu.VMEM((N_HEADS, HEAD_DIM, tq), F32),
                        pltpu.SMEM((1,), F32)],
        compiler_params=_params(("arbitrary", "arbitrary")),
        name="sb_attention",
    )(proj, proj, proj)


def _post_kernel(a_ref, ap_ref, yb_ref, yc_ref, yd_ref, x_ref, cw_ref, cb_ref, onw_ref, wo_ref, n2w_ref,
                 w1_ref, w2_ref, o_ref, *, tm, s_len, ffc):
    i = pl.program_id(0)
    a = a_ref[...]
    gw = GROUP_WIDTH
    v = a[:, gw:2 * gw] * a[:, 2 * gw:3 * gw]
    ap = ap_ref[...]
    first = (i * tm) % s_len == 0
    vp = jnp.where(first, 0.0, ap[:, gw:2 * gw] * ap[:, 2 * gw:3 * gw])
    row = _row(v.shape)
    v1 = jnp.where(row == 0, vp[7:8, :], pltpu.roll(v, 1, 0))
    v2 = jnp.where(row == 0, vp[6:7, :], jnp.where(row == 1, vp[7:8, :], pltpu.roll(v, 2, 0)))
    conv = cw_ref[0:1, :] * v2 + cw_ref[1:2, :] * v1 + cw_ref[2:3, :] * v
    ya = a[:, 0:gw] * (conv + cb_ref[...])

    mix = None
    for g, y in enumerate((ya, yb_ref[...], yc_ref[...], yd_ref[...])):
        ms = jnp.mean(y * y, axis=-1, keepdims=True)
        yn = (y * lax.rsqrt(ms + EPS) * onw_ref[:, gw * g:gw * (g + 1)]).astype(BF16)
        part = _dot(yn, wo_ref[gw * g:gw * (g + 1), :])
        mix = part if mix is None else mix + part
    x1 = x_ref[...] + mix

    ms = jnp.mean(x1 * x1, axis=-1, keepdims=True)
    h2 = (x1 * lax.rsqrt(ms + EPS) * n2w_ref[...]).astype(BF16)
    ff = None
    for cidx in range(D_FF // ffc):
        u = _dot(h2, w1_ref[:, ffc * cidx:ffc * (cidx + 1)])
        u = jnp.square(jnp.maximum(u, 0.0)).astype(BF16)
        part = _dot(u, w2_ref[ffc * cidx:ffc * (cidx + 1), :])
        ff = part if ff is None else ff + part
    o_ref[...] = x1 + ff


def _post(proj, yb, yc, yd, x2d, cw, cb, onw, wo, n2w, w1, w2, l, s, tm):
    t = x2d.shape[0]
    gw = GROUP_WIDTH
    row = lambda i: (i, 0)
    const = lambda i: (0, 0)
    layer = lambda i: (l, 0, 0)
    once = pl.Buffered(1)
    kern = functools.partial(_post_kernel, tm=tm, s_len=s, ffc=1024)
    return pl.pallas_call(
        kern,
        grid=(t // tm,),
        in_specs=[
            pl.BlockSpec((tm, 3 * gw), row),
            pl.BlockSpec((8, 3 * gw), lambda i: (jnp.maximum(i * (tm // 8) - 1, 0), 0)),
            pl.BlockSpec((tm, gw), row),
            pl.BlockSpec((tm, gw), row),
            pl.BlockSpec((tm, gw), row),
            pl.BlockSpec((tm, D_MODEL), row),
            pl.BlockSpec((3, gw), const),
            pl.BlockSpec((1, gw), const),
            pl.BlockSpec((1, D_MODEL), const),
            pl.BlockSpec((None, D_MODEL, D_MODEL), layer, pipeline_mode=once),
            pl.BlockSpec((1, D_MODEL), const),
            pl.BlockSpec((None, D_MODEL, D_FF), layer, pipeline_mode=once),
            pl.BlockSpec((None, D_FF, D_MODEL), layer, pipeline_mode=once),
        ],
        out_specs=pl.BlockSpec((tm, D_MODEL), row),
        out_shape=jax.ShapeDtypeStruct((t, D_MODEL), F32),
        compiler_params=_params(("arbitrary",)),
        name="post",
    )(proj, proj, yb, yc, yd, x2d, cw, cb, onw, wo, n2w, w1, w2)


def _t5_bucket(dist):
    max_exact = N_BUCKETS // 2
    d = jnp.maximum(dist, 0)
    large = max_exact + (jnp.log(jnp.maximum(d, 1).astype(F32) / max_exact)
                         / math.log(MAX_DISTANCE / max_exact) * (N_BUCKETS - max_exact)).astype(jnp.int32)
    large = jnp.minimum(large, N_BUCKETS - 1)
    return jnp.where(d < max_exact, d, large)


def _tables(s, tq_nsa):
    n_cmp = (s - CMP_LEN) // CMP_STRIDE + 1
    ng = s // CMP_STRIDE
    n_slc = s // SLC_LEN
    tpos = jnp.arange(s)[None, :]
    n = jnp.arange(ng)[:, None]
    dist_c = tpos - (n * CMP_STRIDE + CMP_LEN - 1)
    bidxct = jnp.where((dist_c >= 0) & (n < n_cmp), _t5_bucket(dist_c), -1).astype(jnp.int32)
    key = jnp.arange(tq_nsa)[:, None]
    qry = jnp.arange(tq_nsa)[None, :]
    bidx2t = jnp.stack([_t5_bucket(qry - key), _t5_bucket(tq_nsa + qry - key)]).astype(jnp.int32)
    starts = np.arange(n_cmp) * CMP_STRIDE
    ends = starts + CMP_LEN
    s0 = np.arange(n_slc) * SLC_LEN
    s1 = s0 + SLC_LEN
    ovl = np.clip(np.minimum(ends[:, None], s1[None]) - np.maximum(starts[:, None], s0[None]), 0, None) / CMP_LEN
    ovt = np.zeros((LANES, ng), np.float32)
    ovt[:n_slc, :n_cmp] = ovl.T
    emt = (np.arange(LANES)[None, :] == (np.arange(s) // SLC_LEN)[:, None]).astype(np.float32)
    inv = 1.0 / (ROPE_THETA ** (jnp.arange(0, ROPE_DIM, 2, dtype=F32) / ROPE_DIM))
    ang = jnp.arange(s, dtype=F32)[:, None] * inv[None, :]
    cos, sin = jnp.cos(ang), jnp.sin(ang)
    z16 = jnp.zeros((s, 16), F32)
    z32 = jnp.zeros((s, 32), F32)
    z64 = jnp.zeros((s, 64), F32)
    one64 = jnp.ones((s, 64), F32)
    cq_t = jnp.concatenate([one64, cos, cos, z32], axis=1)
    ck_t = jnp.concatenate([z64, cos, cos, z32], axis=1)
    sk_t = jnp.concatenate([z64, -sin, sin, z32], axis=1)
    return dict(bidxct=bidxct, bidx2t=bidx2t, ovt=jnp.asarray(ovt, BF16), emt=jnp.asarray(emt, BF16),
                rope=(cq_t, ck_t, sk_t))


def _shift_bound(d, q_gain, k_gain, extra=0.0):
    return (d * jnp.max(jnp.abs(q_gain)) * jnp.max(jnp.abs(k_gain)) * 1.01 + extra + 0.1).reshape(1)


MAX_SHIFT = 60.0


def _pad_cols(w, width):
    return jnp.pad(w, ((0, 0), (0, width - w.shape[1])))


def _layer_weights(l, conv_w, conv_b, nsa_q_norm, nsa_k_norm, cmp_pos, cmp_w1, cmp_w2, mla_q_a_norm,
                   mla_kv_norm, mla_wq_b, mla_wkv_b, mla_q_norm, mla_k_norm, out_norm_w, norm2_w):
    w1 = cmp_w1[l].reshape(2, CMP_LEN, HEAD_DIM, CMP_HIDDEN)
    zw = jnp.zeros((CMP_LEN, HEAD_DIM, CMP_HIDDEN), F32)
    cw1 = jnp.concatenate([jnp.concatenate([w1[0], zw], axis=2), jnp.concatenate([zw, w1[1]], axis=2)],
                          axis=1).astype(BF16)
    zc = jnp.zeros((CMP_HIDDEN, HEAD_DIM), F32)
    cw2 = jnp.concatenate([jnp.concatenate([cmp_w2[l, 0], zc], axis=1),
                           jnp.concatenate([zc, cmp_w2[l, 1]], axis=1)], axis=0).astype(BF16)
    cpos = jnp.concatenate([cmp_pos[l, 0], cmp_pos[l, 1]], axis=1)
    kn = nsa_k_norm[l]
    ones64 = jnp.ones((HEAD_DIM,), F32)
    knw_c = jnp.concatenate([kn[0], ones64])[None, :]
    knw_sw = jnp.stack([jnp.concatenate([kn[1], ones64]), jnp.concatenate([kn[2], ones64])])
    qnw = jnp.tile(nsa_q_norm[l], N_HEADS)[None, :]
    wq = mla_wq_b[l].reshape(Q_LORA, N_HEADS, QK_DIM)
    half = ROPE_DIM // 2
    wqs = jnp.concatenate([jnp.zeros((Q_LORA, N_HEADS, HEAD_DIM), F32), wq[:, :, HEAD_DIM + half:],
                           wq[:, :, HEAD_DIM:HEAD_DIM + half]], axis=2)
    pad_q = lambda w: jnp.pad(w, ((0, 2 * LANES - Q_LORA), (0, 0), (0, LANES - QK_DIM))).reshape(
        2 * LANES, N_HEADS * LANES).astype(BF16)
    wkv = mla_wkv_b[l].reshape(KV_LORA, N_HEADS, 2 * HEAD_DIM)
    wk = jnp.pad(wkv[:, :, :HEAD_DIM], ((0, 0), (0, 0), (0, LANES - HEAD_DIM))).reshape(KV_LORA, N_HEADS * LANES)
    wvt = jnp.pad(wkv[:, :, HEAD_DIM:], ((0, 0), (0, 0), (0, DV_PAD - HEAD_DIM))).reshape(KV_LORA, -1).T
    return dict(
        cw1=cw1, cw2=cw2, cw2t=cw2.T, cpos=cpos, knw_c=knw_c, knw_sw=knw_sw, qnw=qnw,
        qaw=_pad_cols(mla_q_a_norm[l][None, :], 2 * LANES), kvw=mla_kv_norm[l][None, :],
        wq=pad_q(wq), wqs=pad_q(wqs), wk=wk.astype(BF16), wvt=wvt.astype(BF16),
        mqn=_pad_cols(mla_q_norm[l][None, :] * (QK_DIM ** -0.5 * LOG2E), LANES),
        mkn=_pad_cols(mla_k_norm[l][None, :], LANES),
        cw=conv_w[l], cb=conv_b[l][None, :], onw=out_norm_w[l][None, :], n2w=norm2_w[l][None, :])


TM_PROJ = 512
TM_PREP = 512
TM_POST = 512
TQ_NSA = 256
TQ_MLA = 256
TQ_SB = 256


def kernel(x, rel_bias, norm1_w, w_in, conv_w, conv_b, nsa_q_norm, nsa_k_norm, cmp_pos, cmp_w1, cmp_w2,
           mla_q_a_norm, mla_kv_norm, mla_wq_b, mla_wkv_b, mla_q_norm, mla_k_norm, out_norm_w, w_out, norm2_w,
           ffn_w1, ffn_w2):
    b, s, d = x.shape
    depth = w_in.shape[0]
    tabs = _tables(s, TQ_NSA)
    x2d = x.reshape(b * s, d)
    wo_b, w1_b, w2_b = w_out.astype(BF16), ffn_w1.astype(BF16), ffn_w2.astype(BF16)
    for l in range(depth):
        w = _layer_weights(l, conv_w, conv_b, nsa_q_norm, nsa_k_norm, cmp_pos, cmp_w1, cmp_w2,
                           mla_q_a_norm, mla_kv_norm, mla_wq_b, mla_wkv_b, mla_q_norm, mla_k_norm, out_norm_w,
                           norm2_w)
        proj = _inproj(x2d, norm1_w[l][None, :], w_in, l, TM_PROJ)
        kcvc, kcvct = _compress(proj, w["cpos"], w["cw1"], w["cw2"], w["cw2t"], w["knw_c"], b, s)
        nsa_shift = _shift_bound(HEAD_DIM, w["qnw"] * (HEAD_DIM ** -0.5 * LOG2E), w["knw_sw"][:, :HEAD_DIM],
                                 jnp.max(jnp.abs(rel_bias)) * LOG2E)
        nsa_args = (proj, kcvc, kcvct, rel_bias, nsa_shift, tabs["bidxct"], tabs["bidx2t"], tabs["ovt"], tabs["emt"],
                    w["qnw"], w["knw_sw"])
        yb = lax.cond(nsa_shift[0] <= MAX_SHIFT,
                      lambda *a: _nsa_attention(*a, b, s, TQ_NSA, True),
                      lambda *a: _nsa_attention(*a, b, s, TQ_NSA, False), *nsa_args)
        mla_shift = _shift_bound(QK_DIM, w["mqn"], w["mkn"])
        qm, km, vmt = _mla_prep(proj, mla_shift, w["qaw"], w["kvw"], w["wq"], w["wqs"], w["wk"], w["wvt"], w["mqn"],
                                w["mkn"], tabs["rope"], s, TM_PREP)
        yc = lax.cond(mla_shift[0] <= MAX_SHIFT,
                      lambda *a: _mla_attention(*a, b, s, TQ_MLA, True),
                      lambda *a: _mla_attention(*a, b, s, TQ_MLA, False), qm, km, vmt)
        yd = _sb_attention(proj, b, s, TQ_SB)
        x2d = _post(proj, yb, yc, yd, x2d, w["cw"], w["cb"], w["onw"], wo_b, w["n2w"], w1_b, w2_b, l, s, TM_POST)
    return x2d.reshape(b, s, d)
```

```python
import functools
import math

import jax
import jax.numpy as jnp
import numpy as np
from jax import lax
from jax.experimental import pallas as pl
from jax.experimental.pallas import tpu as pltpu

F32 = jnp.float32
BF16 = jnp.bfloat16

D_MODEL = 1024
GROUP_WIDTH = 256
HEAD_DIM = 64
N_HEADS = 4
LANES = 128
CMP_LEN = 32
CMP_STRIDE = 16
SLC_LEN = 64
N_SEL = 16
WINDOW = 512
CMP_HIDDEN = 256
Q_LORA = 192
KV_LORA = 128
ROPE_DIM = 32
QK_DIM = 96
ROPE_THETA = 10000.0
N_BUCKETS = 32
MAX_DISTANCE = 128
D_FF = 4096
EPS = 1e-6
NEG = -1e30
LOG2E = math.log2(math.e)
DV_PAD = 80

NP = 2816
CB_NQ = 3
CB_KCVC = 8
CB_KSVS = 9
CB_KWVW = 10
CB_GATE = 11
CB_CQ = 6
CB_CKV = 14
CB_KR = 15
CB_SQ = 8
CB_SK = 9
CB_SV = 10

VMEM_LIMIT = 56 * 1024 * 1024

NT_DIMS = (((1,), (1,)), ((), ()))


def _params(sem):
    return pltpu.CompilerParams(dimension_semantics=sem, vmem_limit_bytes=VMEM_LIMIT)


def _nt(a, b):
    return lax.dot_general(a, b, NT_DIMS, preferred_element_type=F32)


def _dot(a, b):
    return jnp.dot(a, b, preferred_element_type=F32)


def _lane(shape):
    return lax.broadcasted_iota(jnp.int32, shape, len(shape) - 1)


def _row(shape):
    return lax.broadcasted_iota(jnp.int32, shape, len(shape) - 2)


def _lane_tile(j, width):
    return pl.ds(pl.multiple_of(j * width, width), width)


_IN_SEGMENTS = (
    ((0, 1408), 0),
    ((1408, 1420), 1408),
    ((1420, 1612), 1536),
    ((1612, 1740), 1792),
    ((1756, 1772), 1920),
    ((1740, 1756), 1936),
    ((1740, 1772), 1984),
    ((1772, 2540), 2048),
)
IN_COLS = 2540


def _inproj_kernel(x_ref, nw_ref, w_ref, o_ref, w_s):
    @pl.when(pl.program_id(0) == 0)
    def _relayout_weights():
        end = 0
        for (a, b), dst in _IN_SEGMENTS:
            if dst > end:
                w_s[:, end:dst] = jnp.zeros((D_MODEL, dst - end), BF16)
            w_s[:, dst:dst + b - a] = w_ref[:, a:b].astype(BF16)
            end = dst + b - a
        assert end == NP

    x = x_ref[...]
    ms = jnp.mean(x * x, axis=-1, keepdims=True)
    h = (x * lax.rsqrt(ms + EPS) * nw_ref[...]).astype(BF16)
    o_ref[...] = _dot(h, w_s[...])


def _inproj(x2d, nw, w_in, l, tm):
    t = x2d.shape[0]
    return pl.pallas_call(
        _inproj_kernel,
        grid=(t // tm,),
        in_specs=[
            pl.BlockSpec((tm, D_MODEL), lambda i: (i, 0)),
            pl.BlockSpec((1, D_MODEL), lambda i: (0, 0)),
            pl.BlockSpec((None, D_MODEL, IN_COLS), lambda i: (l, 0, 0), pipeline_mode=pl.Buffered(1)),
        ],
        out_specs=pl.BlockSpec((tm, NP), lambda i: (i, 0)),
        out_shape=jax.ShapeDtypeStruct((t, NP), F32),
        scratch_shapes=[pltpu.VMEM((D_MODEL, NP), BF16)],
        compiler_params=_params(("arbitrary",)),
        name="inproj",
    )(x2d, nw, w_in)


def _compress_kernel(x_ref, pos_ref, w1_ref, w2_ref, w2t_ref, knw_ref, o_ref, ot_ref):
    ng = x_ref.shape[1]
    acc_a = jnp.zeros((ng, 2 * CMP_HIDDEN), F32)
    acc_b = jnp.zeros((ng, 2 * CMP_HIDDEN), F32)
    for i in range(CMP_STRIDE):
        x = x_ref[0, :, i, :]
        xa = (x + pos_ref[i:i + 1, :]).astype(BF16)
        xb = (x + pos_ref[CMP_STRIDE + i:CMP_STRIDE + i + 1, :]).astype(BF16)
        acc_a += _dot(xa, w1_ref[i])
        acc_b += _dot(xb, w1_ref[CMP_STRIDE + i])
    pre = acc_a + pltpu.roll(acc_b, ng - 1, 0)
    hdn = (pre * jax.nn.sigmoid(pre)).astype(BF16)
    out = _dot(hdn, w2_ref[...])
    lane = _lane(out.shape)
    is_k = lane < HEAD_DIM
    ss = jnp.sum(jnp.where(is_k, out * out, 0.0), axis=-1, keepdims=True) * (1.0 / HEAD_DIM)
    o_ref[0] = jnp.where(is_k, out * lax.rsqrt(ss + EPS) * knw_ref[...], out)
    ot_ref[0] = _nt(w2t_ref[...], hdn)


def _compress(proj, pos, w1, w2, w2t, knw, b, s):
    ng = s // CMP_STRIDE
    x4 = proj.reshape(b, ng, CMP_STRIDE, NP)
    return pl.pallas_call(
        _compress_kernel,
        grid=(b,),
        in_specs=[
            pl.BlockSpec((1, ng, CMP_STRIDE, LANES), lambda i: (i, 0, 0, CB_KCVC)),
            pl.BlockSpec((CMP_LEN, LANES), lambda i: (0, 0)),
            pl.BlockSpec((CMP_LEN, LANES, 2 * CMP_HIDDEN), lambda i: (0, 0, 0)),
            pl.BlockSpec((2 * CMP_HIDDEN, LANES), lambda i: (0, 0)),
            pl.BlockSpec((LANES, 2 * CMP_HIDDEN), lambda i: (0, 0)),
            pl.BlockSpec((1, LANES), lambda i: (0, 0)),
        ],
        out_specs=[pl.BlockSpec((1, ng, LANES), lambda i: (i, 0, 0)),
                   pl.BlockSpec((1, LANES, ng), lambda i: (i, 0, 0))],
        out_shape=[jax.ShapeDtypeStruct((b, ng, LANES), F32), jax.ShapeDtypeStruct((b, LANES, ng), F32)],
        compiler_params=_params(("arbitrary",)),
        name="nsa_compress",
    )(x4, pos, w1, w2, w2t, knw)


def _softmax_update(sts, vts, m_old, acc_old, fixed=False):
    if fixed:
        return m_old, [acc + _dot(vt, jnp.exp2(st).astype(BF16)) for acc, vt, st in zip(acc_old, vts, sts)]
    m_new = [jnp.maximum(m, jnp.max(st, axis=0, keepdims=True)) for m, st in zip(m_old, sts)]
    ps = [jnp.exp2(st - m).astype(BF16) for st, m in zip(sts, m_new)]
    alphas = [jnp.exp2(mo - mn) for mo, mn in zip(m_old, m_new)]
    acc_new = [al * acc + _dot(vt, p) for al, acc, vt, p in zip(alphas, acc_old, vts, ps)]
    return m_new, acc_new


def _softmax_steps(sts, vts, m_refs, acc_refs, fixed=False):
    m_old = [None] * len(sts) if fixed else [r[...] for r in m_refs]
    m_new, acc_new = _softmax_update(sts, vts, m_old, [r[...] for r in acc_refs], fixed)
    for r, v in zip(() if fixed else m_refs, m_new):
        r[...] = v
    for r, v in zip(acc_refs, acc_new):
        r[...] = v


def _with_ones_row(vt):
    pad = jnp.where(_row((DV_PAD - HEAD_DIM, vt.shape[1])) == 0, 1.0, 0.0).astype(vt.dtype)
    return jnp.concatenate([vt, pad], axis=0)


def _bucket_bias(bidx, relb_ref, h, fill):
    acc = jnp.full(bidx.shape, fill, F32)
    for bk in range(N_BUCKETS):
        acc = jnp.where(bidx == bk, relb_ref[bk, h] * LOG2E, acc)
    return acc


def _group_sums(x, member):
    g = jnp.where(member, 1.0, 0.0).astype(BF16)
    hi = x.astype(BF16)
    lo = (x - hi.astype(F32)).astype(BF16)
    return _dot(hi, g) + _dot(lo, g)


def _dup_low_half(x):
    y = jnp.where(_lane(x.shape) < HEAD_DIM, x, 0.0)
    return y + pltpu.roll(y, HEAD_DIM, 1)


def _nsa_kernel(relb_ref, shift_ref, q_ref, g_ref, kcvc_ref, kcvct_ref, ksvs_ref, kwvw_ref, bidxct_ref, bidx2t_ref, ovt_ref,
                emt_ref, qnw_ref, knw_ref, o_ref,
                biasc_s, bias2_s, ks_s, vst_s, kw_s, vwt_s, kc_s, vct_s, qx_s, m_s, acc_s, *, tq, s_len, fixed):
    b = pl.program_id(0)
    shift = shift_ref[0] if fixed else 0.0
    i = pl.program_id(1)
    n_win = WINDOW // tq
    ng = s_len // CMP_STRIDE

    @pl.when((b == 0) & (i == 0))
    def _build_bias_tables():
        key = _row((tq, tq))
        qry = _lane((tq, tq))
        for h in range(N_HEADS):
            cols = slice(h * tq, (h + 1) * tq)
            far = jnp.full((tq, tq), relb_ref[N_BUCKETS - 1, h] * LOG2E - shift, F32)
            bias2_s[0, :, cols] = jnp.where(key <= qry, _bucket_bias(bidx2t_ref[0], relb_ref, h, NEG) - shift, NEG)
            bias2_s[1, :, cols] = _bucket_bias(bidx2t_ref[1], relb_ref, h, NEG) - shift
            bias2_s[2, :, cols] = far
            bias2_s[3, :, cols] = jnp.where(key > qry, far, NEG)
            bias2_s[4, :, cols] = jnp.full((tq, tq), NEG, F32)

        def body(t, carry):
            bi = bidxct_ref[:, _lane_tile(t, tq)]
            for h in range(N_HEADS):
                biasc_s[t, :, h * tq:(h + 1) * tq] = _bucket_bias(bi, relb_ref, h, NEG)
            return carry

        lax.fori_loop(0, s_len // tq, body, 0)

    @pl.when(i == 0)
    def _prep_kv():
        ch = 256

        def body(t, carry):
            rows = pl.ds(pl.multiple_of(t * ch, ch), ch)
            for src, kdst, vdst, widx in ((ksvs_ref, ks_s, vst_s, 0), (kwvw_ref, kw_s, vwt_s, 1)):
                x = src[rows, :]
                ss = _group_sums(x * x, _row((LANES, LANES)) < HEAD_DIM) * (1.0 / HEAD_DIM)
                kn = x * lax.rsqrt(ss + EPS) * knw_ref[widx:widx + 1, :]
                kdst[rows, 0:LANES] = _dup_low_half(kn).astype(BF16)
                vdst[:, _lane_tile(t, ch)] = _with_ones_row(x.T[HEAD_DIM:, :]).astype(BF16)
            ks_s[rows, LANES:] = emt_ref[rows, :]
            return carry

        lax.fori_loop(0, s_len // ch, body, 0)
        kc_s[...] = _dup_low_half(kcvc_ref[0]).astype(BF16)
        vct_s[...] = kcvct_ref[0][HEAD_DIM:, :].astype(BF16)

    q = q_ref[...]
    lane = _lane((tq, LANES))
    heads = range(N_HEADS)
    gw = 2 * LANES
    head_shift = HEAD_DIM.bit_length() - 1
    same_head = (lax.shift_right_logical(_row((gw, gw)), head_shift)
                 == lax.shift_right_logical(_lane((gw, gw)), head_shift))
    ss = _group_sums(q * q, same_head) * (1.0 / HEAD_DIM)
    qn = q * lax.rsqrt(ss + EPS) * qnw_ref[...] * (HEAD_DIM ** -0.5 * LOG2E)
    for h in heads:
        mine = (lane < HEAD_DIM) if h % 2 == 0 else (lane >= HEAD_DIM)
        qx_s[h, :, 0:LANES] = jnp.where(mine, qn[:, LANES * (h // 2):LANES * (h // 2 + 1)], 0.0).astype(BF16)

    lcs = [_nt(kc_s[...], qx_s[h, :, 0:LANES]) + biasc_s[i, :, h * tq:(h + 1) * tq] for h in heads]
    pcs = [jnp.where(lc > 0.5 * NEG, jnp.exp2(lc - jnp.max(lc, axis=0, keepdims=True)), 0.0) for lc in lcs]
    dens = [jnp.sum(pc, axis=0, keepdims=True) for pc in pcs]
    pcs = [pc / jnp.where(den > 0.0, den, 1.0) for pc, den in zip(pcs, dens)]
    o_cmp = [_dot(vct_s[...], pc.astype(BF16)) for pc in pcs]
    psum = (pcs[0] + pcs[1]) + (pcs[2] + pcs[3])

    n_slc = s_len // SLC_LEN
    n_sel = min(N_SEL, n_slc)
    blk = _row((n_slc, tq))
    tpos = i * tq + _lane((n_slc, tq))
    tblk = tpos // SLC_LEN
    valid = blk * SLC_LEN <= tpos

    p_hi = psum.astype(BF16)
    p_lo = (psum - p_hi.astype(F32)).astype(BF16)
    score = _dot(ovt_ref[...], p_hi) + _dot(ovt_ref[...], p_lo)
    forced = (blk == 0) | (blk == tblk) | (blk == tblk - 1)
    sc = jnp.where(forced, jnp.inf, jnp.where(valid, score[0:n_slc], -jnp.inf))
    rank = jnp.zeros((n_slc, tq), F32)
    for k in range(n_slc):
        ck = sc[k:k + 1, :]
        beats = (ck > sc) | ((ck == sc) & (blk > k))
        rank += jnp.where(beats, 1.0, 0.0)
    pen = jnp.where((rank < float(n_sel)) & valid, 0.0, NEG)
    pen = jnp.concatenate([pen, jnp.zeros((LANES - n_slc, tq), F32)], axis=0)
    pen_t = pen.T.astype(BF16)
    pen = pen.astype(BF16)
    for h in heads:
        qx_s[h, :, LANES:] = pen_t

    ms = [jnp.full((1, tq), NEG, F32)] * (2 * N_HEADS)
    accs = [jnp.zeros((DV_PAD, tq), F32)] * (2 * N_HEADS)
    for jj in range(n_win + 1):
        exists = i >= jj
        rows = pl.ds(pl.multiple_of(jnp.maximum(i - jj, 0) * tq, tq), tq)
        kind = jnp.where(exists, min(jj, 2), 4)
        kind_w = jnp.where(exists, 3 if jj == n_win else min(jj, 2), 4)
        ks = ks_s[rows, 0:LANES]
        kw = kw_s[rows, :]
        masked = _dot(ks_s[rows, LANES:], pen)
        sts = [_nt(ks, qx_s[h, :, 0:LANES]) + masked + bias2_s[kind, :, h * tq:(h + 1) * tq] for h in heads]
        sts += [_nt(kw, qx_s[h, :, 0:LANES]) + bias2_s[kind_w, :, h * tq:(h + 1) * tq] for h in heads]
        vts = [vst_s[:, rows]] * N_HEADS + [vwt_s[:, rows]] * N_HEADS
        ms, accs = _softmax_update(sts, vts, ms, accs, fixed)
    for h in heads:
        if not fixed:
            m_s[h] = ms[h]
        acc_s[h] = accs[h]
    o_win = [accs[N_HEADS + h][0:HEAD_DIM, :] / accs[N_HEADS + h][HEAD_DIM:HEAD_DIM + 1, :] for h in heads]

    def far_step(j, width):
        rows = pl.ds(j * tq if isinstance(j, int) else pl.multiple_of(j * tq, tq), width * tq)
        ks = ks_s[rows, :]
        sts = [_nt(ks, qx_s[h]) + (relb_ref[N_BUCKETS - 1, h] * LOG2E - shift) for h in heads]
        _softmax_steps(sts, [vst_s[:, rows]] * N_HEADS, [m_s.at[h] for h in heads], [acc_s.at[h] for h in heads],
                       fixed)

    n_far = jnp.maximum(i - n_win, 0)
    if fixed:
        def far_sweep(n):
            accs = [acc_s[h] for h in heads]
            for start in range(0, n, 2):
                rows = pl.ds(start * tq, min(2, n - start) * tq)
                ks = ks_s[rows, :]
                sts = [_nt(ks, qx_s[h]) + (relb_ref[N_BUCKETS - 1, h] * LOG2E - shift) for h in heads]
                _, accs = _softmax_update(sts, [vst_s[:, rows]] * N_HEADS, None, accs, True)
            for h in heads:
                acc_s[h] = accs[h]

        for n in range(1, s_len // tq - n_win):
            pl.when(n_far == n)(functools.partial(far_sweep, n))
    else:
        def far_body(p, carry):
            far_step(n_far - 2 * (p + 1), 2)
            return carry

        lax.fori_loop(0, n_far // 2, far_body, 0)
        pl.when(n_far % 2 == 1)(functools.partial(far_step, 0, 1))

    gt = jax.nn.sigmoid(g_ref[...]).T
    ys = []
    for h in heads:
        o_slc = acc_s[h, 0:HEAD_DIM, :] / acc_s[h, HEAD_DIM:HEAD_DIM + 1, :]
        ys.append(gt[3 * h:3 * h + 1, :] * o_cmp[h] + gt[3 * h + 1:3 * h + 2, :] * o_slc
                  + gt[3 * h + 2:3 * h + 3, :] * o_win[h])
    o_ref[...] = jnp.concatenate(ys, axis=0).T


def _nsa_attention(proj, kcvc, kcvct, rel_bias, shift, bidxct, bidx2t, ovt, emt, qnw, knw, b, s, tq, fixed):
    t = b * s
    nq = s // tq
    m_rows = N_HEADS * tq
    ng = s // CMP_STRIDE
    kern = functools.partial(_nsa_kernel, tq=tq, s_len=s, fixed=fixed)
    return pl.pallas_call(
        kern,
        grid=(b, nq),
        in_specs=[
            pl.BlockSpec(memory_space=pltpu.SMEM),
            pl.BlockSpec(memory_space=pltpu.SMEM),
            pl.BlockSpec((tq, 2 * LANES), lambda bi, i: (bi * nq + i, CB_NQ)),
            pl.BlockSpec((tq, LANES), lambda bi, i: (bi * nq + i, CB_GATE)),
            pl.BlockSpec((1, ng, LANES), lambda bi, i: (bi, 0, 0)),
            pl.BlockSpec((1, LANES, ng), lambda bi, i: (bi, 0, 0)),
            pl.BlockSpec((s, LANES), lambda bi, i: (bi, CB_KSVS)),
            pl.BlockSpec((s, LANES), lambda bi, i: (bi, CB_KWVW)),
            pl.BlockSpec((ng, s), lambda bi, i: (0, 0)),
            pl.BlockSpec((2, tq, tq), lambda bi, i: (0, 0, 0)),
            pl.BlockSpec((LANES, ng), lambda bi, i: (0, 0)),
            pl.BlockSpec((s, LANES), lambda bi, i: (0, 0)),
            pl.BlockSpec((1, 2 * LANES), lambda bi, i: (0, 0)),
            pl.BlockSpec((2, LANES), lambda bi, i: (0, 0)),
        ],
        out_specs=pl.BlockSpec((tq, 2 * LANES), lambda bi, i: (bi * nq + i, 0)),
        out_shape=jax.ShapeDtypeStruct((t, GROUP_WIDTH), F32),
        scratch_shapes=[
            pltpu.VMEM((nq, ng, m_rows), F32),
            pltpu.VMEM((5, tq, m_rows), F32),
            pltpu.VMEM((s, 2 * LANES), BF16),
            pltpu.VMEM((DV_PAD, s), BF16),
            pltpu.VMEM((s, LANES), BF16),
            pltpu.VMEM((DV_PAD, s), BF16),
            pltpu.VMEM((ng, LANES), BF16),
            pltpu.VMEM((HEAD_DIM, ng), BF16),
            pltpu.VMEM((N_HEADS, tq, 2 * LANES), BF16),
            pltpu.VMEM((N_HEADS, 1, tq), F32),
            pltpu.VMEM((N_HEADS, DV_PAD, tq), F32),
        ],
        compiler_params=_params(("arbitrary", "arbitrary")),
        name="nsa_attention",
    )(rel_bias, shift, proj, proj, kcvc, kcvct, proj, proj, bidxct, bidx2t, ovt, emt, qnw, knw)


def _mla_prep_kernel(shift_ref, cq_ref, ckv_ref, kr_ref, qaw_ref, kvw_ref, wq_ref, wqs_ref, wk_ref, wvt_ref, qnw_ref, knw_ref,
                     cq_t_ref, ck_t_ref, sk_t_ref, qo_ref, ko_ref, vto_ref):
    cq = cq_ref[...]
    ms = jnp.sum(cq * cq, axis=-1, keepdims=True) * (1.0 / Q_LORA)
    hq = (cq * lax.rsqrt(ms + EPS) * qaw_ref[...]).astype(BF16)
    qf = _dot(hq, wq_ref[...])
    qsw = _dot(hq, wqs_ref[...])
    ckv = ckv_ref[...]
    ms = jnp.mean(ckv * ckv, axis=-1, keepdims=True)
    hkv = (ckv * lax.rsqrt(ms + EPS) * kvw_ref[...]).astype(BF16)
    kf = _dot(hkv, wk_ref[...])
    vt = _nt(wvt_ref[...], hkv)
    vto_ref[...] = jnp.where(_row(vt.shape) % DV_PAD == HEAD_DIM, 1.0, vt).astype(BF16)
    krb = kr_ref[...]
    kr_rot = krb * ck_t_ref[...] + pltpu.roll(krb, HEAD_DIM, 1) * sk_t_ref[...]
    is_shift = _lane((cq.shape[0], LANES)) == QK_DIM
    for h in range(N_HEADS):
        cols = slice(LANES * h, LANES * (h + 1))
        x = qf[:, cols] * cq_t_ref[...] + qsw[:, cols] * sk_t_ref[...]
        ss = jnp.sum(x * x, axis=-1, keepdims=True) * (1.0 / QK_DIM)
        qn = x * lax.rsqrt(ss + EPS) * qnw_ref[...]
        qo_ref[:, cols] = jnp.where(is_shift, -shift_ref[0], qn).astype(BF16)
        k = kf[:, cols] + kr_rot
        ss = jnp.sum(k * k, axis=-1, keepdims=True) * (1.0 / QK_DIM)
        ko_ref[:, cols] = jnp.where(is_shift, 1.0, k * lax.rsqrt(ss + EPS) * knw_ref[...]).astype(BF16)


def _mla_prep(proj, shift, qaw, kvw, wq, wqs, wk, wvt, qnw, knw, tabs, s, tm):
    t = proj.shape[0]
    npos = s // tm
    row = lambda i: (i, 0)
    const = lambda i: (0, 0)
    tab = pl.BlockSpec((tm, LANES), lambda i: (i % npos, 0))
    out = jax.ShapeDtypeStruct((t, N_HEADS * LANES), BF16)
    return pl.pallas_call(
        _mla_prep_kernel,
        grid=(t // tm,),
        in_specs=[
            pl.BlockSpec(memory_space=pltpu.SMEM),
            pl.BlockSpec((tm, 2 * LANES), lambda i: (i, CB_CQ)),
            pl.BlockSpec((tm, LANES), lambda i: (i, CB_CKV)),
            pl.BlockSpec((tm, LANES), lambda i: (i, CB_KR)),
            pl.BlockSpec((1, 2 * LANES), const),
            pl.BlockSpec((1, LANES), const),
            pl.BlockSpec((2 * LANES, N_HEADS * LANES), const),
            pl.BlockSpec((2 * LANES, N_HEADS * LANES), const),
            pl.BlockSpec((LANES, N_HEADS * LANES), const),
            pl.BlockSpec((N_HEADS * DV_PAD, LANES), const),
            pl.BlockSpec((1, LANES), const),
            pl.BlockSpec((1, LANES), const),
            tab, tab, tab,
        ],
        out_specs=[pl.BlockSpec((tm, N_HEADS * LANES), row), pl.BlockSpec((tm, N_HEADS * LANES), row),
                   pl.BlockSpec((N_HEADS * DV_PAD, tm), lambda i: (0, i))],
        out_shape=[out, out, jax.ShapeDtypeStruct((N_HEADS * DV_PAD, t), BF16)],
        compiler_params=_params(("arbitrary",)),
        name="mla_prep",
    )(shift, proj, proj, proj, qaw, kvw, wq, wqs, wk, wvt, qnw, knw, *tabs)


def _mla_attn_kernel(q_ref, k_ref, vt_ref, o_ref, m_s, acc_s, *, tq, nq, fixed):
    i = pl.program_id(1)
    heads = range(N_HEADS)
    causal = _row((tq, tq)) <= _lane((tq, tq))

    def logits(rows, masked):
        sts = [_nt(k_ref[rows, LANES * h:LANES * (h + 1)], q_ref[:, LANES * h:LANES * (h + 1)]) for h in heads]
        return [jnp.where(causal, st, NEG) for st in sts] if masked else sts

    def values(rows):
        return [vt_ref[DV_PAD * h:DV_PAD * (h + 1), rows] for h in heads]

    if fixed:
        def sweep(n):
            accs = [jnp.zeros((DV_PAD, tq), F32)] * N_HEADS
            for start in range(0, n, 2):
                rows = pl.ds(start * tq, min(2, n - start) * tq)
                _, accs = _softmax_update(logits(rows, False), values(rows), None, accs, True)
            rows = pl.ds(n * tq, tq)
            _, accs = _softmax_update(logits(rows, True), values(rows), None, accs, True)
            for h in heads:
                acc_s[h] = accs[h]

        for n in range(nq):
            pl.when(i == n)(functools.partial(sweep, n))
    else:
        for h in heads:
            m_s[h] = jnp.full((1, tq), NEG, F32)
            acc_s[h] = jnp.zeros((DV_PAD, tq), F32)

        def step(j, width, masked):
            rows = pl.ds(j * tq if isinstance(j, int) else pl.multiple_of(j * tq, tq), width * tq)
            _softmax_steps(logits(rows, masked), values(rows), [m_s.at[h] for h in heads],
                           [acc_s.at[h] for h in heads])

        step(i, 1, True)

        def body(jj, carry):
            step(i - 2 * jj, 2, False)
            return carry

        lax.fori_loop(1, i // 2 + 1, body, 0)
        pl.when(i % 2 == 1)(functools.partial(step, 0, 1, False))
    yt = jnp.concatenate([acc_s[h, 0:HEAD_DIM, :] / acc_s[h, HEAD_DIM:HEAD_DIM + 1, :] for h in range(N_HEADS)],
                         axis=0)
    o_ref[...] = yt.T


def _mla_attention(qm, km, vmt, b, s, tq, fixed):
    t = b * s
    nq = s // tq
    w = N_HEADS * LANES
    return pl.pallas_call(
        functools.partial(_mla_attn_kernel, tq=tq, nq=nq, fixed=fixed),
        grid=(b, nq),
        in_specs=[
            pl.BlockSpec((tq, w), lambda bi, i: (bi * nq + i, 0)),
            pl.BlockSpec((s, w), lambda bi, i: (bi, 0)),
            pl.BlockSpec((N_HEADS * DV_PAD, s), lambda bi, i: (0, bi)),
        ],
        out_specs=pl.BlockSpec((tq, 2 * LANES), lambda bi, i: (bi * nq + i, 0)),
        out_shape=jax.ShapeDtypeStruct((t, GROUP_WIDTH), F32),
        scratch_shapes=[pltpu.VMEM((N_HEADS, 1, tq), F32), pltpu.VMEM((N_HEADS, DV_PAD, tq), F32)],
        compiler_params=_params(("arbitrary", "arbitrary")),
        name="mla_attention",
    )(qm, km, vmt)


def _sb_kernel(q_ref, k_ref, v_ref, o_ref, kb_s, vt_s, q_s, r_s, acc_s, kmax_s, *, tq, s_len):
    i = pl.program_id(1)
    gw = 2 * LANES
    head_shift = HEAD_DIM.bit_length() - 1
    same_head = (lax.shift_right_logical(_row((gw, gw)), head_shift)
                 == lax.shift_right_logical(_lane((gw, gw)), head_shift))

    @pl.when(i == 0)
    def _cast_kv():
        ch = 256

        def body(t, kmax):
            rows = pl.ds(pl.multiple_of(t * ch, ch), ch)
            k = k_ref[rows, :]
            kb_s[rows, :] = k.astype(BF16)
            vt_s[:, _lane_tile(t, ch)] = v_ref[rows, :].T.astype(BF16)
            return jnp.maximum(kmax, _group_sums(k * k, same_head))

        kmax = lax.fori_loop(0, s_len // ch, body, jnp.zeros((ch, gw), F32))
        kmax_s[0] = jnp.max(kmax)

    key = _row((tq, tq))
    qry = _lane((tq, tq))
    strict = key < qry
    tri = jnp.where(key <= qry, 1.0, 0.0).astype(BF16)
    tri2 = jnp.concatenate([tri, tri], axis=1)
    lane = _lane((tq, LANES))
    q = q_ref[...] * (HEAD_DIM ** -0.5 * LOG2E)
    for h in range(N_HEADS):
        mine = (lane < HEAD_DIM) if h % 2 == 0 else (lane >= HEAD_DIM)
        q_s[h] = jnp.where(mine, q[:, LANES * (h // 2):LANES * (h // 2 + 1)], 0.0).astype(BF16)
        r_s[h] = jnp.zeros((1, tq), F32)
        acc_s[h] = jnp.zeros((HEAD_DIM, tq), F32)
    qmax = jnp.max(_group_sums(q * q, same_head))
    z_bound = jnp.sqrt(jnp.full((1, tq), qmax * kmax_s[0], F32)) * 1.01 + 1.0

    heads = range(N_HEADS)

    def step(j, width, masked):
        rs = [r_s[h] for h in heads]
        accs = [acc_s[h] for h in heads]
        tiles = [j + width - 1 - w for w in range(width)]
        rows = [pl.ds(t * tq if isinstance(t, int) else pl.multiple_of(t * tq, tq), tq) for t in tiles]
        zs = [[_nt(kb_s[r, LANES * (h // 2):LANES * (h // 2 + 1)], q_s[h]) for h in heads] for r in rows]
        part = []
        for w, zt in enumerate(zs):
            negabs = [pltpu.bitcast(pltpu.bitcast(z, jnp.uint32) | jnp.uint32(0x80000000), F32) for z in zt]
            sps = [jnp.maximum(z, 0.0) + jnp.log2(1.0 + jnp.exp2(na)) for z, na in zip(zt, negabs)]
            if masked and w == 0:
                sps = [jnp.where(strict, sp, 0.0) for sp in sps]
            his = [sp.astype(BF16) for sp in sps]
            los = [(sp - hi.astype(F32)).astype(BF16) for sp, hi in zip(sps, his)]
            part.append([_dot(tri2, jnp.concatenate([hi, lo], axis=0)) for hi, lo in zip(his, los)])
        for w, (r, zt, pt) in enumerate(zip(rows, zs, part)):
            csums = [p + rc for p, rc in zip(pt, rs)]
            als = [jnp.exp2(z - cs) for z, cs in zip(zt, csums)]
            if masked and w == 0:
                als = [jnp.where(strict, a, 0.0) for a in als]
            accs = [acc + _dot(vt_s[HEAD_DIM * h:HEAD_DIM * (h + 1), r], als[h].astype(BF16))
                    for h, acc in zip(heads, accs)]
            rs = [cs[0:1, :] for cs in csums]
        for h in heads:
            acc_s[h] = accs[h]
            r_s[h] = rs[h]

    def live():
        r_min = jnp.minimum(jnp.minimum(r_s[0], r_s[1]), jnp.minimum(r_s[2], r_s[3]))
        return jnp.max(z_bound - r_min) >= -150.0

    pl.when(i == 0)(functools.partial(step, 0, 1, True))
    pl.when(i >= 1)(functools.partial(step, i - 1, 2, True))

    n_rest = jnp.maximum(i - 1, 0)
    n_pairs = n_rest // 2

    def cond(carry):
        p, alive = carry
        return (p < n_pairs) & alive

    def body(carry):
        p, _ = carry
        step(i - 1 - 2 * (p + 1), 2, False)
        return p + 1, live()

    _, alive = lax.while_loop(cond, body, (jnp.int32(0), live()))
    pl.when((n_rest % 2 == 1) & alive)(functools.partial(step, 0, 1, False))
    o_ref[...] = jnp.concatenate([acc_s[h] for h in range(N_HEADS)], axis=0).T


def _sb_attention(proj, b, s, tq):
    t = b * s
    nq = s // tq
    w = 2 * LANES
    return pl.pallas_call(
        functools.partial(_sb_kernel, tq=tq, s_len=s),
        grid=(b, nq),
        in_specs=[
            pl.BlockSpec((tq, w), lambda bi, i: (bi * nq + i, CB_SQ)),
            pl.BlockSpec((s, w), lambda bi, i: (bi, CB_SK)),
            pl.BlockSpec((s, w), lambda bi, i: (bi, CB_SV)),
        ],
        out_specs=pl.BlockSpec((tq, w), lambda bi, i: (bi * nq + i, 0)),
        out_shape=jax.ShapeDtypeStruct((t, GROUP_WIDTH), F32),
        scratch_shapes=[pltpu.VMEM((s, w), BF16), pltpu.VMEM((w, s), BF16),
                        pltpu.VMEM((N_HEADS, tq, LANES), BF16),
                        pltpu.VMEM((N_HEADS, 1, tq), F32), pltpu.VMEM((N_HEADS, HEAD_DIM, tq), F32),
                        pltpu.SMEM((1,), F32)],
        compiler_params=_params(("arbitrary", "arbitrary")),
        name="sb_attention",
    )(proj, proj, proj)


def _post_kernel(a_ref, ap_ref, yb_ref, yc_ref, yd_ref, x_ref, cw_ref, cb_ref, onw_ref, wo_ref, n2w_ref,
                 w1_ref, w2_ref, o_ref, *, tm, s_len, ffc):
    i = pl.program_id(0)
    a = a_ref[...]
    gw = GROUP_WIDTH
    v = a[:, gw:2 * gw] * a[:, 2 * gw:3 * gw]
    ap = ap_ref[...]
    first = (i * tm) % s_len == 0
    vp = jnp.where(first, 0.0, ap[:, gw:2 * gw] * ap[:, 2 * gw:3 * gw])
    row = _row(v.shape)
    v1 = jnp.where(row == 0, vp[7:8, :], pltpu.roll(v, 1, 0))
    v2 = jnp.where(row == 0, vp[6:7, :], jnp.where(row == 1, vp[7:8, :], pltpu.roll(v, 2, 0)))
    conv = cw_ref[0:1, :] * v2 + cw_ref[1:2, :] * v1 + cw_ref[2:3, :] * v
    ya = a[:, 0:gw] * (conv + cb_ref[...])

    mix = None
    for g, y in enumerate((ya, yb_ref[...], yc_ref[...], yd_ref[...])):
        ms = jnp.mean(y * y, axis=-1, keepdims=True)
        yn = (y * lax.rsqrt(ms + EPS) * onw_ref[:, gw * g:gw * (g + 1)]).astype(BF16)
        part = _dot(yn, wo_ref[gw * g:gw * (g + 1), :])
        mix = part if mix is None else mix + part
    x1 = x_ref[...] + mix

    ms = jnp.mean(x1 * x1, axis=-1, keepdims=True)
    h2 = (x1 * lax.rsqrt(ms + EPS) * n2w_ref[...]).astype(BF16)
    ff = None
    for cidx in range(D_FF // ffc):
        u = _dot(h2, w1_ref[:, ffc * cidx:ffc * (cidx + 1)])
        u = jnp.square(jnp.maximum(u, 0.0)).astype(BF16)
        part = _dot(u, w2_ref[ffc * cidx:ffc * (cidx + 1), :])
        ff = part if ff is None else ff + part
    o_ref[...] = x1 + ff


def _post(proj, yb, yc, yd, x2d, cw, cb, onw, wo, n2w, w1, w2, l, s, tm):
    t = x2d.shape[0]
    gw = GROUP_WIDTH
    row = lambda i: (i, 0)
    const = lambda i: (0, 0)
    layer = lambda i: (l, 0, 0)
    once = pl.Buffered(1)
    kern = functools.partial(_post_kernel, tm=tm, s_len=s, ffc=1024)
    return pl.pallas_call(
        kern,
        grid=(t // tm,),
        in_specs=[
            pl.BlockSpec((tm, 3 * gw), row),
            pl.BlockSpec((8, 3 * gw), lambda i: (jnp.maximum(i * (tm // 8) - 1, 0), 0)),
            pl.BlockSpec((tm, gw), row),
            pl.BlockSpec((tm, gw), row),
            pl.BlockSpec((tm, gw), row),
            pl.BlockSpec((tm, D_MODEL), row),
            pl.BlockSpec((3, gw), const),
            pl.BlockSpec((1, gw), const),
            pl.BlockSpec((1, D_MODEL), const),
            pl.BlockSpec((None, D_MODEL, D_MODEL), layer, pipeline_mode=once),
            pl.BlockSpec((1, D_MODEL), const),
            pl.BlockSpec((None, D_MODEL, D_FF), layer, pipeline_mode=once),
            pl.BlockSpec((None, D_FF, D_MODEL), layer, pipeline_mode=once),
        ],
        out_specs=pl.BlockSpec((tm, D_MODEL), row),
        out_shape=jax.ShapeDtypeStruct((t, D_MODEL), F32),
        compiler_params=_params(("arbitrary",)),
        name="post",
    )(proj, proj, yb, yc, yd, x2d, cw, cb, onw, wo, n2w, w1, w2)


def _t5_bucket(dist):
    max_exact = N_BUCKETS // 2
    d = np.maximum(dist, 0)
    large = max_exact + (np.log(np.maximum(d, 1) / max_exact) / math.log(MAX_DISTANCE / max_exact)
                         * (N_BUCKETS - max_exact)).astype(np.int32)
    return np.where(d < max_exact, d, np.minimum(large, N_BUCKETS - 1)).astype(np.int32)


def _tables(s, tq_nsa):
    n_cmp = (s - CMP_LEN) // CMP_STRIDE + 1
    ng = s // CMP_STRIDE
    n_slc = s // SLC_LEN
    tpos = np.arange(s)[None, :]
    n = np.arange(ng)[:, None]
    dist_c = tpos - (n * CMP_STRIDE + CMP_LEN - 1)
    bidxct = np.where((dist_c >= 0) & (n < n_cmp), _t5_bucket(dist_c), -1).astype(np.int32)
    key = np.arange(tq_nsa)[:, None]
    qry = np.arange(tq_nsa)[None, :]
    bidx2t = np.stack([_t5_bucket(qry - key), _t5_bucket(tq_nsa + qry - key)])
    starts = np.arange(n_cmp) * CMP_STRIDE
    ends = starts + CMP_LEN
    s0 = np.arange(n_slc) * SLC_LEN
    s1 = s0 + SLC_LEN
    ovl = np.clip(np.minimum(ends[:, None], s1[None]) - np.maximum(starts[:, None], s0[None]), 0, None) / CMP_LEN
    ovt = np.zeros((LANES, ng), np.float32)
    ovt[:n_slc, :n_cmp] = ovl.T
    emt = (np.arange(LANES)[None, :] == (np.arange(s) // SLC_LEN)[:, None]).astype(np.float32)
    inv = 1.0 / (ROPE_THETA ** (np.arange(0, ROPE_DIM, 2, dtype=np.float64) / ROPE_DIM))
    ang = (np.arange(s, dtype=np.float32)[:, None] * inv.astype(np.float32)[None, :]).astype(np.float64)
    cos, sin = np.cos(ang).astype(np.float32), np.sin(ang).astype(np.float32)
    z32 = np.zeros((s, 32), np.float32)
    z64 = np.zeros((s, 64), np.float32)
    one64 = np.ones((s, 64), np.float32)
    cq_t = np.concatenate([one64, cos, cos, z32], axis=1)
    ck_t = np.concatenate([z64, cos, cos, z32], axis=1)
    sk_t = np.concatenate([z64, -sin, sin, z32], axis=1)
    return dict(bidxct=jnp.asarray(bidxct), bidx2t=jnp.asarray(bidx2t), ovt=jnp.asarray(ovt, BF16),
                emt=jnp.asarray(emt, BF16), rope=(jnp.asarray(cq_t), jnp.asarray(ck_t), jnp.asarray(sk_t)))


def _shift_bound(d, q_gain, k_gain, extra=0.0):
    return (d * jnp.max(jnp.abs(q_gain)) * jnp.max(jnp.abs(k_gain)) * 1.01 + extra + 0.1).reshape(1)


MAX_SHIFT = 60.0


def _pad_cols(w, width):
    return jnp.pad(w, ((0, 0), (0, width - w.shape[1])))


def _layer_weights(l, conv_w, conv_b, nsa_q_norm, nsa_k_norm, cmp_pos, cmp_w1, cmp_w2, mla_q_a_norm,
                   mla_kv_norm, mla_wq_b, mla_wkv_b, mla_q_norm, mla_k_norm, out_norm_w, norm2_w):
    w1 = cmp_w1[l].reshape(2, CMP_LEN, HEAD_DIM, CMP_HIDDEN)
    zw = jnp.zeros((CMP_LEN, HEAD_DIM, CMP_HIDDEN), F32)
    cw1 = jnp.concatenate([jnp.concatenate([w1[0], zw], axis=2), jnp.concatenate([zw, w1[1]], axis=2)],
                          axis=1).astype(BF16)
    zc = jnp.zeros((CMP_HIDDEN, HEAD_DIM), F32)
    cw2 = jnp.concatenate([jnp.concatenate([cmp_w2[l, 0], zc], axis=1),
                           jnp.concatenate([zc, cmp_w2[l, 1]], axis=1)], axis=0).astype(BF16)
    cpos = jnp.concatenate([cmp_pos[l, 0], cmp_pos[l, 1]], axis=1)
    kn = nsa_k_norm[l]
    ones64 = jnp.ones((HEAD_DIM,), F32)
    knw_c = jnp.concatenate([kn[0], ones64])[None, :]
    knw_sw = jnp.stack([jnp.concatenate([kn[1], ones64]), jnp.concatenate([kn[2], ones64])])
    qnw = jnp.tile(nsa_q_norm[l], N_HEADS)[None, :]
    wq = mla_wq_b[l].reshape(Q_LORA, N_HEADS, QK_DIM)
    half = ROPE_DIM // 2
    wqs = jnp.concatenate([jnp.zeros((Q_LORA, N_HEADS, HEAD_DIM), F32), wq[:, :, HEAD_DIM + half:],
                           wq[:, :, HEAD_DIM:HEAD_DIM + half]], axis=2)
    pad_q = lambda w: jnp.pad(w, ((0, 2 * LANES - Q_LORA), (0, 0), (0, LANES - QK_DIM))).reshape(
        2 * LANES, N_HEADS * LANES).astype(BF16)
    wkv = mla_wkv_b[l].reshape(KV_LORA, N_HEADS, 2 * HEAD_DIM)
    wk = jnp.pad(wkv[:, :, :HEAD_DIM], ((0, 0), (0, 0), (0, LANES - HEAD_DIM))).reshape(KV_LORA, N_HEADS * LANES)
    wvt = jnp.pad(wkv[:, :, HEAD_DIM:], ((0, 0), (0, 0), (0, DV_PAD - HEAD_DIM))).reshape(KV_LORA, -1).T
    return dict(
        cw1=cw1, cw2=cw2, cw2t=cw2.T, cpos=cpos, knw_c=knw_c, knw_sw=knw_sw, qnw=qnw,
        qaw=_pad_cols(mla_q_a_norm[l][None, :], 2 * LANES), kvw=mla_kv_norm[l][None, :],
        wq=pad_q(wq), wqs=pad_q(wqs), wk=wk.astype(BF16), wvt=wvt.astype(BF16),
        mqn=_pad_cols(mla_q_norm[l][None, :] * (QK_DIM ** -0.5 * LOG2E), LANES),
        mkn=_pad_cols(mla_k_norm[l][None, :], LANES),
        cw=conv_w[l], cb=conv_b[l][None, :], onw=out_norm_w[l][None, :], n2w=norm2_w[l][None, :])


TM_PROJ = 512
TM_PREP = 512
TM_POST = 512
TQ_NSA = 256
TQ_MLA = 256
TQ_SB = 256


def kernel(x, rel_bias, norm1_w, w_in, conv_w, conv_b, nsa_q_norm, nsa_k_norm, cmp_pos, cmp_w1, cmp_w2,
           mla_q_a_norm, mla_kv_norm, mla_wq_b, mla_wkv_b, mla_q_norm, mla_k_norm, out_norm_w, w_out, norm2_w,
           ffn_w1, ffn_w2):
    b, s, d = x.shape
    depth = w_in.shape[0]
    tabs = _tables(s, TQ_NSA)
    x2d = x.reshape(b * s, d)
    wo_b, w1_b, w2_b = w_out.astype(BF16), ffn_w1.astype(BF16), ffn_w2.astype(BF16)
    for l in range(depth):
        w = _layer_weights(l, conv_w, conv_b, nsa_q_norm, nsa_k_norm, cmp_pos, cmp_w1, cmp_w2,
                           mla_q_a_norm, mla_kv_norm, mla_wq_b, mla_wkv_b, mla_q_norm, mla_k_norm, out_norm_w,
                           norm2_w)
        proj = _inproj(x2d, norm1_w[l][None, :], w_in, l, TM_PROJ)
        kcvc, kcvct = _compress(proj, w["cpos"], w["cw1"], w["cw2"], w["cw2t"], w["knw_c"], b, s)
        nsa_shift = _shift_bound(HEAD_DIM, w["qnw"] * (HEAD_DIM ** -0.5 * LOG2E), w["knw_sw"][:, :HEAD_DIM],
                                 jnp.max(jnp.abs(rel_bias)) * LOG2E)
        nsa_args = (proj, kcvc, kcvct, rel_bias, nsa_shift, tabs["bidxct"], tabs["bidx2t"], tabs["ovt"], tabs["emt"],
                    w["qnw"], w["knw_sw"])
        yb = lax.cond(nsa_shift[0] <= MAX_SHIFT,
                      lambda *a: _nsa_attention(*a, b, s, TQ_NSA, True),
                      lambda *a: _nsa_attention(*a, b, s, TQ_NSA, False), *nsa_args)
        mla_shift = _shift_bound(QK_DIM, w["mqn"], w["mkn"])
        qm, km, vmt = _mla_prep(proj, mla_shift, w["qaw"], w["kvw"], w["wq"], w["wqs"], w["wk"], w["wvt"], w["mqn"],
                                w["mkn"], tabs["rope"], s, TM_PREP)
        yc = lax.cond(mla_shift[0] <= MAX_SHIFT,
                      lambda *a: _mla_attention(*a, b, s, TQ_MLA, True),
                      lambda *a: _mla_attention(*a, b, s, TQ_MLA, False), qm, km, vmt)
        yd = _sb_attention(proj, b, s, TQ_SB)
        x2d = _post(proj, yb, yc, yd, x2d, w["cw"], w["cb"], w["onw"], wo_b, w["n2w"], w1_b, w2_b, l, s, TM_POST)
    return x2d.reshape(b, s, d)
```

```python
import functools
import math

import jax
import jax.numpy as jnp
import numpy as np
from jax import lax
from jax.experimental import pallas as pl
from jax.experimental.pallas import tpu as pltpu

F32 = jnp.float32
BF16 = jnp.bfloat16

D_MODEL = 1024
GROUP_WIDTH = 256
HEAD_DIM = 64
N_HEADS = 4
LANES = 128
CMP_LEN = 32
CMP_STRIDE = 16
SLC_LEN = 64
N_SEL = 16
WINDOW = 512
CMP_HIDDEN = 256
Q_LORA = 192
KV_LORA = 128
ROPE_DIM = 32
QK_DIM = 96
ROPE_THETA = 10000.0
N_BUCKETS = 32
MAX_DISTANCE = 128
D_FF = 4096
EPS = 1e-6
NEG = -1e30
LOG2E = math.log2(math.e)
DV_PAD = 80

NP = 2816
CB_NQ = 3
CB_KCVC = 8
CB_KSVS = 9
CB_KWVW = 10
CB_GATE = 11
CB_CQ = 6
CB_CKV = 14
CB_KR = 15
CB_SQ = 8
CB_SK = 9
CB_SV = 10

VMEM_LIMIT = 56 * 1024 * 1024

NT_DIMS = (((1,), (1,)), ((), ()))


def _params(sem):
    return pltpu.CompilerParams(dimension_semantics=sem, vmem_limit_bytes=VMEM_LIMIT)


def _nt(a, b):
    return lax.dot_general(a, b, NT_DIMS, preferred_element_type=F32)


def _dot(a, b):
    return jnp.dot(a, b, preferred_element_type=F32)


def _lane(shape):
    return lax.broadcasted_iota(jnp.int32, shape, len(shape) - 1)


def _row(shape):
    return lax.broadcasted_iota(jnp.int32, shape, len(shape) - 2)


def _lane_tile(j, width):
    return pl.ds(pl.multiple_of(j * width, width), width)


_IN_SEGMENTS = (
    ((0, 1408), 0),
    ((1408, 1420), 1408),
    ((1420, 1612), 1536),
    ((1612, 1740), 1792),
    ((1756, 1772), 1920),
    ((1740, 1756), 1936),
    ((1740, 1772), 1984),
    ((1772, 2540), 2048),
)
IN_COLS = 2540


def _inproj_kernel(x_ref, nw_ref, w_ref, o_ref, w_s):
    @pl.when(pl.program_id(0) == 0)
    def _relayout_weights():
        end = 0
        for (a, b), dst in _IN_SEGMENTS:
            if dst > end:
                w_s[:, end:dst] = jnp.zeros((D_MODEL, dst - end), BF16)
            w_s[:, dst:dst + b - a] = w_ref[:, a:b].astype(BF16)
            end = dst + b - a
        assert end == NP

    x = x_ref[...]
    ms = jnp.mean(x * x, axis=-1, keepdims=True)
    h = (x * lax.rsqrt(ms + EPS) * nw_ref[...]).astype(BF16)
    o_ref[...] = _dot(h, w_s[...])


def _inproj(x2d, nw, w_in, l, tm):
    t = x2d.shape[0]
    return pl.pallas_call(
        _inproj_kernel,
        grid=(t // tm,),
        in_specs=[
            pl.BlockSpec((tm, D_MODEL), lambda i: (i, 0)),
            pl.BlockSpec((1, D_MODEL), lambda i: (0, 0)),
            pl.BlockSpec((None, D_MODEL, IN_COLS), lambda i: (l, 0, 0), pipeline_mode=pl.Buffered(1)),
        ],
        out_specs=pl.BlockSpec((tm, NP), lambda i: (i, 0)),
        out_shape=jax.ShapeDtypeStruct((t, NP), F32),
        scratch_shapes=[pltpu.VMEM((D_MODEL, NP), BF16)],
        compiler_params=_params(("arbitrary",)),
        name="inproj",
    )(x2d, nw, w_in)


def _compress_kernel(x_ref, pos_ref, w1_ref, w2_ref, w2t_ref, knw_ref, o_ref, ot_ref):
    ng = x_ref.shape[1]
    acc_a = jnp.zeros((ng, 2 * CMP_HIDDEN), F32)
    acc_b = jnp.zeros((ng, 2 * CMP_HIDDEN), F32)
    for i in range(CMP_STRIDE):
        x = x_ref[0, :, i, :]
        xa = (x + pos_ref[i:i + 1, :]).astype(BF16)
        xb = (x + pos_ref[CMP_STRIDE + i:CMP_STRIDE + i + 1, :]).astype(BF16)
        acc_a += _dot(xa, w1_ref[i])
        acc_b += _dot(xb, w1_ref[CMP_STRIDE + i])
    pre = acc_a + pltpu.roll(acc_b, ng - 1, 0)
    hdn = (pre * jax.nn.sigmoid(pre)).astype(BF16)
    out = _dot(hdn, w2_ref[...])
    lane = _lane(out.shape)
    is_k = lane < HEAD_DIM
    ss = jnp.sum(jnp.where(is_k, out * out, 0.0), axis=-1, keepdims=True) * (1.0 / HEAD_DIM)
    o_ref[0] = jnp.where(is_k, out * lax.rsqrt(ss + EPS) * knw_ref[...], out)
    ot_ref[0] = _nt(w2t_ref[...], hdn)


def _compress(proj, pos, w1, w2, w2t, knw, b, s):
    ng = s // CMP_STRIDE
    x4 = proj.reshape(b, ng, CMP_STRIDE, NP)
    return pl.pallas_call(
        _compress_kernel,
        grid=(b,),
        in_specs=[
            pl.BlockSpec((1, ng, CMP_STRIDE, LANES), lambda i: (i, 0, 0, CB_KCVC)),
            pl.BlockSpec((CMP_LEN, LANES), lambda i: (0, 0)),
            pl.BlockSpec((CMP_LEN, LANES, 2 * CMP_HIDDEN), lambda i: (0, 0, 0)),
            pl.BlockSpec((2 * CMP_HIDDEN, LANES), lambda i: (0, 0)),
            pl.BlockSpec((LANES, 2 * CMP_HIDDEN), lambda i: (0, 0)),
            pl.BlockSpec((1, LANES), lambda i: (0, 0)),
        ],
        out_specs=[pl.BlockSpec((1, ng, LANES), lambda i: (i, 0, 0)),
                   pl.BlockSpec((1, LANES, ng), lambda i: (i, 0, 0))],
        out_shape=[jax.ShapeDtypeStruct((b, ng, LANES), F32), jax.ShapeDtypeStruct((b, LANES, ng), F32)],
        compiler_params=_params(("arbitrary",)),
        name="nsa_compress",
    )(x4, pos, w1, w2, w2t, knw)


def _softmax_update(sts, vts, m_old, acc_old, fixed=False):
    if fixed:
        return m_old, [acc + _dot(vt, jnp.exp2(st).astype(BF16)) for acc, vt, st in zip(acc_old, vts, sts)]
    m_new = [jnp.maximum(m, jnp.max(st, axis=0, keepdims=True)) for m, st in zip(m_old, sts)]
    ps = [jnp.exp2(st - m).astype(BF16) for st, m in zip(sts, m_new)]
    alphas = [jnp.exp2(mo - mn) for mo, mn in zip(m_old, m_new)]
    acc_new = [al * acc + _dot(vt, p) for al, acc, vt, p in zip(alphas, acc_old, vts, ps)]
    return m_new, acc_new


def _softmax_steps(sts, vts, m_refs, acc_refs, fixed=False):
    m_old = [None] * len(sts) if fixed else [r[...] for r in m_refs]
    m_new, acc_new = _softmax_update(sts, vts, m_old, [r[...] for r in acc_refs], fixed)
    for r, v in zip(() if fixed else m_refs, m_new):
        r[...] = v
    for r, v in zip(acc_refs, acc_new):
        r[...] = v


def _with_ones_row(vt):
    pad = jnp.where(_row((DV_PAD - HEAD_DIM, vt.shape[1])) == 0, 1.0, 0.0).astype(vt.dtype)
    return jnp.concatenate([vt, pad], axis=0)


def _bucket_bias(bidx, relb_ref, h, fill):
    acc = jnp.full(bidx.shape, fill, F32)
    for bk in range(N_BUCKETS):
        acc = jnp.where(bidx == bk, relb_ref[bk, h] * LOG2E, acc)
    return acc


def _group_sums(x, member):
    g = jnp.where(member, 1.0, 0.0).astype(BF16)
    hi = x.astype(BF16)
    lo = (x - hi.astype(F32)).astype(BF16)
    return _dot(hi, g) + _dot(lo, g)


def _dup_low_half(x):
    y = jnp.where(_lane(x.shape) < HEAD_DIM, x, 0.0)
    return y + pltpu.roll(y, HEAD_DIM, 1)


def _nsa_kernel(relb_ref, shift_ref, q_ref, g_ref, kcvc_ref, kcvct_ref, ksvs_ref, kwvw_ref, bidxct_ref, bidx2t_ref, ovt_ref,
                emt_ref, qnw_ref, knw_ref, o_ref,
                biasc_s, bias2_s, ks_s, vst_s, kw_s, vwt_s, kc_s, vct_s, qx_s, m_s, acc_s, *, tq, s_len, fixed):
    b = pl.program_id(0)
    shift = shift_ref[0] if fixed else 0.0
    i = pl.program_id(1)
    n_win = WINDOW // tq
    ng = s_len // CMP_STRIDE

    @pl.when((b == 0) & (i == 0))
    def _build_bias_tables():
        key = _row((tq, tq))
        qry = _lane((tq, tq))
        for h in range(N_HEADS):
            cols = slice(h * tq, (h + 1) * tq)
            far = jnp.full((tq, tq), relb_ref[N_BUCKETS - 1, h] * LOG2E - shift, F32)
            bias2_s[0, :, cols] = jnp.where(key <= qry, _bucket_bias(bidx2t_ref[0], relb_ref, h, NEG) - shift, NEG)
            bias2_s[1, :, cols] = _bucket_bias(bidx2t_ref[1], relb_ref, h, NEG) - shift
            bias2_s[2, :, cols] = far
            bias2_s[3, :, cols] = jnp.where(key > qry, far, NEG)
            bias2_s[4, :, cols] = jnp.full((tq, tq), NEG, F32)

        def body(t, carry):
            bi = bidxct_ref[:, _lane_tile(t, tq)]
            for h in range(N_HEADS):
                biasc_s[t, :, h * tq:(h + 1) * tq] = _bucket_bias(bi, relb_ref, h, NEG)
            return carry

        lax.fori_loop(0, s_len // tq, body, 0)

    @pl.when(i == 0)
    def _prep_kv():
        ch = 256

        def body(t, carry):
            rows = pl.ds(pl.multiple_of(t * ch, ch), ch)
            for src, kdst, vdst, widx in ((ksvs_ref, ks_s, vst_s, 0), (kwvw_ref, kw_s, vwt_s, 1)):
                x = src[rows, :]
                ss = _group_sums(x * x, _row((LANES, LANES)) < HEAD_DIM) * (1.0 / HEAD_DIM)
                kn = x * lax.rsqrt(ss + EPS) * knw_ref[widx:widx + 1, :]
                kdst[rows, 0:LANES] = _dup_low_half(kn).astype(BF16)
                vdst[:, _lane_tile(t, ch)] = _with_ones_row(x.T[HEAD_DIM:, :]).astype(BF16)
            ks_s[rows, LANES:] = emt_ref[rows, :]
            return carry

        lax.fori_loop(0, s_len // ch, body, 0)
        kc_s[...] = _dup_low_half(kcvc_ref[0]).astype(BF16)
        vct_s[...] = kcvct_ref[0][HEAD_DIM:, :].astype(BF16)

    q = q_ref[...]
    lane = _lane((tq, LANES))
    heads = range(N_HEADS)
    gw = 2 * LANES
    head_shift = HEAD_DIM.bit_length() - 1
    same_head = (lax.shift_right_logical(_row((gw, gw)), head_shift)
                 == lax.shift_right_logical(_lane((gw, gw)), head_shift))
    ss = _group_sums(q * q, same_head) * (1.0 / HEAD_DIM)
    qn = q * lax.rsqrt(ss + EPS) * qnw_ref[...] * (HEAD_DIM ** -0.5 * LOG2E)
    for h in heads:
        mine = (lane < HEAD_DIM) if h % 2 == 0 else (lane >= HEAD_DIM)
        qx_s[h, :, 0:LANES] = jnp.where(mine, qn[:, LANES * (h // 2):LANES * (h // 2 + 1)], 0.0).astype(BF16)

    lcs = [_nt(kc_s[...], qx_s[h, :, 0:LANES]) + biasc_s[i, :, h * tq:(h + 1) * tq] for h in heads]
    pcs = [jnp.where(lc > 0.5 * NEG, jnp.exp2(lc - jnp.max(lc, axis=0, keepdims=True)), 0.0) for lc in lcs]
    dens = [jnp.sum(pc, axis=0, keepdims=True) for pc in pcs]
    pcs = [pc / jnp.where(den > 0.0, den, 1.0) for pc, den in zip(pcs, dens)]
    o_cmp = [_dot(vct_s[...], pc.astype(BF16)) for pc in pcs]
    psum = (pcs[0] + pcs[1]) + (pcs[2] + pcs[3])

    n_slc = s_len // SLC_LEN
    n_sel = min(N_SEL, n_slc)
    blk = _row((n_slc, tq))
    tpos = i * tq + _lane((n_slc, tq))
    tblk = tpos // SLC_LEN
    valid = blk * SLC_LEN <= tpos

    p_hi = psum.astype(BF16)
    p_lo = (psum - p_hi.astype(F32)).astype(BF16)
    score = _dot(ovt_ref[...], p_hi) + _dot(ovt_ref[...], p_lo)
    forced = (blk == 0) | (blk == tblk) | (blk == tblk - 1)
    sc = jnp.where(forced, jnp.inf, jnp.where(valid, score[0:n_slc], -jnp.inf))
    rank = jnp.zeros((n_slc, tq), F32)
    for k in range(n_slc):
        ck = sc[k:k + 1, :]
        beats = (ck > sc) | ((ck == sc) & (blk > k))
        rank += jnp.where(beats, 1.0, 0.0)
    pen = jnp.where((rank < float(n_sel)) & valid, 0.0, NEG)
    pen = jnp.concatenate([pen, jnp.zeros((LANES - n_slc, tq), F32)], axis=0)
    pen_t = pen.T.astype(BF16)
    pen = pen.astype(BF16)
    for h in heads:
        qx_s[h, :, LANES:] = pen_t

    ms = [jnp.full((1, tq), NEG, F32)] * (2 * N_HEADS)
    accs = [jnp.zeros((DV_PAD, tq), F32)] * (2 * N_HEADS)
    for jj in range(n_win + 1):
        exists = i >= jj
        rows = pl.ds(pl.multiple_of(jnp.maximum(i - jj, 0) * tq, tq), tq)
        kind = jnp.where(exists, min(jj, 2), 4)
        kind_w = jnp.where(exists, 3 if jj == n_win else min(jj, 2), 4)
        ks = ks_s[rows, 0:LANES]
        kw = kw_s[rows, :]
        masked = _dot(ks_s[rows, LANES:], pen)
        sts = [_nt(ks, qx_s[h, :, 0:LANES]) + masked + bias2_s[kind, :, h * tq:(h + 1) * tq] for h in heads]
        sts += [_nt(kw, qx_s[h, :, 0:LANES]) + bias2_s[kind_w, :, h * tq:(h + 1) * tq] for h in heads]
        vts = [vst_s[:, rows]] * N_HEADS + [vwt_s[:, rows]] * N_HEADS
        ms, accs = _softmax_update(sts, vts, ms, accs, fixed)
    for h in heads:
        if not fixed:
            m_s[h] = ms[h]
        acc_s[h] = accs[h]
    o_win = [accs[N_HEADS + h][0:HEAD_DIM, :] / accs[N_HEADS + h][HEAD_DIM:HEAD_DIM + 1, :] for h in heads]

    def far_step(j, width):
        rows = pl.ds(j * tq if isinstance(j, int) else pl.multiple_of(j * tq, tq), width * tq)
        ks = ks_s[rows, :]
        sts = [_nt(ks, qx_s[h]) + (relb_ref[N_BUCKETS - 1, h] * LOG2E - shift) for h in heads]
        _softmax_steps(sts, [vst_s[:, rows]] * N_HEADS, [m_s.at[h] for h in heads], [acc_s.at[h] for h in heads],
                       fixed)

    n_far = jnp.maximum(i - n_win, 0)
    if fixed:
        def far_sweep(n):
            accs = [acc_s[h] for h in heads]
            for start in range(0, n, 2):
                rows = pl.ds(start * tq, min(2, n - start) * tq)
                ks = ks_s[rows, :]
                sts = [_nt(ks, qx_s[h]) + (relb_ref[N_BUCKETS - 1, h] * LOG2E - shift) for h in heads]
                _, accs = _softmax_update(sts, [vst_s[:, rows]] * N_HEADS, None, accs, True)
            for h in heads:
                acc_s[h] = accs[h]

        for n in range(1, s_len // tq - n_win):
            pl.when(n_far == n)(functools.partial(far_sweep, n))
    else:
        def far_body(p, carry):
            far_step(n_far - 2 * (p + 1), 2)
            return carry

        lax.fori_loop(0, n_far // 2, far_body, 0)
        pl.when(n_far % 2 == 1)(functools.partial(far_step, 0, 1))

    gt = jax.nn.sigmoid(g_ref[...]).T
    ys = []
    for h in heads:
        o_slc = acc_s[h, 0:HEAD_DIM, :] / acc_s[h, HEAD_DIM:HEAD_DIM + 1, :]
        ys.append(gt[3 * h:3 * h + 1, :] * o_cmp[h] + gt[3 * h + 1:3 * h + 2, :] * o_slc
                  + gt[3 * h + 2:3 * h + 3, :] * o_win[h])
    o_ref[...] = jnp.concatenate(ys, axis=0).T


def _nsa_attention(proj, kcvc, kcvct, rel_bias, shift, bidxct, bidx2t, ovt, emt, qnw, knw, b, s, tq, fixed):
    t = b * s
    nq = s // tq
    m_rows = N_HEADS * tq
    ng = s // CMP_STRIDE
    kern = functools.partial(_nsa_kernel, tq=tq, s_len=s, fixed=fixed)
    return pl.pallas_call(
        kern,
        grid=(b, nq),
        in_specs=[
            pl.BlockSpec(memory_space=pltpu.SMEM),
            pl.BlockSpec(memory_space=pltpu.SMEM),
            pl.BlockSpec((tq, 2 * LANES), lambda bi, i: (bi * nq + i, CB_NQ)),
            pl.BlockSpec((tq, LANES), lambda bi, i: (bi * nq + i, CB_GATE)),
            pl.BlockSpec((1, ng, LANES), lambda bi, i: (bi, 0, 0)),
            pl.BlockSpec((1, LANES, ng), lambda bi, i: (bi, 0, 0)),
            pl.BlockSpec((s, LANES), lambda bi, i: (bi, CB_KSVS)),
            pl.BlockSpec((s, LANES), lambda bi, i: (bi, CB_KWVW)),
            pl.BlockSpec((ng, s), lambda bi, i: (0, 0)),
            pl.BlockSpec((2, tq, tq), lambda bi, i: (0, 0, 0)),
            pl.BlockSpec((LANES, ng), lambda bi, i: (0, 0)),
            pl.BlockSpec((s, LANES), lambda bi, i: (0, 0)),
            pl.BlockSpec((1, 2 * LANES), lambda bi, i: (0, 0)),
            pl.BlockSpec((2, LANES), lambda bi, i: (0, 0)),
        ],
        out_specs=pl.BlockSpec((tq, 2 * LANES), lambda bi, i: (bi * nq + i, 0)),
        out_shape=jax.ShapeDtypeStruct((t, GROUP_WIDTH), F32),
        scratch_shapes=[
            pltpu.VMEM((nq, ng, m_rows), F32),
            pltpu.VMEM((5, tq, m_rows), F32),
            pltpu.VMEM((s, 2 * LANES), BF16),
            pltpu.VMEM((DV_PAD, s), BF16),
            pltpu.VMEM((s, LANES), BF16),
            pltpu.VMEM((DV_PAD, s), BF16),
            pltpu.VMEM((ng, LANES), BF16),
            pltpu.VMEM((HEAD_DIM, ng), BF16),
            pltpu.VMEM((N_HEADS, tq, 2 * LANES), BF16),
            pltpu.VMEM((N_HEADS, 1, tq), F32),
            pltpu.VMEM((N_HEADS, DV_PAD, tq), F32),
        ],
        compiler_params=_params(("arbitrary", "arbitrary")),
        name="nsa_attention",
    )(rel_bias, shift, proj, proj, kcvc, kcvct, proj, proj, bidxct, bidx2t, ovt, emt, qnw, knw)


def _mla_prep_kernel(shift_ref, cq_ref, ckv_ref, kr_ref, qaw_ref, kvw_ref, wq_ref, wqs_ref, wk_ref, wvt_ref, qnw_ref, knw_ref,
                     cq_t_ref, ck_t_ref, sk_t_ref, qo_ref, ko_ref, vto_ref):
    cq = cq_ref[...]
    ms = jnp.sum(cq * cq, axis=-1, keepdims=True) * (1.0 / Q_LORA)
    hq = (cq * lax.rsqrt(ms + EPS) * qaw_ref[...]).astype(BF16)
    qf = _dot(hq, wq_ref[...])
    qsw = _dot(hq, wqs_ref[...])
    ckv = ckv_ref[...]
    ms = jnp.mean(ckv * ckv, axis=-1, keepdims=True)
    hkv = (ckv * lax.rsqrt(ms + EPS) * kvw_ref[...]).astype(BF16)
    kf = _dot(hkv, wk_ref[...])
    vt = _nt(wvt_ref[...], hkv)
    vto_ref[...] = jnp.where(_row(vt.shape) % DV_PAD == HEAD_DIM, 1.0, vt).astype(BF16)
    krb = kr_ref[...]
    kr_rot = krb * ck_t_ref[...] + pltpu.roll(krb, HEAD_DIM, 1) * sk_t_ref[...]
    is_shift = _lane((cq.shape[0], LANES)) == QK_DIM
    for h in range(N_HEADS):
        cols = slice(LANES * h, LANES * (h + 1))
        x = qf[:, cols] * cq_t_ref[...] + qsw[:, cols] * sk_t_ref[...]
        ss = jnp.sum(x * x, axis=-1, keepdims=True) * (1.0 / QK_DIM)
        qn = x * lax.rsqrt(ss + EPS) * qnw_ref[...]
        qo_ref[:, cols] = jnp.where(is_shift, -shift_ref[0], qn).astype(BF16)
        k = kf[:, cols] + kr_rot
        ss = jnp.sum(k * k, axis=-1, keepdims=True) * (1.0 / QK_DIM)
        ko_ref[:, cols] = jnp.where(is_shift, 1.0, k * lax.rsqrt(ss + EPS) * knw_ref[...]).astype(BF16)


def _mla_prep(proj, shift, qaw, kvw, wq, wqs, wk, wvt, qnw, knw, tabs, s, tm):
    t = proj.shape[0]
    npos = s // tm
    row = lambda i: (i, 0)
    const = lambda i: (0, 0)
    tab = pl.BlockSpec((tm, LANES), lambda i: (i % npos, 0))
    out = jax.ShapeDtypeStruct((t, N_HEADS * LANES), BF16)
    return pl.pallas_call(
        _mla_prep_kernel,
        grid=(t // tm,),
        in_specs=[
            pl.BlockSpec(memory_space=pltpu.SMEM),
            pl.BlockSpec((tm, 2 * LANES), lambda i: (i, CB_CQ)),
            pl.BlockSpec((tm, LANES), lambda i: (i, CB_CKV)),
            pl.BlockSpec((tm, LANES), lambda i: (i, CB_KR)),
            pl.BlockSpec((1, 2 * LANES), const),
            pl.BlockSpec((1, LANES), const),
            pl.BlockSpec((2 * LANES, N_HEADS * LANES), const),
            pl.BlockSpec((2 * LANES, N_HEADS * LANES), const),
            pl.BlockSpec((LANES, N_HEADS * LANES), const),
            pl.BlockSpec((N_HEADS * DV_PAD, LANES), const),
            pl.BlockSpec((1, LANES), const),
            pl.BlockSpec((1, LANES), const),
            tab, tab, tab,
        ],
        out_specs=[pl.BlockSpec((tm, N_HEADS * LANES), row), pl.BlockSpec((tm, N_HEADS * LANES), row),
                   pl.BlockSpec((N_HEADS * DV_PAD, tm), lambda i: (0, i))],
        out_shape=[out, out, jax.ShapeDtypeStruct((N_HEADS * DV_PAD, t), BF16)],
        compiler_params=_params(("arbitrary",)),
        name="mla_prep",
    )(shift, proj, proj, proj, qaw, kvw, wq, wqs, wk, wvt, qnw, knw, *tabs)


def _mla_attn_kernel(q_ref, k_ref, vt_ref, o_ref, m_s, acc_s, *, tq, nq, fixed):
    i = pl.program_id(1)
    heads = range(N_HEADS)
    causal = _row((tq, tq)) <= _lane((tq, tq))

    def logits(rows, masked):
        sts = [_nt(k_ref[rows, LANES * h:LANES * (h + 1)], q_ref[:, LANES * h:LANES * (h + 1)]) for h in heads]
        return [jnp.where(causal, st, NEG) for st in sts] if masked else sts

    def values(rows):
        return [vt_ref[DV_PAD * h:DV_PAD * (h + 1), rows] for h in heads]

    if fixed:
        def sweep(n):
            accs = [jnp.zeros((DV_PAD, tq), F32)] * N_HEADS
            for start in range(0, n, 2):
                rows = pl.ds(start * tq, min(2, n - start) * tq)
                _, accs = _softmax_update(logits(rows, False), values(rows), None, accs, True)
            rows = pl.ds(n * tq, tq)
            _, accs = _softmax_update(logits(rows, True), values(rows), None, accs, True)
            for h in heads:
                acc_s[h] = accs[h]

        for n in range(nq):
            pl.when(i == n)(functools.partial(sweep, n))
    else:
        for h in heads:
            m_s[h] = jnp.full((1, tq), NEG, F32)
            acc_s[h] = jnp.zeros((DV_PAD, tq), F32)

        def step(j, width, masked):
            rows = pl.ds(j * tq if isinstance(j, int) else pl.multiple_of(j * tq, tq), width * tq)
            _softmax_steps(logits(rows, masked), values(rows), [m_s.at[h] for h in heads],
                           [acc_s.at[h] for h in heads])

        step(i, 1, True)

        def body(jj, carry):
            step(i - 2 * jj, 2, False)
            return carry

        lax.fori_loop(1, i // 2 + 1, body, 0)
        pl.when(i % 2 == 1)(functools.partial(step, 0, 1, False))
    yt = jnp.concatenate([acc_s[h, 0:HEAD_DIM, :] / acc_s[h, HEAD_DIM:HEAD_DIM + 1, :] for h in range(N_HEADS)],
                         axis=0)
    o_ref[...] = yt.T


def _mla_attention(qm, km, vmt, b, s, tq, fixed):
    t = b * s
    nq = s // tq
    w = N_HEADS * LANES
    return pl.pallas_call(
        functools.partial(_mla_attn_kernel, tq=tq, nq=nq, fixed=fixed),
        grid=(b, nq),
        in_specs=[
            pl.BlockSpec((tq, w), lambda bi, i: (bi * nq + i, 0)),
            pl.BlockSpec((s, w), lambda bi, i: (bi, 0)),
            pl.BlockSpec((N_HEADS * DV_PAD, s), lambda bi, i: (0, bi)),
        ],
        out_specs=pl.BlockSpec((tq, 2 * LANES), lambda bi, i: (bi * nq + i, 0)),
        out_shape=jax.ShapeDtypeStruct((t, GROUP_WIDTH), F32),
        scratch_shapes=[pltpu.VMEM((N_HEADS, 1, tq), F32), pltpu.VMEM((N_HEADS, DV_PAD, tq), F32)],
        compiler_params=_params(("arbitrary", "arbitrary")),
        name="mla_attention",
    )(qm, km, vmt)


def _sb_kernel(q_ref, k_ref, v_ref, o_ref, kb_s, vt_s, q_s, r_s, acc_s, kmax_s, *, tq, s_len):
    i = pl.program_id(1)
    gw = 2 * LANES
    head_shift = HEAD_DIM.bit_length() - 1
    same_head = (lax.shift_right_logical(_row((gw, gw)), head_shift)
                 == lax.shift_right_logical(_lane((gw, gw)), head_shift))

    @pl.when(i == 0)
    def _cast_kv():
        ch = 256

        def body(t, kmax):
            rows = pl.ds(pl.multiple_of(t * ch, ch), ch)
            k = k_ref[rows, :]
            kb_s[rows, :] = k.astype(BF16)
            vt_s[:, _lane_tile(t, ch)] = v_ref[rows, :].T.astype(BF16)
            return jnp.maximum(kmax, _group_sums(k * k, same_head))

        kmax = lax.fori_loop(0, s_len // ch, body, jnp.zeros((ch, gw), F32))
        kmax_s[0] = jnp.max(kmax)

    key = _row((tq, tq))
    qry = _lane((tq, tq))
    strict = key < qry
    tri = jnp.where(key <= qry, 1.0, 0.0).astype(BF16)
    tri2 = jnp.concatenate([tri, tri], axis=1)
    lane = _lane((tq, LANES))
    q = q_ref[...] * (HEAD_DIM ** -0.5 * LOG2E)
    for h in range(N_HEADS):
        mine = (lane < HEAD_DIM) if h % 2 == 0 else (lane >= HEAD_DIM)
        q_s[h] = jnp.where(mine, q[:, LANES * (h // 2):LANES * (h // 2 + 1)], 0.0).astype(BF16)
        r_s[h] = jnp.zeros((1, tq), F32)
        acc_s[h] = jnp.zeros((HEAD_DIM, tq), F32)
    qmax = jnp.max(_group_sums(q * q, same_head))
    z_bound = jnp.sqrt(jnp.full((1, tq), qmax * kmax_s[0], F32)) * 1.01 + 1.0

    heads = range(N_HEADS)

    def step(j, width, masked):
        rs = [r_s[h] for h in heads]
        accs = [acc_s[h] for h in heads]
        tiles = [j + width - 1 - w for w in range(width)]
        rows = [pl.ds(t * tq if isinstance(t, int) else pl.multiple_of(t * tq, tq), tq) for t in tiles]
        zs = [[_nt(kb_s[r, LANES * (h // 2):LANES * (h // 2 + 1)], q_s[h]) for h in heads] for r in rows]
        part = []
        for w, zt in enumerate(zs):
            negabs = [pltpu.bitcast(pltpu.bitcast(z, jnp.uint32) | jnp.uint32(0x80000000), F32) for z in zt]
            sps = [jnp.maximum(z, 0.0) + jnp.log2(1.0 + jnp.exp2(na)) for z, na in zip(zt, negabs)]
            if masked and w == 0:
                sps = [jnp.where(strict, sp, 0.0) for sp in sps]
            his = [sp.astype(BF16) for sp in sps]
            los = [(sp - hi.astype(F32)).astype(BF16) for sp, hi in zip(sps, his)]
            part.append([_dot(tri2, jnp.concatenate([hi, lo], axis=0)) for hi, lo in zip(his, los)])
        for w, (r, zt, pt) in enumerate(zip(rows, zs, part)):
            csums = [p + rc for p, rc in zip(pt, rs)]
            als = [jnp.exp2(z - cs) for z, cs in zip(zt, csums)]
            if masked and w == 0:
                als = [jnp.where(strict, a, 0.0) for a in als]
            accs = [acc + _dot(vt_s[HEAD_DIM * h:HEAD_DIM * (h + 1), r], als[h].astype(BF16))
                    for h, acc in zip(heads, accs)]
            rs = [cs[0:1, :] for cs in csums]
        for h in heads:
            acc_s[h] = accs[h]
            r_s[h] = rs[h]

    def live():
        r_min = jnp.minimum(jnp.minimum(r_s[0], r_s[1]), jnp.minimum(r_s[2], r_s[3]))
        return jnp.max(z_bound - r_min) >= -150.0

    step(i, 1, True)
    pl.when(i >= 1)(functools.partial(step, i - 1, 1, False))

    n_rest = jnp.maximum(i - 1, 0)
    n_pairs = n_rest // 2

    def cond(carry):
        p, alive = carry
        return (p < n_pairs) & alive

    def body(carry):
        p, _ = carry
        step(i - 1 - 2 * (p + 1), 2, False)
        return p + 1, live()

    _, alive = lax.while_loop(cond, body, (jnp.int32(0), live()))
    pl.when((n_rest % 2 == 1) & alive)(functools.partial(step, 0, 1, False))
    o_ref[...] = jnp.concatenate([acc_s[h] for h in range(N_HEADS)], axis=0).T


def _sb_attention(proj, b, s, tq):
    t = b * s
    nq = s // tq
    w = 2 * LANES
    return pl.pallas_call(
        functools.partial(_sb_kernel, tq=tq, s_len=s),
        grid=(b, nq),
        in_specs=[
            pl.BlockSpec((tq, w), lambda bi, i: (bi * nq + i, CB_SQ)),
            pl.BlockSpec((s, w), lambda bi, i: (bi, CB_SK)),
            pl.BlockSpec((s, w), lambda bi, i: (bi, CB_SV)),
        ],
        out_specs=pl.BlockSpec((tq, w), lambda bi, i: (bi * nq + i, 0)),
        out_shape=jax.ShapeDtypeStruct((t, GROUP_WIDTH), F32),
        scratch_shapes=[pltpu.VMEM((s, w), BF16), pltpu.VMEM((w, s), BF16),
                        pltpu.VMEM((N_HEADS, tq, LANES), BF16),
                        pltpu.VMEM((N_HEADS, 1, tq), F32), pltpu.VMEM((N_HEADS, HEAD_DIM, tq), F32),
                        pltpu.SMEM((1,), F32)],
        compiler_params=_params(("arbitrary", "arbitrary")),
        name="sb_attention",
    )(proj, proj, proj)


def _post_kernel(a_ref, ap_ref, yb_ref, yc_ref, yd_ref, x_ref, cw_ref, cb_ref, onw_ref, wo_ref, n2w_ref,
                 w1_ref, w2_ref, o_ref, *, tm, s_len, ffc):
    i = pl.program_id(0)
    a = a_ref[...]
    gw = GROUP_WIDTH
    v = a[:, gw:2 * gw] * a[:, 2 * gw:3 * gw]
    ap = ap_ref[...]
    first = (i * tm) % s_len == 0
    vp = jnp.where(first, 0.0, ap[:, gw:2 * gw] * ap[:, 2 * gw:3 * gw])
    row = _row(v.shape)
    v1 = jnp.where(row == 0, vp[7:8, :], pltpu.roll(v, 1, 0))
    v2 = jnp.where(row == 0, vp[6:7, :], jnp.where(row == 1, vp[7:8, :], pltpu.roll(v, 2, 0)))
    conv = cw_ref[0:1, :] * v2 + cw_ref[1:2, :] * v1 + cw_ref[2:3, :] * v
    ya = a[:, 0:gw] * (conv + cb_ref[...])

    mix = None
    for g, y in enumerate((ya, yb_ref[...], yc_ref[...], yd_ref[...])):
        ms = jnp.mean(y * y, axis=-1, keepdims=True)
        yn = (y * lax.rsqrt(ms + EPS) * onw_ref[:, gw * g:gw * (g + 1)]).astype(BF16)
        part = _dot(yn, wo_ref[gw * g:gw * (g + 1), :])
        mix = part if mix is None else mix + part
    x1 = x_ref[...] + mix

    ms = jnp.mean(x1 * x1, axis=-1, keepdims=True)
    h2 = (x1 * lax.rsqrt(ms + EPS) * n2w_ref[...]).astype(BF16)
    ff = None
    for cidx in range(D_FF // ffc):
        u = _dot(h2, w1_ref[:, ffc * cidx:ffc * (cidx + 1)])
        u = jnp.square(jnp.maximum(u, 0.0)).astype(BF16)
        part = _dot(u, w2_ref[ffc * cidx:ffc * (cidx + 1), :])
        ff = part if ff is None else ff + part
    o_ref[...] = x1 + ff


def _post(proj, yb, yc, yd, x2d, cw, cb, onw, wo, n2w, w1, w2, l, s, tm):
    t = x2d.shape[0]
    gw = GROUP_WIDTH
    row = lambda i: (i, 0)
    const = lambda i: (0, 0)
    layer = lambda i: (l, 0, 0)
    once = pl.Buffered(1)
    kern = functools.partial(_post_kernel, tm=tm, s_len=s, ffc=1024)
    return pl.pallas_call(
        kern,
        grid=(t // tm,),
        in_specs=[
            pl.BlockSpec((tm, 3 * gw), row),
            pl.BlockSpec((8, 3 * gw), lambda i: (jnp.maximum(i * (tm // 8) - 1, 0), 0)),
            pl.BlockSpec((tm, gw), row),
            pl.BlockSpec((tm, gw), row),
            pl.BlockSpec((tm, gw), row),
            pl.BlockSpec((tm, D_MODEL), row),
            pl.BlockSpec((3, gw), const),
            pl.BlockSpec((1, gw), const),
            pl.BlockSpec((1, D_MODEL), const),
            pl.BlockSpec((None, D_MODEL, D_MODEL), layer, pipeline_mode=once),
            pl.BlockSpec((1, D_MODEL), const),
            pl.BlockSpec((None, D_MODEL, D_FF), layer, pipeline_mode=once),
            pl.BlockSpec((None, D_FF, D_MODEL), layer, pipeline_mode=once),
        ],
        out_specs=pl.BlockSpec((tm, D_MODEL), row),
        out_shape=jax.ShapeDtypeStruct((t, D_MODEL), F32),
        compiler_params=_params(("arbitrary",)),
        name="post",
    )(proj, proj, yb, yc, yd, x2d, cw, cb, onw, wo, n2w, w1, w2)


def _t5_bucket(dist):
    max_exact = N_BUCKETS // 2
    d = np.maximum(dist, 0)
    large = max_exact + (np.log(np.maximum(d, 1) / max_exact) / math.log(MAX_DISTANCE / max_exact)
                         * (N_BUCKETS - max_exact)).astype(np.int32)
    return np.where(d < max_exact, d, np.minimum(large, N_BUCKETS - 1)).astype(np.int32)


def _tables(s, tq_nsa):
    n_cmp = (s - CMP_LEN) // CMP_STRIDE + 1
    ng = s // CMP_STRIDE
    n_slc = s // SLC_LEN
    tpos = np.arange(s)[None, :]
    n = np.arange(ng)[:, None]
    dist_c = tpos - (n * CMP_STRIDE + CMP_LEN - 1)
    bidxct = np.where((dist_c >= 0) & (n < n_cmp), _t5_bucket(dist_c), -1).astype(np.int32)
    key = np.arange(tq_nsa)[:, None]
    qry = np.arange(tq_nsa)[None, :]
    bidx2t = np.stack([_t5_bucket(qry - key), _t5_bucket(tq_nsa + qry - key)])
    starts = np.arange(n_cmp) * CMP_STRIDE
    ends = starts + CMP_LEN
    s0 = np.arange(n_slc) * SLC_LEN
    s1 = s0 + SLC_LEN
    ovl = np.clip(np.minimum(ends[:, None], s1[None]) - np.maximum(starts[:, None], s0[None]), 0, None) / CMP_LEN
    ovt = np.zeros((LANES, ng), np.float32)
    ovt[:n_slc, :n_cmp] = ovl.T
    emt = (np.arange(LANES)[None, :] == (np.arange(s) // SLC_LEN)[:, None]).astype(np.float32)
    inv = 1.0 / (ROPE_THETA ** (np.arange(0, ROPE_DIM, 2, dtype=np.float64) / ROPE_DIM))
    ang = (np.arange(s, dtype=np.float32)[:, None] * inv.astype(np.float32)[None, :]).astype(np.float64)
    cos, sin = np.cos(ang).astype(np.float32), np.sin(ang).astype(np.float32)
    z32 = np.zeros((s, 32), np.float32)
    z64 = np.zeros((s, 64), np.float32)
    one64 = np.ones((s, 64), np.float32)
    cq_t = np.concatenate([one64, cos, cos, z32], axis=1)
    ck_t = np.concatenate([z64, cos, cos, z32], axis=1)
    sk_t = np.concatenate([z64, -sin, sin, z32], axis=1)
    return dict(bidxct=jnp.asarray(bidxct), bidx2t=jnp.asarray(bidx2t), ovt=jnp.asarray(ovt, BF16),
                emt=jnp.asarray(emt, BF16), rope=(jnp.asarray(cq_t), jnp.asarray(ck_t), jnp.asarray(sk_t)))


def _shift_bound(d, q_gain, k_gain, extra=0.0):
    return (d * jnp.max(jnp.abs(q_gain)) * jnp.max(jnp.abs(k_gain)) * 1.01 + extra + 0.1).reshape(1)


MAX_SHIFT = 60.0


def _pad_cols(w, width):
    return jnp.pad(w, ((0, 0), (0, width - w.shape[1])))


def _layer_weights(l, conv_w, conv_b, nsa_q_norm, nsa_k_norm, cmp_pos, cmp_w1, cmp_w2, mla_q_a_norm,
                   mla_kv_norm, mla_wq_b, mla_wkv_b, mla_q_norm, mla_k_norm, out_norm_w, norm2_w):
    w1 = cmp_w1[l].reshape(2, CMP_LEN, HEAD_DIM, CMP_HIDDEN)
    zw = jnp.zeros((CMP_LEN, HEAD_DIM, CMP_HIDDEN), F32)
    cw1 = jnp.concatenate([jnp.concatenate([w1[0], zw], axis=2), jnp.concatenate([zw, w1[1]], axis=2)],
                          axis=1).astype(BF16)
    zc = jnp.zeros((CMP_HIDDEN, HEAD_DIM), F32)
    cw2 = jnp.concatenate([jnp.concatenate([cmp_w2[l, 0], zc], axis=1),
                           jnp.concatenate([zc, cmp_w2[l, 1]], axis=1)], axis=0).astype(BF16)
    cpos = jnp.concatenate([cmp_pos[l, 0], cmp_pos[l, 1]], axis=1)
    kn = nsa_k_norm[l]
    ones64 = jnp.ones((HEAD_DIM,), F32)
    knw_c = jnp.concatenate([kn[0], ones64])[None, :]
    knw_sw = jnp.stack([jnp.concatenate([kn[1], ones64]), jnp.concatenate([kn[2], ones64])])
    qnw = jnp.tile(nsa_q_norm[l], N_HEADS)[None, :]
    wq = mla_wq_b[l].reshape(Q_LORA, N_HEADS, QK_DIM)
    half = ROPE_DIM // 2
    wqs = jnp.concatenate([jnp.zeros((Q_LORA, N_HEADS, HEAD_DIM), F32), wq[:, :, HEAD_DIM + half:],
                           wq[:, :, HEAD_DIM:HEAD_DIM + half]], axis=2)
    pad_q = lambda w: jnp.pad(w, ((0, 2 * LANES - Q_LORA), (0, 0), (0, LANES - QK_DIM))).reshape(
        2 * LANES, N_HEADS * LANES).astype(BF16)
    wkv = mla_wkv_b[l].reshape(KV_LORA, N_HEADS, 2 * HEAD_DIM)
    wk = jnp.pad(wkv[:, :, :HEAD_DIM], ((0, 0), (0, 0), (0, LANES - HEAD_DIM))).reshape(KV_LORA, N_HEADS * LANES)
    wvt = jnp.pad(wkv[:, :, HEAD_DIM:], ((0, 0), (0, 0), (0, DV_PAD - HEAD_DIM))).reshape(KV_LORA, -1).T
    return dict(
        cw1=cw1, cw2=cw2, cw2t=cw2.T, cpos=cpos, knw_c=knw_c, knw_sw=knw_sw, qnw=qnw,
        qaw=_pad_cols(mla_q_a_norm[l][None, :], 2 * LANES), kvw=mla_kv_norm[l][None, :],
        wq=pad_q(wq), wqs=pad_q(wqs), wk=wk.astype(BF16), wvt=wvt.astype(BF16),
        mqn=_pad_cols(mla_q_norm[l][None, :] * (QK_DIM ** -0.5 * LOG2E), LANES),
        mkn=_pad_cols(mla_k_norm[l][None, :], LANES),
        cw=conv_w[l], cb=conv_b[l][None, :], onw=out_norm_w[l][None, :], n2w=norm2_w[l][None, :])


TM_PROJ = 512
TM_PREP = 512
TM_POST = 512
TQ_NSA = 256
TQ_MLA = 256
TQ_SB = 256


def kernel(x, rel_bias, norm1_w, w_in, conv_w, conv_b, nsa_q_norm, nsa_k_norm, cmp_pos, cmp_w1, cmp_w2,
           mla_q_a_norm, mla_kv_norm, mla_wq_b, mla_wkv_b, mla_q_norm, mla_k_norm, out_norm_w, w_out, norm2_w,
           ffn_w1, ffn_w2):
    b, s, d = x.shape
    depth = w_in.shape[0]
    tabs = _tables(s, TQ_NSA)
    x2d = x.reshape(b * s, d)
    wo_b, w1_b, w2_b = w_out.astype(BF16), ffn_w1.astype(BF16), ffn_w2.astype(BF16)
    for l in range(depth):
        w = _layer_weights(l, conv_w, conv_b, nsa_q_norm, nsa_k_norm, cmp_pos, cmp_w1, cmp_w2,
                           mla_q_a_norm, mla_kv_norm, mla_wq_b, mla_wkv_b, mla_q_norm, mla_k_norm, out_norm_w,
                           norm2_w)
        proj = _inproj(x2d, norm1_w[l][None, :], w_in, l, TM_PROJ)
        kcvc, kcvct = _compress(proj, w["cpos"], w["cw1"], w["cw2"], w["cw2t"], w["knw_c"], b, s)
        nsa_shift = _shift_bound(HEAD_DIM, w["qnw"] * (HEAD_DIM ** -0.5 * LOG2E), w["knw_sw"][:, :HEAD_DIM],
                                 jnp.max(jnp.abs(rel_bias)) * LOG2E)
        nsa_args = (proj, kcvc, kcvct, rel_bias, nsa_shift, tabs["bidxct"], tabs["bidx2t"], tabs["ovt"], tabs["emt"],
                    w["qnw"], w["knw_sw"])
        yb = lax.cond(nsa_shift[0] <= MAX_SHIFT,
                      lambda *a: _nsa_attention(*a, b, s, TQ_NSA, True),
                      lambda *a: _nsa_attention(*a, b, s, TQ_NSA, False), *nsa_args)
        mla_shift = _shift_bound(QK_DIM, w["mqn"], w["mkn"])
        qm, km, vmt = _mla_prep(proj, mla_shift, w["qaw"], w["kvw"], w["wq"], w["wqs"], w["wk"], w["wvt"], w["mqn"],
                                w["mkn"], tabs["rope"], s, TM_PREP)
        yc = lax.cond(mla_shift[0] <= MAX_SHIFT,
                      lambda *a: _mla_attention(*a, b, s, TQ_MLA, True),
                      lambda *a: _mla_attention(*a, b, s, TQ_MLA, False), qm, km, vmt)
        yd = _sb_attention(proj, b, s, TQ_SB)
        x2d = _post(proj, yb, yc, yd, x2d, w["cw"], w["cb"], w["onw"], wo_b, w["n2w"], w1_b, w2_b, l, s, TM_POST)
    return x2d.reshape(b, s, d)
```

```python
import functools
import math

import jax
import jax.numpy as jnp
import numpy as np
from jax import lax
from jax.experimental import pallas as pl
from jax.experimental.pallas import tpu as pltpu

F32 = jnp.float32
BF16 = jnp.bfloat16

D_MODEL = 1024
GROUP_WIDTH = 256
HEAD_DIM = 64
N_HEADS = 4
LANES = 128
CMP_LEN = 32
CMP_STRIDE = 16
SLC_LEN = 64
N_SEL = 16
WINDOW = 512
CMP_HIDDEN = 256
Q_LORA = 192
KV_LORA = 128
ROPE_DIM = 32
QK_DIM = 96
ROPE_THETA = 10000.0
N_BUCKETS = 32
MAX_DISTANCE = 128
D_FF = 4096
EPS = 1e-6
NEG = -1e30
LOG2E = math.log2(math.e)
DV_PAD = 80

NP = 2816
CB_NQ = 3
CB_KCVC = 8
CB_KSVS = 9
CB_KWVW = 10
CB_GATE = 11
CB_CQ = 6
CB_CKV = 14
CB_KR = 15
CB_SQ = 8
CB_SK = 9
CB_SV = 10

VMEM_LIMIT = 56 * 1024 * 1024

NT_DIMS = (((1,), (1,)), ((), ()))


def _params(sem):
    return pltpu.CompilerParams(dimension_semantics=sem, vmem_limit_bytes=VMEM_LIMIT)


def _nt(a, b):
    return lax.dot_general(a, b, NT_DIMS, preferred_element_type=F32)


def _dot(a, b):
    return jnp.dot(a, b, preferred_element_type=F32)


def _lane(shape):
    return lax.broadcasted_iota(jnp.int32, shape, len(shape) - 1)


def _row(shape):
    return lax.broadcasted_iota(jnp.int32, shape, len(shape) - 2)


def _lane_tile(j, width):
    return pl.ds(pl.multiple_of(j * width, width), width)


_IN_SEGMENTS = (
    ((0, 1408), 0),
    ((1408, 1420), 1408),
    ((1420, 1612), 1536),
    ((1612, 1740), 1792),
    ((1756, 1772), 1920),
    ((1740, 1756), 1936),
    ((1740, 1772), 1984),
    ((1772, 2540), 2048),
)
IN_COLS = 2540


def _inproj_kernel(x_ref, nw_ref, w_ref, o_ref, w_s):
    @pl.when(pl.program_id(0) == 0)
    def _relayout_weights():
        end = 0
        for (a, b), dst in _IN_SEGMENTS:
            if dst > end:
                w_s[:, end:dst] = jnp.zeros((D_MODEL, dst - end), BF16)
            w_s[:, dst:dst + b - a] = w_ref[:, a:b].astype(BF16)
            end = dst + b - a
        assert end == NP

    x = x_ref[...]
    ms = jnp.mean(x * x, axis=-1, keepdims=True)
    h = (x * lax.rsqrt(ms + EPS) * nw_ref[...]).astype(BF16)
    o_ref[...] = _dot(h, w_s[...])


def _inproj(x2d, nw, w_in, l, tm):
    t = x2d.shape[0]
    return pl.pallas_call(
        _inproj_kernel,
        grid=(t // tm,),
        in_specs=[
            pl.BlockSpec((tm, D_MODEL), lambda i: (i, 0)),
            pl.BlockSpec((1, D_MODEL), lambda i: (0, 0)),
            pl.BlockSpec((None, D_MODEL, IN_COLS), lambda i: (l, 0, 0), pipeline_mode=pl.Buffered(1)),
        ],
        out_specs=pl.BlockSpec((tm, NP), lambda i: (i, 0)),
        out_shape=jax.ShapeDtypeStruct((t, NP), F32),
        scratch_shapes=[pltpu.VMEM((D_MODEL, NP), BF16)],
        compiler_params=_params(("arbitrary",)),
        name="inproj",
    )(x2d, nw, w_in)


def _compress_kernel(x_ref, pos_ref, w1_ref, w2_ref, w2t_ref, knw_ref, o_ref, ot_ref):
    ng = x_ref.shape[1]
    x = x_ref[0]
    acc_a = _dot((x + pos_ref[0:1, :]).astype(BF16), w1_ref[0])
    acc_b = _dot((x + pos_ref[1:2, :]).astype(BF16), w1_ref[1])
    pre = acc_a + pltpu.roll(acc_b, ng - 1, 0)
    hdn = (pre * jax.nn.sigmoid(pre)).astype(BF16)
    out = _dot(hdn, w2_ref[...])
    lane = _lane(out.shape)
    is_k = lane < HEAD_DIM
    ss = jnp.sum(jnp.where(is_k, out * out, 0.0), axis=-1, keepdims=True) * (1.0 / HEAD_DIM)
    o_ref[0] = jnp.where(is_k, out * lax.rsqrt(ss + EPS) * knw_ref[...], out)
    ot_ref[0] = _nt(w2t_ref[...], hdn)


def _compress(proj, pos, w1, w2, w2t, knw, b, s):
    ng = s // CMP_STRIDE
    gl = CMP_STRIDE * LANES
    xg = proj[:, CB_KCVC * LANES:(CB_KCVC + 1) * LANES].reshape(b, ng, gl)
    return pl.pallas_call(
        _compress_kernel,
        grid=(b,),
        in_specs=[
            pl.BlockSpec((1, ng, gl), lambda i: (i, 0, 0)),
            pl.BlockSpec((2, gl), lambda i: (0, 0)),
            pl.BlockSpec((2, gl, 2 * CMP_HIDDEN), lambda i: (0, 0, 0)),
            pl.BlockSpec((2 * CMP_HIDDEN, LANES), lambda i: (0, 0)),
            pl.BlockSpec((LANES, 2 * CMP_HIDDEN), lambda i: (0, 0)),
            pl.BlockSpec((1, LANES), lambda i: (0, 0)),
        ],
        out_specs=[pl.BlockSpec((1, ng, LANES), lambda i: (i, 0, 0)),
                   pl.BlockSpec((1, LANES, ng), lambda i: (i, 0, 0))],
        out_shape=[jax.ShapeDtypeStruct((b, ng, LANES), F32), jax.ShapeDtypeStruct((b, LANES, ng), F32)],
        compiler_params=_params(("arbitrary",)),
        name="nsa_compress",
    )(xg, pos, w1, w2, w2t, knw)


def _softmax_update(sts, vts, m_old, acc_old, fixed=False):
    if fixed:
        return m_old, [acc + _dot(vt, jnp.exp2(st).astype(BF16)) for acc, vt, st in zip(acc_old, vts, sts)]
    m_new = [jnp.maximum(m, jnp.max(st, axis=0, keepdims=True)) for m, st in zip(m_old, sts)]
    ps = [jnp.exp2(st - m).astype(BF16) for st, m in zip(sts, m_new)]
    alphas = [jnp.exp2(mo - mn) for mo, mn in zip(m_old, m_new)]
    acc_new = [al * acc + _dot(vt, p) for al, acc, vt, p in zip(alphas, acc_old, vts, ps)]
    return m_new, acc_new


def _softmax_steps(sts, vts, m_refs, acc_refs, fixed=False):
    m_old = [None] * len(sts) if fixed else [r[...] for r in m_refs]
    m_new, acc_new = _softmax_update(sts, vts, m_old, [r[...] for r in acc_refs], fixed)
    for r, v in zip(() if fixed else m_refs, m_new):
        r[...] = v
    for r, v in zip(acc_refs, acc_new):
        r[...] = v


def _with_ones_row(vt):
    pad = jnp.where(_row((DV_PAD - HEAD_DIM, vt.shape[1])) == 0, 1.0, 0.0).astype(vt.dtype)
    return jnp.concatenate([vt, pad], axis=0)


def _bucket_bias(bidx, relb_ref, h, fill):
    acc = jnp.full(bidx.shape, fill, F32)
    for bk in range(N_BUCKETS):
        acc = jnp.where(bidx == bk, relb_ref[bk, h] * LOG2E, acc)
    return acc


def _group_sums(x, member):
    g = jnp.where(member, 1.0, 0.0).astype(BF16)
    hi = x.astype(BF16)
    lo = (x - hi.astype(F32)).astype(BF16)
    return _dot(hi, g) + _dot(lo, g)


def _dup_low_half(x):
    y = jnp.where(_lane(x.shape) < HEAD_DIM, x, 0.0)
    return y + pltpu.roll(y, HEAD_DIM, 1)


def _nsa_kernel(relb_ref, shift_ref, q_ref, g_ref, kcvc_ref, kcvct_ref, ksvs_ref, kwvw_ref, bidxct_ref, bidx2t_ref, ovt_ref,
                emt_ref, qnw_ref, knw_ref, o_ref,
                biasc_s, bias2_s, ks_s, vst_s, kw_s, vwt_s, kc_s, vct_s, qx_s, m_s, acc_s, *, tq, s_len, fixed):
    b = pl.program_id(0)
    shift = shift_ref[0] if fixed else 0.0
    i = pl.program_id(1)
    n_win = WINDOW // tq
    ng = s_len // CMP_STRIDE

    @pl.when((b == 0) & (i == 0))
    def _build_bias_tables():
        key = _row((tq, tq))
        qry = _lane((tq, tq))
        for h in range(N_HEADS):
            cols = slice(h * tq, (h + 1) * tq)
            far = jnp.full((tq, tq), relb_ref[N_BUCKETS - 1, h] * LOG2E - shift, F32)
            bias2_s[0, :, cols] = jnp.where(key <= qry, _bucket_bias(bidx2t_ref[0], relb_ref, h, NEG) - shift, NEG)
            bias2_s[1, :, cols] = _bucket_bias(bidx2t_ref[1], relb_ref, h, NEG) - shift
            bias2_s[2, :, cols] = far
            bias2_s[3, :, cols] = jnp.where(key > qry, far, NEG)
            bias2_s[4, :, cols] = jnp.full((tq, tq), NEG, F32)

        def body(t, carry):
            bi = bidxct_ref[:, _lane_tile(t, tq)]
            for h in range(N_HEADS):
                biasc_s[t, :, h * tq:(h + 1) * tq] = _bucket_bias(bi, relb_ref, h, NEG)
            return carry

        lax.fori_loop(0, s_len // tq, body, 0)

    @pl.when(i == 0)
    def _prep_kv():
        ch = 256

        def body(t, carry):
            rows = pl.ds(pl.multiple_of(t * ch, ch), ch)
            for src, kdst, vdst, widx in ((ksvs_ref, ks_s, vst_s, 0), (kwvw_ref, kw_s, vwt_s, 1)):
                x = src[rows, :]
                ss = _group_sums(x * x, _row((LANES, LANES)) < HEAD_DIM) * (1.0 / HEAD_DIM)
                kn = x * lax.rsqrt(ss + EPS) * knw_ref[widx:widx + 1, :]
                kdst[rows, 0:LANES] = _dup_low_half(kn).astype(BF16)
                vdst[:, _lane_tile(t, ch)] = _with_ones_row(x.T[HEAD_DIM:, :]).astype(BF16)
            ks_s[rows, LANES:] = emt_ref[rows, :]
            return carry

        lax.fori_loop(0, s_len // ch, body, 0)
        kc_s[...] = _dup_low_half(kcvc_ref[0]).astype(BF16)
        vct_s[...] = kcvct_ref[0][HEAD_DIM:, :].astype(BF16)

    q = q_ref[...]
    lane = _lane((tq, LANES))
    heads = range(N_HEADS)
    gw = 2 * LANES
    head_shift = HEAD_DIM.bit_length() - 1
    same_head = (lax.shift_right_logical(_row((gw, gw)), head_shift)
                 == lax.shift_right_logical(_lane((gw, gw)), head_shift))
    ss = _group_sums(q * q, same_head) * (1.0 / HEAD_DIM)
    qn = q * lax.rsqrt(ss + EPS) * qnw_ref[...] * (HEAD_DIM ** -0.5 * LOG2E)
    for h in heads:
        mine = (lane < HEAD_DIM) if h % 2 == 0 else (lane >= HEAD_DIM)
        qx_s[h, :, 0:LANES] = jnp.where(mine, qn[:, LANES * (h // 2):LANES * (h // 2 + 1)], 0.0).astype(BF16)

    lcs = [_nt(kc_s[...], qx_s[h, :, 0:LANES]) + biasc_s[i, :, h * tq:(h + 1) * tq] for h in heads]
    pcs = [jnp.where(lc > 0.5 * NEG, jnp.exp2(lc - jnp.max(lc, axis=0, keepdims=True)), 0.0) for lc in lcs]
    dens = [jnp.sum(pc, axis=0, keepdims=True) for pc in pcs]
    pcs = [pc / jnp.where(den > 0.0, den, 1.0) for pc, den in zip(pcs, dens)]
    o_cmp = [_dot(vct_s[...], pc.astype(BF16)) for pc in pcs]
    psum = (pcs[0] + pcs[1]) + (pcs[2] + pcs[3])

    n_slc = s_len // SLC_LEN
    n_sel = min(N_SEL, n_slc)
    blk = _row((n_slc, tq))
    tpos = i * tq + _lane((n_slc, tq))
    tblk = tpos // SLC_LEN
    valid = blk * SLC_LEN <= tpos

    p_hi = psum.astype(BF16)
    p_lo = (psum - p_hi.astype(F32)).astype(BF16)
    score = _dot(ovt_ref[...], p_hi) + _dot(ovt_ref[...], p_lo)
    forced = (blk == 0) | (blk == tblk) | (blk == tblk - 1)
    sc = jnp.where(forced, jnp.inf, jnp.where(valid, score[0:n_slc], -jnp.inf))
    rank = jnp.zeros((n_slc, tq), F32)
    for k in range(n_slc):
        ck = sc[k:k + 1, :]
        beats = (ck > sc) | ((ck == sc) & (blk > k))
        rank += jnp.where(beats, 1.0, 0.0)
    pen = jnp.where((rank < float(n_sel)) & valid, 0.0, NEG)
    pen = jnp.concatenate([pen, jnp.zeros((LANES - n_slc, tq), F32)], axis=0)
    pen_t = pen.T.astype(BF16)
    pen = pen.astype(BF16)
    for h in heads:
        qx_s[h, :, LANES:] = pen_t

    ms = [jnp.full((1, tq), NEG, F32)] * (2 * N_HEADS)
    accs = [jnp.zeros((DV_PAD, tq), F32)] * (2 * N_HEADS)
    for jj in range(n_win + 1):
        exists = i >= jj
        rows = pl.ds(pl.multiple_of(jnp.maximum(i - jj, 0) * tq, tq), tq)
        kind = jnp.where(exists, min(jj, 2), 4)
        kind_w = jnp.where(exists, 3 if jj == n_win else min(jj, 2), 4)
        ks = ks_s[rows, 0:LANES]
        kw = kw_s[rows, :]
        masked = _dot(ks_s[rows, LANES:], pen)
        sts = [_nt(ks, qx_s[h, :, 0:LANES]) + masked + bias2_s[kind, :, h * tq:(h + 1) * tq] for h in heads]
        sts += [_nt(kw, qx_s[h, :, 0:LANES]) + bias2_s[kind_w, :, h * tq:(h + 1) * tq] for h in heads]
        vts = [vst_s[:, rows]] * N_HEADS + [vwt_s[:, rows]] * N_HEADS
        ms, accs = _softmax_update(sts, vts, ms, accs, fixed)
    for h in heads:
        if not fixed:
            m_s[h] = ms[h]
        acc_s[h] = accs[h]
    o_win = [accs[N_HEADS + h][0:HEAD_DIM, :] / accs[N_HEADS + h][HEAD_DIM:HEAD_DIM + 1, :] for h in heads]

    def far_step(j, width):
        rows = pl.ds(j * tq if isinstance(j, int) else pl.multiple_of(j * tq, tq), width * tq)
        ks = ks_s[rows, :]
        sts = [_nt(ks, qx_s[h]) + (relb_ref[N_BUCKETS - 1, h] * LOG2E - shift) for h in heads]
        _softmax_steps(sts, [vst_s[:, rows]] * N_HEADS, [m_s.at[h] for h in heads], [acc_s.at[h] for h in heads],
                       fixed)

    n_far = jnp.maximum(i - n_win, 0)
    if fixed:
        def far_sweep(n):
            accs = [acc_s[h] for h in heads]
            for start in range(0, n, 2):
                rows = pl.ds(start * tq, min(2, n - start) * tq)
                ks = ks_s[rows, :]
                sts = [_nt(ks, qx_s[h]) + (relb_ref[N_BUCKETS - 1, h] * LOG2E - shift) for h in heads]
                _, accs = _softmax_update(sts, [vst_s[:, rows]] * N_HEADS, None, accs, True)
            for h in heads:
                acc_s[h] = accs[h]

        for n in range(1, s_len // tq - n_win):
            pl.when(n_far == n)(functools.partial(far_sweep, n))
    else:
        def far_body(p, carry):
            far_step(n_far - 2 * (p + 1), 2)
            return carry

        lax.fori_loop(0, n_far // 2, far_body, 0)
        pl.when(n_far % 2 == 1)(functools.partial(far_step, 0, 1))

    gt = jax.nn.sigmoid(g_ref[...]).T
    ys = []
    for h in heads:
        o_slc = acc_s[h, 0:HEAD_DIM, :] / acc_s[h, HEAD_DIM:HEAD_DIM + 1, :]
        ys.append(gt[3 * h:3 * h + 1, :] * o_cmp[h] + gt[3 * h + 1:3 * h + 2, :] * o_slc
                  + gt[3 * h + 2:3 * h + 3, :] * o_win[h])
    o_ref[...] = jnp.concatenate(ys, axis=0).T


def _nsa_attention(proj, kcvc, kcvct, rel_bias, shift, bidxct, bidx2t, ovt, emt, qnw, knw, b, s, tq, fixed):
    t = b * s
    nq = s // tq
    m_rows = N_HEADS * tq
    ng = s // CMP_STRIDE
    kern = functools.partial(_nsa_kernel, tq=tq, s_len=s, fixed=fixed)
    return pl.pallas_call(
        kern,
        grid=(b, nq),
        in_specs=[
            pl.BlockSpec(memory_space=pltpu.SMEM),
            pl.BlockSpec(memory_space=pltpu.SMEM),
            pl.BlockSpec((tq, 2 * LANES), lambda bi, i: (bi * nq + i, CB_NQ)),
            pl.BlockSpec((tq, LANES), lambda bi, i: (bi * nq + i, CB_GATE)),
            pl.BlockSpec((1, ng, LANES), lambda bi, i: (bi, 0, 0)),
            pl.BlockSpec((1, LANES, ng), lambda bi, i: (bi, 0, 0)),
            pl.BlockSpec((s, LANES), lambda bi, i: (bi, CB_KSVS)),
            pl.BlockSpec((s, LANES), lambda bi, i: (bi, CB_KWVW)),
            pl.BlockSpec((ng, s), lambda bi, i: (0, 0)),
            pl.BlockSpec((2, tq, tq), lambda bi, i: (0, 0, 0)),
            pl.BlockSpec((LANES, ng), lambda bi, i: (0, 0)),
            pl.BlockSpec((s, LANES), lambda bi, i: (0, 0)),
            pl.BlockSpec((1, 2 * LANES), lambda bi, i: (0, 0)),
            pl.BlockSpec((2, LANES), lambda bi, i: (0, 0)),
        ],
        out_specs=pl.BlockSpec((tq, 2 * LANES), lambda bi, i: (bi * nq + i, 0)),
        out_shape=jax.ShapeDtypeStruct((t, GROUP_WIDTH), F32),
        scratch_shapes=[
            pltpu.VMEM((nq, ng, m_rows), F32),
            pltpu.VMEM((5, tq, m_rows), F32),
            pltpu.VMEM((s, 2 * LANES), BF16),
            pltpu.VMEM((DV_PAD, s), BF16),
            pltpu.VMEM((s, LANES), BF16),
            pltpu.VMEM((DV_PAD, s), BF16),
            pltpu.VMEM((ng, LANES), BF16),
            pltpu.VMEM((HEAD_DIM, ng), BF16),
            pltpu.VMEM((N_HEADS, tq, 2 * LANES), BF16),
            pltpu.VMEM((N_HEADS, 1, tq), F32),
            pltpu.VMEM((N_HEADS, DV_PAD, tq), F32),
        ],
        compiler_params=_params(("arbitrary", "arbitrary")),
        name="nsa_attention",
    )(rel_bias, shift, proj, proj, kcvc, kcvct, proj, proj, bidxct, bidx2t, ovt, emt, qnw, knw)


def _mla_prep_kernel(shift_ref, cq_ref, ckv_ref, kr_ref, qaw_ref, kvw_ref, wq_ref, wqs_ref, wk_ref, wvt_ref, qnw_ref, knw_ref,
                     cq_t_ref, ck_t_ref, sk_t_ref, qo_ref, ko_ref, vto_ref):
    cq = cq_ref[...]
    ms = jnp.sum(cq * cq, axis=-1, keepdims=True) * (1.0 / Q_LORA)
    hq = (cq * lax.rsqrt(ms + EPS) * qaw_ref[...]).astype(BF16)
    qf = _dot(hq, wq_ref[...])
    qsw = _dot(hq, wqs_ref[...])
    ckv = ckv_ref[...]
    ms = jnp.mean(ckv * ckv, axis=-1, keepdims=True)
    hkv = (ckv * lax.rsqrt(ms + EPS) * kvw_ref[...]).astype(BF16)
    kf = _dot(hkv, wk_ref[...])
    vt = _nt(wvt_ref[...], hkv)
    vto_ref[...] = jnp.where(_row(vt.shape) % DV_PAD == HEAD_DIM, 1.0, vt).astype(BF16)
    krb = kr_ref[...]
    kr_rot = krb * ck_t_ref[...] + pltpu.roll(krb, HEAD_DIM, 1) * sk_t_ref[...]
    is_shift = _lane((cq.shape[0], LANES)) == QK_DIM
    for h in range(N_HEADS):
        cols = slice(LANES * h, LANES * (h + 1))
        x = qf[:, cols] * cq_t_ref[...] + qsw[:, cols] * sk_t_ref[...]
        ss = jnp.sum(x * x, axis=-1, keepdims=True) * (1.0 / QK_DIM)
        qn = x * lax.rsqrt(ss + EPS) * qnw_ref[...]
        qo_ref[:, cols] = jnp.where(is_shift, -shift_ref[0], qn).astype(BF16)
        k = kf[:, cols] + kr_rot
        ss = jnp.sum(k * k, axis=-1, keepdims=True) * (1.0 / QK_DIM)
        ko_ref[:, cols] = jnp.where(is_shift, 1.0, k * lax.rsqrt(ss + EPS) * knw_ref[...]).astype(BF16)


def _mla_prep(proj, shift, qaw, kvw, wq, wqs, wk, wvt, qnw, knw, tabs, s, tm):
    t = proj.shape[0]
    npos = s // tm
    row = lambda i: (i, 0)
    const = lambda i: (0, 0)
    tab = pl.BlockSpec((tm, LANES), lambda i: (i % npos, 0))
    out = jax.ShapeDtypeStruct((t, N_HEADS * LANES), BF16)
    return pl.pallas_call(
        _mla_prep_kernel,
        grid=(t // tm,),
        in_specs=[
            pl.BlockSpec(memory_space=pltpu.SMEM),
            pl.BlockSpec((tm, 2 * LANES), lambda i: (i, CB_CQ)),
            pl.BlockSpec((tm, LANES), lambda i: (i, CB_CKV)),
            pl.BlockSpec((tm, LANES), lambda i: (i, CB_KR)),
            pl.BlockSpec((1, 2 * LANES), const),
            pl.BlockSpec((1, LANES), const),
            pl.BlockSpec((2 * LANES, N_HEADS * LANES), const),
            pl.BlockSpec((2 * LANES, N_HEADS * LANES), const),
            pl.BlockSpec((LANES, N_HEADS * LANES), const),
            pl.BlockSpec((N_HEADS * DV_PAD, LANES), const),
            pl.BlockSpec((1, LANES), const),
            pl.BlockSpec((1, LANES), const),
            tab, tab, tab,
        ],
        out_specs=[pl.BlockSpec((tm, N_HEADS * LANES), row), pl.BlockSpec((tm, N_HEADS * LANES), row),
                   pl.BlockSpec((N_HEADS * DV_PAD, tm), lambda i: (0, i))],
        out_shape=[out, out, jax.ShapeDtypeStruct((N_HEADS * DV_PAD, t), BF16)],
        compiler_params=_params(("arbitrary",)),
        name="mla_prep",
    )(shift, proj, proj, proj, qaw, kvw, wq, wqs, wk, wvt, qnw, knw, *tabs)


def _mla_attn_kernel(q_ref, k_ref, vt_ref, o_ref, m_s, acc_s, *, tq, nq, fixed):
    i = pl.program_id(1)
    heads = range(N_HEADS)
    causal = _row((tq, tq)) <= _lane((tq, tq))

    def logits(rows, masked):
        sts = [_nt(k_ref[rows, LANES * h:LANES * (h + 1)], q_ref[:, LANES * h:LANES * (h + 1)]) for h in heads]
        return [jnp.where(causal, st, NEG) for st in sts] if masked else sts

    def values(rows):
        return [vt_ref[DV_PAD * h:DV_PAD * (h + 1), rows] for h in heads]

    if fixed:
        def sweep(n):
            accs = [jnp.zeros((DV_PAD, tq), F32)] * N_HEADS
            for start in range(0, n, 2):
                rows = pl.ds(start * tq, min(2, n - start) * tq)
                _, accs = _softmax_update(logits(rows, False), values(rows), None, accs, True)
            rows = pl.ds(n * tq, tq)
            _, accs = _softmax_update(logits(rows, True), values(rows), None, accs, True)
            for h in heads:
                acc_s[h] = accs[h]

        for n in range(nq):
            pl.when(i == n)(functools.partial(sweep, n))
    else:
        for h in heads:
            m_s[h] = jnp.full((1, tq), NEG, F32)
            acc_s[h] = jnp.zeros((DV_PAD, tq), F32)

        def step(j, width, masked):
            rows = pl.ds(j * tq if isinstance(j, int) else pl.multiple_of(j * tq, tq), width * tq)
            _softmax_steps(logits(rows, masked), values(rows), [m_s.at[h] for h in heads],
                           [acc_s.at[h] for h in heads])

        step(i, 1, True)

        def body(jj, carry):
            step(i - 2 * jj, 2, False)
            return carry

        lax.fori_loop(1, i // 2 + 1, body, 0)
        pl.when(i % 2 == 1)(functools.partial(step, 0, 1, False))
    yt = jnp.concatenate([acc_s[h, 0:HEAD_DIM, :] / acc_s[h, HEAD_DIM:HEAD_DIM + 1, :] for h in range(N_HEADS)],
                         axis=0)
    o_ref[...] = yt.T


def _mla_attention(qm, km, vmt, b, s, tq, fixed):
    t = b * s
    nq = s // tq
    w = N_HEADS * LANES
    return pl.pallas_call(
        functools.partial(_mla_attn_kernel, tq=tq, nq=nq, fixed=fixed),
        grid=(b, nq),
        in_specs=[
            pl.BlockSpec((tq, w), lambda bi, i: (bi * nq + i, 0)),
            pl.BlockSpec((s, w), lambda bi, i: (bi, 0)),
            pl.BlockSpec((N_HEADS * DV_PAD, s), lambda bi, i: (0, bi)),
        ],
        out_specs=pl.BlockSpec((tq, 2 * LANES), lambda bi, i: (bi * nq + i, 0)),
        out_shape=jax.ShapeDtypeStruct((t, GROUP_WIDTH), F32),
        scratch_shapes=[pltpu.VMEM((N_HEADS, 1, tq), F32), pltpu.VMEM((N_HEADS, DV_PAD, tq), F32)],
        compiler_params=_params(("arbitrary", "arbitrary")),
        name="mla_attention",
    )(qm, km, vmt)


def _sb_kernel(q_ref, k_ref, v_ref, o_ref, kb_s, vt_s, q_s, r_s, acc_s, kmax_s, *, tq, s_len):
    i = pl.program_id(1)
    gw = 2 * LANES
    head_shift = HEAD_DIM.bit_length() - 1
    same_head = (lax.shift_right_logical(_row((gw, gw)), head_shift)
                 == lax.shift_right_logical(_lane((gw, gw)), head_shift))

    @pl.when(i == 0)
    def _cast_kv():
        ch = 256

        def body(t, kmax):
            rows = pl.ds(pl.multiple_of(t * ch, ch), ch)
            k = k_ref[rows, :]
            kb_s[rows, :] = k.astype(BF16)
            vt_s[:, _lane_tile(t, ch)] = v_ref[rows, :].T.astype(BF16)
            return jnp.maximum(kmax, _group_sums(k * k, same_head))

        kmax = lax.fori_loop(0, s_len // ch, body, jnp.zeros((ch, gw), F32))
        kmax_s[0] = jnp.max(kmax)

    key = _row((tq, tq))
    qry = _lane((tq, tq))
    strict = key < qry
    tri = jnp.where(key <= qry, 1.0, 0.0).astype(BF16)
    tri2 = jnp.concatenate([tri, tri], axis=1)
    lane = _lane((tq, LANES))
    q = q_ref[...] * (HEAD_DIM ** -0.5 * LOG2E)
    for h in range(N_HEADS):
        mine = (lane < HEAD_DIM) if h % 2 == 0 else (lane >= HEAD_DIM)
        q_s[h] = jnp.where(mine, q[:, LANES * (h // 2):LANES * (h // 2 + 1)], 0.0).astype(BF16)
        r_s[h] = jnp.zeros((1, tq), F32)
        acc_s[h] = jnp.zeros((HEAD_DIM, tq), F32)
    qmax = jnp.max(_group_sums(q * q, same_head))
    z_bound = jnp.sqrt(jnp.full((1, tq), qmax * kmax_s[0], F32)) * 1.01 + 1.0

    heads = range(N_HEADS)

    def step(j, width, masked):
        rs = [r_s[h] for h in heads]
        accs = [acc_s[h] for h in heads]
        tiles = [j + width - 1 - w for w in range(width)]
        rows = [pl.ds(t * tq if isinstance(t, int) else pl.multiple_of(t * tq, tq), tq) for t in tiles]
        zs = [[_nt(kb_s[r, LANES * (h // 2):LANES * (h // 2 + 1)], q_s[h]) for h in heads] for r in rows]
        part = []
        for w, zt in enumerate(zs):
            negabs = [pltpu.bitcast(pltpu.bitcast(z, jnp.uint32) | jnp.uint32(0x80000000), F32) for z in zt]
            sps = [jnp.maximum(z, 0.0) + jnp.log2(1.0 + jnp.exp2(na)) for z, na in zip(zt, negabs)]
            if masked and w == 0:
                sps = [jnp.where(strict, sp, 0.0) for sp in sps]
            his = [sp.astype(BF16) for sp in sps]
            los = [(sp - hi.astype(F32)).astype(BF16) for sp, hi in zip(sps, his)]
            part.append([_dot(tri2, jnp.concatenate([hi, lo], axis=0)) for hi, lo in zip(his, los)])
        for w, (r, zt, pt) in enumerate(zip(rows, zs, part)):
            csums = [p + rc for p, rc in zip(pt, rs)]
            als = [jnp.exp2(z - cs) for z, cs in zip(zt, csums)]
            if masked and w == 0:
                als = [jnp.where(strict, a, 0.0) for a in als]
            accs = [acc + _dot(vt_s[HEAD_DIM * h:HEAD_DIM * (h + 1), r], als[h].astype(BF16))
                    for h, acc in zip(heads, accs)]
            rs = [cs[0:1, :] for cs in csums]
        for h in heads:
            acc_s[h] = accs[h]
            r_s[h] = rs[h]

    def live():
        r_min = jnp.minimum(jnp.minimum(r_s[0], r_s[1]), jnp.minimum(r_s[2], r_s[3]))
        return jnp.max(z_bound - r_min) >= -150.0

    step(i, 1, True)
    pl.when(i >= 1)(functools.partial(step, i - 1, 1, False))

    n_rest = jnp.maximum(i - 1, 0)
    n_pairs = n_rest // 2

    def cond(carry):
        p, alive = carry
        return (p < n_pairs) & alive

    def body(carry):
        p, _ = carry
        step(i - 1 - 2 * (p + 1), 2, False)
        return p + 1, live()

    _, alive = lax.while_loop(cond, body, (jnp.int32(0), live()))
    pl.when((n_rest % 2 == 1) & alive)(functools.partial(step, 0, 1, False))
    o_ref[...] = jnp.concatenate([acc_s[h] for h in range(N_HEADS)], axis=0).T


def _sb_attention(proj, b, s, tq):
    t = b * s
    nq = s // tq
    w = 2 * LANES
    return pl.pallas_call(
        functools.partial(_sb_kernel, tq=tq, s_len=s),
        grid=(b, nq),
        in_specs=[
            pl.BlockSpec((tq, w), lambda bi, i: (bi * nq + i, CB_SQ)),
            pl.BlockSpec((s, w), lambda bi, i: (bi, CB_SK)),
            pl.BlockSpec((s, w), lambda bi, i: (bi, CB_SV)),
        ],
        out_specs=pl.BlockSpec((tq, w), lambda bi, i: (bi * nq + i, 0)),
        out_shape=jax.ShapeDtypeStruct((t, GROUP_WIDTH), F32),
        scratch_shapes=[pltpu.VMEM((s, w), BF16), pltpu.VMEM((w, s), BF16),
                        pltpu.VMEM((N_HEADS, tq, LANES), BF16),
                        pltpu.VMEM((N_HEADS, 1, tq), F32), pltpu.VMEM((N_HEADS, HEAD_DIM, tq), F32),
                        pltpu.SMEM((1,), F32)],
        compiler_params=_params(("arbitrary", "arbitrary")),
        name="sb_attention",
    )(proj, proj, proj)


def _post_kernel(a_ref, ap_ref, yb_ref, yc_ref, yd_ref, x_ref, cw_ref, cb_ref, onw_ref, wo_ref, n2w_ref,
                 w1_ref, w2_ref, o_ref, *, tm, s_len, ffc):
    i = pl.program_id(0)
    a = a_ref[...]
    gw = GROUP_WIDTH
    v = a[:, gw:2 * gw] * a[:, 2 * gw:3 * gw]
    ap = ap_ref[...]
    first = (i * tm) % s_len == 0
    vp = jnp.where(first, 0.0, ap[:, gw:2 * gw] * ap[:, 2 * gw:3 * gw])
    row = _row(v.shape)
    v1 = jnp.where(row == 0, vp[7:8, :], pltpu.roll(v, 1, 0))
    v2 = jnp.where(row == 0, vp[6:7, :], jnp.where(row == 1, vp[7:8, :], pltpu.roll(v, 2, 0)))
    conv = cw_ref[0:1, :] * v2 + cw_ref[1:2, :] * v1 + cw_ref[2:3, :] * v
    ya = a[:, 0:gw] * (conv + cb_ref[...])

    mix = None
    for g, y in enumerate((ya, yb_ref[...], yc_ref[...], yd_ref[...])):
        ms = jnp.mean(y * y, axis=-1, keepdims=True)
        yn = (y * lax.rsqrt(ms + EPS) * onw_ref[:, gw * g:gw * (g + 1)]).astype(BF16)
        part = _dot(yn, wo_ref[gw * g:gw * (g + 1), :])
        mix = part if mix is None else mix + part
    x1 = x_ref[...] + mix

    ms = jnp.mean(x1 * x1, axis=-1, keepdims=True)
    h2 = (x1 * lax.rsqrt(ms + EPS) * n2w_ref[...]).astype(BF16)
    ff = None
    for cidx in range(D_FF // ffc):
        u = _dot(h2, w1_ref[:, ffc * cidx:ffc * (cidx + 1)])
        u = jnp.square(jnp.maximum(u, 0.0)).astype(BF16)
        part = _dot(u, w2_ref[ffc * cidx:ffc * (cidx + 1), :])
        ff = part if ff is None else ff + part
    o_ref[...] = x1 + ff


def _post(proj, yb, yc, yd, x2d, cw, cb, onw, wo, n2w, w1, w2, l, s, tm):
    t = x2d.shape[0]
    gw = GROUP_WIDTH
    row = lambda i: (i, 0)
    const = lambda i: (0, 0)
    layer = lambda i: (l, 0, 0)
    once = pl.Buffered(1)
    kern = functools.partial(_post_kernel, tm=tm, s_len=s, ffc=1024)
    return pl.pallas_call(
        kern,
        grid=(t // tm,),
        in_specs=[
            pl.BlockSpec((tm, 3 * gw), row),
            pl.BlockSpec((8, 3 * gw), lambda i: (jnp.maximum(i * (tm // 8) - 1, 0), 0)),
            pl.BlockSpec((tm, gw), row),
            pl.BlockSpec((tm, gw), row),
            pl.BlockSpec((tm, gw), row),
            pl.BlockSpec((tm, D_MODEL), row),
            pl.BlockSpec((3, gw), const),
            pl.BlockSpec((1, gw), const),
            pl.BlockSpec((1, D_MODEL), const),
            pl.BlockSpec((None, D_MODEL, D_MODEL), layer, pipeline_mode=once),
            pl.BlockSpec((1, D_MODEL), const),
            pl.BlockSpec((None, D_MODEL, D_FF), layer, pipeline_mode=once),
            pl.BlockSpec((None, D_FF, D_MODEL), layer, pipeline_mode=once),
        ],
        out_specs=pl.BlockSpec((tm, D_MODEL), row),
        out_shape=jax.ShapeDtypeStruct((t, D_MODEL), F32),
        compiler_params=_params(("arbitrary",)),
        name="post",
    )(proj, proj, yb, yc, yd, x2d, cw, cb, onw, wo, n2w, w1, w2)


def _t5_bucket(dist):
    max_exact = N_BUCKETS // 2
    d = np.maximum(dist, 0)
    large = max_exact + (np.log(np.maximum(d, 1) / max_exact) / math.log(MAX_DISTANCE / max_exact)
                         * (N_BUCKETS - max_exact)).astype(np.int32)
    return np.where(d < max_exact, d, np.minimum(large, N_BUCKETS - 1)).astype(np.int32)


def _tables(s, tq_nsa):
    n_cmp = (s - CMP_LEN) // CMP_STRIDE + 1
    ng = s // CMP_STRIDE
    n_slc = s // SLC_LEN
    tpos = np.arange(s)[None, :]
    n = np.arange(ng)[:, None]
    dist_c = tpos - (n * CMP_STRIDE + CMP_LEN - 1)
    bidxct = np.where((dist_c >= 0) & (n < n_cmp), _t5_bucket(dist_c), -1).astype(np.int32)
    key = np.arange(tq_nsa)[:, None]
    qry = np.arange(tq_nsa)[None, :]
    bidx2t = np.stack([_t5_bucket(qry - key), _t5_bucket(tq_nsa + qry - key)])
    starts = np.arange(n_cmp) * CMP_STRIDE
    ends = starts + CMP_LEN
    s0 = np.arange(n_slc) * SLC_LEN
    s1 = s0 + SLC_LEN
    ovl = np.clip(np.minimum(ends[:, None], s1[None]) - np.maximum(starts[:, None], s0[None]), 0, None) / CMP_LEN
    ovt = np.zeros((LANES, ng), np.float32)
    ovt[:n_slc, :n_cmp] = ovl.T
    emt = (np.arange(LANES)[None, :] == (np.arange(s) // SLC_LEN)[:, None]).astype(np.float32)
    inv = 1.0 / (ROPE_THETA ** (np.arange(0, ROPE_DIM, 2, dtype=np.float64) / ROPE_DIM))
    ang = (np.arange(s, dtype=np.float32)[:, None] * inv.astype(np.float32)[None, :]).astype(np.float64)
    cos, sin = np.cos(ang).astype(np.float32), np.sin(ang).astype(np.float32)
    z32 = np.zeros((s, 32), np.float32)
    z64 = np.zeros((s, 64), np.float32)
    one64 = np.ones((s, 64), np.float32)
    cq_t = np.concatenate([one64, cos, cos, z32], axis=1)
    ck_t = np.concatenate([z64, cos, cos, z32], axis=1)
    sk_t = np.concatenate([z64, -sin, sin, z32], axis=1)
    return dict(bidxct=jnp.asarray(bidxct), bidx2t=jnp.asarray(bidx2t), ovt=jnp.asarray(ovt, BF16),
                emt=jnp.asarray(emt, BF16), rope=(jnp.asarray(cq_t), jnp.asarray(ck_t), jnp.asarray(sk_t)))


def _shift_bound(d, q_gain, k_gain, extra=0.0):
    return (d * jnp.max(jnp.abs(q_gain)) * jnp.max(jnp.abs(k_gain)) * 1.01 + extra + 0.1).reshape(1)


MAX_SHIFT = 60.0


def _pad_cols(w, width):
    return jnp.pad(w, ((0, 0), (0, width - w.shape[1])))


def _layer_weights(l, conv_w, conv_b, nsa_q_norm, nsa_k_norm, cmp_pos, cmp_w1, cmp_w2, mla_q_a_norm,
                   mla_kv_norm, mla_wq_b, mla_wkv_b, mla_q_norm, mla_k_norm, out_norm_w, norm2_w):
    w1 = cmp_w1[l].reshape(2, CMP_LEN, HEAD_DIM, CMP_HIDDEN)
    zw = jnp.zeros((CMP_LEN, HEAD_DIM, CMP_HIDDEN), F32)
    cw1 = jnp.concatenate([jnp.concatenate([w1[0], zw], axis=2), jnp.concatenate([zw, w1[1]], axis=2)],
                          axis=1).astype(BF16)
    cw1 = cw1.reshape(2, CMP_STRIDE * LANES, 2 * CMP_HIDDEN)
    zc = jnp.zeros((CMP_HIDDEN, HEAD_DIM), F32)
    cw2 = jnp.concatenate([jnp.concatenate([cmp_w2[l, 0], zc], axis=1),
                           jnp.concatenate([zc, cmp_w2[l, 1]], axis=1)], axis=0).astype(BF16)
    cpos = jnp.concatenate([cmp_pos[l, 0], cmp_pos[l, 1]], axis=1).reshape(2, CMP_STRIDE * LANES)
    kn = nsa_k_norm[l]
    ones64 = jnp.ones((HEAD_DIM,), F32)
    knw_c = jnp.concatenate([kn[0], ones64])[None, :]
    knw_sw = jnp.stack([jnp.concatenate([kn[1], ones64]), jnp.concatenate([kn[2], ones64])])
    qnw = jnp.tile(nsa_q_norm[l], N_HEADS)[None, :]
    wq = mla_wq_b[l].reshape(Q_LORA, N_HEADS, QK_DIM)
    half = ROPE_DIM // 2
    wqs = jnp.concatenate([jnp.zeros((Q_LORA, N_HEADS, HEAD_DIM), F32), wq[:, :, HEAD_DIM + half:],
                           wq[:, :, HEAD_DIM:HEAD_DIM + half]], axis=2)
    pad_q = lambda w: jnp.pad(w, ((0, 2 * LANES - Q_LORA), (0, 0), (0, LANES - QK_DIM))).reshape(
        2 * LANES, N_HEADS * LANES).astype(BF16)
    wkv = mla_wkv_b[l].reshape(KV_LORA, N_HEADS, 2 * HEAD_DIM)
    wk = jnp.pad(wkv[:, :, :HEAD_DIM], ((0, 0), (0, 0), (0, LANES - HEAD_DIM))).reshape(KV_LORA, N_HEADS * LANES)
    wvt = jnp.pad(wkv[:, :, HEAD_DIM:], ((0, 0), (0, 0), (0, DV_PAD - HEAD_DIM))).reshape(KV_LORA, -1).T
    return dict(
        cw1=cw1, cw2=cw2, cw2t=cw2.T, cpos=cpos, knw_c=knw_c, knw_sw=knw_sw, qnw=qnw,
        qaw=_pad_cols(mla_q_a_norm[l][None, :], 2 * LANES), kvw=mla_kv_norm[l][None, :],
        wq=pad_q(wq), wqs=pad_q(wqs), wk=wk.astype(BF16), wvt=wvt.astype(BF16),
        mqn=_pad_cols(mla_q_norm[l][None, :] * (QK_DIM ** -0.5 * LOG2E), LANES),
        mkn=_pad_cols(mla_k_norm[l][None, :], LANES),
        cw=conv_w[l], cb=conv_b[l][None, :], onw=out_norm_w[l][None, :], n2w=norm2_w[l][None, :])


TM_PROJ = 512
TM_PREP = 512
TM_POST = 512
TQ_NSA = 256
TQ_MLA = 256
TQ_SB = 256


def kernel(x, rel_bias, norm1_w, w_in, conv_w, conv_b, nsa_q_norm, nsa_k_norm, cmp_pos, cmp_w1, cmp_w2,
           mla_q_a_norm, mla_kv_norm, mla_wq_b, mla_wkv_b, mla_q_norm, mla_k_norm, out_norm_w, w_out, norm2_w,
           ffn_w1, ffn_w2):
    b, s, d = x.shape
    depth = w_in.shape[0]
    tabs = _tables(s, TQ_NSA)
    x2d = x.reshape(b * s, d)
    wo_b, w1_b, w2_b = w_out.astype(BF16), ffn_w1.astype(BF16), ffn_w2.astype(BF16)
    for l in range(depth):
        w = _layer_weights(l, conv_w, conv_b, nsa_q_norm, nsa_k_norm, cmp_pos, cmp_w1, cmp_w2,
                           mla_q_a_norm, mla_kv_norm, mla_wq_b, mla_wkv_b, mla_q_norm, mla_k_norm, out_norm_w,
                           norm2_w)
        proj = _inproj(x2d, norm1_w[l][None, :], w_in, l, TM_PROJ)
        kcvc, kcvct = _compress(proj, w["cpos"], w["cw1"], w["cw2"], w["cw2t"], w["knw_c"], b, s)
        nsa_shift = _shift_bound(HEAD_DIM, w["qnw"] * (HEAD_DIM ** -0.5 * LOG2E), w["knw_sw"][:, :HEAD_DIM],
                                 jnp.max(jnp.abs(rel_bias)) * LOG2E)
        nsa_args = (proj, kcvc, kcvct, rel_bias, nsa_shift, tabs["bidxct"], tabs["bidx2t"], tabs["ovt"], tabs["emt"],
                    w["qnw"], w["knw_sw"])
        yb = lax.cond(nsa_shift[0] <= MAX_SHIFT,
                      lambda *a: _nsa_attention(*a, b, s, TQ_NSA, True),
                      lambda *a: _nsa_attention(*a, b, s, TQ_NSA, False), *nsa_args)
        mla_shift = _shift_bound(QK_DIM, w["mqn"], w["mkn"])
        qm, km, vmt = _mla_prep(proj, mla_shift, w["qaw"], w["kvw"], w["wq"], w["wqs"], w["wk"], w["wvt"], w["mqn"],
                                w["mkn"], tabs["rope"], s, TM_PREP)
        yc = lax.cond(mla_shift[0] <= MAX_SHIFT,
                      lambda *a: _mla_attention(*a, b, s, TQ_MLA, True),
                      lambda *a: _mla_attention(*a, b, s, TQ_MLA, False), qm, km, vmt)
        yd = _sb_attention(proj, b, s, TQ_SB)
        x2d = _post(proj, yb, yc, yd, x2d, w["cw"], w["cb"], w["onw"], wo_b, w["n2w"], w1_b, w2_b, l, s, TM_POST)
    return x2d.reshape(b, s, d)
```

```python
import functools
import math

import jax
import jax.numpy as jnp
import numpy as np
from jax import lax
from jax.experimental import pallas as pl
from jax.experimental.pallas import tpu as pltpu

F32 = jnp.float32
BF16 = jnp.bfloat16

D_MODEL = 1024
GROUP_WIDTH = 256
HEAD_DIM = 64
N_HEADS = 4
LANES = 128
CMP_LEN = 32
CMP_STRIDE = 16
SLC_LEN = 64
N_SEL = 16
WINDOW = 512
CMP_HIDDEN = 256
Q_LORA = 192
KV_LORA = 128
ROPE_DIM = 32
QK_DIM = 96
ROPE_THETA = 10000.0
N_BUCKETS = 32
MAX_DISTANCE = 128
D_FF = 4096
EPS = 1e-6
NEG = -1e30
LOG2E = math.log2(math.e)
DV_PAD = 80

NP = 2816
CB_NQ = 3
CB_KCVC = 8
CB_KSVS = 9
CB_KWVW = 10
CB_GATE = 11
CB_CQ = 6
CB_CKV = 14
CB_KR = 15
CB_SQ = 8
CB_SK = 9
CB_SV = 10

VMEM_LIMIT = 56 * 1024 * 1024

NT_DIMS = (((1,), (1,)), ((), ()))


def _params(sem):
    return pltpu.CompilerParams(dimension_semantics=sem, vmem_limit_bytes=VMEM_LIMIT)


def _nt(a, b):
    return lax.dot_general(a, b, NT_DIMS, preferred_element_type=F32)


def _dot(a, b):
    return jnp.dot(a, b, preferred_element_type=F32)


def _lane(shape):
    return lax.broadcasted_iota(jnp.int32, shape, len(shape) - 1)


def _row(shape):
    return lax.broadcasted_iota(jnp.int32, shape, len(shape) - 2)


def _lane_tile(j, width):
    return pl.ds(pl.multiple_of(j * width, width), width)


_IN_SEGMENTS = (
    ((0, 1408), 0),
    ((1408, 1420), 1408),
    ((1420, 1612), 1536),
    ((1612, 1740), 1792),
    ((1756, 1772), 1920),
    ((1740, 1756), 1936),
    ((1740, 1772), 1984),
    ((1772, 2540), 2048),
)
IN_COLS = 2540


def _inproj_kernel(x_ref, nw_ref, w_ref, o_ref, w_s):
    @pl.when(pl.program_id(0) == 0)
    def _relayout_weights():
        end = 0
        for (a, b), dst in _IN_SEGMENTS:
            if dst > end:
                w_s[:, end:dst] = jnp.zeros((D_MODEL, dst - end), BF16)
            w_s[:, dst:dst + b - a] = w_ref[:, a:b].astype(BF16)
            end = dst + b - a
        assert end == NP

    x = x_ref[...]
    ms = jnp.mean(x * x, axis=-1, keepdims=True)
    h = (x * lax.rsqrt(ms + EPS) * nw_ref[...]).astype(BF16)
    o_ref[...] = _dot(h, w_s[...])


def _inproj(x2d, nw, w_in, l, tm):
    t = x2d.shape[0]
    return pl.pallas_call(
        _inproj_kernel,
        grid=(t // tm,),
        in_specs=[
            pl.BlockSpec((tm, D_MODEL), lambda i: (i, 0)),
            pl.BlockSpec((1, D_MODEL), lambda i: (0, 0)),
            pl.BlockSpec((None, D_MODEL, IN_COLS), lambda i: (l, 0, 0), pipeline_mode=pl.Buffered(1)),
        ],
        out_specs=pl.BlockSpec((tm, NP), lambda i: (i, 0)),
        out_shape=jax.ShapeDtypeStruct((t, NP), F32),
        scratch_shapes=[pltpu.VMEM((D_MODEL, NP), BF16)],
        compiler_params=_params(("arbitrary",)),
        name="inproj",
    )(x2d, nw, w_in)


def _compress_kernel(x_ref, pos_ref, w1_ref, w2_ref, w2t_ref, knw_ref, o_ref, ot_ref):
    ng = x_ref.shape[1]
    x = x_ref[0]
    acc_a = _dot((x + pos_ref[0:1, :]).astype(BF16), w1_ref[0])
    acc_b = _dot((x + pos_ref[1:2, :]).astype(BF16), w1_ref[1])
    pre = acc_a + pltpu.roll(acc_b, ng - 1, 0)
    hdn = (pre * jax.nn.sigmoid(pre)).astype(BF16)
    out = _dot(hdn, w2_ref[...])
    lane = _lane(out.shape)
    is_k = lane < HEAD_DIM
    ss = jnp.sum(jnp.where(is_k, out * out, 0.0), axis=-1, keepdims=True) * (1.0 / HEAD_DIM)
    o_ref[0] = jnp.where(is_k, out * lax.rsqrt(ss + EPS) * knw_ref[...], out)
    ot_ref[0] = _nt(w2t_ref[...], hdn)


def _compress(proj, pos, w1, w2, w2t, knw, b, s):
    ng = s // CMP_STRIDE
    gl = CMP_STRIDE * LANES
    xg = proj[:, CB_KCVC * LANES:(CB_KCVC + 1) * LANES].reshape(b, ng, gl)
    return pl.pallas_call(
        _compress_kernel,
        grid=(b,),
        in_specs=[
            pl.BlockSpec((1, ng, gl), lambda i: (i, 0, 0)),
            pl.BlockSpec((2, gl), lambda i: (0, 0)),
            pl.BlockSpec((2, gl, 2 * CMP_HIDDEN), lambda i: (0, 0, 0)),
            pl.BlockSpec((2 * CMP_HIDDEN, LANES), lambda i: (0, 0)),
            pl.BlockSpec((LANES, 2 * CMP_HIDDEN), lambda i: (0, 0)),
            pl.BlockSpec((1, LANES), lambda i: (0, 0)),
        ],
        out_specs=[pl.BlockSpec((1, ng, LANES), lambda i: (i, 0, 0)),
                   pl.BlockSpec((1, LANES, ng), lambda i: (i, 0, 0))],
        out_shape=[jax.ShapeDtypeStruct((b, ng, LANES), F32), jax.ShapeDtypeStruct((b, LANES, ng), F32)],
        compiler_params=_params(("arbitrary",)),
        name="nsa_compress",
    )(xg, pos, w1, w2, w2t, knw)


def _softmax_update(sts, vts, m_old, acc_old, fixed=False):
    if fixed:
        return m_old, [acc + _dot(vt, jnp.exp2(st).astype(BF16)) for acc, vt, st in zip(acc_old, vts, sts)]
    m_new = [jnp.maximum(m, jnp.max(st, axis=0, keepdims=True)) for m, st in zip(m_old, sts)]
    ps = [jnp.exp2(st - m).astype(BF16) for st, m in zip(sts, m_new)]
    alphas = [jnp.exp2(mo - mn) for mo, mn in zip(m_old, m_new)]
    acc_new = [al * acc + _dot(vt, p) for al, acc, vt, p in zip(alphas, acc_old, vts, ps)]
    return m_new, acc_new


def _softmax_steps(sts, vts, m_refs, acc_refs, fixed=False):
    m_old = [None] * len(sts) if fixed else [r[...] for r in m_refs]
    m_new, acc_new = _softmax_update(sts, vts, m_old, [r[...] for r in acc_refs], fixed)
    for r, v in zip(() if fixed else m_refs, m_new):
        r[...] = v
    for r, v in zip(acc_refs, acc_new):
        r[...] = v


def _with_ones_row(vt):
    pad = jnp.where(_row((DV_PAD - HEAD_DIM, vt.shape[1])) == 0, 1.0, 0.0).astype(vt.dtype)
    return jnp.concatenate([vt, pad], axis=0)


def _bucket_bias(bidx, relb_ref, h, fill):
    acc = jnp.full(bidx.shape, fill, F32)
    for bk in range(N_BUCKETS):
        acc = jnp.where(bidx == bk, relb_ref[bk, h] * LOG2E, acc)
    return acc


def _group_sums(x, member):
    g = jnp.where(member, 1.0, 0.0).astype(BF16)
    hi = x.astype(BF16)
    lo = (x - hi.astype(F32)).astype(BF16)
    return _dot(hi, g) + _dot(lo, g)


def _dup_low_half(x):
    y = jnp.where(_lane(x.shape) < HEAD_DIM, x, 0.0)
    return y + pltpu.roll(y, HEAD_DIM, 1)


def _nsa_kernel(relb_ref, shift_ref, q_ref, g_ref, kcvc_ref, kcvct_ref, ksvs_ref, kwvw_ref, bidxct_ref, bidx2t_ref, ovt_ref,
                emt_ref, qnw_ref, knw_ref, o_ref,
                biasc_s, bias2_s, ks_s, vst_s, kw_s, vwt_s, kc_s, vct_s, qx_s, m_s, acc_s, *, tq, s_len, fixed):
    b = pl.program_id(0)
    shift = shift_ref[0] if fixed else 0.0
    i = pl.program_id(1)
    n_win = WINDOW // tq
    ng = s_len // CMP_STRIDE

    @pl.when((b == 0) & (i == 0))
    def _build_bias_tables():
        key = _row((tq, tq))
        qry = _lane((tq, tq))
        for h in range(N_HEADS):
            cols = slice(h * tq, (h + 1) * tq)
            far = jnp.full((tq, tq), relb_ref[N_BUCKETS - 1, h] * LOG2E - shift, F32)
            bias2_s[0, :, cols] = jnp.where(key <= qry, _bucket_bias(bidx2t_ref[0], relb_ref, h, NEG) - shift, NEG)
            bias2_s[1, :, cols] = _bucket_bias(bidx2t_ref[1], relb_ref, h, NEG) - shift
            bias2_s[2, :, cols] = far
            bias2_s[3, :, cols] = jnp.where(key > qry, far, NEG)
            bias2_s[4, :, cols] = jnp.full((tq, tq), NEG, F32)

        def body(t, carry):
            bi = bidxct_ref[:, _lane_tile(t, tq)]
            for h in range(N_HEADS):
                biasc_s[t, :, h * tq:(h + 1) * tq] = _bucket_bias(bi, relb_ref, h, NEG)
            return carry

        lax.fori_loop(0, s_len // tq, body, 0)

    @pl.when(i == 0)
    def _prep_kv():
        ch = 256

        def body(t, carry):
            rows = pl.ds(pl.multiple_of(t * ch, ch), ch)
            for src, kdst, vdst, widx in ((ksvs_ref, ks_s, vst_s, 0), (kwvw_ref, kw_s, vwt_s, 1)):
                x = src[rows, :]
                ss = _group_sums(x * x, _row((LANES, LANES)) < HEAD_DIM) * (1.0 / HEAD_DIM)
                kn = x * lax.rsqrt(ss + EPS) * knw_ref[widx:widx + 1, :]
                kdst[rows, 0:LANES] = _dup_low_half(kn).astype(BF16)
                vdst[:, _lane_tile(t, ch)] = _with_ones_row(x.T[HEAD_DIM:, :]).astype(BF16)
            ks_s[rows, LANES:] = emt_ref[rows, :]
            return carry

        lax.fori_loop(0, s_len // ch, body, 0)
        kc_s[...] = _dup_low_half(kcvc_ref[0]).astype(BF16)
        vct_s[...] = kcvct_ref[0][HEAD_DIM:, :].astype(BF16)

    q = q_ref[...]
    lane = _lane((tq, LANES))
    heads = range(N_HEADS)
    gw = 2 * LANES
    head_shift = HEAD_DIM.bit_length() - 1
    same_head = (lax.shift_right_logical(_row((gw, gw)), head_shift)
                 == lax.shift_right_logical(_lane((gw, gw)), head_shift))
    ss = _group_sums(q * q, same_head) * (1.0 / HEAD_DIM)
    qn = q * lax.rsqrt(ss + EPS) * qnw_ref[...] * (HEAD_DIM ** -0.5 * LOG2E)
    for h in heads:
        mine = (lane < HEAD_DIM) if h % 2 == 0 else (lane >= HEAD_DIM)
        qx_s[h, :, 0:LANES] = jnp.where(mine, qn[:, LANES * (h // 2):LANES * (h // 2 + 1)], 0.0).astype(BF16)

    lcs = [_nt(kc_s[...], qx_s[h, :, 0:LANES]) + biasc_s[i, :, h * tq:(h + 1) * tq] for h in heads]
    pcs = [jnp.where(lc > 0.5 * NEG, jnp.exp2(lc - jnp.max(lc, axis=0, keepdims=True)), 0.0) for lc in lcs]
    dens = [jnp.sum(pc, axis=0, keepdims=True) for pc in pcs]
    pcs = [pc / jnp.where(den > 0.0, den, 1.0) for pc, den in zip(pcs, dens)]
    o_cmp = [_dot(vct_s[...], pc.astype(BF16)) for pc in pcs]
    psum = (pcs[0] + pcs[1]) + (pcs[2] + pcs[3])

    n_slc = s_len // SLC_LEN
    n_sel = min(N_SEL, n_slc)
    blk = _row((n_slc, tq))
    tpos = i * tq + _lane((n_slc, tq))
    tblk = tpos // SLC_LEN
    valid = blk * SLC_LEN <= tpos

    p_hi = psum.astype(BF16)
    p_lo = (psum - p_hi.astype(F32)).astype(BF16)
    score = _dot(ovt_ref[...], p_hi) + _dot(ovt_ref[...], p_lo)
    forced = (blk == 0) | (blk == tblk) | (blk == tblk - 1)
    sc = jnp.where(forced, jnp.inf, jnp.where(valid, score[0:n_slc], -jnp.inf))
    rank = jnp.zeros((n_slc, tq), F32)
    for k in range(n_slc):
        ck = sc[k:k + 1, :]
        beats = (ck > sc) | ((ck == sc) & (blk > k))
        rank += jnp.where(beats, 1.0, 0.0)
    pen = jnp.where((rank < float(n_sel)) & valid, 0.0, NEG)
    pen = jnp.concatenate([pen, jnp.zeros((LANES - n_slc, tq), F32)], axis=0)
    pen_t = pen.T.astype(BF16)
    pen = pen.astype(BF16)
    for h in heads:
        qx_s[h, :, LANES:] = pen_t

    ms = [jnp.full((1, tq), NEG, F32)] * (2 * N_HEADS)
    accs = [jnp.zeros((DV_PAD, tq), F32)] * (2 * N_HEADS)
    for jj in range(n_win + 1):
        exists = i >= jj
        rows = pl.ds(pl.multiple_of(jnp.maximum(i - jj, 0) * tq, tq), tq)
        kind = jnp.where(exists, min(jj, 2), 4)
        kind_w = jnp.where(exists, 3 if jj == n_win else min(jj, 2), 4)
        ks = ks_s[rows, 0:LANES]
        kw = kw_s[rows, :]
        masked = _dot(ks_s[rows, LANES:], pen)
        sts = [_nt(ks, qx_s[h, :, 0:LANES]) + masked + bias2_s[kind, :, h * tq:(h + 1) * tq] for h in heads]
        sts += [_nt(kw, qx_s[h, :, 0:LANES]) + bias2_s[kind_w, :, h * tq:(h + 1) * tq] for h in heads]
        vts = [vst_s[:, rows]] * N_HEADS + [vwt_s[:, rows]] * N_HEADS
        ms, accs = _softmax_update(sts, vts, ms, accs, fixed)
    for h in heads:
        if not fixed:
            m_s[h] = ms[h]
        acc_s[h] = accs[h]
    o_win = [accs[N_HEADS + h][0:HEAD_DIM, :] / accs[N_HEADS + h][HEAD_DIM:HEAD_DIM + 1, :] for h in heads]

    def far_step(j, width):
        rows = pl.ds(j * tq if isinstance(j, int) else pl.multiple_of(j * tq, tq), width * tq)
        ks = ks_s[rows, :]
        sts = [_nt(ks, qx_s[h]) + (relb_ref[N_BUCKETS - 1, h] * LOG2E - shift) for h in heads]
        _softmax_steps(sts, [vst_s[:, rows]] * N_HEADS, [m_s.at[h] for h in heads], [acc_s.at[h] for h in heads],
                       fixed)

    n_far = jnp.maximum(i - n_win, 0)
    if fixed:
        def far_sweep(n):
            accs = [acc_s[h] for h in heads]
            for start in range(0, n, 2):
                rows = pl.ds(start * tq, min(2, n - start) * tq)
                ks = ks_s[rows, :]
                sts = [_nt(ks, qx_s[h]) + (relb_ref[N_BUCKETS - 1, h] * LOG2E - shift) for h in heads]
                _, accs = _softmax_update(sts, [vst_s[:, rows]] * N_HEADS, None, accs, True)
            for h in heads:
                acc_s[h] = accs[h]

        for n in range(1, s_len // tq - n_win):
            pl.when(n_far == n)(functools.partial(far_sweep, n))
    else:
        def far_body(p, carry):
            far_step(n_far - 2 * (p + 1), 2)
            return carry

        lax.fori_loop(0, n_far // 2, far_body, 0)
        pl.when(n_far % 2 == 1)(functools.partial(far_step, 0, 1))

    gt = jax.nn.sigmoid(g_ref[...]).T
    ys = []
    for h in heads:
        o_slc = acc_s[h, 0:HEAD_DIM, :] / acc_s[h, HEAD_DIM:HEAD_DIM + 1, :]
        ys.append(gt[3 * h:3 * h + 1, :] * o_cmp[h] + gt[3 * h + 1:3 * h + 2, :] * o_slc
                  + gt[3 * h + 2:3 * h + 3, :] * o_win[h])
    o_ref[...] = jnp.concatenate(ys, axis=0).T


def _nsa_attention(proj, kcvc, kcvct, rel_bias, shift, bidxct, bidx2t, ovt, emt, qnw, knw, b, s, tq, fixed):
    t = b * s
    nq = s // tq
    m_rows = N_HEADS * tq
    ng = s // CMP_STRIDE
    kern = functools.partial(_nsa_kernel, tq=tq, s_len=s, fixed=fixed)
    return pl.pallas_call(
        kern,
        grid=(b, nq),
        in_specs=[
            pl.BlockSpec(memory_space=pltpu.SMEM),
            pl.BlockSpec(memory_space=pltpu.SMEM),
            pl.BlockSpec((tq, 2 * LANES), lambda bi, i: (bi * nq + i, CB_NQ)),
            pl.BlockSpec((tq, LANES), lambda bi, i: (bi * nq + i, CB_GATE)),
            pl.BlockSpec((1, ng, LANES), lambda bi, i: (bi, 0, 0)),
            pl.BlockSpec((1, LANES, ng), lambda bi, i: (bi, 0, 0)),
            pl.BlockSpec((s, LANES), lambda bi, i: (bi, CB_KSVS)),
            pl.BlockSpec((s, LANES), lambda bi, i: (bi, CB_KWVW)),
            pl.BlockSpec((ng, s), lambda bi, i: (0, 0)),
            pl.BlockSpec((2, tq, tq), lambda bi, i: (0, 0, 0)),
            pl.BlockSpec((LANES, ng), lambda bi, i: (0, 0)),
            pl.BlockSpec((s, LANES), lambda bi, i: (0, 0)),
            pl.BlockSpec((1, 2 * LANES), lambda bi, i: (0, 0)),
            pl.BlockSpec((2, LANES), lambda bi, i: (0, 0)),
        ],
        out_specs=pl.BlockSpec((tq, 2 * LANES), lambda bi, i: (bi * nq + i, 0)),
        out_shape=jax.ShapeDtypeStruct((t, GROUP_WIDTH), F32),
        scratch_shapes=[
            pltpu.VMEM((nq, ng, m_rows), F32),
            pltpu.VMEM((5, tq, m_rows), F32),
            pltpu.VMEM((s, 2 * LANES), BF16),
            pltpu.VMEM((DV_PAD, s), BF16),
            pltpu.VMEM((s, LANES), BF16),
            pltpu.VMEM((DV_PAD, s), BF16),
            pltpu.VMEM((ng, LANES), BF16),
            pltpu.VMEM((HEAD_DIM, ng), BF16),
            pltpu.VMEM((N_HEADS, tq, 2 * LANES), BF16),
            pltpu.VMEM((N_HEADS, 1, tq), F32),
            pltpu.VMEM((N_HEADS, DV_PAD, tq), F32),
        ],
        compiler_params=_params(("arbitrary", "arbitrary")),
        name="nsa_attention",
    )(rel_bias, shift, proj, proj, kcvc, kcvct, proj, proj, bidxct, bidx2t, ovt, emt, qnw, knw)


def _mla_prep_kernel(shift_ref, cq_ref, ckv_ref, kr_ref, qaw_ref, kvw_ref, wq_ref, wqs_ref, wk_ref, wvt_ref, qnw_ref, knw_ref,
                     cq_t_ref, ck_t_ref, sk_t_ref, qo_ref, ko_ref, vto_ref):
    cq = cq_ref[...]
    ms = jnp.sum(cq * cq, axis=-1, keepdims=True) * (1.0 / Q_LORA)
    hq = (cq * lax.rsqrt(ms + EPS) * qaw_ref[...]).astype(BF16)
    qf = _dot(hq, wq_ref[...])
    qsw = _dot(hq, wqs_ref[...])
    ckv = ckv_ref[...]
    ms = jnp.mean(ckv * ckv, axis=-1, keepdims=True)
    hkv = (ckv * lax.rsqrt(ms + EPS) * kvw_ref[...]).astype(BF16)
    kf = _dot(hkv, wk_ref[...])
    vt = _nt(wvt_ref[...], hkv)
    vto_ref[...] = jnp.where(_row(vt.shape) % DV_PAD == HEAD_DIM, 1.0, vt).astype(BF16)
    krb = kr_ref[...]
    kr_rot = krb * ck_t_ref[...] + pltpu.roll(krb, HEAD_DIM, 1) * sk_t_ref[...]
    is_shift = _lane((cq.shape[0], LANES)) == QK_DIM
    for h in range(N_HEADS):
        cols = slice(LANES * h, LANES * (h + 1))
        x = qf[:, cols] * cq_t_ref[...] + qsw[:, cols] * sk_t_ref[...]
        ss = jnp.sum(x * x, axis=-1, keepdims=True) * (1.0 / QK_DIM)
        qn = x * lax.rsqrt(ss + EPS) * qnw_ref[...]
        qo_ref[:, cols] = jnp.where(is_shift, -shift_ref[0], qn).astype(BF16)
        k = kf[:, cols] + kr_rot
        ss = jnp.sum(k * k, axis=-1, keepdims=True) * (1.0 / QK_DIM)
        ko_ref[:, cols] = jnp.where(is_shift, 1.0, k * lax.rsqrt(ss + EPS) * knw_ref[...]).astype(BF16)


def _mla_prep(proj, shift, qaw, kvw, wq, wqs, wk, wvt, qnw, knw, tabs, s, tm):
    t = proj.shape[0]
    npos = s // tm
    row = lambda i: (i, 0)
    const = lambda i: (0, 0)
    tab = pl.BlockSpec((tm, LANES), lambda i: (i % npos, 0))
    out = jax.ShapeDtypeStruct((t, N_HEADS * LANES), BF16)
    return pl.pallas_call(
        _mla_prep_kernel,
        grid=(t // tm,),
        in_specs=[
            pl.BlockSpec(memory_space=pltpu.SMEM),
            pl.BlockSpec((tm, 2 * LANES), lambda i: (i, CB_CQ)),
            pl.BlockSpec((tm, LANES), lambda i: (i, CB_CKV)),
            pl.BlockSpec((tm, LANES), lambda i: (i, CB_KR)),
            pl.BlockSpec((1, 2 * LANES), const),
            pl.BlockSpec((1, LANES), const),
            pl.BlockSpec((2 * LANES, N_HEADS * LANES), const),
            pl.BlockSpec((2 * LANES, N_HEADS * LANES), const),
            pl.BlockSpec((LANES, N_HEADS * LANES), const),
            pl.BlockSpec((N_HEADS * DV_PAD, LANES), const),
            pl.BlockSpec((1, LANES), const),
            pl.BlockSpec((1, LANES), const),
            tab, tab, tab,
        ],
        out_specs=[pl.BlockSpec((tm, N_HEADS * LANES), row), pl.BlockSpec((tm, N_HEADS * LANES), row),
                   pl.BlockSpec((N_HEADS * DV_PAD, tm), lambda i: (0, i))],
        out_shape=[out, out, jax.ShapeDtypeStruct((N_HEADS * DV_PAD, t), BF16)],
        compiler_params=_params(("arbitrary",)),
        name="mla_prep",
    )(shift, proj, proj, proj, qaw, kvw, wq, wqs, wk, wvt, qnw, knw, *tabs)


def _mla_attn_kernel(q_ref, k_ref, vt_ref, o_ref, m_s, acc_s, *, tq, nq, fixed):
    i = pl.program_id(1)
    heads = range(N_HEADS)
    causal = _row((tq, tq)) <= _lane((tq, tq))

    def logits(rows, masked):
        sts = [_nt(k_ref[rows, LANES * h:LANES * (h + 1)], q_ref[:, LANES * h:LANES * (h + 1)]) for h in heads]
        return [jnp.where(causal, st, NEG) for st in sts] if masked else sts

    def values(rows):
        return [vt_ref[DV_PAD * h:DV_PAD * (h + 1), rows] for h in heads]

    if fixed:
        def sweep(n):
            accs = [jnp.zeros((DV_PAD, tq), F32)] * N_HEADS
            for start in range(0, n, 2):
                rows = pl.ds(start * tq, min(2, n - start) * tq)
                _, accs = _softmax_update(logits(rows, False), values(rows), None, accs, True)
            rows = pl.ds(n * tq, tq)
            _, accs = _softmax_update(logits(rows, True), values(rows), None, accs, True)
            for h in heads:
                acc_s[h] = accs[h]

        for n in range(nq):
            pl.when(i == n)(functools.partial(sweep, n))
    else:
        for h in heads:
            m_s[h] = jnp.full((1, tq), NEG, F32)
            acc_s[h] = jnp.zeros((DV_PAD, tq), F32)

        def step(j, width, masked):
            rows = pl.ds(j * tq if isinstance(j, int) else pl.multiple_of(j * tq, tq), width * tq)
            _softmax_steps(logits(rows, masked), values(rows), [m_s.at[h] for h in heads],
                           [acc_s.at[h] for h in heads])

        step(i, 1, True)

        def body(jj, carry):
            step(i - 2 * jj, 2, False)
            return carry

        lax.fori_loop(1, i // 2 + 1, body, 0)
        pl.when(i % 2 == 1)(functools.partial(step, 0, 1, False))
    yt = jnp.concatenate([acc_s[h, 0:HEAD_DIM, :] / acc_s[h, HEAD_DIM:HEAD_DIM + 1, :] for h in range(N_HEADS)],
                         axis=0)
    o_ref[...] = yt.T


def _mla_attention(qm, km, vmt, b, s, tq, fixed):
    t = b * s
    nq = s // tq
    w = N_HEADS * LANES
    return pl.pallas_call(
        functools.partial(_mla_attn_kernel, tq=tq, nq=nq, fixed=fixed),
        grid=(b, nq),
        in_specs=[
            pl.BlockSpec((tq, w), lambda bi, i: (bi * nq + i, 0)),
            pl.BlockSpec((s, w), lambda bi, i: (bi, 0)),
            pl.BlockSpec((N_HEADS * DV_PAD, s), lambda bi, i: (0, bi)),
        ],
        out_specs=pl.BlockSpec((tq, 2 * LANES), lambda bi, i: (bi * nq + i, 0)),
        out_shape=jax.ShapeDtypeStruct((t, GROUP_WIDTH), F32),
        scratch_shapes=[pltpu.VMEM((N_HEADS, 1, tq), F32), pltpu.VMEM((N_HEADS, DV_PAD, tq), F32)],
        compiler_params=_params(("arbitrary", "arbitrary")),
        name="mla_attention",
    )(qm, km, vmt)


def _sb_kernel(q_ref, k_ref, v_ref, o_ref, kb_s, vt_s, q_s, r_s, acc_s, kmax_s, *, tq, s_len):
    i = pl.program_id(1)
    gw = 2 * LANES
    head_shift = HEAD_DIM.bit_length() - 1
    same_head = (lax.shift_right_logical(_row((gw, gw)), head_shift)
                 == lax.shift_right_logical(_lane((gw, gw)), head_shift))

    @pl.when(i == 0)
    def _cast_kv():
        ch = 256

        def body(t, kmax):
            rows = pl.ds(pl.multiple_of(t * ch, ch), ch)
            k = k_ref[rows, :]
            kb_s[rows, :] = k.astype(BF16)
            vt_s[:, _lane_tile(t, ch)] = v_ref[rows, :].T.astype(BF16)
            return jnp.maximum(kmax, _group_sums(k * k, same_head))

        kmax = lax.fori_loop(0, s_len // ch, body, jnp.zeros((ch, gw), F32))
        kmax_s[0] = jnp.max(kmax)

    key = _row((tq, tq))
    qry = _lane((tq, tq))
    strict = key < qry
    tri = jnp.where(key <= qry, 1.0, 0.0).astype(BF16)
    tri2 = jnp.concatenate([tri, tri], axis=1)
    lane = _lane((tq, LANES))
    q = q_ref[...] * (HEAD_DIM ** -0.5 * LOG2E)
    for h in range(N_HEADS):
        mine = (lane < HEAD_DIM) if h % 2 == 0 else (lane >= HEAD_DIM)
        q_s[h] = jnp.where(mine, q[:, LANES * (h // 2):LANES * (h // 2 + 1)], 0.0).astype(BF16)
        r_s[h] = jnp.zeros((1, tq), F32)
        acc_s[h] = jnp.zeros((HEAD_DIM, tq), F32)
    qmax = jnp.max(_group_sums(q * q, same_head))
    z_bound = jnp.sqrt(jnp.full((1, tq), qmax * kmax_s[0], F32)) * 1.01 + 1.0

    heads = range(N_HEADS)

    def step(j, width, masked):
        rs = [r_s[h] for h in heads]
        accs = [acc_s[h] for h in heads]
        tiles = [j + width - 1 - w for w in range(width)]
        rows = [pl.ds(t * tq if isinstance(t, int) else pl.multiple_of(t * tq, tq), tq) for t in tiles]
        zs = [[_nt(kb_s[r, LANES * (h // 2):LANES * (h // 2 + 1)], q_s[h]) for h in heads] for r in rows]
        part = []
        for w, zt in enumerate(zs):
            negabs = [pltpu.bitcast(pltpu.bitcast(z, jnp.uint32) | jnp.uint32(0x80000000), F32) for z in zt]
            sps = [jnp.maximum(z, 0.0) + jnp.log2(1.0 + jnp.exp2(na)) for z, na in zip(zt, negabs)]
            if masked and w == 0:
                sps = [jnp.where(strict, sp, 0.0) for sp in sps]
            his = [sp.astype(BF16) for sp in sps]
            los = [(sp - hi.astype(F32)).astype(BF16) for sp, hi in zip(sps, his)]
            part.append([_dot(tri2, jnp.concatenate([hi, lo], axis=0)) for hi, lo in zip(his, los)])
        for w, (r, zt, pt) in enumerate(zip(rows, zs, part)):
            csums = [p + rc for p, rc in zip(pt, rs)]
            als = [jnp.exp2(z - cs) for z, cs in zip(zt, csums)]
            if masked and w == 0:
                als = [jnp.where(strict, a, 0.0) for a in als]
            accs = [acc + _dot(vt_s[HEAD_DIM * h:HEAD_DIM * (h + 1), r], als[h].astype(BF16))
                    for h, acc in zip(heads, accs)]
            rs = [cs[0:1, :] for cs in csums]
        for h in heads:
            acc_s[h] = accs[h]
            r_s[h] = rs[h]

    def live():
        r_min = jnp.minimum(jnp.minimum(r_s[0], r_s[1]), jnp.minimum(r_s[2], r_s[3]))
        return jnp.max(z_bound - r_min) >= -150.0

    step(i, 1, True)
    pl.when(i >= 1)(functools.partial(step, i - 1, 1, False))

    n_rest = jnp.maximum(i - 1, 0)
    n_pairs = n_rest // 2

    def cond(carry):
        p, alive = carry
        return (p < n_pairs) & alive

    def body(carry):
        p, _ = carry
        step(i - 1 - 2 * (p + 1), 2, False)
        return p + 1, live()

    _, alive = lax.while_loop(cond, body, (jnp.int32(0), live()))
    pl.when((n_rest % 2 == 1) & alive)(functools.partial(step, 0, 1, False))
    o_ref[...] = jnp.concatenate([acc_s[h] for h in range(N_HEADS)], axis=0).T


def _sb_attention(proj, b, s, tq):
    t = b * s
    nq = s // tq
    w = 2 * LANES
    return pl.pallas_call(
        functools.partial(_sb_kernel, tq=tq, s_len=s),
        grid=(b, nq),
        in_specs=[
            pl.BlockSpec((tq, w), lambda bi, i: (bi * nq + i, CB_SQ)),
            pl.BlockSpec((s, w), lambda bi, i: (bi, CB_SK)),
            pl.BlockSpec((s, w), lambda bi, i: (bi, CB_SV)),
        ],
        out_specs=pl.BlockSpec((tq, w), lambda bi, i: (bi * nq + i, 0)),
        out_shape=jax.ShapeDtypeStruct((t, GROUP_WIDTH), F32),
        scratch_shapes=[pltpu.VMEM((s, w), BF16), pltpu.VMEM((w, s), BF16),
                        pltpu.VMEM((N_HEADS, tq, LANES), BF16),
                        pltpu.VMEM((N_HEADS, 1, tq), F32), pltpu.VMEM((N_HEADS, HEAD_DIM, tq), F32),
                        pltpu.SMEM((1,), F32)],
        compiler_params=_params(("arbitrary", "arbitrary")),
        name="sb_attention",
    )(proj, proj, proj)


def _post_kernel(a_ref, ap_ref, yb_ref, yc_ref, yd_ref, x_ref, cw_ref, cb_ref, onw_ref, wo_ref, n2w_ref,
                 w1_ref, w2_ref, o_ref, *, tm, s_len, ffc):
    i = pl.program_id(0)
    a = a_ref[...]
    gw = GROUP_WIDTH
    v = a[:, gw:2 * gw] * a[:, 2 * gw:3 * gw]
    ap = ap_ref[...]
    first = (i * tm) % s_len == 0
    vp = jnp.where(first, 0.0, ap[:, gw:2 * gw] * ap[:, 2 * gw:3 * gw])
    row = _row(v.shape)
    v1 = jnp.where(row == 0, vp[7:8, :], pltpu.roll(v, 1, 0))
    v2 = jnp.where(row == 0, vp[6:7, :], jnp.where(row == 1, vp[7:8, :], pltpu.roll(v, 2, 0)))
    conv = cw_ref[0:1, :] * v2 + cw_ref[1:2, :] * v1 + cw_ref[2:3, :] * v
    ya = a[:, 0:gw] * (conv + cb_ref[...])

    mix = None
    for g, y in enumerate((ya, yb_ref[...], yc_ref[...], yd_ref[...])):
        ms = jnp.mean(y * y, axis=-1, keepdims=True)
        yn = (y * lax.rsqrt(ms + EPS) * onw_ref[:, gw * g:gw * (g + 1)]).astype(BF16)
        part = _dot(yn, wo_ref[gw * g:gw * (g + 1), :])
        mix = part if mix is None else mix + part
    x1 = x_ref[...] + mix

    ms = jnp.mean(x1 * x1, axis=-1, keepdims=True)
    h2 = (x1 * lax.rsqrt(ms + EPS) * n2w_ref[...]).astype(BF16)
    ff = None
    for cidx in range(D_FF // ffc):
        u = _dot(h2, w1_ref[:, ffc * cidx:ffc * (cidx + 1)])
        u = jnp.square(jnp.maximum(u, 0.0)).astype(BF16)
        part = _dot(u, w2_ref[ffc * cidx:ffc * (cidx + 1), :])
        ff = part if ff is None else ff + part
    o_ref[...] = x1 + ff


def _post(proj, yb, yc, yd, x2d, cw, cb, onw, wo, n2w, w1, w2, l, s, tm):
    t = x2d.shape[0]
    gw = GROUP_WIDTH
    row = lambda i: (i, 0)
    const = lambda i: (0, 0)
    layer = lambda i: (l, 0, 0)
    once = pl.Buffered(1)
    kern = functools.partial(_post_kernel, tm=tm, s_len=s, ffc=1024)
    return pl.pallas_call(
        kern,
        grid=(t // tm,),
        in_specs=[
            pl.BlockSpec((tm, 3 * gw), row),
            pl.BlockSpec((8, 3 * gw), lambda i: (jnp.maximum(i * (tm // 8) - 1, 0), 0)),
            pl.BlockSpec((tm, gw), row),
            pl.BlockSpec((tm, gw), row),
            pl.BlockSpec((tm, gw), row),
            pl.BlockSpec((tm, D_MODEL), row),
            pl.BlockSpec((3, gw), const),
            pl.BlockSpec((1, gw), const),
            pl.BlockSpec((1, D_MODEL), const),
            pl.BlockSpec((None, D_MODEL, D_MODEL), layer, pipeline_mode=once),
            pl.BlockSpec((1, D_MODEL), const),
            pl.BlockSpec((None, D_MODEL, D_FF), layer, pipeline_mode=once),
            pl.BlockSpec((None, D_FF, D_MODEL), layer, pipeline_mode=once),
        ],
        out_specs=pl.BlockSpec((tm, D_MODEL), row),
        out_shape=jax.ShapeDtypeStruct((t, D_MODEL), F32),
        compiler_params=_params(("arbitrary",)),
        name="post",
    )(proj, proj, yb, yc, yd, x2d, cw, cb, onw, wo, n2w, w1, w2)


def _t5_bucket(dist):
    max_exact = N_BUCKETS // 2
    d = np.maximum(dist, 0)
    large = max_exact + (np.log(np.maximum(d, 1) / max_exact) / math.log(MAX_DISTANCE / max_exact)
                         * (N_BUCKETS - max_exact)).astype(np.int32)
    return np.where(d < max_exact, d, np.minimum(large, N_BUCKETS - 1)).astype(np.int32)


def _tables(s, tq_nsa):
    n_cmp = (s - CMP_LEN) // CMP_STRIDE + 1
    ng = s // CMP_STRIDE
    n_slc = s // SLC_LEN
    tpos = np.arange(s)[None, :]
    n = np.arange(ng)[:, None]
    dist_c = tpos - (n * CMP_STRIDE + CMP_LEN - 1)
    bidxct = np.where((dist_c >= 0) & (n < n_cmp), _t5_bucket(dist_c), -1).astype(np.int32)
    key = np.arange(tq_nsa)[:, None]
    qry = np.arange(tq_nsa)[None, :]
    bidx2t = np.stack([_t5_bucket(qry - key), _t5_bucket(tq_nsa + qry - key)])
    starts = np.arange(n_cmp) * CMP_STRIDE
    ends = starts + CMP_LEN
    s0 = np.arange(n_slc) * SLC_LEN
    s1 = s0 + SLC_LEN
    ovl = np.clip(np.minimum(ends[:, None], s1[None]) - np.maximum(starts[:, None], s0[None]), 0, None) / CMP_LEN
    ovt = np.zeros((LANES, ng), np.float32)
    ovt[:n_slc, :n_cmp] = ovl.T
    emt = (np.arange(LANES)[None, :] == (np.arange(s) // SLC_LEN)[:, None]).astype(np.float32)
    inv = 1.0 / (ROPE_THETA ** (np.arange(0, ROPE_DIM, 2, dtype=np.float64) / ROPE_DIM))
    ang = (np.arange(s, dtype=np.float32)[:, None] * inv.astype(np.float32)[None, :]).astype(np.float64)
    cos, sin = np.cos(ang).astype(np.float32), np.sin(ang).astype(np.float32)
    z32 = np.zeros((s, 32), np.float32)
    z64 = np.zeros((s, 64), np.float32)
    one64 = np.ones((s, 64), np.float32)
    cq_t = np.concatenate([one64, cos, cos, z32], axis=1)
    ck_t = np.concatenate([z64, cos, cos, z32], axis=1)
    sk_t = np.concatenate([z64, -sin, sin, z32], axis=1)
    return dict(bidxct=jnp.asarray(bidxct), bidx2t=jnp.asarray(bidx2t), ovt=jnp.asarray(ovt, BF16),
                emt=jnp.asarray(emt, BF16), rope=(jnp.asarray(cq_t), jnp.asarray(ck_t), jnp.asarray(sk_t)))


def _shift_bound(d, q_gain, k_gain, extra=0.0):
    return (d * jnp.max(jnp.abs(q_gain)) * jnp.max(jnp.abs(k_gain)) * 1.01 + extra + 0.1).reshape(1)


MAX_SHIFT = 60.0


def _pad_cols(w, width):
    return jnp.pad(w, ((0, 0), (0, width - w.shape[1])))


def _layer_weights(l, conv_w, conv_b, nsa_q_norm, nsa_k_norm, cmp_pos, cmp_w1, cmp_w2, mla_q_a_norm,
                   mla_kv_norm, mla_wq_b, mla_wkv_b, mla_q_norm, mla_k_norm, out_norm_w, norm2_w):
    w1 = cmp_w1[l].reshape(2, CMP_LEN, HEAD_DIM, CMP_HIDDEN)
    zw = jnp.zeros((CMP_LEN, HEAD_DIM, CMP_HIDDEN), F32)
    cw1 = jnp.concatenate([jnp.concatenate([w1[0], zw], axis=2), jnp.concatenate([zw, w1[1]], axis=2)],
                          axis=1).astype(BF16)
    cw1 = cw1.reshape(2, CMP_STRIDE * LANES, 2 * CMP_HIDDEN)
    zc = jnp.zeros((CMP_HIDDEN, HEAD_DIM), F32)
    cw2 = jnp.concatenate([jnp.concatenate([cmp_w2[l, 0], zc], axis=1),
                           jnp.concatenate([zc, cmp_w2[l, 1]], axis=1)], axis=0).astype(BF16)
    cpos = jnp.concatenate([cmp_pos[l, 0], cmp_pos[l, 1]], axis=1).reshape(2, CMP_STRIDE * LANES)
    kn = nsa_k_norm[l]
    ones64 = jnp.ones((HEAD_DIM,), F32)
    knw_c = jnp.concatenate([kn[0], ones64])[None, :]
    knw_sw = jnp.stack([jnp.concatenate([kn[1], ones64]), jnp.concatenate([kn[2], ones64])])
    qnw = jnp.tile(nsa_q_norm[l], N_HEADS)[None, :]
    wq = mla_wq_b[l].reshape(Q_LORA, N_HEADS, QK_DIM)
    half = ROPE_DIM // 2
    wqs = jnp.concatenate([jnp.zeros((Q_LORA, N_HEADS, HEAD_DIM), F32), wq[:, :, HEAD_DIM + half:],
                           wq[:, :, HEAD_DIM:HEAD_DIM + half]], axis=2)
    pad_q = lambda w: jnp.pad(w, ((0, 2 * LANES - Q_LORA), (0, 0), (0, LANES - QK_DIM))).reshape(
        2 * LANES, N_HEADS * LANES).astype(BF16)
    wkv = mla_wkv_b[l].reshape(KV_LORA, N_HEADS, 2 * HEAD_DIM)
    wk = jnp.pad(wkv[:, :, :HEAD_DIM], ((0, 0), (0, 0), (0, LANES - HEAD_DIM))).reshape(KV_LORA, N_HEADS * LANES)
    wvt = jnp.pad(wkv[:, :, HEAD_DIM:], ((0, 0), (0, 0), (0, DV_PAD - HEAD_DIM))).reshape(KV_LORA, -1).T
    return dict(
        cw1=cw1, cw2=cw2, cw2t=cw2.T, cpos=cpos, knw_c=knw_c, knw_sw=knw_sw, qnw=qnw,
        qaw=_pad_cols(mla_q_a_norm[l][None, :], 2 * LANES), kvw=mla_kv_norm[l][None, :],
        wq=pad_q(wq), wqs=pad_q(wqs), wk=wk.astype(BF16), wvt=wvt.astype(BF16),
        mqn=_pad_cols(mla_q_norm[l][None, :] * (QK_DIM ** -0.5 * LOG2E), LANES),
        mkn=_pad_cols(mla_k_norm[l][None, :], LANES),
        cw=conv_w[l], cb=conv_b[l][None, :], onw=out_norm_w[l][None, :], n2w=norm2_w[l][None, :])


TM_PROJ = 512
TM_PREP = 1024
TM_POST = 512
TQ_NSA = 256
TQ_MLA = 256
TQ_SB = 256


def kernel(x, rel_bias, norm1_w, w_in, conv_w, conv_b, nsa_q_norm, nsa_k_norm, cmp_pos, cmp_w1, cmp_w2,
           mla_q_a_norm, mla_kv_norm, mla_wq_b, mla_wkv_b, mla_q_norm, mla_k_norm, out_norm_w, w_out, norm2_w,
           ffn_w1, ffn_w2):
    b, s, d = x.shape
    depth = w_in.shape[0]
    tabs = _tables(s, TQ_NSA)
    x2d = x.reshape(b * s, d)
    wo_b, w1_b, w2_b = w_out.astype(BF16), ffn_w1.astype(BF16), ffn_w2.astype(BF16)
    for l in range(depth):
        w = _layer_weights(l, conv_w, conv_b, nsa_q_norm, nsa_k_norm, cmp_pos, cmp_w1, cmp_w2,
                           mla_q_a_norm, mla_kv_norm, mla_wq_b, mla_wkv_b, mla_q_norm, mla_k_norm, out_norm_w,
                           norm2_w)
        proj = _inproj(x2d, norm1_w[l][None, :], w_in, l, TM_PROJ)
        kcvc, kcvct = _compress(proj, w["cpos"], w["cw1"], w["cw2"], w["cw2t"], w["knw_c"], b, s)
        nsa_shift = _shift_bound(HEAD_DIM, w["qnw"] * (HEAD_DIM ** -0.5 * LOG2E), w["knw_sw"][:, :HEAD_DIM],
                                 jnp.max(jnp.abs(rel_bias)) * LOG2E)
        nsa_args = (proj, kcvc, kcvct, rel_bias, nsa_shift, tabs["bidxct"], tabs["bidx2t"], tabs["ovt"], tabs["emt"],
                    w["qnw"], w["knw_sw"])
        yb = lax.cond(nsa_shift[0] <= MAX_SHIFT,
                      lambda *a: _nsa_attention(*a, b, s, TQ_NSA, True),
                      lambda *a: _nsa_attention(*a, b, s, TQ_NSA, False), *nsa_args)
        mla_shift = _shift_bound(QK_DIM, w["mqn"], w["mkn"])
        qm, km, vmt = _mla_prep(proj, mla_shift, w["qaw"], w["kvw"], w["wq"], w["wqs"], w["wk"], w["wvt"], w["mqn"],
                                w["mkn"], tabs["rope"], s, TM_PREP)
        yc = lax.cond(mla_shift[0] <= MAX_SHIFT,
                      lambda *a: _mla_attention(*a, b, s, TQ_MLA, True),
                      lambda *a: _mla_attention(*a, b, s, TQ_MLA, False), qm, km, vmt)
        yd = _sb_attention(proj, b, s, TQ_SB)
        x2d = _post(proj, yb, yc, yd, x2d, w["cw"], w["cb"], w["onw"], wo_b, w["n2w"], w1_b, w2_b, l, s, TM_POST)
    return x2d.reshape(b, s, d)
```

```python
import functools
import math

import jax
import jax.numpy as jnp
import numpy as np
from jax import lax
from jax.experimental import pallas as pl
from jax.experimental.pallas import tpu as pltpu

F32 = jnp.float32
BF16 = jnp.bfloat16

D_MODEL = 1024
GROUP_WIDTH = 256
HEAD_DIM = 64
N_HEADS = 4
LANES = 128
CMP_LEN = 32
CMP_STRIDE = 16
SLC_LEN = 64
N_SEL = 16
WINDOW = 512
CMP_HIDDEN = 256
Q_LORA = 192
KV_LORA = 128
ROPE_DIM = 32
QK_DIM = 96
ROPE_THETA = 10000.0
N_BUCKETS = 32
MAX_DISTANCE = 128
D_FF = 4096
EPS = 1e-6
NEG = -1e30
LOG2E = math.log2(math.e)
DV_PAD = 80

NP = 2816
CB_NQ = 3
CB_KCVC = 8
CB_KSVS = 9
CB_KWVW = 10
CB_GATE = 11
CB_CQ = 6
CB_CKV = 14
CB_KR = 15
CB_SQ = 8
CB_SK = 9
CB_SV = 10

VMEM_LIMIT = 56 * 1024 * 1024

NT_DIMS = (((1,), (1,)), ((), ()))


def _params(sem):
    return pltpu.CompilerParams(dimension_semantics=sem, vmem_limit_bytes=VMEM_LIMIT)


def _nt(a, b):
    return lax.dot_general(a, b, NT_DIMS, preferred_element_type=F32)


def _dot(a, b):
    return jnp.dot(a, b, preferred_element_type=F32)


def _lane(shape):
    return lax.broadcasted_iota(jnp.int32, shape, len(shape) - 1)


def _row(shape):
    return lax.broadcasted_iota(jnp.int32, shape, len(shape) - 2)


def _lane_tile(j, width):
    return pl.ds(pl.multiple_of(j * width, width), width)


_IN_SEGMENTS = (
    ((0, 1408), 0),
    ((1408, 1420), 1408),
    ((1420, 1612), 1536),
    ((1612, 1740), 1792),
    ((1756, 1772), 1920),
    ((1740, 1756), 1936),
    ((1740, 1772), 1984),
    ((1772, 2540), 2048),
)
IN_COLS = 2540


def _inproj_kernel(x_ref, nw_ref, w_ref, o_ref, w_s):
    @pl.when(pl.program_id(0) == 0)
    def _relayout_weights():
        end = 0
        for (a, b), dst in _IN_SEGMENTS:
            if dst > end:
                w_s[:, end:dst] = jnp.zeros((D_MODEL, dst - end), BF16)
            w_s[:, dst:dst + b - a] = w_ref[:, a:b].astype(BF16)
            end = dst + b - a
        assert end == NP

    x = x_ref[...]
    ms = jnp.mean(x * x, axis=-1, keepdims=True)
    h = (x * lax.rsqrt(ms + EPS) * nw_ref[...]).astype(BF16)
    o_ref[...] = _dot(h, w_s[...])


def _inproj(x2d, nw, w_in, l, tm):
    t = x2d.shape[0]
    return pl.pallas_call(
        _inproj_kernel,
        grid=(t // tm,),
        in_specs=[
            pl.BlockSpec((tm, D_MODEL), lambda i: (i, 0)),
            pl.BlockSpec((1, D_MODEL), lambda i: (0, 0)),
            pl.BlockSpec((None, D_MODEL, IN_COLS), lambda i: (l, 0, 0), pipeline_mode=pl.Buffered(1)),
        ],
        out_specs=pl.BlockSpec((tm, NP), lambda i: (i, 0)),
        out_shape=jax.ShapeDtypeStruct((t, NP), F32),
        scratch_shapes=[pltpu.VMEM((D_MODEL, NP), BF16)],
        compiler_params=_params(("arbitrary",)),
        name="inproj",
    )(x2d, nw, w_in)


def _compress_kernel(x_ref, pos_ref, w1_ref, w2_ref, w2t_ref, knw_ref, o_ref, ot_ref):
    ng = x_ref.shape[1]
    x = x_ref[0]
    acc_a = _dot((x + pos_ref[0:1, :]).astype(BF16), w1_ref[0])
    acc_b = _dot((x + pos_ref[1:2, :]).astype(BF16), w1_ref[1])
    pre = acc_a + pltpu.roll(acc_b, ng - 1, 0)
    hdn = (pre * jax.nn.sigmoid(pre)).astype(BF16)
    out = _dot(hdn, w2_ref[...])
    lane = _lane(out.shape)
    is_k = lane < HEAD_DIM
    ss = jnp.sum(jnp.where(is_k, out * out, 0.0), axis=-1, keepdims=True) * (1.0 / HEAD_DIM)
    o_ref[0] = jnp.where(is_k, out * lax.rsqrt(ss + EPS) * knw_ref[...], out)
    ot_ref[0] = _nt(w2t_ref[...], hdn)


def _compress(proj, pos, w1, w2, w2t, knw, b, s):
    ng = s // CMP_STRIDE
    gl = CMP_STRIDE * LANES
    xg = proj[:, CB_KCVC * LANES:(CB_KCVC + 1) * LANES].reshape(b, ng, gl)
    return pl.pallas_call(
        _compress_kernel,
        grid=(b,),
        in_specs=[
            pl.BlockSpec((1, ng, gl), lambda i: (i, 0, 0)),
            pl.BlockSpec((2, gl), lambda i: (0, 0)),
            pl.BlockSpec((2, gl, 2 * CMP_HIDDEN), lambda i: (0, 0, 0)),
            pl.BlockSpec((2 * CMP_HIDDEN, LANES), lambda i: (0, 0)),
            pl.BlockSpec((LANES, 2 * CMP_HIDDEN), lambda i: (0, 0)),
            pl.BlockSpec((1, LANES), lambda i: (0, 0)),
        ],
        out_specs=[pl.BlockSpec((1, ng, LANES), lambda i: (i, 0, 0)),
                   pl.BlockSpec((1, LANES, ng), lambda i: (i, 0, 0))],
        out_shape=[jax.ShapeDtypeStruct((b, ng, LANES), F32), jax.ShapeDtypeStruct((b, LANES, ng), F32)],
        compiler_params=_params(("arbitrary",)),
        name="nsa_compress",
    )(xg, pos, w1, w2, w2t, knw)


def _softmax_update(sts, vts, m_old, acc_old, fixed=False):
    if fixed:
        return m_old, [acc + _dot(vt, jnp.exp2(st).astype(BF16)) for acc, vt, st in zip(acc_old, vts, sts)]
    m_new = [jnp.maximum(m, jnp.max(st, axis=0, keepdims=True)) for m, st in zip(m_old, sts)]
    ps = [jnp.exp2(st - m).astype(BF16) for st, m in zip(sts, m_new)]
    alphas = [jnp.exp2(mo - mn) for mo, mn in zip(m_old, m_new)]
    acc_new = [al * acc + _dot(vt, p) for al, acc, vt, p in zip(alphas, acc_old, vts, ps)]
    return m_new, acc_new


def _softmax_steps(sts, vts, m_refs, acc_refs, fixed=False):
    m_old = [None] * len(sts) if fixed else [r[...] for r in m_refs]
    m_new, acc_new = _softmax_update(sts, vts, m_old, [r[...] for r in acc_refs], fixed)
    for r, v in zip(() if fixed else m_refs, m_new):
        r[...] = v
    for r, v in zip(acc_refs, acc_new):
        r[...] = v


def _with_ones_row(vt):
    pad = jnp.where(_row((DV_PAD - HEAD_DIM, vt.shape[1])) == 0, 1.0, 0.0).astype(vt.dtype)
    return jnp.concatenate([vt, pad], axis=0)


def _bucket_bias(bidx, relb_ref, h, fill):
    acc = jnp.full(bidx.shape, fill, F32)
    for bk in range(N_BUCKETS):
        acc = jnp.where(bidx == bk, relb_ref[bk, h] * LOG2E, acc)
    return acc


def _group_sums(x, member):
    g = jnp.where(member, 1.0, 0.0).astype(BF16)
    hi = x.astype(BF16)
    lo = (x - hi.astype(F32)).astype(BF16)
    return _dot(hi, g) + _dot(lo, g)


def _dup_low_half(x):
    y = jnp.where(_lane(x.shape) < HEAD_DIM, x, 0.0)
    return y + pltpu.roll(y, HEAD_DIM, 1)


def _nsa_kernel(relb_ref, shift_ref, q_ref, g_ref, kcvc_ref, kcvct_ref, ksvs_ref, kwvw_ref, bidxct_ref, bidx2t_ref, ovt_ref,
                emt_ref, qnw_ref, knw_ref, o_ref,
                biasc_s, bias2_s, ks_s, vst_s, kw_s, vwt_s, kc_s, vct_s, qx_s, m_s, acc_s, *, tq, s_len, fixed):
    b = pl.program_id(0)
    shift = shift_ref[0] if fixed else 0.0
    i = pl.program_id(1)
    n_win = WINDOW // tq
    ng = s_len // CMP_STRIDE

    @pl.when((b == 0) & (i == 0))
    def _build_bias_tables():
        key = _row((tq, tq))
        qry = _lane((tq, tq))
        for h in range(N_HEADS):
            cols = slice(h * tq, (h + 1) * tq)
            far = jnp.full((tq, tq), relb_ref[N_BUCKETS - 1, h] * LOG2E - shift, F32)
            bias2_s[0, :, cols] = jnp.where(key <= qry, _bucket_bias(bidx2t_ref[0], relb_ref, h, NEG) - shift, NEG)
            bias2_s[1, :, cols] = _bucket_bias(bidx2t_ref[1], relb_ref, h, NEG) - shift
            bias2_s[2, :, cols] = far
            bias2_s[3, :, cols] = jnp.where(key > qry, far, NEG)
            bias2_s[4, :, cols] = jnp.full((tq, tq), NEG, F32)

        def body(t, carry):
            bi = bidxct_ref[:, _lane_tile(t, tq)]
            for h in range(N_HEADS):
                biasc_s[t, :, h * tq:(h + 1) * tq] = _bucket_bias(bi, relb_ref, h, NEG)
            return carry

        lax.fori_loop(0, s_len // tq, body, 0)

    @pl.when(i == 0)
    def _prep_kv():
        ch = 256

        def body(t, carry):
            rows = pl.ds(pl.multiple_of(t * ch, ch), ch)
            for src, kdst, vdst, widx in ((ksvs_ref, ks_s, vst_s, 0), (kwvw_ref, kw_s, vwt_s, 1)):
                x = src[rows, :]
                ss = _group_sums(x * x, _row((LANES, LANES)) < HEAD_DIM) * (1.0 / HEAD_DIM)
                kn = x * lax.rsqrt(ss + EPS) * knw_ref[widx:widx + 1, :]
                kdst[rows, 0:LANES] = _dup_low_half(kn).astype(BF16)
                vdst[:, _lane_tile(t, ch)] = _with_ones_row(x.T[HEAD_DIM:, :]).astype(BF16)
            ks_s[rows, LANES:] = emt_ref[rows, :]
            return carry

        lax.fori_loop(0, s_len // ch, body, 0)
        kc_s[...] = _dup_low_half(kcvc_ref[0]).astype(BF16)
        vct_s[...] = kcvct_ref[0][HEAD_DIM:, :].astype(BF16)

    q = q_ref[...]
    lane = _lane((tq, LANES))
    heads = range(N_HEADS)
    gw = 2 * LANES
    head_shift = HEAD_DIM.bit_length() - 1
    same_head = (lax.shift_right_logical(_row((gw, gw)), head_shift)
                 == lax.shift_right_logical(_lane((gw, gw)), head_shift))
    ss = _group_sums(q * q, same_head) * (1.0 / HEAD_DIM)
    qn = q * lax.rsqrt(ss + EPS) * qnw_ref[...] * (HEAD_DIM ** -0.5 * LOG2E)
    for h in heads:
        mine = (lane < HEAD_DIM) if h % 2 == 0 else (lane >= HEAD_DIM)
        qx_s[h, :, 0:LANES] = jnp.where(mine, qn[:, LANES * (h // 2):LANES * (h // 2 + 1)], 0.0).astype(BF16)

    lcs = [_nt(kc_s[...], qx_s[h, :, 0:LANES]) + biasc_s[i, :, h * tq:(h + 1) * tq] for h in heads]
    pcs = [jnp.where(lc > 0.5 * NEG, jnp.exp2(lc - jnp.max(lc, axis=0, keepdims=True)), 0.0) for lc in lcs]
    dens = [jnp.sum(pc, axis=0, keepdims=True) for pc in pcs]
    pcs = [pc / jnp.where(den > 0.0, den, 1.0) for pc, den in zip(pcs, dens)]
    o_cmp = [_dot(vct_s[...], pc.astype(BF16)) for pc in pcs]
    psum = (pcs[0] + pcs[1]) + (pcs[2] + pcs[3])

    n_slc = s_len // SLC_LEN
    n_sel = min(N_SEL, n_slc)
    blk = _row((n_slc, tq))
    tpos = i * tq + _lane((n_slc, tq))
    tblk = tpos // SLC_LEN
    valid = blk * SLC_LEN <= tpos

    p_hi = psum.astype(BF16)
    p_lo = (psum - p_hi.astype(F32)).astype(BF16)
    score = _dot(ovt_ref[...], p_hi) + _dot(ovt_ref[...], p_lo)
    forced = (blk == 0) | (blk == tblk) | (blk == tblk - 1)
    sc = jnp.where(forced, jnp.inf, jnp.where(valid, score[0:n_slc], -jnp.inf))
    rank = jnp.zeros((n_slc, tq), F32)
    for k in range(n_slc):
        ck = sc[k:k + 1, :]
        beats = (ck > sc) | ((ck == sc) & (blk > k))
        rank += jnp.where(beats, 1.0, 0.0)
    pen = jnp.where((rank < float(n_sel)) & valid, 0.0, NEG)
    pen = jnp.concatenate([pen, jnp.zeros((LANES - n_slc, tq), F32)], axis=0)
    pen_t = pen.T.astype(BF16)
    pen = pen.astype(BF16)
    for h in heads:
        qx_s[h, :, LANES:] = pen_t

    ms = [jnp.full((1, tq), NEG, F32)] * (2 * N_HEADS)
    accs = [jnp.zeros((DV_PAD, tq), F32)] * (2 * N_HEADS)
    for jj in range(n_win + 1):
        exists = i >= jj
        rows = pl.ds(pl.multiple_of(jnp.maximum(i - jj, 0) * tq, tq), tq)
        kind = jnp.where(exists, min(jj, 2), 4)
        kind_w = jnp.where(exists, 3 if jj == n_win else min(jj, 2), 4)
        ks = ks_s[rows, 0:LANES]
        kw = kw_s[rows, :]
        masked = _dot(ks_s[rows, LANES:], pen)
        sts = [_nt(ks, qx_s[h, :, 0:LANES]) + masked + bias2_s[kind, :, h * tq:(h + 1) * tq] for h in heads]
        sts += [_nt(kw, qx_s[h, :, 0:LANES]) + bias2_s[kind_w, :, h * tq:(h + 1) * tq] for h in heads]
        vts = [vst_s[:, rows]] * N_HEADS + [vwt_s[:, rows]] * N_HEADS
        ms, accs = _softmax_update(sts, vts, ms, accs, fixed)
    for h in heads:
        if not fixed:
            m_s[h] = ms[h]
        acc_s[h] = accs[h]
    o_win = [accs[N_HEADS + h][0:HEAD_DIM, :] / accs[N_HEADS + h][HEAD_DIM:HEAD_DIM + 1, :] for h in heads]

    def far_step(j, width):
        rows = pl.ds(j * tq if isinstance(j, int) else pl.multiple_of(j * tq, tq), width * tq)
        ks = ks_s[rows, :]
        sts = [_nt(ks, qx_s[h]) + (relb_ref[N_BUCKETS - 1, h] * LOG2E - shift) for h in heads]
        _softmax_steps(sts, [vst_s[:, rows]] * N_HEADS, [m_s.at[h] for h in heads], [acc_s.at[h] for h in heads],
                       fixed)

    n_far = jnp.maximum(i - n_win, 0)
    if fixed:
        def far_sweep(n):
            accs = [acc_s[h] for h in heads]
            for start in range(0, n, 2):
                rows = pl.ds(start * tq, min(2, n - start) * tq)
                ks = ks_s[rows, :]
                sts = [_nt(ks, qx_s[h]) + (relb_ref[N_BUCKETS - 1, h] * LOG2E - shift) for h in heads]
                _, accs = _softmax_update(sts, [vst_s[:, rows]] * N_HEADS, None, accs, True)
            for h in heads:
                acc_s[h] = accs[h]

        for n in range(1, s_len // tq - n_win):
            pl.when(n_far == n)(functools.partial(far_sweep, n))
    else:
        def far_body(p, carry):
            far_step(n_far - 2 * (p + 1), 2)
            return carry

        lax.fori_loop(0, n_far // 2, far_body, 0)
        pl.when(n_far % 2 == 1)(functools.partial(far_step, 0, 1))

    gt = jax.nn.sigmoid(g_ref[...]).T
    ys = []
    for h in heads:
        o_slc = acc_s[h, 0:HEAD_DIM, :] / acc_s[h, HEAD_DIM:HEAD_DIM + 1, :]
        ys.append(gt[3 * h:3 * h + 1, :] * o_cmp[h] + gt[3 * h + 1:3 * h + 2, :] * o_slc
                  + gt[3 * h + 2:3 * h + 3, :] * o_win[h])
    o_ref[...] = jnp.concatenate(ys, axis=0).T


def _nsa_attention(proj, kcvc, kcvct, rel_bias, shift, bidxct, bidx2t, ovt, emt, qnw, knw, b, s, tq, fixed):
    t = b * s
    nq = s // tq
    m_rows = N_HEADS * tq
    ng = s // CMP_STRIDE
    kern = functools.partial(_nsa_kernel, tq=tq, s_len=s, fixed=fixed)
    return pl.pallas_call(
        kern,
        grid=(b, nq),
        in_specs=[
            pl.BlockSpec(memory_space=pltpu.SMEM),
            pl.BlockSpec(memory_space=pltpu.SMEM),
            pl.BlockSpec((tq, 2 * LANES), lambda bi, i: (bi * nq + i, CB_NQ)),
            pl.BlockSpec((tq, LANES), lambda bi, i: (bi * nq + i, CB_GATE)),
            pl.BlockSpec((1, ng, LANES), lambda bi, i: (bi, 0, 0)),
            pl.BlockSpec((1, LANES, ng), lambda bi, i: (bi, 0, 0)),
            pl.BlockSpec((s, LANES), lambda bi, i: (bi, CB_KSVS)),
            pl.BlockSpec((s, LANES), lambda bi, i: (bi, CB_KWVW)),
            pl.BlockSpec((ng, s), lambda bi, i: (0, 0)),
            pl.BlockSpec((2, tq, tq), lambda bi, i: (0, 0, 0)),
            pl.BlockSpec((LANES, ng), lambda bi, i: (0, 0)),
            pl.BlockSpec((s, LANES), lambda bi, i: (0, 0)),
            pl.BlockSpec((1, 2 * LANES), lambda bi, i: (0, 0)),
            pl.BlockSpec((2, LANES), lambda bi, i: (0, 0)),
        ],
        out_specs=pl.BlockSpec((tq, 2 * LANES), lambda bi, i: (bi * nq + i, 0)),
        out_shape=jax.ShapeDtypeStruct((t, GROUP_WIDTH), F32),
        scratch_shapes=[
            pltpu.VMEM((nq, ng, m_rows), F32),
            pltpu.VMEM((5, tq, m_rows), F32),
            pltpu.VMEM((s, 2 * LANES), BF16),
            pltpu.VMEM((DV_PAD, s), BF16),
            pltpu.VMEM((s, LANES), BF16),
            pltpu.VMEM((DV_PAD, s), BF16),
            pltpu.VMEM((ng, LANES), BF16),
            pltpu.VMEM((HEAD_DIM, ng), BF16),
            pltpu.VMEM((N_HEADS, tq, 2 * LANES), BF16),
            pltpu.VMEM((N_HEADS, 1, tq), F32),
            pltpu.VMEM((N_HEADS, DV_PAD, tq), F32),
        ],
        compiler_params=_params(("arbitrary", "arbitrary")),
        name="nsa_attention",
    )(rel_bias, shift, proj, proj, kcvc, kcvct, proj, proj, bidxct, bidx2t, ovt, emt, qnw, knw)


def _mla_prep_kernel(shift_ref, cq_ref, ckv_ref, kr_ref, qaw_ref, kvw_ref, wq_ref, wqs_ref, wk_ref, wvt_ref, qnw_ref, knw_ref,
                     cq_t_ref, ck_t_ref, sk_t_ref, qo_ref, ko_ref, vto_ref):
    cq = cq_ref[...]
    ms = jnp.sum(cq * cq, axis=-1, keepdims=True) * (1.0 / Q_LORA)
    hq = (cq * lax.rsqrt(ms + EPS) * qaw_ref[...]).astype(BF16)
    qf = _dot(hq, wq_ref[...])
    qsw = _dot(hq, wqs_ref[...])
    ckv = ckv_ref[...]
    ms = jnp.mean(ckv * ckv, axis=-1, keepdims=True)
    hkv = (ckv * lax.rsqrt(ms + EPS) * kvw_ref[...]).astype(BF16)
    kf = _dot(hkv, wk_ref[...])
    vt = _nt(wvt_ref[...], hkv)
    vto_ref[...] = jnp.where(_row(vt.shape) % DV_PAD == HEAD_DIM, 1.0, vt).astype(BF16)
    krb = kr_ref[...]
    kr_rot = krb * ck_t_ref[...] + pltpu.roll(krb, HEAD_DIM, 1) * sk_t_ref[...]
    is_shift = _lane((cq.shape[0], LANES)) == QK_DIM
    for h in range(N_HEADS):
        cols = slice(LANES * h, LANES * (h + 1))
        x = qf[:, cols] * cq_t_ref[...] + qsw[:, cols] * sk_t_ref[...]
        ss = jnp.sum(x * x, axis=-1, keepdims=True) * (1.0 / QK_DIM)
        qn = x * lax.rsqrt(ss + EPS) * qnw_ref[...]
        qo_ref[:, cols] = jnp.where(is_shift, -shift_ref[0], qn).astype(BF16)
        k = kf[:, cols] + kr_rot
        ss = jnp.sum(k * k, axis=-1, keepdims=True) * (1.0 / QK_DIM)
        ko_ref[:, cols] = jnp.where(is_shift, 1.0, k * lax.rsqrt(ss + EPS) * knw_ref[...]).astype(BF16)


def _mla_prep(proj, shift, qaw, kvw, wq, wqs, wk, wvt, qnw, knw, tabs, s, tm):
    t = proj.shape[0]
    npos = s // tm
    row = lambda i: (i, 0)
    const = lambda i: (0, 0)
    tab = pl.BlockSpec((tm, LANES), lambda i: (i % npos, 0))
    out = jax.ShapeDtypeStruct((t, N_HEADS * LANES), BF16)
    return pl.pallas_call(
        _mla_prep_kernel,
        grid=(t // tm,),
        in_specs=[
            pl.BlockSpec(memory_space=pltpu.SMEM),
            pl.BlockSpec((tm, 2 * LANES), lambda i: (i, CB_CQ)),
            pl.BlockSpec((tm, LANES), lambda i: (i, CB_CKV)),
            pl.BlockSpec((tm, LANES), lambda i: (i, CB_KR)),
            pl.BlockSpec((1, 2 * LANES), const),
            pl.BlockSpec((1, LANES), const),
            pl.BlockSpec((2 * LANES, N_HEADS * LANES), const),
            pl.BlockSpec((2 * LANES, N_HEADS * LANES), const),
            pl.BlockSpec((LANES, N_HEADS * LANES), const),
            pl.BlockSpec((N_HEADS * DV_PAD, LANES), const),
            pl.BlockSpec((1, LANES), const),
            pl.BlockSpec((1, LANES), const),
            tab, tab, tab,
        ],
        out_specs=[pl.BlockSpec((tm, N_HEADS * LANES), row), pl.BlockSpec((tm, N_HEADS * LANES), row),
                   pl.BlockSpec((N_HEADS * DV_PAD, tm), lambda i: (0, i))],
        out_shape=[out, out, jax.ShapeDtypeStruct((N_HEADS * DV_PAD, t), BF16)],
        compiler_params=_params(("arbitrary",)),
        name="mla_prep",
    )(shift, proj, proj, proj, qaw, kvw, wq, wqs, wk, wvt, qnw, knw, *tabs)


def _mla_attn_kernel(q_ref, k_ref, vt_ref, o_ref, m_s, acc_s, *, tq, nq, fixed):
    i = pl.program_id(1)
    heads = range(N_HEADS)
    causal = _row((tq, tq)) <= _lane((tq, tq))

    def logits(rows, masked):
        sts = [_nt(k_ref[rows, LANES * h:LANES * (h + 1)], q_ref[:, LANES * h:LANES * (h + 1)]) for h in heads]
        return [jnp.where(causal, st, NEG) for st in sts] if masked else sts

    def values(rows):
        return [vt_ref[DV_PAD * h:DV_PAD * (h + 1), rows] for h in heads]

    if fixed:
        def sweep(n):
            accs = [jnp.zeros((DV_PAD, tq), F32)] * N_HEADS
            for start in range(0, n, 2):
                rows = pl.ds(start * tq, min(2, n - start) * tq)
                _, accs = _softmax_update(logits(rows, False), values(rows), None, accs, True)
            rows = pl.ds(n * tq, tq)
            _, accs = _softmax_update(logits(rows, True), values(rows), None, accs, True)
            for h in heads:
                acc_s[h] = accs[h]

        for n in range(nq):
            pl.when(i == n)(functools.partial(sweep, n))
    else:
        for h in heads:
            m_s[h] = jnp.full((1, tq), NEG, F32)
            acc_s[h] = jnp.zeros((DV_PAD, tq), F32)

        def step(j, width, masked):
            rows = pl.ds(j * tq if isinstance(j, int) else pl.multiple_of(j * tq, tq), width * tq)
            _softmax_steps(logits(rows, masked), values(rows), [m_s.at[h] for h in heads],
                           [acc_s.at[h] for h in heads])

        step(i, 1, True)

        def body(jj, carry):
            step(i - 2 * jj, 2, False)
            return carry

        lax.fori_loop(1, i // 2 + 1, body, 0)
        pl.when(i % 2 == 1)(functools.partial(step, 0, 1, False))
    yt = jnp.concatenate([acc_s[h, 0:HEAD_DIM, :] / acc_s[h, HEAD_DIM:HEAD_DIM + 1, :] for h in range(N_HEADS)],
                         axis=0)
    o_ref[...] = yt.T


def _mla_attention(qm, km, vmt, b, s, tq, fixed):
    t = b * s
    nq = s // tq
    w = N_HEADS * LANES
    return pl.pallas_call(
        functools.partial(_mla_attn_kernel, tq=tq, nq=nq, fixed=fixed),
        grid=(b, nq),
        in_specs=[
            pl.BlockSpec((tq, w), lambda bi, i: (bi * nq + i, 0)),
            pl.BlockSpec((s, w), lambda bi, i: (bi, 0)),
            pl.BlockSpec((N_HEADS * DV_PAD, s), lambda bi, i: (0, bi)),
        ],
        out_specs=pl.BlockSpec((tq, 2 * LANES), lambda bi, i: (bi * nq + i, 0)),
        out_shape=jax.ShapeDtypeStruct((t, GROUP_WIDTH), F32),
        scratch_shapes=[pltpu.VMEM((N_HEADS, 1, tq), F32), pltpu.VMEM((N_HEADS, DV_PAD, tq), F32)],
        compiler_params=_params(("arbitrary", "arbitrary")),
        name="mla_attention",
    )(qm, km, vmt)


def _sb_kernel(q_ref, k_ref, v_ref, o_ref, kb_s, vt_s, q_s, r_s, acc_s, kmax_s, *, tq, s_len):
    i = pl.program_id(1)
    gw = 2 * LANES
    head_shift = HEAD_DIM.bit_length() - 1
    same_head = (lax.shift_right_logical(_row((gw, gw)), head_shift)
                 == lax.shift_right_logical(_lane((gw, gw)), head_shift))

    @pl.when(i == 0)
    def _cast_kv():
        ch = 256

        def body(t, kmax):
            rows = pl.ds(pl.multiple_of(t * ch, ch), ch)
            k = k_ref[rows, :]
            kb_s[rows, :] = k.astype(BF16)
            vt_s[:, _lane_tile(t, ch)] = v_ref[rows, :].T.astype(BF16)
            return jnp.maximum(kmax, _group_sums(k * k, same_head))

        kmax = lax.fori_loop(0, s_len // ch, body, jnp.zeros((ch, gw), F32))
        for h in range(N_HEADS):
            kmax_s[h] = jnp.max(kmax[:, HEAD_DIM * h:HEAD_DIM * (h + 1)])

    key = _row((tq, tq))
    qry = _lane((tq, tq))
    strict = key < qry
    tri = jnp.where(key <= qry, 1.0, 0.0).astype(BF16)
    tri2 = jnp.concatenate([tri, tri], axis=1)
    lane = _lane((tq, LANES))
    q = q_ref[...] * (HEAD_DIM ** -0.5 * LOG2E)
    for h in range(N_HEADS):
        mine = (lane < HEAD_DIM) if h % 2 == 0 else (lane >= HEAD_DIM)
        q_s[h] = jnp.where(mine, q[:, LANES * (h // 2):LANES * (h // 2 + 1)], 0.0).astype(BF16)
        r_s[h] = jnp.zeros((1, tq), F32)
        acc_s[h] = jnp.zeros((HEAD_DIM, tq), F32)
    q_sq = _group_sums(q * q, same_head).T
    z_bound = [jnp.sqrt(q_sq[HEAD_DIM * h:HEAD_DIM * h + 1, :] * kmax_s[h]) * 1.01 + 1.0 for h in range(N_HEADS)]

    heads = range(N_HEADS)

    def step(j, width, masked):
        rs = [r_s[h] for h in heads]
        accs = [acc_s[h] for h in heads]
        tiles = [j + width - 1 - w for w in range(width)]
        rows = [pl.ds(t * tq if isinstance(t, int) else pl.multiple_of(t * tq, tq), tq) for t in tiles]
        zs = [[_nt(kb_s[r, LANES * (h // 2):LANES * (h // 2 + 1)], q_s[h]) for h in heads] for r in rows]
        part = []
        for w, zt in enumerate(zs):
            negabs = [pltpu.bitcast(pltpu.bitcast(z, jnp.uint32) | jnp.uint32(0x80000000), F32) for z in zt]
            sps = [jnp.maximum(z, 0.0) + jnp.log2(1.0 + jnp.exp2(na)) for z, na in zip(zt, negabs)]
            if masked and w == 0:
                sps = [jnp.where(strict, sp, 0.0) for sp in sps]
            his = [sp.astype(BF16) for sp in sps]
            los = [(sp - hi.astype(F32)).astype(BF16) for sp, hi in zip(sps, his)]
            part.append([_dot(tri2, jnp.concatenate([hi, lo], axis=0)) for hi, lo in zip(his, los)])
        for w, (r, zt, pt) in enumerate(zip(rows, zs, part)):
            csums = [p + rc for p, rc in zip(pt, rs)]
            als = [jnp.exp2(z - cs) for z, cs in zip(zt, csums)]
            if masked and w == 0:
                als = [jnp.where(strict, a, 0.0) for a in als]
            accs = [acc + _dot(vt_s[HEAD_DIM * h:HEAD_DIM * (h + 1), r], als[h].astype(BF16))
                    for h, acc in zip(heads, accs)]
            rs = [cs[0:1, :] for cs in csums]
        for h in heads:
            acc_s[h] = accs[h]
            r_s[h] = rs[h]

    def live():
        slack = [zb - r_s[h] for h, zb in enumerate(z_bound)]
        return jnp.max(jnp.maximum(jnp.maximum(slack[0], slack[1]), jnp.maximum(slack[2], slack[3]))) >= -150.0

    step(i, 1, True)
    pl.when(i >= 1)(functools.partial(step, i - 1, 1, False))

    n_rest = jnp.maximum(i - 1, 0)
    n_pairs = n_rest // 2

    def cond(carry):
        p, alive = carry
        return (p < n_pairs) & alive

    def body(carry):
        p, _ = carry
        step(i - 1 - 2 * (p + 1), 2, False)
        return p + 1, live()

    _, alive = lax.while_loop(cond, body, (jnp.int32(0), live()))
    pl.when((n_rest % 2 == 1) & alive)(functools.partial(step, 0, 1, False))
    o_ref[...] = jnp.concatenate([acc_s[h] for h in range(N_HEADS)], axis=0).T


def _sb_attention(proj, b, s, tq):
    t = b * s
    nq = s // tq
    w = 2 * LANES
    return pl.pallas_call(
        functools.partial(_sb_kernel, tq=tq, s_len=s),
        grid=(b, nq),
        in_specs=[
            pl.BlockSpec((tq, w), lambda bi, i: (bi * nq + i, CB_SQ)),
            pl.BlockSpec((s, w), lambda bi, i: (bi, CB_SK)),
            pl.BlockSpec((s, w), lambda bi, i: (bi, CB_SV)),
        ],
        out_specs=pl.BlockSpec((tq, w), lambda bi, i: (bi * nq + i, 0)),
        out_shape=jax.ShapeDtypeStruct((t, GROUP_WIDTH), F32),
        scratch_shapes=[pltpu.VMEM((s, w), BF16), pltpu.VMEM((w, s), BF16),
                        pltpu.VMEM((N_HEADS, tq, LANES), BF16),
                        pltpu.VMEM((N_HEADS, 1, tq), F32), pltpu.VMEM((N_HEADS, HEAD_DIM, tq), F32),
                        pltpu.SMEM((N_HEADS,), F32)],
        compiler_params=_params(("arbitrary", "arbitrary")),
        name="sb_attention",
    )(proj, proj, proj)


def _post_kernel(a_ref, ap_ref, yb_ref, yc_ref, yd_ref, x_ref, cw_ref, cb_ref, onw_ref, wo_ref, n2w_ref,
                 w1_ref, w2_ref, o_ref, *, tm, s_len, ffc):
    i = pl.program_id(0)
    a = a_ref[...]
    gw = GROUP_WIDTH
    v = a[:, gw:2 * gw] * a[:, 2 * gw:3 * gw]
    ap = ap_ref[...]
    first = (i * tm) % s_len == 0
    vp = jnp.where(first, 0.0, ap[:, gw:2 * gw] * ap[:, 2 * gw:3 * gw])
    row = _row(v.shape)
    v1 = jnp.where(row == 0, vp[7:8, :], pltpu.roll(v, 1, 0))
    v2 = jnp.where(row == 0, vp[6:7, :], jnp.where(row == 1, vp[7:8, :], pltpu.roll(v, 2, 0)))
    conv = cw_ref[0:1, :] * v2 + cw_ref[1:2, :] * v1 + cw_ref[2:3, :] * v
    ya = a[:, 0:gw] * (conv + cb_ref[...])

    mix = None
    for g, y in enumerate((ya, yb_ref[...], yc_ref[...], yd_ref[...])):
        ms = jnp.mean(y * y, axis=-1, keepdims=True)
        yn = (y * lax.rsqrt(ms + EPS) * onw_ref[:, gw * g:gw * (g + 1)]).astype(BF16)
        part = _dot(yn, wo_ref[gw * g:gw * (g + 1), :])
        mix = part if mix is None else mix + part
    x1 = x_ref[...] + mix

    ms = jnp.mean(x1 * x1, axis=-1, keepdims=True)
    h2 = (x1 * lax.rsqrt(ms + EPS) * n2w_ref[...]).astype(BF16)
    ff = None
    for cidx in range(D_FF // ffc):
        u = _dot(h2, w1_ref[:, ffc * cidx:ffc * (cidx + 1)])
        u = jnp.square(jnp.maximum(u, 0.0)).astype(BF16)
        part = _dot(u, w2_ref[ffc * cidx:ffc * (cidx + 1), :])
        ff = part if ff is None else ff + part
    o_ref[...] = x1 + ff


def _post(proj, yb, yc, yd, x2d, cw, cb, onw, wo, n2w, w1, w2, l, s, tm):
    t = x2d.shape[0]
    gw = GROUP_WIDTH
    row = lambda i: (i, 0)
    const = lambda i: (0, 0)
    layer = lambda i: (l, 0, 0)
    once = pl.Buffered(1)
    kern = functools.partial(_post_kernel, tm=tm, s_len=s, ffc=1024)
    return pl.pallas_call(
        kern,
        grid=(t // tm,),
        in_specs=[
            pl.BlockSpec((tm, 3 * gw), row),
            pl.BlockSpec((8, 3 * gw), lambda i: (jnp.maximum(i * (tm // 8) - 1, 0), 0)),
            pl.BlockSpec((tm, gw), row),
            pl.BlockSpec((tm, gw), row),
            pl.BlockSpec((tm, gw), row),
            pl.BlockSpec((tm, D_MODEL), row),
            pl.BlockSpec((3, gw), const),
            pl.BlockSpec((1, gw), const),
            pl.BlockSpec((1, D_MODEL), const),
            pl.BlockSpec((None, D_MODEL, D_MODEL), layer, pipeline_mode=once),
            pl.BlockSpec((1, D_MODEL), const),
            pl.BlockSpec((None, D_MODEL, D_FF), layer, pipeline_mode=once),
            pl.BlockSpec((None, D_FF, D_MODEL), layer, pipeline_mode=once),
        ],
        out_specs=pl.BlockSpec((tm, D_MODEL), row),
        out_shape=jax.ShapeDtypeStruct((t, D_MODEL), F32),
        compiler_params=_params(("arbitrary",)),
        name="post",
    )(proj, proj, yb, yc, yd, x2d, cw, cb, onw, wo, n2w, w1, w2)


def _t5_bucket(dist):
    max_exact = N_BUCKETS // 2
    d = np.maximum(dist, 0)
    large = max_exact + (np.log(np.maximum(d, 1) / max_exact) / math.log(MAX_DISTANCE / max_exact)
                         * (N_BUCKETS - max_exact)).astype(np.int32)
    return np.where(d < max_exact, d, np.minimum(large, N_BUCKETS - 1)).astype(np.int32)


def _tables(s, tq_nsa):
    n_cmp = (s - CMP_LEN) // CMP_STRIDE + 1
    ng = s // CMP_STRIDE
    n_slc = s // SLC_LEN
    tpos = np.arange(s)[None, :]
    n = np.arange(ng)[:, None]
    dist_c = tpos - (n * CMP_STRIDE + CMP_LEN - 1)
    bidxct = np.where((dist_c >= 0) & (n < n_cmp), _t5_bucket(dist_c), -1).astype(np.int32)
    key = np.arange(tq_nsa)[:, None]
    qry = np.arange(tq_nsa)[None, :]
    bidx2t = np.stack([_t5_bucket(qry - key), _t5_bucket(tq_nsa + qry - key)])
    starts = np.arange(n_cmp) * CMP_STRIDE
    ends = starts + CMP_LEN
    s0 = np.arange(n_slc) * SLC_LEN
    s1 = s0 + SLC_LEN
    ovl = np.clip(np.minimum(ends[:, None], s1[None]) - np.maximum(starts[:, None], s0[None]), 0, None) / CMP_LEN
    ovt = np.zeros((LANES, ng), np.float32)
    ovt[:n_slc, :n_cmp] = ovl.T
    emt = (np.arange(LANES)[None, :] == (np.arange(s) // SLC_LEN)[:, None]).astype(np.float32)
    inv = 1.0 / (ROPE_THETA ** (np.arange(0, ROPE_DIM, 2, dtype=np.float64) / ROPE_DIM))
    ang = (np.arange(s, dtype=np.float32)[:, None] * inv.astype(np.float32)[None, :]).astype(np.float64)
    cos, sin = np.cos(ang).astype(np.float32), np.sin(ang).astype(np.float32)
    z32 = np.zeros((s, 32), np.float32)
    z64 = np.zeros((s, 64), np.float32)
    one64 = np.ones((s, 64), np.float32)
    cq_t = np.concatenate([one64, cos, cos, z32], axis=1)
    ck_t = np.concatenate([z64, cos, cos, z32], axis=1)
    sk_t = np.concatenate([z64, -sin, sin, z32], axis=1)
    return dict(bidxct=jnp.asarray(bidxct), bidx2t=jnp.asarray(bidx2t), ovt=jnp.asarray(ovt, BF16),
                emt=jnp.asarray(emt, BF16), rope=(jnp.asarray(cq_t), jnp.asarray(ck_t), jnp.asarray(sk_t)))


def _shift_bound(d, q_gain, k_gain, extra=0.0):
    return (d * jnp.max(jnp.abs(q_gain)) * jnp.max(jnp.abs(k_gain)) * 1.01 + extra + 0.1).reshape(1)


MAX_SHIFT = 60.0


def _pad_cols(w, width):
    return jnp.pad(w, ((0, 0), (0, width - w.shape[1])))


def _layer_weights(l, conv_w, conv_b, nsa_q_norm, nsa_k_norm, cmp_pos, cmp_w1, cmp_w2, mla_q_a_norm,
                   mla_kv_norm, mla_wq_b, mla_wkv_b, mla_q_norm, mla_k_norm, out_norm_w, norm2_w):
    w1 = cmp_w1[l].reshape(2, CMP_LEN, HEAD_DIM, CMP_HIDDEN)
    zw = jnp.zeros((CMP_LEN, HEAD_DIM, CMP_HIDDEN), F32)
    cw1 = jnp.concatenate([jnp.concatenate([w1[0], zw], axis=2), jnp.concatenate([zw, w1[1]], axis=2)],
                          axis=1).astype(BF16)
    cw1 = cw1.reshape(2, CMP_STRIDE * LANES, 2 * CMP_HIDDEN)
    zc = jnp.zeros((CMP_HIDDEN, HEAD_DIM), F32)
    cw2 = jnp.concatenate([jnp.concatenate([cmp_w2[l, 0], zc], axis=1),
                           jnp.concatenate([zc, cmp_w2[l, 1]], axis=1)], axis=0).astype(BF16)
    cpos = jnp.concatenate([cmp_pos[l, 0], cmp_pos[l, 1]], axis=1).reshape(2, CMP_STRIDE * LANES)
    kn = nsa_k_norm[l]
    ones64 = jnp.ones((HEAD_DIM,), F32)
    knw_c = jnp.concatenate([kn[0], ones64])[None, :]
    knw_sw = jnp.stack([jnp.concatenate([kn[1], ones64]), jnp.concatenate([kn[2], ones64])])
    qnw = jnp.tile(nsa_q_norm[l], N_HEADS)[None, :]
    wq = mla_wq_b[l].reshape(Q_LORA, N_HEADS, QK_DIM)
    half = ROPE_DIM // 2
    wqs = jnp.concatenate([jnp.zeros((Q_LORA, N_HEADS, HEAD_DIM), F32), wq[:, :, HEAD_DIM + half:],
                           wq[:, :, HEAD_DIM:HEAD_DIM + half]], axis=2)
    pad_q = lambda w: jnp.pad(w, ((0, 2 * LANES - Q_LORA), (0, 0), (0, LANES - QK_DIM))).reshape(
        2 * LANES, N_HEADS * LANES).astype(BF16)
    wkv = mla_wkv_b[l].reshape(KV_LORA, N_HEADS, 2 * HEAD_DIM)
    wk = jnp.pad(wkv[:, :, :HEAD_DIM], ((0, 0), (0, 0), (0, LANES - HEAD_DIM))).reshape(KV_LORA, N_HEADS * LANES)
    wvt = jnp.pad(wkv[:, :, HEAD_DIM:], ((0, 0), (0, 0), (0, DV_PAD - HEAD_DIM))).reshape(KV_LORA, -1).T
    return dict(
        cw1=cw1, cw2=cw2, cw2t=cw2.T, cpos=cpos, knw_c=knw_c, knw_sw=knw_sw, qnw=qnw,
        qaw=_pad_cols(mla_q_a_norm[l][None, :], 2 * LANES), kvw=mla_kv_norm[l][None, :],
        wq=pad_q(wq), wqs=pad_q(wqs), wk=wk.astype(BF16), wvt=wvt.astype(BF16),
        mqn=_pad_cols(mla_q_norm[l][None, :] * (QK_DIM ** -0.5 * LOG2E), LANES),
        mkn=_pad_cols(mla_k_norm[l][None, :], LANES),
        cw=conv_w[l], cb=conv_b[l][None, :], onw=out_norm_w[l][None, :], n2w=norm2_w[l][None, :])


TM_PROJ = 1024
TM_PREP = 1024
TM_POST = 512
TQ_NSA = 256
TQ_MLA = 256
TQ_SB = 256


def kernel(x, rel_bias, norm1_w, w_in, conv_w, conv_b, nsa_q_norm, nsa_k_norm, cmp_pos, cmp_w1, cmp_w2,
           mla_q_a_norm, mla_kv_norm, mla_wq_b, mla_wkv_b, mla_q_norm, mla_k_norm, out_norm_w, w_out, norm2_w,
           ffn_w1, ffn_w2):
    b, s, d = x.shape
    depth = w_in.shape[0]
    tabs = _tables(s, TQ_NSA)
    x2d = x.reshape(b * s, d)
    wo_b, w1_b, w2_b = w_out.astype(BF16), ffn_w1.astype(BF16), ffn_w2.astype(BF16)
    for l in range(depth):
        w = _layer_weights(l, conv_w, conv_b, nsa_q_norm, nsa_k_norm, cmp_pos, cmp_w1, cmp_w2,
                           mla_q_a_norm, mla_kv_norm, mla_wq_b, mla_wkv_b, mla_q_norm, mla_k_norm, out_norm_w,
                           norm2_w)
        proj = _inproj(x2d, norm1_w[l][None, :], w_in, l, TM_PROJ)
        kcvc, kcvct = _compress(proj, w["cpos"], w["cw1"], w["cw2"], w["cw2t"], w["knw_c"], b, s)
        nsa_shift = _shift_bound(HEAD_DIM, w["qnw"] * (HEAD_DIM ** -0.5 * LOG2E), w["knw_sw"][:, :HEAD_DIM],
                                 jnp.max(jnp.abs(rel_bias)) * LOG2E)
        nsa_args = (proj, kcvc, kcvct, rel_bias, nsa_shift, tabs["bidxct"], tabs["bidx2t"], tabs["ovt"], tabs["emt"],
                    w["qnw"], w["knw_sw"])
        yb = lax.cond(nsa_shift[0] <= MAX_SHIFT,
                      lambda *a: _nsa_attention(*a, b, s, TQ_NSA, True),
                      lambda *a: _nsa_attention(*a, b, s, TQ_NSA, False), *nsa_args)
        mla_shift = _shift_bound(QK_DIM, w["mqn"], w["mkn"])
        qm, km, vmt = _mla_prep(proj, mla_shift, w["qaw"], w["kvw"], w["wq"], w["wqs"], w["wk"], w["wvt"], w["mqn"],
                                w["mkn"], tabs["rope"], s, TM_PREP)
        yc = lax.cond(mla_shift[0] <= MAX_SHIFT,
                      lambda *a: _mla_attention(*a, b, s, TQ_MLA, True),
                      lambda *a: _mla_attention(*a, b, s, TQ_MLA, False), qm, km, vmt)
        yd = _sb_attention(proj, b, s, TQ_SB)
        x2d = _post(proj, yb, yc, yd, x2d, w["cw"], w["cb"], w["onw"], wo_b, w["n2w"], w1_b, w2_b, l, s, TM_POST)
    return x2d.reshape(b, s, d)
```

```python
import functools
import math

import jax
import jax.numpy as jnp
import numpy as np
from jax import lax
from jax.experimental import pallas as pl
from jax.experimental.pallas import tpu as pltpu

F32 = jnp.float32
BF16 = jnp.bfloat16

D_MODEL = 1024
GROUP_WIDTH = 256
HEAD_DIM = 64
N_HEADS = 4
LANES = 128
CMP_LEN = 32
CMP_STRIDE = 16
SLC_LEN = 64
N_SEL = 16
WINDOW = 512
CMP_HIDDEN = 256
Q_LORA = 192
KV_LORA = 128
ROPE_DIM = 32
QK_DIM = 96
ROPE_THETA = 10000.0
N_BUCKETS = 32
MAX_DISTANCE = 128
D_FF = 4096
EPS = 1e-6
NEG = -1e30
LOG2E = math.log2(math.e)
DV_PAD = 80

NP = 2816
CB_NQ = 3
CB_KCVC = 8
CB_KSVS = 9
CB_KWVW = 10
CB_GATE = 11
CB_CQ = 6
CB_CKV = 14
CB_KR = 15
CB_SQ = 8
CB_SK = 9
CB_SV = 10

VMEM_LIMIT = 56 * 1024 * 1024

NT_DIMS = (((1,), (1,)), ((), ()))


def _params(sem):
    return pltpu.CompilerParams(dimension_semantics=sem, vmem_limit_bytes=VMEM_LIMIT)


def _nt(a, b):
    return lax.dot_general(a, b, NT_DIMS, preferred_element_type=F32)


def _dot(a, b):
    return jnp.dot(a, b, preferred_element_type=F32)


def _lane(shape):
    return lax.broadcasted_iota(jnp.int32, shape, len(shape) - 1)


def _row(shape):
    return lax.broadcasted_iota(jnp.int32, shape, len(shape) - 2)


def _lane_tile(j, width):
    return pl.ds(pl.multiple_of(j * width, width), width)


_IN_SEGMENTS = (
    ((0, 1408), 0),
    ((1408, 1420), 1408),
    ((1420, 1612), 1536),
    ((1612, 1740), 1792),
    ((1756, 1772), 1920),
    ((1740, 1756), 1936),
    ((1740, 1772), 1984),
    ((1772, 2540), 2048),
)
IN_COLS = 2540


def _inproj_kernel(x_ref, nw_ref, w_ref, o_ref, w_s):
    @pl.when(pl.program_id(0) == 0)
    def _relayout_weights():
        end = 0
        for (a, b), dst in _IN_SEGMENTS:
            if dst > end:
                w_s[:, end:dst] = jnp.zeros((D_MODEL, dst - end), BF16)
            w_s[:, dst:dst + b - a] = w_ref[:, a:b].astype(BF16)
            end = dst + b - a
        assert end == NP

    x = x_ref[...]
    ms = jnp.mean(x * x, axis=-1, keepdims=True)
    h = (x * lax.rsqrt(ms + EPS) * nw_ref[...]).astype(BF16)
    o_ref[...] = _dot(h, w_s[...])


def _inproj(x2d, nw, w_in, l, tm):
    t = x2d.shape[0]
    return pl.pallas_call(
        _inproj_kernel,
        grid=(t // tm,),
        in_specs=[
            pl.BlockSpec((tm, D_MODEL), lambda i: (i, 0)),
            pl.BlockSpec((1, D_MODEL), lambda i: (0, 0)),
            pl.BlockSpec((None, D_MODEL, IN_COLS), lambda i: (l, 0, 0), pipeline_mode=pl.Buffered(1)),
        ],
        out_specs=pl.BlockSpec((tm, NP), lambda i: (i, 0)),
        out_shape=jax.ShapeDtypeStruct((t, NP), F32),
        scratch_shapes=[pltpu.VMEM((D_MODEL, NP), BF16)],
        compiler_params=_params(("arbitrary",)),
        name="inproj",
    )(x2d, nw, w_in)


def _compress_kernel(x_ref, pos_ref, w1_ref, w2_ref, w2t_ref, knw_ref, o_ref, ot_ref):
    ng = x_ref.shape[1]
    x = x_ref[0]
    acc_a = _dot((x + pos_ref[0:1, :]).astype(BF16), w1_ref[0])
    acc_b = _dot((x + pos_ref[1:2, :]).astype(BF16), w1_ref[1])
    pre = acc_a + pltpu.roll(acc_b, ng - 1, 0)
    hdn = (pre * jax.nn.sigmoid(pre)).astype(BF16)
    out = _dot(hdn, w2_ref[...])
    lane = _lane(out.shape)
    is_k = lane < HEAD_DIM
    ss = jnp.sum(jnp.where(is_k, out * out, 0.0), axis=-1, keepdims=True) * (1.0 / HEAD_DIM)
    o_ref[0] = jnp.where(is_k, out * lax.rsqrt(ss + EPS) * knw_ref[...], out)
    ot_ref[0] = _nt(w2t_ref[...], hdn)


def _compress(proj, pos, w1, w2, w2t, knw, b, s):
    ng = s // CMP_STRIDE
    gl = CMP_STRIDE * LANES
    xg = proj[:, CB_KCVC * LANES:(CB_KCVC + 1) * LANES].reshape(b, ng, gl)
    return pl.pallas_call(
        _compress_kernel,
        grid=(b,),
        in_specs=[
            pl.BlockSpec((1, ng, gl), lambda i: (i, 0, 0)),
            pl.BlockSpec((2, gl), lambda i: (0, 0)),
            pl.BlockSpec((2, gl, 2 * CMP_HIDDEN), lambda i: (0, 0, 0)),
            pl.BlockSpec((2 * CMP_HIDDEN, LANES), lambda i: (0, 0)),
            pl.BlockSpec((LANES, 2 * CMP_HIDDEN), lambda i: (0, 0)),
            pl.BlockSpec((1, LANES), lambda i: (0, 0)),
        ],
        out_specs=[pl.BlockSpec((1, ng, LANES), lambda i: (i, 0, 0)),
                   pl.BlockSpec((1, LANES, ng), lambda i: (i, 0, 0))],
        out_shape=[jax.ShapeDtypeStruct((b, ng, LANES), F32), jax.ShapeDtypeStruct((b, LANES, ng), F32)],
        compiler_params=_params(("arbitrary",)),
        name="nsa_compress",
    )(xg, pos, w1, w2, w2t, knw)


def _softmax_update(sts, vts, m_old, acc_old, fixed=False):
    if fixed:
        return m_old, [acc + _dot(vt, jnp.exp2(st).astype(BF16)) for acc, vt, st in zip(acc_old, vts, sts)]
    m_new = [jnp.maximum(m, jnp.max(st, axis=0, keepdims=True)) for m, st in zip(m_old, sts)]
    ps = [jnp.exp2(st - m).astype(BF16) for st, m in zip(sts, m_new)]
    alphas = [jnp.exp2(mo - mn) for mo, mn in zip(m_old, m_new)]
    acc_new = [al * acc + _dot(vt, p) for al, acc, vt, p in zip(alphas, acc_old, vts, ps)]
    return m_new, acc_new


def _softmax_steps(sts, vts, m_refs, acc_refs, fixed=False):
    m_old = [None] * len(sts) if fixed else [r[...] for r in m_refs]
    m_new, acc_new = _softmax_update(sts, vts, m_old, [r[...] for r in acc_refs], fixed)
    for r, v in zip(() if fixed else m_refs, m_new):
        r[...] = v
    for r, v in zip(acc_refs, acc_new):
        r[...] = v


def _with_ones_row(vt):
    pad = jnp.where(_row((DV_PAD - HEAD_DIM, vt.shape[1])) == 0, 1.0, 0.0).astype(vt.dtype)
    return jnp.concatenate([vt, pad], axis=0)


def _bucket_bias(bidx, relb_ref, h, fill):
    acc = jnp.full(bidx.shape, fill, F32)
    for bk in range(N_BUCKETS):
        acc = jnp.where(bidx == bk, relb_ref[bk, h] * LOG2E, acc)
    return acc


def _group_sums(x, member):
    g = jnp.where(member, 1.0, 0.0).astype(BF16)
    hi = x.astype(BF16)
    lo = (x - hi.astype(F32)).astype(BF16)
    return _dot(hi, g) + _dot(lo, g)


def _dup_low_half(x):
    y = jnp.where(_lane(x.shape) < HEAD_DIM, x, 0.0)
    return y + pltpu.roll(y, HEAD_DIM, 1)


def _nsa_kernel(relb_ref, shift_ref, q_ref, g_ref, kcvc_ref, kcvct_ref, ksvs_ref, kwvw_ref, bidxct_ref, bidx2t_ref, ovt_ref,
                emt_ref, qnw_ref, knw_ref, o_ref,
                biasc_s, bias2_s, ks_s, vst_s, kw_s, vwt_s, kc_s, vct_s, qx_s, m_s, acc_s, *, tq, s_len, fixed):
    b = pl.program_id(0)
    shift = shift_ref[0] if fixed else 0.0
    i = pl.program_id(1)
    n_win = WINDOW // tq
    ng = s_len // CMP_STRIDE

    @pl.when((b == 0) & (i == 0))
    def _build_bias_tables():
        key = _row((tq, tq))
        qry = _lane((tq, tq))
        for h in range(N_HEADS):
            cols = slice(h * tq, (h + 1) * tq)
            far = jnp.full((tq, tq), relb_ref[N_BUCKETS - 1, h] * LOG2E - shift, F32)
            bias2_s[0, :, cols] = jnp.where(key <= qry, _bucket_bias(bidx2t_ref[0], relb_ref, h, NEG) - shift, NEG)
            bias2_s[1, :, cols] = _bucket_bias(bidx2t_ref[1], relb_ref, h, NEG) - shift
            bias2_s[2, :, cols] = far
            bias2_s[3, :, cols] = jnp.where(key > qry, far, NEG)
            bias2_s[4, :, cols] = jnp.full((tq, tq), NEG, F32)

        def body(t, carry):
            bi = bidxct_ref[:, _lane_tile(t, tq)]
            for h in range(N_HEADS):
                biasc_s[t, :, h * tq:(h + 1) * tq] = _bucket_bias(bi, relb_ref, h, NEG)
            return carry

        lax.fori_loop(0, s_len // tq, body, 0)

    @pl.when(i == 0)
    def _prep_kv():
        ch = 256

        def body(t, carry):
            rows = pl.ds(pl.multiple_of(t * ch, ch), ch)
            for src, kdst, vdst, widx in ((ksvs_ref, ks_s, vst_s, 0), (kwvw_ref, kw_s, vwt_s, 1)):
                x = src[rows, :]
                ss = _group_sums(x * x, _row((LANES, LANES)) < HEAD_DIM) * (1.0 / HEAD_DIM)
                kn = x * lax.rsqrt(ss + EPS) * knw_ref[widx:widx + 1, :]
                kdst[rows, 0:LANES] = _dup_low_half(kn).astype(BF16)
                vdst[:, _lane_tile(t, ch)] = _with_ones_row(x.T[HEAD_DIM:, :]).astype(BF16)
            ks_s[rows, LANES:] = emt_ref[rows, :]
            return carry

        lax.fori_loop(0, s_len // ch, body, 0)
        kc_s[...] = _dup_low_half(kcvc_ref[0]).astype(BF16)
        vct_s[...] = kcvct_ref[0][HEAD_DIM:, :].astype(BF16)

    q = q_ref[...]
    lane = _lane((tq, LANES))
    heads = range(N_HEADS)
    gw = 2 * LANES
    head_shift = HEAD_DIM.bit_length() - 1
    same_head = (lax.shift_right_logical(_row((gw, gw)), head_shift)
                 == lax.shift_right_logical(_lane((gw, gw)), head_shift))
    ss = _group_sums(q * q, same_head) * (1.0 / HEAD_DIM)
    qn = q * lax.rsqrt(ss + EPS) * qnw_ref[...] * (HEAD_DIM ** -0.5 * LOG2E)
    for h in heads:
        mine = (lane < HEAD_DIM) if h % 2 == 0 else (lane >= HEAD_DIM)
        qx_s[h, :, 0:LANES] = jnp.where(mine, qn[:, LANES * (h // 2):LANES * (h // 2 + 1)], 0.0).astype(BF16)

    lcs = [_nt(kc_s[...], qx_s[h, :, 0:LANES]) + biasc_s[i, :, h * tq:(h + 1) * tq] for h in heads]
    pcs = [jnp.where(lc > 0.5 * NEG, jnp.exp2(lc - jnp.max(lc, axis=0, keepdims=True)), 0.0) for lc in lcs]
    dens = [jnp.sum(pc, axis=0, keepdims=True) for pc in pcs]
    pcs = [pc / jnp.where(den > 0.0, den, 1.0) for pc, den in zip(pcs, dens)]
    o_cmp = [_dot(vct_s[...], pc.astype(BF16)) for pc in pcs]
    psum = (pcs[0] + pcs[1]) + (pcs[2] + pcs[3])

    n_slc = s_len // SLC_LEN
    n_sel = min(N_SEL, n_slc)
    blk = _row((n_slc, tq))
    tpos = i * tq + _lane((n_slc, tq))
    tblk = tpos // SLC_LEN
    valid = blk * SLC_LEN <= tpos

    p_hi = psum.astype(BF16)
    p_lo = (psum - p_hi.astype(F32)).astype(BF16)
    score = _dot(ovt_ref[...], p_hi) + _dot(ovt_ref[...], p_lo)
    forced = (blk == 0) | (blk == tblk) | (blk == tblk - 1)
    sc = jnp.where(forced, jnp.inf, jnp.where(valid, score[0:n_slc], -jnp.inf))
    rank = jnp.zeros((n_slc, tq), F32)
    for k in range(n_slc):
        ck = sc[k:k + 1, :]
        beats = (ck > sc) | ((ck == sc) & (blk > k))
        rank += jnp.where(beats, 1.0, 0.0)
    pen = jnp.where((rank < float(n_sel)) & valid, 0.0, NEG)
    pen = jnp.concatenate([pen, jnp.zeros((LANES - n_slc, tq), F32)], axis=0)
    pen_t = pen.T.astype(BF16)
    pen = pen.astype(BF16)
    for h in heads:
        qx_s[h, :, LANES:] = pen_t

    ms = [jnp.full((1, tq), NEG, F32)] * (2 * N_HEADS)
    accs = [jnp.zeros((DV_PAD, tq), F32)] * (2 * N_HEADS)
    for jj in range(n_win + 1):
        exists = i >= jj
        rows = pl.ds(pl.multiple_of(jnp.maximum(i - jj, 0) * tq, tq), tq)
        kind = jnp.where(exists, min(jj, 2), 4)
        kind_w = jnp.where(exists, 3 if jj == n_win else min(jj, 2), 4)
        ks = ks_s[rows, 0:LANES]
        kw = kw_s[rows, :]
        masked = _dot(ks_s[rows, LANES:], pen)
        sts = [_nt(ks, qx_s[h, :, 0:LANES]) + masked + bias2_s[kind, :, h * tq:(h + 1) * tq] for h in heads]
        sts += [_nt(kw, qx_s[h, :, 0:LANES]) + bias2_s[kind_w, :, h * tq:(h + 1) * tq] for h in heads]
        vts = [vst_s[:, rows]] * N_HEADS + [vwt_s[:, rows]] * N_HEADS
        ms, accs = _softmax_update(sts, vts, ms, accs, fixed)
    for h in heads:
        if not fixed:
            m_s[h] = ms[h]
        acc_s[h] = accs[h]
    o_win = [accs[N_HEADS + h][0:HEAD_DIM, :] / accs[N_HEADS + h][HEAD_DIM:HEAD_DIM + 1, :] for h in heads]

    def far_step(j, width):
        rows = pl.ds(j * tq if isinstance(j, int) else pl.multiple_of(j * tq, tq), width * tq)
        ks = ks_s[rows, :]
        sts = [_nt(ks, qx_s[h]) + (relb_ref[N_BUCKETS - 1, h] * LOG2E - shift) for h in heads]
        _softmax_steps(sts, [vst_s[:, rows]] * N_HEADS, [m_s.at[h] for h in heads], [acc_s.at[h] for h in heads],
                       fixed)

    n_far = jnp.maximum(i - n_win, 0)
    if fixed:
        def far_sweep(n):
            accs = [acc_s[h] for h in heads]
            for start in range(0, n, 2):
                rows = pl.ds(start * tq, min(2, n - start) * tq)
                ks = ks_s[rows, :]
                sts = [_nt(ks, qx_s[h]) + (relb_ref[N_BUCKETS - 1, h] * LOG2E - shift) for h in heads]
                _, accs = _softmax_update(sts, [vst_s[:, rows]] * N_HEADS, None, accs, True)
            for h in heads:
                acc_s[h] = accs[h]

        for n in range(1, s_len // tq - n_win):
            pl.when(n_far == n)(functools.partial(far_sweep, n))
    else:
        def far_body(p, carry):
            far_step(n_far - 2 * (p + 1), 2)
            return carry

        lax.fori_loop(0, n_far // 2, far_body, 0)
        pl.when(n_far % 2 == 1)(functools.partial(far_step, 0, 1))

    gt = jax.nn.sigmoid(g_ref[...]).T
    ys = []
    for h in heads:
        o_slc = acc_s[h, 0:HEAD_DIM, :] / acc_s[h, HEAD_DIM:HEAD_DIM + 1, :]
        ys.append(gt[3 * h:3 * h + 1, :] * o_cmp[h] + gt[3 * h + 1:3 * h + 2, :] * o_slc
                  + gt[3 * h + 2:3 * h + 3, :] * o_win[h])
    o_ref[...] = jnp.concatenate(ys, axis=0).T


def _nsa_attention(proj, kcvc, kcvct, rel_bias, shift, bidxct, bidx2t, ovt, emt, qnw, knw, b, s, tq, fixed):
    t = b * s
    nq = s // tq
    m_rows = N_HEADS * tq
    ng = s // CMP_STRIDE
    kern = functools.partial(_nsa_kernel, tq=tq, s_len=s, fixed=fixed)
    return pl.pallas_call(
        kern,
        grid=(b, nq),
        in_specs=[
            pl.BlockSpec(memory_space=pltpu.SMEM),
            pl.BlockSpec(memory_space=pltpu.SMEM),
            pl.BlockSpec((tq, 2 * LANES), lambda bi, i: (bi * nq + i, CB_NQ)),
            pl.BlockSpec((tq, LANES), lambda bi, i: (bi * nq + i, CB_GATE)),
            pl.BlockSpec((1, ng, LANES), lambda bi, i: (bi, 0, 0)),
            pl.BlockSpec((1, LANES, ng), lambda bi, i: (bi, 0, 0)),
            pl.BlockSpec((s, LANES), lambda bi, i: (bi, CB_KSVS)),
            pl.BlockSpec((s, LANES), lambda bi, i: (bi, CB_KWVW)),
            pl.BlockSpec((ng, s), lambda bi, i: (0, 0)),
            pl.BlockSpec((2, tq, tq), lambda bi, i: (0, 0, 0)),
            pl.BlockSpec((LANES, ng), lambda bi, i: (0, 0)),
            pl.BlockSpec((s, LANES), lambda bi, i: (0, 0)),
            pl.BlockSpec((1, 2 * LANES), lambda bi, i: (0, 0)),
            pl.BlockSpec((2, LANES), lambda bi, i: (0, 0)),
        ],
        out_specs=pl.BlockSpec((tq, 2 * LANES), lambda bi, i: (bi * nq + i, 0)),
        out_shape=jax.ShapeDtypeStruct((t, GROUP_WIDTH), F32),
        scratch_shapes=[
            pltpu.VMEM((nq, ng, m_rows), F32),
            pltpu.VMEM((5, tq, m_rows), F32),
            pltpu.VMEM((s, 2 * LANES), BF16),
            pltpu.VMEM((DV_PAD, s), BF16),
            pltpu.VMEM((s, LANES), BF16),
            pltpu.VMEM((DV_PAD, s), BF16),
            pltpu.VMEM((ng, LANES), BF16),
            pltpu.VMEM((HEAD_DIM, ng), BF16),
            pltpu.VMEM((N_HEADS, tq, 2 * LANES), BF16),
            pltpu.VMEM((N_HEADS, 1, tq), F32),
            pltpu.VMEM((N_HEADS, DV_PAD, tq), F32),
        ],
        compiler_params=_params(("arbitrary", "arbitrary")),
        name="nsa_attention",
    )(rel_bias, shift, proj, proj, kcvc, kcvct, proj, proj, bidxct, bidx2t, ovt, emt, qnw, knw)


def _mla_prep_kernel(shift_ref, cq_ref, ckv_ref, kr_ref, qaw_ref, kvw_ref, wq_ref, wqs_ref, wk_ref, wvt_ref, qnw_ref, knw_ref,
                     cq_t_ref, ck_t_ref, sk_t_ref, qo_ref, ko_ref, vto_ref):
    cq = cq_ref[...]
    ms = jnp.sum(cq * cq, axis=-1, keepdims=True) * (1.0 / Q_LORA)
    hq = (cq * lax.rsqrt(ms + EPS) * qaw_ref[...]).astype(BF16)
    qf = _dot(hq, wq_ref[...])
    qsw = _dot(hq, wqs_ref[...])
    ckv = ckv_ref[...]
    ms = jnp.mean(ckv * ckv, axis=-1, keepdims=True)
    hkv = (ckv * lax.rsqrt(ms + EPS) * kvw_ref[...]).astype(BF16)
    kf = _dot(hkv, wk_ref[...])
    vt = _nt(wvt_ref[...], hkv)
    vto_ref[...] = jnp.where(_row(vt.shape) % DV_PAD == HEAD_DIM, 1.0, vt).astype(BF16)
    krb = kr_ref[...]
    kr_rot = krb * ck_t_ref[...] + pltpu.roll(krb, HEAD_DIM, 1) * sk_t_ref[...]
    is_shift = _lane((cq.shape[0], LANES)) == QK_DIM
    for h in range(N_HEADS):
        cols = slice(LANES * h, LANES * (h + 1))
        x = qf[:, cols] * cq_t_ref[...] + qsw[:, cols] * sk_t_ref[...]
        ss = jnp.sum(x * x, axis=-1, keepdims=True) * (1.0 / QK_DIM)
        qn = x * lax.rsqrt(ss + EPS) * qnw_ref[...]
        qo_ref[:, cols] = jnp.where(is_shift, -shift_ref[0], qn).astype(BF16)
        k = kf[:, cols] + kr_rot
        ss = jnp.sum(k * k, axis=-1, keepdims=True) * (1.0 / QK_DIM)
        ko_ref[:, cols] = jnp.where(is_shift, 1.0, k * lax.rsqrt(ss + EPS) * knw_ref[...]).astype(BF16)


def _mla_prep(proj, shift, qaw, kvw, wq, wqs, wk, wvt, qnw, knw, tabs, s, tm):
    t = proj.shape[0]
    npos = s // tm
    row = lambda i: (i, 0)
    const = lambda i: (0, 0)
    tab = pl.BlockSpec((tm, LANES), lambda i: (i % npos, 0))
    out = jax.ShapeDtypeStruct((t, N_HEADS * LANES), BF16)
    return pl.pallas_call(
        _mla_prep_kernel,
        grid=(t // tm,),
        in_specs=[
            pl.BlockSpec(memory_space=pltpu.SMEM),
            pl.BlockSpec((tm, 2 * LANES), lambda i: (i, CB_CQ)),
            pl.BlockSpec((tm, LANES), lambda i: (i, CB_CKV)),
            pl.BlockSpec((tm, LANES), lambda i: (i, CB_KR)),
            pl.BlockSpec((1, 2 * LANES), const),
            pl.BlockSpec((1, LANES), const),
            pl.BlockSpec((2 * LANES, N_HEADS * LANES), const),
            pl.BlockSpec((2 * LANES, N_HEADS * LANES), const),
            pl.BlockSpec((LANES, N_HEADS * LANES), const),
            pl.BlockSpec((N_HEADS * DV_PAD, LANES), const),
            pl.BlockSpec((1, LANES), const),
            pl.BlockSpec((1, LANES), const),
            tab, tab, tab,
        ],
        out_specs=[pl.BlockSpec((tm, N_HEADS * LANES), row), pl.BlockSpec((tm, N_HEADS * LANES), row),
                   pl.BlockSpec((N_HEADS * DV_PAD, tm), lambda i: (0, i))],
        out_shape=[out, out, jax.ShapeDtypeStruct((N_HEADS * DV_PAD, t), BF16)],
        compiler_params=_params(("arbitrary",)),
        name="mla_prep",
    )(shift, proj, proj, proj, qaw, kvw, wq, wqs, wk, wvt, qnw, knw, *tabs)


def _mla_attn_kernel(q_ref, k_ref, vt_ref, o_ref, m_s, acc_s, *, tq, nq, fixed):
    i = pl.program_id(1)
    heads = range(N_HEADS)
    causal = _row((tq, tq)) <= _lane((tq, tq))

    def logits(rows, masked):
        sts = [_nt(k_ref[rows, LANES * h:LANES * (h + 1)], q_ref[:, LANES * h:LANES * (h + 1)]) for h in heads]
        return [jnp.where(causal, st, NEG) for st in sts] if masked else sts

    def values(rows):
        return [vt_ref[DV_PAD * h:DV_PAD * (h + 1), rows] for h in heads]

    if fixed:
        def sweep(n):
            accs = [jnp.zeros((DV_PAD, tq), F32)] * N_HEADS
            for start in range(0, n, 2):
                rows = pl.ds(start * tq, min(2, n - start) * tq)
                _, accs = _softmax_update(logits(rows, False), values(rows), None, accs, True)
            rows = pl.ds(n * tq, tq)
            _, accs = _softmax_update(logits(rows, True), values(rows), None, accs, True)
            for h in heads:
                acc_s[h] = accs[h]

        for n in range(nq):
            pl.when(i == n)(functools.partial(sweep, n))
    else:
        for h in heads:
            m_s[h] = jnp.full((1, tq), NEG, F32)
            acc_s[h] = jnp.zeros((DV_PAD, tq), F32)

        def step(j, width, masked):
            rows = pl.ds(j * tq if isinstance(j, int) else pl.multiple_of(j * tq, tq), width * tq)
            _softmax_steps(logits(rows, masked), values(rows), [m_s.at[h] for h in heads],
                           [acc_s.at[h] for h in heads])

        step(i, 1, True)

        def body(jj, carry):
            step(i - 2 * jj, 2, False)
            return carry

        lax.fori_loop(1, i // 2 + 1, body, 0)
        pl.when(i % 2 == 1)(functools.partial(step, 0, 1, False))
    yt = jnp.concatenate([acc_s[h, 0:HEAD_DIM, :] / acc_s[h, HEAD_DIM:HEAD_DIM + 1, :] for h in range(N_HEADS)],
                         axis=0)
    o_ref[...] = yt.T


def _mla_attention(qm, km, vmt, b, s, tq, fixed):
    t = b * s
    nq = s // tq
    w = N_HEADS * LANES
    return pl.pallas_call(
        functools.partial(_mla_attn_kernel, tq=tq, nq=nq, fixed=fixed),
        grid=(b, nq),
        in_specs=[
            pl.BlockSpec((tq, w), lambda bi, i: (bi * nq + i, 0)),
            pl.BlockSpec((s, w), lambda bi, i: (bi, 0)),
            pl.BlockSpec((N_HEADS * DV_PAD, s), lambda bi, i: (0, bi)),
        ],
        out_specs=pl.BlockSpec((tq, 2 * LANES), lambda bi, i: (bi * nq + i, 0)),
        out_shape=jax.ShapeDtypeStruct((t, GROUP_WIDTH), F32),
        scratch_shapes=[pltpu.VMEM((N_HEADS, 1, tq), F32), pltpu.VMEM((N_HEADS, DV_PAD, tq), F32)],
        compiler_params=_params(("arbitrary", "arbitrary")),
        name="mla_attention",
    )(qm, km, vmt)


def _sb_kernel(q_ref, k_ref, v_ref, o_ref, kb_s, vt_s, q_s, r_s, acc_s, kmax_s, *, tq, s_len):
    i = pl.program_id(1)
    gw = 2 * LANES
    head_shift = HEAD_DIM.bit_length() - 1
    same_head = (lax.shift_right_logical(_row((gw, gw)), head_shift)
                 == lax.shift_right_logical(_lane((gw, gw)), head_shift))

    @pl.when(i == 0)
    def _cast_kv():
        ch = 256

        def body(t, kmax):
            rows = pl.ds(pl.multiple_of(t * ch, ch), ch)
            k = k_ref[rows, :]
            kb_s[rows, :] = k.astype(BF16)
            vt_s[:, _lane_tile(t, ch)] = v_ref[rows, :].T.astype(BF16)
            return jnp.maximum(kmax, _group_sums(k * k, same_head))

        kmax = lax.fori_loop(0, s_len // ch, body, jnp.zeros((ch, gw), F32))
        for h in range(N_HEADS):
            kmax_s[h] = jnp.max(kmax[:, HEAD_DIM * h:HEAD_DIM * (h + 1)])

    key = _row((tq, tq))
    qry = _lane((tq, tq))
    strict = key < qry
    tri = jnp.where(key <= qry, 1.0, 0.0).astype(BF16)
    tri2 = jnp.concatenate([tri, tri], axis=1)
    lane = _lane((tq, LANES))
    q = q_ref[...] * (HEAD_DIM ** -0.5 * LOG2E)
    for h in range(N_HEADS):
        mine = (lane < HEAD_DIM) if h % 2 == 0 else (lane >= HEAD_DIM)
        q_s[h] = jnp.where(mine, q[:, LANES * (h // 2):LANES * (h // 2 + 1)], 0.0).astype(BF16)
        r_s[h] = jnp.zeros((1, tq), F32)
        acc_s[h] = jnp.zeros((HEAD_DIM, tq), F32)
    q_sq = _group_sums(q * q, same_head).T
    z_bound = [jnp.sqrt(q_sq[HEAD_DIM * h:HEAD_DIM * h + 1, :] * kmax_s[h]) * 1.01 + 1.0 for h in range(N_HEADS)]

    heads = range(N_HEADS)

    def step(j, width, masked):
        rs = [r_s[h] for h in heads]
        accs = [acc_s[h] for h in heads]
        tiles = [j + width - 1 - w for w in range(width)]
        rows = [pl.ds(t * tq if isinstance(t, int) else pl.multiple_of(t * tq, tq), tq) for t in tiles]
        zs = [[_nt(kb_s[r, LANES * (h // 2):LANES * (h // 2 + 1)], q_s[h]) for h in heads] for r in rows]
        part = []
        for w, zt in enumerate(zs):
            negabs = [pltpu.bitcast(pltpu.bitcast(z, jnp.uint32) | jnp.uint32(0x80000000), F32) for z in zt]
            sps = [jnp.maximum(z, 0.0) + jnp.log2(1.0 + jnp.exp2(na)) for z, na in zip(zt, negabs)]
            if masked and w == 0:
                sps = [jnp.where(strict, sp, 0.0) for sp in sps]
            his = [sp.astype(BF16) for sp in sps]
            los = [(sp - hi.astype(F32)).astype(BF16) for sp, hi in zip(sps, his)]
            part.append([_dot(tri2, jnp.concatenate([hi, lo], axis=0)) for hi, lo in zip(his, los)])
        for w, (r, zt, pt) in enumerate(zip(rows, zs, part)):
            csums = [p + rc for p, rc in zip(pt, rs)]
            als = [jnp.exp2(z - cs) for z, cs in zip(zt, csums)]
            if masked and w == 0:
                als = [jnp.where(strict, a, 0.0) for a in als]
            accs = [acc + _dot(vt_s[HEAD_DIM * h:HEAD_DIM * (h + 1), r], als[h].astype(BF16))
                    for h, acc in zip(heads, accs)]
            rs = [cs[0:1, :] for cs in csums]
        for h in heads:
            acc_s[h] = accs[h]
            r_s[h] = rs[h]

    def live():
        slack = [zb - r_s[h] for h, zb in enumerate(z_bound)]
        return jnp.max(jnp.maximum(jnp.maximum(slack[0], slack[1]), jnp.maximum(slack[2], slack[3]))) >= -150.0

    step(i, 1, True)
    pl.when(i >= 1)(functools.partial(step, i - 1, 1, False))

    n_rest = jnp.maximum(i - 1, 0)

    def cond(carry):
        p, alive = carry
        return (p < n_rest) & alive

    def body(carry):
        p, _ = carry
        step(i - 2 - p, 1, False)
        return p + 1, live()

    lax.while_loop(cond, body, (jnp.int32(0), live()))
    o_ref[...] = jnp.concatenate([acc_s[h] for h in range(N_HEADS)], axis=0).T


def _sb_attention(proj, b, s, tq):
    t = b * s
    nq = s // tq
    w = 2 * LANES
    return pl.pallas_call(
        functools.partial(_sb_kernel, tq=tq, s_len=s),
        grid=(b, nq),
        in_specs=[
            pl.BlockSpec((tq, w), lambda bi, i: (bi * nq + i, CB_SQ)),
            pl.BlockSpec((s, w), lambda bi, i: (bi, CB_SK)),
            pl.BlockSpec((s, w), lambda bi, i: (bi, CB_SV)),
        ],
        out_specs=pl.BlockSpec((tq, w), lambda bi, i: (bi * nq + i, 0)),
        out_shape=jax.ShapeDtypeStruct((t, GROUP_WIDTH), F32),
        scratch_shapes=[pltpu.VMEM((s, w), BF16), pltpu.VMEM((w, s), BF16),
                        pltpu.VMEM((N_HEADS, tq, LANES), BF16),
                        pltpu.VMEM((N_HEADS, 1, tq), F32), pltpu.VMEM((N_HEADS, HEAD_DIM, tq), F32),
                        pltpu.SMEM((N_HEADS,), F32)],
        compiler_params=_params(("arbitrary", "arbitrary")),
        name="sb_attention",
    )(proj, proj, proj)


def _post_kernel(a_ref, ap_ref, yb_ref, yc_ref, yd_ref, x_ref, cw_ref, cb_ref, onw_ref, wo_ref, n2w_ref,
                 w1_ref, w2_ref, o_ref, *, tm, s_len, ffc):
    i = pl.program_id(0)
    a = a_ref[...]
    gw = GROUP_WIDTH
    v = a[:, gw:2 * gw] * a[:, 2 * gw:3 * gw]
    ap = ap_ref[...]
    first = (i * tm) % s_len == 0
    vp = jnp.where(first, 0.0, ap[:, gw:2 * gw] * ap[:, 2 * gw:3 * gw])
    row = _row(v.shape)
    v1 = jnp.where(row == 0, vp[7:8, :], pltpu.roll(v, 1, 0))
    v2 = jnp.where(row == 0, vp[6:7, :], jnp.where(row == 1, vp[7:8, :], pltpu.roll(v, 2, 0)))
    conv = cw_ref[0:1, :] * v2 + cw_ref[1:2, :] * v1 + cw_ref[2:3, :] * v
    ya = a[:, 0:gw] * (conv + cb_ref[...])

    mix = None
    for g, y in enumerate((ya, yb_ref[...], yc_ref[...], yd_ref[...])):
        ms = jnp.mean(y * y, axis=-1, keepdims=True)
        yn = (y * lax.rsqrt(ms + EPS) * onw_ref[:, gw * g:gw * (g + 1)]).astype(BF16)
        part = _dot(yn, wo_ref[gw * g:gw * (g + 1), :])
        mix = part if mix is None else mix + part
    x1 = x_ref[...] + mix

    ms = jnp.mean(x1 * x1, axis=-1, keepdims=True)
    h2 = (x1 * lax.rsqrt(ms + EPS) * n2w_ref[...]).astype(BF16)
    ff = None
    for cidx in range(D_FF // ffc):
        u = _dot(h2, w1_ref[:, ffc * cidx:ffc * (cidx + 1)])
        u = jnp.square(jnp.maximum(u, 0.0)).astype(BF16)
        part = _dot(u, w2_ref[ffc * cidx:ffc * (cidx + 1), :])
        ff = part if ff is None else ff + part
    o_ref[...] = x1 + ff


def _post(proj, yb, yc, yd, x2d, cw, cb, onw, wo, n2w, w1, w2, l, s, tm):
    t = x2d.shape[0]
    gw = GROUP_WIDTH
    row = lambda i: (i, 0)
    const = lambda i: (0, 0)
    layer = lambda i: (l, 0, 0)
    once = pl.Buffered(1)
    kern = functools.partial(_post_kernel, tm=tm, s_len=s, ffc=1024)
    return pl.pallas_call(
        kern,
        grid=(t // tm,),
        in_specs=[
            pl.BlockSpec((tm, 3 * gw), row),
            pl.BlockSpec((8, 3 * gw), lambda i: (jnp.maximum(i * (tm // 8) - 1, 0), 0)),
            pl.BlockSpec((tm, gw), row),
            pl.BlockSpec((tm, gw), row),
            pl.BlockSpec((tm, gw), row),
            pl.BlockSpec((tm, D_MODEL), row),
            pl.BlockSpec((3, gw), const),
            pl.BlockSpec((1, gw), const),
            pl.BlockSpec((1, D_MODEL), const),
            pl.BlockSpec((None, D_MODEL, D_MODEL), layer, pipeline_mode=once),
            pl.BlockSpec((1, D_MODEL), const),
            pl.BlockSpec((None, D_MODEL, D_FF), layer, pipeline_mode=once),
            pl.BlockSpec((None, D_FF, D_MODEL), layer, pipeline_mode=once),
        ],
        out_specs=pl.BlockSpec((tm, D_MODEL), row),
        out_shape=jax.ShapeDtypeStruct((t, D_MODEL), F32),
        compiler_params=_params(("arbitrary",)),
        name="post",
    )(proj, proj, yb, yc, yd, x2d, cw, cb, onw, wo, n2w, w1, w2)


def _t5_bucket(dist):
    max_exact = N_BUCKETS // 2
    d = np.maximum(dist, 0)
    large = max_exact + (np.log(np.maximum(d, 1) / max_exact) / math.log(MAX_DISTANCE / max_exact)
                         * (N_BUCKETS - max_exact)).astype(np.int32)
    return np.where(d < max_exact, d, np.minimum(large, N_BUCKETS - 1)).astype(np.int32)


def _tables(s, tq_nsa):
    n_cmp = (s - CMP_LEN) // CMP_STRIDE + 1
    ng = s // CMP_STRIDE
    n_slc = s // SLC_LEN
    tpos = np.arange(s)[None, :]
    n = np.arange(ng)[:, None]
    dist_c = tpos - (n * CMP_STRIDE + CMP_LEN - 1)
    bidxct = np.where((dist_c >= 0) & (n < n_cmp), _t5_bucket(dist_c), -1).astype(np.int32)
    key = np.arange(tq_nsa)[:, None]
    qry = np.arange(tq_nsa)[None, :]
    bidx2t = np.stack([_t5_bucket(qry - key), _t5_bucket(tq_nsa + qry - key)])
    starts = np.arange(n_cmp) * CMP_STRIDE
    ends = starts + CMP_LEN
    s0 = np.arange(n_slc) * SLC_LEN
    s1 = s0 + SLC_LEN
    ovl = np.clip(np.minimum(ends[:, None], s1[None]) - np.maximum(starts[:, None], s0[None]), 0, None) / CMP_LEN
    ovt = np.zeros((LANES, ng), np.float32)
    ovt[:n_slc, :n_cmp] = ovl.T
    emt = (np.arange(LANES)[None, :] == (np.arange(s) // SLC_LEN)[:, None]).astype(np.float32)
    inv = 1.0 / (ROPE_THETA ** (np.arange(0, ROPE_DIM, 2, dtype=np.float64) / ROPE_DIM))
    ang = (np.arange(s, dtype=np.float32)[:, None] * inv.astype(np.float32)[None, :]).astype(np.float64)
    cos, sin = np.cos(ang).astype(np.float32), np.sin(ang).astype(np.float32)
    z32 = np.zeros((s, 32), np.float32)
    z64 = np.zeros((s, 64), np.float32)
    one64 = np.ones((s, 64), np.float32)
    cq_t = np.concatenate([one64, cos, cos, z32], axis=1)
    ck_t = np.concatenate([z64, cos, cos, z32], axis=1)
    sk_t = np.concatenate([z64, -sin, sin, z32], axis=1)
    return dict(bidxct=jnp.asarray(bidxct), bidx2t=jnp.asarray(bidx2t), ovt=jnp.asarray(ovt, BF16),
                emt=jnp.asarray(emt, BF16), rope=(jnp.asarray(cq_t), jnp.asarray(ck_t), jnp.asarray(sk_t)))


def _shift_bound(d, q_gain, k_gain, extra=0.0):
    return (d * jnp.max(jnp.abs(q_gain)) * jnp.max(jnp.abs(k_gain)) * 1.01 + extra + 0.1).reshape(1)


MAX_SHIFT = 60.0


def _pad_cols(w, width):
    return jnp.pad(w, ((0, 0), (0, width - w.shape[1])))


def _layer_weights(l, conv_w, conv_b, nsa_q_norm, nsa_k_norm, cmp_pos, cmp_w1, cmp_w2, mla_q_a_norm,
                   mla_kv_norm, mla_wq_b, mla_wkv_b, mla_q_norm, mla_k_norm, out_norm_w, norm2_w):
    w1 = cmp_w1[l].reshape(2, CMP_LEN, HEAD_DIM, CMP_HIDDEN)
    zw = jnp.zeros((CMP_LEN, HEAD_DIM, CMP_HIDDEN), F32)
    cw1 = jnp.concatenate([jnp.concatenate([w1[0], zw], axis=2), jnp.concatenate([zw, w1[1]], axis=2)],
                          axis=1).astype(BF16)
    cw1 = cw1.reshape(2, CMP_STRIDE * LANES, 2 * CMP_HIDDEN)
    zc = jnp.zeros((CMP_HIDDEN, HEAD_DIM), F32)
    cw2 = jnp.concatenate([jnp.concatenate([cmp_w2[l, 0], zc], axis=1),
                           jnp.concatenate([zc, cmp_w2[l, 1]], axis=1)], axis=0).astype(BF16)
    cpos = jnp.concatenate([cmp_pos[l, 0], cmp_pos[l, 1]], axis=1).reshape(2, CMP_STRIDE * LANES)
    kn = nsa_k_norm[l]
    ones64 = jnp.ones((HEAD_DIM,), F32)
    knw_c = jnp.concatenate([kn[0], ones64])[None, :]
    knw_sw = jnp.stack([jnp.concatenate([kn[1], ones64]), jnp.concatenate([kn[2], ones64])])
    qnw = jnp.tile(nsa_q_norm[l], N_HEADS)[None, :]
    wq = mla_wq_b[l].reshape(Q_LORA, N_HEADS, QK_DIM)
    half = ROPE_DIM // 2
    wqs = jnp.concatenate([jnp.zeros((Q_LORA, N_HEADS, HEAD_DIM), F32), wq[:, :, HEAD_DIM + half:],
                           wq[:, :, HEAD_DIM:HEAD_DIM + half]], axis=2)
    pad_q = lambda w: jnp.pad(w, ((0, 2 * LANES - Q_LORA), (0, 0), (0, LANES - QK_DIM))).reshape(
        2 * LANES, N_HEADS * LANES).astype(BF16)
    wkv = mla_wkv_b[l].reshape(KV_LORA, N_HEADS, 2 * HEAD_DIM)
    wk = jnp.pad(wkv[:, :, :HEAD_DIM], ((0, 0), (0, 0), (0, LANES - HEAD_DIM))).reshape(KV_LORA, N_HEADS * LANES)
    wvt = jnp.pad(wkv[:, :, HEAD_DIM:], ((0, 0), (0, 0), (0, DV_PAD - HEAD_DIM))).reshape(KV_LORA, -1).T
    return dict(
        cw1=cw1, cw2=cw2, cw2t=cw2.T, cpos=cpos, knw_c=knw_c, knw_sw=knw_sw, qnw=qnw,
        qaw=_pad_cols(mla_q_a_norm[l][None, :], 2 * LANES), kvw=mla_kv_norm[l][None, :],
        wq=pad_q(wq), wqs=pad_q(wqs), wk=wk.astype(BF16), wvt=wvt.astype(BF16),
        mqn=_pad_cols(mla_q_norm[l][None, :] * (QK_DIM ** -0.5 * LOG2E), LANES),
        mkn=_pad_cols(mla_k_norm[l][None, :], LANES),
        cw=conv_w[l], cb=conv_b[l][None, :], onw=out_norm_w[l][None, :], n2w=norm2_w[l][None, :])


TM_PROJ = 1024
TM_PREP = 1024
TM_POST = 512
TQ_NSA = 256
TQ_MLA = 256
TQ_SB = 256


def kernel(x, rel_bias, norm1_w, w_in, conv_w, conv_b, nsa_q_norm, nsa_k_norm, cmp_pos, cmp_w1, cmp_w2,
           mla_q_a_norm, mla_kv_norm, mla_wq_b, mla_wkv_b, mla_q_norm, mla_k_norm, out_norm_w, w_out, norm2_w,
           ffn_w1, ffn_w2):
    b, s, d = x.shape
    depth = w_in.shape[0]
    tabs = _tables(s, TQ_NSA)
    x2d = x.reshape(b * s, d)
    wo_b, w1_b, w2_b = w_out.astype(BF16), ffn_w1.astype(BF16), ffn_w2.astype(BF16)
    for l in range(depth):
        w = _layer_weights(l, conv_w, conv_b, nsa_q_norm, nsa_k_norm, cmp_pos, cmp_w1, cmp_w2,
                           mla_q_a_norm, mla_kv_norm, mla_wq_b, mla_wkv_b, mla_q_norm, mla_k_norm, out_norm_w,
                           norm2_w)
        proj = _inproj(x2d, norm1_w[l][None, :], w_in, l, TM_PROJ)
        kcvc, kcvct = _compress(proj, w["cpos"], w["cw1"], w["cw2"], w["cw2t"], w["knw_c"], b, s)
        nsa_shift = _shift_bound(HEAD_DIM, w["qnw"] * (HEAD_DIM ** -0.5 * LOG2E), w["knw_sw"][:, :HEAD_DIM],
                                 jnp.max(jnp.abs(rel_bias)) * LOG2E)
        nsa_args = (proj, kcvc, kcvct, rel_bias, nsa_shift, tabs["bidxct"], tabs["bidx2t"], tabs["ovt"], tabs["emt"],
                    w["qnw"], w["knw_sw"])
        yb = lax.cond(nsa_shift[0] <= MAX_SHIFT,
                      lambda *a: _nsa_attention(*a, b, s, TQ_NSA, True),
                      lambda *a: _nsa_attention(*a, b, s, TQ_NSA, False), *nsa_args)
        mla_shift = _shift_bound(QK_DIM, w["mqn"], w["mkn"])
        qm, km, vmt = _mla_prep(proj, mla_shift, w["qaw"], w["kvw"], w["wq"], w["wqs"], w["wk"], w["wvt"], w["mqn"],
                                w["mkn"], tabs["rope"], s, TM_PREP)
        yc = lax.cond(mla_shift[0] <= MAX_SHIFT,
                      lambda *a: _mla_attention(*a, b, s, TQ_MLA, True),
                      lambda *a: _mla_attention(*a, b, s, TQ_MLA, False), qm, km, vmt)
        yd = _sb_attention(proj, b, s, TQ_SB)
        x2d = _post(proj, yb, yc, yd, x2d, w["cw"], w["cb"], w["onw"], wo_b, w["n2w"], w1_b, w2_b, l, s, TM_POST)
    return x2d.reshape(b, s, d)
```

```python
import functools
import math

import jax
import jax.numpy as jnp
import numpy as np
from jax import lax
from jax.experimental import pallas as pl
from jax.experimental.pallas import tpu as pltpu

F32 = jnp.float32
BF16 = jnp.bfloat16

D_MODEL = 1024
GROUP_WIDTH = 256
HEAD_DIM = 64
N_HEADS = 4
LANES = 128
CMP_LEN = 32
CMP_STRIDE = 16
SLC_LEN = 64
N_SEL = 16
WINDOW = 512
CMP_HIDDEN = 256
Q_LORA = 192
KV_LORA = 128
ROPE_DIM = 32
QK_DIM = 96
ROPE_THETA = 10000.0
N_BUCKETS = 32
MAX_DISTANCE = 128
D_FF = 4096
EPS = 1e-6
NEG = -1e30
LOG2E = math.log2(math.e)
DV_PAD = 80

NP = 2816
CB_NQ = 3
CB_KCVC = 8
CB_KSVS = 9
CB_KWVW = 10
CB_GATE = 11
CB_CQ = 6
CB_CKV = 14
CB_KR = 15
CB_SQ = 8
CB_SK = 9
CB_SV = 10

VMEM_LIMIT = 56 * 1024 * 1024

NT_DIMS = (((1,), (1,)), ((), ()))


def _params(sem):
    return pltpu.CompilerParams(dimension_semantics=sem, vmem_limit_bytes=VMEM_LIMIT)


def _nt(a, b):
    return lax.dot_general(a, b, NT_DIMS, preferred_element_type=F32)


def _dot(a, b):
    return jnp.dot(a, b, preferred_element_type=F32)


def _lane(shape):
    return lax.broadcasted_iota(jnp.int32, shape, len(shape) - 1)


def _row(shape):
    return lax.broadcasted_iota(jnp.int32, shape, len(shape) - 2)


def _lane_tile(j, width):
    return pl.ds(pl.multiple_of(j * width, width), width)


_IN_SEGMENTS = (
    ((0, 1408), 0),
    ((1408, 1420), 1408),
    ((1420, 1612), 1536),
    ((1612, 1740), 1792),
    ((1756, 1772), 1920),
    ((1740, 1756), 1936),
    ((1740, 1772), 1984),
    ((1772, 2540), 2048),
)
IN_COLS = 2540


def _inproj_kernel(x_ref, nw_ref, w_ref, o_ref, w_s):
    @pl.when(pl.program_id(0) == 0)
    def _relayout_weights():
        end = 0
        for (a, b), dst in _IN_SEGMENTS:
            if dst > end:
                w_s[:, end:dst] = jnp.zeros((D_MODEL, dst - end), BF16)
            w_s[:, dst:dst + b - a] = w_ref[:, a:b].astype(BF16)
            end = dst + b - a
        assert end == NP

    x = x_ref[...]
    ms = jnp.mean(x * x, axis=-1, keepdims=True)
    h = (x * lax.rsqrt(ms + EPS) * nw_ref[...]).astype(BF16)
    o_ref[...] = _dot(h, w_s[...])


def _inproj(x2d, nw, w_in, l, tm):
    t = x2d.shape[0]
    return pl.pallas_call(
        _inproj_kernel,
        grid=(t // tm,),
        in_specs=[
            pl.BlockSpec((tm, D_MODEL), lambda i: (i, 0)),
            pl.BlockSpec((1, D_MODEL), lambda i: (0, 0)),
            pl.BlockSpec((None, D_MODEL, IN_COLS), lambda i: (l, 0, 0), pipeline_mode=pl.Buffered(1)),
        ],
        out_specs=pl.BlockSpec((tm, NP), lambda i: (i, 0)),
        out_shape=jax.ShapeDtypeStruct((t, NP), F32),
        scratch_shapes=[pltpu.VMEM((D_MODEL, NP), BF16)],
        compiler_params=_params(("arbitrary",)),
        name="inproj",
    )(x2d, nw, w_in)


def _compress_kernel(x_ref, pos_ref, w1_ref, w2_ref, w2t_ref, knw_ref, o_ref, ot_ref):
    ng = x_ref.shape[1]
    x = x_ref[0]
    acc_a = _dot((x + pos_ref[0:1, :]).astype(BF16), w1_ref[0])
    acc_b = _dot((x + pos_ref[1:2, :]).astype(BF16), w1_ref[1])
    pre = acc_a + pltpu.roll(acc_b, ng - 1, 0)
    hdn = (pre * jax.nn.sigmoid(pre)).astype(BF16)
    out = _dot(hdn, w2_ref[...])
    lane = _lane(out.shape)
    is_k = lane < HEAD_DIM
    ss = jnp.sum(jnp.where(is_k, out * out, 0.0), axis=-1, keepdims=True) * (1.0 / HEAD_DIM)
    o_ref[0] = jnp.where(is_k, out * lax.rsqrt(ss + EPS) * knw_ref[...], out)
    ot_ref[0] = _nt(w2t_ref[...], hdn)


def _compress(proj, pos, w1, w2, w2t, knw, b, s):
    ng = s // CMP_STRIDE
    gl = CMP_STRIDE * LANES
    xg = proj[:, CB_KCVC * LANES:(CB_KCVC + 1) * LANES].reshape(b, ng, gl)
    return pl.pallas_call(
        _compress_kernel,
        grid=(b,),
        in_specs=[
            pl.BlockSpec((1, ng, gl), lambda i: (i, 0, 0)),
            pl.BlockSpec((2, gl), lambda i: (0, 0)),
            pl.BlockSpec((2, gl, 2 * CMP_HIDDEN), lambda i: (0, 0, 0)),
            pl.BlockSpec((2 * CMP_HIDDEN, LANES), lambda i: (0, 0)),
            pl.BlockSpec((LANES, 2 * CMP_HIDDEN), lambda i: (0, 0)),
            pl.BlockSpec((1, LANES), lambda i: (0, 0)),
        ],
        out_specs=[pl.BlockSpec((1, ng, LANES), lambda i: (i, 0, 0)),
                   pl.BlockSpec((1, LANES, ng), lambda i: (i, 0, 0))],
        out_shape=[jax.ShapeDtypeStruct((b, ng, LANES), F32), jax.ShapeDtypeStruct((b, LANES, ng), F32)],
        compiler_params=_params(("arbitrary",)),
        name="nsa_compress",
    )(xg, pos, w1, w2, w2t, knw)


def _softmax_update(sts, vts, m_old, acc_old, fixed=False):
    if fixed:
        return m_old, [acc + _dot(vt, jnp.exp2(st).astype(BF16)) for acc, vt, st in zip(acc_old, vts, sts)]
    m_new = [jnp.maximum(m, jnp.max(st, axis=0, keepdims=True)) for m, st in zip(m_old, sts)]
    ps = [jnp.exp2(st - m).astype(BF16) for st, m in zip(sts, m_new)]
    alphas = [jnp.exp2(mo - mn) for mo, mn in zip(m_old, m_new)]
    acc_new = [al * acc + _dot(vt, p) for al, acc, vt, p in zip(alphas, acc_old, vts, ps)]
    return m_new, acc_new


def _softmax_steps(sts, vts, m_refs, acc_refs, fixed=False):
    m_old = [None] * len(sts) if fixed else [r[...] for r in m_refs]
    m_new, acc_new = _softmax_update(sts, vts, m_old, [r[...] for r in acc_refs], fixed)
    for r, v in zip(() if fixed else m_refs, m_new):
        r[...] = v
    for r, v in zip(acc_refs, acc_new):
        r[...] = v


def _with_ones_row(vt):
    pad = jnp.where(_row((DV_PAD - HEAD_DIM, vt.shape[1])) == 0, 1.0, 0.0).astype(vt.dtype)
    return jnp.concatenate([vt, pad], axis=0)


def _bucket_bias(bidx, relb_ref, h, fill):
    acc = jnp.full(bidx.shape, fill, F32)
    for bk in range(N_BUCKETS):
        acc = jnp.where(bidx == bk, relb_ref[bk, h] * LOG2E, acc)
    return acc


def _group_sums(x, member):
    g = jnp.where(member, 1.0, 0.0).astype(BF16)
    hi = x.astype(BF16)
    lo = (x - hi.astype(F32)).astype(BF16)
    return _dot(hi, g) + _dot(lo, g)


def _dup_low_half(x):
    y = jnp.where(_lane(x.shape) < HEAD_DIM, x, 0.0)
    return y + pltpu.roll(y, HEAD_DIM, 1)


def _nsa_kernel(relb_ref, shift_ref, q_ref, g_ref, kcvc_ref, kcvct_ref, ksvs_ref, kwvw_ref, bidxct_ref, bidx2t_ref, ovt_ref,
                emt_ref, qnw_ref, knw_ref, o_ref,
                biasc_s, bias2_s, ks_s, vst_s, kw_s, vwt_s, kc_s, vct_s, qx_s, m_s, acc_s, *, tq, s_len, fixed):
    b = pl.program_id(0)
    shift = shift_ref[0] if fixed else 0.0
    i = pl.program_id(1)
    n_win = WINDOW // tq
    ng = s_len // CMP_STRIDE

    @pl.when((b == 0) & (i == 0))
    def _build_bias_tables():
        key = _row((tq, tq))
        qry = _lane((tq, tq))
        for h in range(N_HEADS):
            cols = slice(h * tq, (h + 1) * tq)
            far = jnp.full((tq, tq), relb_ref[N_BUCKETS - 1, h] * LOG2E - shift, F32)
            bias2_s[0, :, cols] = jnp.where(key <= qry, _bucket_bias(bidx2t_ref[0], relb_ref, h, NEG) - shift, NEG)
            bias2_s[1, :, cols] = _bucket_bias(bidx2t_ref[1], relb_ref, h, NEG) - shift
            bias2_s[2, :, cols] = far
            bias2_s[3, :, cols] = jnp.where(key > qry, far, NEG)
            bias2_s[4, :, cols] = jnp.full((tq, tq), NEG, F32)

        half = tq // CMP_STRIDE
        assert MAX_DISTANCE + CMP_LEN - 1 <= tq

        def body(t, carry):
            bi = bidxct_ref[:, _lane_tile(t, tq)]
            near_rows = pl.ds(pl.multiple_of(jnp.maximum(t - 1, 0) * half, half), 2 * half)
            near = bidxct_ref[near_rows, _lane_tile(t, tq)]
            for h in range(N_HEADS):
                cols = slice(h * tq, (h + 1) * tq)
                biasc_s[t, :, cols] = jnp.where(bi == N_BUCKETS - 1, relb_ref[N_BUCKETS - 1, h] * LOG2E, NEG)
                biasc_s[t, near_rows, cols] = _bucket_bias(near, relb_ref, h, NEG)
            return carry

        lax.fori_loop(0, s_len // tq, body, 0)

    @pl.when(i == 0)
    def _prep_kv():
        ch = 256

        def body(t, carry):
            rows = pl.ds(pl.multiple_of(t * ch, ch), ch)
            for src, kdst, vdst, widx in ((ksvs_ref, ks_s, vst_s, 0), (kwvw_ref, kw_s, vwt_s, 1)):
                x = src[rows, :]
                ss = _group_sums(x * x, _row((LANES, LANES)) < HEAD_DIM) * (1.0 / HEAD_DIM)
                kn = x * lax.rsqrt(ss + EPS) * knw_ref[widx:widx + 1, :]
                kdst[rows, 0:LANES] = _dup_low_half(kn).astype(BF16)
                vdst[:, _lane_tile(t, ch)] = _with_ones_row(x.T[HEAD_DIM:, :]).astype(BF16)
            ks_s[rows, LANES:] = emt_ref[rows, :]
            return carry

        lax.fori_loop(0, s_len // ch, body, 0)
        kc_s[...] = _dup_low_half(kcvc_ref[0]).astype(BF16)
        vct_s[...] = kcvct_ref[0][HEAD_DIM:, :].astype(BF16)

    q = q_ref[...]
    lane = _lane((tq, LANES))
    heads = range(N_HEADS)
    gw = 2 * LANES
    head_shift = HEAD_DIM.bit_length() - 1
    same_head = (lax.shift_right_logical(_row((gw, gw)), head_shift)
                 == lax.shift_right_logical(_lane((gw, gw)), head_shift))
    ss = _group_sums(q * q, same_head) * (1.0 / HEAD_DIM)
    qn = q * lax.rsqrt(ss + EPS) * qnw_ref[...] * (HEAD_DIM ** -0.5 * LOG2E)
    for h in heads:
        mine = (lane < HEAD_DIM) if h % 2 == 0 else (lane >= HEAD_DIM)
        qx_s[h, :, 0:LANES] = jnp.where(mine, qn[:, LANES * (h // 2):LANES * (h // 2 + 1)], 0.0).astype(BF16)

    lcs = [_nt(kc_s[...], qx_s[h, :, 0:LANES]) + biasc_s[i, :, h * tq:(h + 1) * tq] for h in heads]
    pcs = [jnp.where(lc > 0.5 * NEG, jnp.exp2(lc - jnp.max(lc, axis=0, keepdims=True)), 0.0) for lc in lcs]
    dens = [jnp.sum(pc, axis=0, keepdims=True) for pc in pcs]
    pcs = [pc / jnp.where(den > 0.0, den, 1.0) for pc, den in zip(pcs, dens)]
    o_cmp = [_dot(vct_s[...], pc.astype(BF16)) for pc in pcs]
    psum = (pcs[0] + pcs[1]) + (pcs[2] + pcs[3])

    n_slc = s_len // SLC_LEN
    n_sel = min(N_SEL, n_slc)
    blk = _row((n_slc, tq))
    tpos = i * tq + _lane((n_slc, tq))
    tblk = tpos // SLC_LEN
    valid = blk * SLC_LEN <= tpos

    p_hi = psum.astype(BF16)
    p_lo = (psum - p_hi.astype(F32)).astype(BF16)
    score = _dot(ovt_ref[...], p_hi) + _dot(ovt_ref[...], p_lo)
    forced = (blk == 0) | (blk == tblk) | (blk == tblk - 1)
    sc = jnp.where(forced, jnp.inf, jnp.where(valid, score[0:n_slc], -jnp.inf))
    rank = jnp.zeros((n_slc, tq), F32)
    for k in range(n_slc):
        ck = sc[k:k + 1, :]
        beats = (ck > sc) | ((ck == sc) & (blk > k))
        rank += jnp.where(beats, 1.0, 0.0)
    pen = jnp.where((rank < float(n_sel)) & valid, 0.0, NEG)
    pen = jnp.concatenate([pen, jnp.zeros((LANES - n_slc, tq), F32)], axis=0)
    pen_t = pen.T.astype(BF16)
    pen = pen.astype(BF16)
    for h in heads:
        qx_s[h, :, LANES:] = pen_t

    ms = [jnp.full((1, tq), NEG, F32)] * (2 * N_HEADS)
    accs = [jnp.zeros((DV_PAD, tq), F32)] * (2 * N_HEADS)
    for jj in range(n_win + 1):
        exists = i >= jj
        rows = pl.ds(pl.multiple_of(jnp.maximum(i - jj, 0) * tq, tq), tq)
        kind = jnp.where(exists, min(jj, 2), 4)
        kind_w = jnp.where(exists, 3 if jj == n_win else min(jj, 2), 4)
        ks = ks_s[rows, 0:LANES]
        kw = kw_s[rows, :]
        masked = _dot(ks_s[rows, LANES:], pen)
        sts = [_nt(ks, qx_s[h, :, 0:LANES]) + masked + bias2_s[kind, :, h * tq:(h + 1) * tq] for h in heads]
        sts += [_nt(kw, qx_s[h, :, 0:LANES]) + bias2_s[kind_w, :, h * tq:(h + 1) * tq] for h in heads]
        vts = [vst_s[:, rows]] * N_HEADS + [vwt_s[:, rows]] * N_HEADS
        ms, accs = _softmax_update(sts, vts, ms, accs, fixed)
    for h in heads:
        if not fixed:
            m_s[h] = ms[h]
        acc_s[h] = accs[h]
    o_win = [accs[N_HEADS + h][0:HEAD_DIM, :] / accs[N_HEADS + h][HEAD_DIM:HEAD_DIM + 1, :] for h in heads]

    def far_step(j, width):
        rows = pl.ds(j * tq if isinstance(j, int) else pl.multiple_of(j * tq, tq), width * tq)
        ks = ks_s[rows, :]
        sts = [_nt(ks, qx_s[h]) + (relb_ref[N_BUCKETS - 1, h] * LOG2E - shift) for h in heads]
        _softmax_steps(sts, [vst_s[:, rows]] * N_HEADS, [m_s.at[h] for h in heads], [acc_s.at[h] for h in heads],
                       fixed)

    n_far = jnp.maximum(i - n_win, 0)
    if fixed:
        def far_sweep(n):
            accs = [acc_s[h] for h in heads]
            for start in range(0, n, 2):
                rows = pl.ds(start * tq, min(2, n - start) * tq)
                ks = ks_s[rows, :]
                sts = [_nt(ks, qx_s[h]) + (relb_ref[N_BUCKETS - 1, h] * LOG2E - shift) for h in heads]
                _, accs = _softmax_update(sts, [vst_s[:, rows]] * N_HEADS, None, accs, True)
            for h in heads:
                acc_s[h] = accs[h]

        for n in range(1, s_len // tq - n_win):
            pl.when(n_far == n)(functools.partial(far_sweep, n))
    else:
        def far_body(p, carry):
            far_step(n_far - 2 * (p + 1), 2)
            return carry

        lax.fori_loop(0, n_far // 2, far_body, 0)
        pl.when(n_far % 2 == 1)(functools.partial(far_step, 0, 1))

    gt = jax.nn.sigmoid(g_ref[...]).T
    ys = []
    for h in heads:
        o_slc = acc_s[h, 0:HEAD_DIM, :] / acc_s[h, HEAD_DIM:HEAD_DIM + 1, :]
        ys.append(gt[3 * h:3 * h + 1, :] * o_cmp[h] + gt[3 * h + 1:3 * h + 2, :] * o_slc
                  + gt[3 * h + 2:3 * h + 3, :] * o_win[h])
    o_ref[...] = jnp.concatenate(ys, axis=0).T


def _nsa_attention(proj, kcvc, kcvct, rel_bias, shift, bidxct, bidx2t, ovt, emt, qnw, knw, b, s, tq, fixed):
    t = b * s
    nq = s // tq
    m_rows = N_HEADS * tq
    ng = s // CMP_STRIDE
    kern = functools.partial(_nsa_kernel, tq=tq, s_len=s, fixed=fixed)
    return pl.pallas_call(
        kern,
        grid=(b, nq),
        in_specs=[
            pl.BlockSpec(memory_space=pltpu.SMEM),
            pl.BlockSpec(memory_space=pltpu.SMEM),
            pl.BlockSpec((tq, 2 * LANES), lambda bi, i: (bi * nq + i, CB_NQ)),
            pl.BlockSpec((tq, LANES), lambda bi, i: (bi * nq + i, CB_GATE)),
            pl.BlockSpec((1, ng, LANES), lambda bi, i: (bi, 0, 0)),
            pl.BlockSpec((1, LANES, ng), lambda bi, i: (bi, 0, 0)),
            pl.BlockSpec((s, LANES), lambda bi, i: (bi, CB_KSVS)),
            pl.BlockSpec((s, LANES), lambda bi, i: (bi, CB_KWVW)),
            pl.BlockSpec((ng, s), lambda bi, i: (0, 0)),
            pl.BlockSpec((2, tq, tq), lambda bi, i: (0, 0, 0)),
            pl.BlockSpec((LANES, ng), lambda bi, i: (0, 0)),
            pl.BlockSpec((s, LANES), lambda bi, i: (0, 0)),
            pl.BlockSpec((1, 2 * LANES), lambda bi, i: (0, 0)),
            pl.BlockSpec((2, LANES), lambda bi, i: (0, 0)),
        ],
        out_specs=pl.BlockSpec((tq, 2 * LANES), lambda bi, i: (bi * nq + i, 0)),
        out_shape=jax.ShapeDtypeStruct((t, GROUP_WIDTH), F32),
        scratch_shapes=[
            pltpu.VMEM((nq, ng, m_rows), F32),
            pltpu.VMEM((5, tq, m_rows), F32),
            pltpu.VMEM((s, 2 * LANES), BF16),
            pltpu.VMEM((DV_PAD, s), BF16),
            pltpu.VMEM((s, LANES), BF16),
            pltpu.VMEM((DV_PAD, s), BF16),
            pltpu.VMEM((ng, LANES), BF16),
            pltpu.VMEM((HEAD_DIM, ng), BF16),
            pltpu.VMEM((N_HEADS, tq, 2 * LANES), BF16),
            pltpu.VMEM((N_HEADS, 1, tq), F32),
            pltpu.VMEM((N_HEADS, DV_PAD, tq), F32),
        ],
        compiler_params=_params(("arbitrary", "arbitrary")),
        name="nsa_attention",
    )(rel_bias, shift, proj, proj, kcvc, kcvct, proj, proj, bidxct, bidx2t, ovt, emt, qnw, knw)


def _mla_prep_kernel(shift_ref, cq_ref, ckv_ref, kr_ref, qaw_ref, kvw_ref, wq_ref, wqs_ref, wk_ref, wvt_ref, qnw_ref, knw_ref,
                     cq_t_ref, ck_t_ref, sk_t_ref, qo_ref, ko_ref, vto_ref):
    cq = cq_ref[...]
    ms = jnp.sum(cq * cq, axis=-1, keepdims=True) * (1.0 / Q_LORA)
    hq = (cq * lax.rsqrt(ms + EPS) * qaw_ref[...]).astype(BF16)
    qf = _dot(hq, wq_ref[...])
    qsw = _dot(hq, wqs_ref[...])
    ckv = ckv_ref[...]
    ms = jnp.mean(ckv * ckv, axis=-1, keepdims=True)
    hkv = (ckv * lax.rsqrt(ms + EPS) * kvw_ref[...]).astype(BF16)
    kf = _dot(hkv, wk_ref[...])
    vt = _nt(wvt_ref[...], hkv)
    vto_ref[...] = jnp.where(_row(vt.shape) % DV_PAD == HEAD_DIM, 1.0, vt).astype(BF16)
    krb = kr_ref[...]
    kr_rot = krb * ck_t_ref[...] + pltpu.roll(krb, HEAD_DIM, 1) * sk_t_ref[...]
    is_shift = _lane((cq.shape[0], LANES)) == QK_DIM
    for h in range(N_HEADS):
        cols = slice(LANES * h, LANES * (h + 1))
        x = qf[:, cols] * cq_t_ref[...] + qsw[:, cols] * sk_t_ref[...]
        ss = jnp.sum(x * x, axis=-1, keepdims=True) * (1.0 / QK_DIM)
        qn = x * lax.rsqrt(ss + EPS) * qnw_ref[...]
        qo_ref[:, cols] = jnp.where(is_shift, -shift_ref[0], qn).astype(BF16)
        k = kf[:, cols] + kr_rot
        ss = jnp.sum(k * k, axis=-1, keepdims=True) * (1.0 / QK_DIM)
        ko_ref[:, cols] = jnp.where(is_shift, 1.0, k * lax.rsqrt(ss + EPS) * knw_ref[...]).astype(BF16)


def _mla_prep(proj, shift, qaw, kvw, wq, wqs, wk, wvt, qnw, knw, tabs, s, tm):
    t = proj.shape[0]
    npos = s // tm
    row = lambda i: (i, 0)
    const = lambda i: (0, 0)
    tab = pl.BlockSpec((tm, LANES), lambda i: (i % npos, 0))
    out = jax.ShapeDtypeStruct((t, N_HEADS * LANES), BF16)
    return pl.pallas_call(
        _mla_prep_kernel,
        grid=(t // tm,),
        in_specs=[
            pl.BlockSpec(memory_space=pltpu.SMEM),
            pl.BlockSpec((tm, 2 * LANES), lambda i: (i, CB_CQ)),
            pl.BlockSpec((tm, LANES), lambda i: (i, CB_CKV)),
            pl.BlockSpec((tm, LANES), lambda i: (i, CB_KR)),
            pl.BlockSpec((1, 2 * LANES), const),
            pl.BlockSpec((1, LANES), const),
            pl.BlockSpec((2 * LANES, N_HEADS * LANES), const),
            pl.BlockSpec((2 * LANES, N_HEADS * LANES), const),
            pl.BlockSpec((LANES, N_HEADS * LANES), const),
            pl.BlockSpec((N_HEADS * DV_PAD, LANES), const),
            pl.BlockSpec((1, LANES), const),
            pl.BlockSpec((1, LANES), const),
            tab, tab, tab,
        ],
        out_specs=[pl.BlockSpec((tm, N_HEADS * LANES), row), pl.BlockSpec((tm, N_HEADS * LANES), row),
                   pl.BlockSpec((N_HEADS * DV_PAD, tm), lambda i: (0, i))],
        out_shape=[out, out, jax.ShapeDtypeStruct((N_HEADS * DV_PAD, t), BF16)],
        compiler_params=_params(("arbitrary",)),
        name="mla_prep",
    )(shift, proj, proj, proj, qaw, kvw, wq, wqs, wk, wvt, qnw, knw, *tabs)


def _mla_attn_kernel(q_ref, k_ref, vt_ref, o_ref, m_s, acc_s, *, tq, nq, fixed):
    i = pl.program_id(1)
    heads = range(N_HEADS)
    causal = _row((tq, tq)) <= _lane((tq, tq))

    def logits(rows, masked):
        sts = [_nt(k_ref[rows, LANES * h:LANES * (h + 1)], q_ref[:, LANES * h:LANES * (h + 1)]) for h in heads]
        return [jnp.where(causal, st, NEG) for st in sts] if masked else sts

    def values(rows):
        return [vt_ref[DV_PAD * h:DV_PAD * (h + 1), rows] for h in heads]

    if fixed:
        def sweep(n):
            accs = [jnp.zeros((DV_PAD, tq), F32)] * N_HEADS
            for start in range(0, n, 2):
                rows = pl.ds(start * tq, min(2, n - start) * tq)
                _, accs = _softmax_update(logits(rows, False), values(rows), None, accs, True)
            rows = pl.ds(n * tq, tq)
            _, accs = _softmax_update(logits(rows, True), values(rows), None, accs, True)
            for h in heads:
                acc_s[h] = accs[h]

        for n in range(nq):
            pl.when(i == n)(functools.partial(sweep, n))
    else:
        for h in heads:
            m_s[h] = jnp.full((1, tq), NEG, F32)
            acc_s[h] = jnp.zeros((DV_PAD, tq), F32)

        def step(j, width, masked):
            rows = pl.ds(j * tq if isinstance(j, int) else pl.multiple_of(j * tq, tq), width * tq)
            _softmax_steps(logits(rows, masked), values(rows), [m_s.at[h] for h in heads],
                           [acc_s.at[h] for h in heads])

        step(i, 1, True)

        def body(jj, carry):
            step(i - 2 * jj, 2, False)
            return carry

        lax.fori_loop(1, i // 2 + 1, body, 0)
        pl.when(i % 2 == 1)(functools.partial(step, 0, 1, False))
    yt = jnp.concatenate([acc_s[h, 0:HEAD_DIM, :] / acc_s[h, HEAD_DIM:HEAD_DIM + 1, :] for h in range(N_HEADS)],
                         axis=0)
    o_ref[...] = yt.T


def _mla_attention(qm, km, vmt, b, s, tq, fixed):
    t = b * s
    nq = s // tq
    w = N_HEADS * LANES
    return pl.pallas_call(
        functools.partial(_mla_attn_kernel, tq=tq, nq=nq, fixed=fixed),
        grid=(b, nq),
        in_specs=[
            pl.BlockSpec((tq, w), lambda bi, i: (bi * nq + i, 0)),
            pl.BlockSpec((s, w), lambda bi, i: (bi, 0)),
            pl.BlockSpec((N_HEADS * DV_PAD, s), lambda bi, i: (0, bi)),
        ],
        out_specs=pl.BlockSpec((tq, 2 * LANES), lambda bi, i: (bi * nq + i, 0)),
        out_shape=jax.ShapeDtypeStruct((t, GROUP_WIDTH), F32),
        scratch_shapes=[pltpu.VMEM((N_HEADS, 1, tq), F32), pltpu.VMEM((N_HEADS, DV_PAD, tq), F32)],
        compiler_params=_params(("arbitrary", "arbitrary")),
        name="mla_attention",
    )(qm, km, vmt)


def _sb_kernel(q_ref, k_ref, v_ref, o_ref, kb_s, vt_s, q_s, r_s, acc_s, kmax_s, *, tq, s_len):
    i = pl.program_id(1)
    gw = 2 * LANES
    head_shift = HEAD_DIM.bit_length() - 1
    same_head = (lax.shift_right_logical(_row((gw, gw)), head_shift)
                 == lax.shift_right_logical(_lane((gw, gw)), head_shift))

    @pl.when(i == 0)
    def _cast_kv():
        ch = 256

        def body(t, kmax):
            rows = pl.ds(pl.multiple_of(t * ch, ch), ch)
            k = k_ref[rows, :]
            kb_s[rows, :] = k.astype(BF16)
            vt_s[:, _lane_tile(t, ch)] = v_ref[rows, :].T.astype(BF16)
            return jnp.maximum(kmax, _group_sums(k * k, same_head))

        kmax = lax.fori_loop(0, s_len // ch, body, jnp.zeros((ch, gw), F32))
        for h in range(N_HEADS):
            kmax_s[h] = jnp.max(kmax[:, HEAD_DIM * h:HEAD_DIM * (h + 1)])

    key = _row((tq, tq))
    qry = _lane((tq, tq))
    strict = key < qry
    tri = jnp.where(key <= qry, 1.0, 0.0).astype(BF16)
    tri2 = jnp.concatenate([tri, tri], axis=1)
    lane = _lane((tq, LANES))
    q = q_ref[...] * (HEAD_DIM ** -0.5 * LOG2E)
    for h in range(N_HEADS):
        mine = (lane < HEAD_DIM) if h % 2 == 0 else (lane >= HEAD_DIM)
        q_s[h] = jnp.where(mine, q[:, LANES * (h // 2):LANES * (h // 2 + 1)], 0.0).astype(BF16)
        r_s[h] = jnp.zeros((1, tq), F32)
        acc_s[h] = jnp.zeros((HEAD_DIM, tq), F32)
    q_sq = _group_sums(q * q, same_head).T
    z_bound = [jnp.sqrt(q_sq[HEAD_DIM * h:HEAD_DIM * h + 1, :] * kmax_s[h]) * 1.01 + 1.0 for h in range(N_HEADS)]

    heads = range(N_HEADS)

    def step(j, width, masked):
        rs = [r_s[h] for h in heads]
        accs = [acc_s[h] for h in heads]
        tiles = [j + width - 1 - w for w in range(width)]
        rows = [pl.ds(t * tq if isinstance(t, int) else pl.multiple_of(t * tq, tq), tq) for t in tiles]
        zs = [[_nt(kb_s[r, LANES * (h // 2):LANES * (h // 2 + 1)], q_s[h]) for h in heads] for r in rows]
        part = []
        for w, zt in enumerate(zs):
            negabs = [pltpu.bitcast(pltpu.bitcast(z, jnp.uint32) | jnp.uint32(0x80000000), F32) for z in zt]
            sps = [jnp.maximum(z, 0.0) + jnp.log2(1.0 + jnp.exp2(na)) for z, na in zip(zt, negabs)]
            if masked and w == 0:
                sps = [jnp.where(strict, sp, 0.0) for sp in sps]
            his = [sp.astype(BF16) for sp in sps]
            los = [(sp - hi.astype(F32)).astype(BF16) for sp, hi in zip(sps, his)]
            part.append([_dot(tri2, jnp.concatenate([hi, lo], axis=0)) for hi, lo in zip(his, los)])
        for w, (r, zt, pt) in enumerate(zip(rows, zs, part)):
            csums = [p + rc for p, rc in zip(pt, rs)]
            als = [jnp.exp2(z - cs) for z, cs in zip(zt, csums)]
            if masked and w == 0:
                als = [jnp.where(strict, a, 0.0) for a in als]
            accs = [acc + _dot(vt_s[HEAD_DIM * h:HEAD_DIM * (h + 1), r], als[h].astype(BF16))
                    for h, acc in zip(heads, accs)]
            rs = [cs[0:1, :] for cs in csums]
        for h in heads:
            acc_s[h] = accs[h]
            r_s[h] = rs[h]

    def live():
        slack = [zb - r_s[h] for h, zb in enumerate(z_bound)]
        return jnp.max(jnp.maximum(jnp.maximum(slack[0], slack[1]), jnp.maximum(slack[2], slack[3]))) >= -150.0

    step(i, 1, True)
    pl.when(i >= 1)(functools.partial(step, i - 1, 1, False))

    n_rest = jnp.maximum(i - 1, 0)

    def cond(carry):
        p, alive = carry
        return (p < n_rest) & alive

    def body(carry):
        p, _ = carry
        step(i - 2 - p, 1, False)
        return p + 1, live()

    lax.while_loop(cond, body, (jnp.int32(0), live()))
    o_ref[...] = jnp.concatenate([acc_s[h] for h in range(N_HEADS)], axis=0).T


def _sb_attention(proj, b, s, tq):
    t = b * s
    nq = s // tq
    w = 2 * LANES
    return pl.pallas_call(
        functools.partial(_sb_kernel, tq=tq, s_len=s),
        grid=(b, nq),
        in_specs=[
            pl.BlockSpec((tq, w), lambda bi, i: (bi * nq + i, CB_SQ)),
            pl.BlockSpec((s, w), lambda bi, i: (bi, CB_SK)),
            pl.BlockSpec((s, w), lambda bi, i: (bi, CB_SV)),
        ],
        out_specs=pl.BlockSpec((tq, w), lambda bi, i: (bi * nq + i, 0)),
        out_shape=jax.ShapeDtypeStruct((t, GROUP_WIDTH), F32),
        scratch_shapes=[pltpu.VMEM((s, w), BF16), pltpu.VMEM((w, s), BF16),
                        pltpu.VMEM((N_HEADS, tq, LANES), BF16),
                        pltpu.VMEM((N_HEADS, 1, tq), F32), pltpu.VMEM((N_HEADS, HEAD_DIM, tq), F32),
                        pltpu.SMEM((N_HEADS,), F32)],
        compiler_params=_params(("arbitrary", "arbitrary")),
        name="sb_attention",
    )(proj, proj, proj)


def _post_kernel(a_ref, ap_ref, yb_ref, yc_ref, yd_ref, x_ref, cw_ref, cb_ref, onw_ref, wo_ref, n2w_ref,
                 w1_ref, w2_ref, o_ref, *, tm, s_len, ffc):
    i = pl.program_id(0)
    a = a_ref[...]
    gw = GROUP_WIDTH
    v = a[:, gw:2 * gw] * a[:, 2 * gw:3 * gw]
    ap = ap_ref[...]
    first = (i * tm) % s_len == 0
    vp = jnp.where(first, 0.0, ap[:, gw:2 * gw] * ap[:, 2 * gw:3 * gw])
    row = _row(v.shape)
    v1 = jnp.where(row == 0, vp[7:8, :], pltpu.roll(v, 1, 0))
    v2 = jnp.where(row == 0, vp[6:7, :], jnp.where(row == 1, vp[7:8, :], pltpu.roll(v, 2, 0)))
    conv = cw_ref[0:1, :] * v2 + cw_ref[1:2, :] * v1 + cw_ref[2:3, :] * v
    ya = a[:, 0:gw] * (conv + cb_ref[...])

    mix = None
    for g, y in enumerate((ya, yb_ref[...], yc_ref[...], yd_ref[...])):
        ms = jnp.mean(y * y, axis=-1, keepdims=True)
        yn = (y * lax.rsqrt(ms + EPS) * onw_ref[:, gw * g:gw * (g + 1)]).astype(BF16)
        part = _dot(yn, wo_ref[gw * g:gw * (g + 1), :])
        mix = part if mix is None else mix + part
    x1 = x_ref[...] + mix

    ms = jnp.mean(x1 * x1, axis=-1, keepdims=True)
    h2 = (x1 * lax.rsqrt(ms + EPS) * n2w_ref[...]).astype(BF16)
    ff = None
    for cidx in range(D_FF // ffc):
        u = _dot(h2, w1_ref[:, ffc * cidx:ffc * (cidx + 1)])
        u = jnp.square(jnp.maximum(u, 0.0)).astype(BF16)
        part = _dot(u, w2_ref[ffc * cidx:ffc * (cidx + 1), :])
        ff = part if ff is None else ff + part
    o_ref[...] = x1 + ff


def _post(proj, yb, yc, yd, x2d, cw, cb, onw, wo, n2w, w1, w2, l, s, tm):
    t = x2d.shape[0]
    gw = GROUP_WIDTH
    row = lambda i: (i, 0)
    const = lambda i: (0, 0)
    layer = lambda i: (l, 0, 0)
    once = pl.Buffered(1)
    kern = functools.partial(_post_kernel, tm=tm, s_len=s, ffc=1024)
    return pl.pallas_call(
        kern,
        grid=(t // tm,),
        in_specs=[
            pl.BlockSpec((tm, 3 * gw), row),
            pl.BlockSpec((8, 3 * gw), lambda i: (jnp.maximum(i * (tm // 8) - 1, 0), 0)),
            pl.BlockSpec((tm, gw), row),
            pl.BlockSpec((tm, gw), row),
            pl.BlockSpec((tm, gw), row),
            pl.BlockSpec((tm, D_MODEL), row),
            pl.BlockSpec((3, gw), const),
            pl.BlockSpec((1, gw), const),
            pl.BlockSpec((1, D_MODEL), const),
            pl.BlockSpec((None, D_MODEL, D_MODEL), layer, pipeline_mode=once),
            pl.BlockSpec((1, D_MODEL), const),
            pl.BlockSpec((None, D_MODEL, D_FF), layer, pipeline_mode=once),
            pl.BlockSpec((None, D_FF, D_MODEL), layer, pipeline_mode=once),
        ],
        out_specs=pl.BlockSpec((tm, D_MODEL), row),
        out_shape=jax.ShapeDtypeStruct((t, D_MODEL), F32),
        compiler_params=_params(("arbitrary",)),
        name="post",
    )(proj, proj, yb, yc, yd, x2d, cw, cb, onw, wo, n2w, w1, w2)


def _t5_bucket(dist):
    max_exact = N_BUCKETS // 2
    d = np.maximum(dist, 0)
    large = max_exact + (np.log(np.maximum(d, 1) / max_exact) / math.log(MAX_DISTANCE / max_exact)
                         * (N_BUCKETS - max_exact)).astype(np.int32)
    return np.where(d < max_exact, d, np.minimum(large, N_BUCKETS - 1)).astype(np.int32)


def _tables(s, tq_nsa):
    n_cmp = (s - CMP_LEN) // CMP_STRIDE + 1
    ng = s // CMP_STRIDE
    n_slc = s // SLC_LEN
    tpos = np.arange(s)[None, :]
    n = np.arange(ng)[:, None]
    dist_c = tpos - (n * CMP_STRIDE + CMP_LEN - 1)
    bidxct = np.where((dist_c >= 0) & (n < n_cmp), _t5_bucket(dist_c), -1).astype(np.int32)
    key = np.arange(tq_nsa)[:, None]
    qry = np.arange(tq_nsa)[None, :]
    bidx2t = np.stack([_t5_bucket(qry - key), _t5_bucket(tq_nsa + qry - key)])
    starts = np.arange(n_cmp) * CMP_STRIDE
    ends = starts + CMP_LEN
    s0 = np.arange(n_slc) * SLC_LEN
    s1 = s0 + SLC_LEN
    ovl = np.clip(np.minimum(ends[:, None], s1[None]) - np.maximum(starts[:, None], s0[None]), 0, None) / CMP_LEN
    ovt = np.zeros((LANES, ng), np.float32)
    ovt[:n_slc, :n_cmp] = ovl.T
    emt = (np.arange(LANES)[None, :] == (np.arange(s) // SLC_LEN)[:, None]).astype(np.float32)
    inv = 1.0 / (ROPE_THETA ** (np.arange(0, ROPE_DIM, 2, dtype=np.float64) / ROPE_DIM))
    ang = (np.arange(s, dtype=np.float32)[:, None] * inv.astype(np.float32)[None, :]).astype(np.float64)
    cos, sin = np.cos(ang).astype(np.float32), np.sin(ang).astype(np.float32)
    z32 = np.zeros((s, 32), np.float32)
    z64 = np.zeros((s, 64), np.float32)
    one64 = np.ones((s, 64), np.float32)
    cq_t = np.concatenate([one64, cos, cos, z32], axis=1)
    ck_t = np.concatenate([z64, cos, cos, z32], axis=1)
    sk_t = np.concatenate([z64, -sin, sin, z32], axis=1)
    return dict(bidxct=jnp.asarray(bidxct), bidx2t=jnp.asarray(bidx2t), ovt=jnp.asarray(ovt, BF16),
                emt=jnp.asarray(emt, BF16), rope=(jnp.asarray(cq_t), jnp.asarray(ck_t), jnp.asarray(sk_t)))


def _shift_bound(d, q_gain, k_gain, extra=0.0):
    return (d * jnp.max(jnp.abs(q_gain)) * jnp.max(jnp.abs(k_gain)) * 1.01 + extra + 0.1).reshape(1)


MAX_SHIFT = 60.0


def _pad_cols(w, width):
    return jnp.pad(w, ((0, 0), (0, width - w.shape[1])))


def _layer_weights(l, conv_w, conv_b, nsa_q_norm, nsa_k_norm, cmp_pos, cmp_w1, cmp_w2, mla_q_a_norm,
                   mla_kv_norm, mla_wq_b, mla_wkv_b, mla_q_norm, mla_k_norm, out_norm_w, norm2_w):
    w1 = cmp_w1[l].reshape(2, CMP_LEN, HEAD_DIM, CMP_HIDDEN)
    zw = jnp.zeros((CMP_LEN, HEAD_DIM, CMP_HIDDEN), F32)
    cw1 = jnp.concatenate([jnp.concatenate([w1[0], zw], axis=2), jnp.concatenate([zw, w1[1]], axis=2)],
                          axis=1).astype(BF16)
    cw1 = cw1.reshape(2, CMP_STRIDE * LANES, 2 * CMP_HIDDEN)
    zc = jnp.zeros((CMP_HIDDEN, HEAD_DIM), F32)
    cw2 = jnp.concatenate([jnp.concatenate([cmp_w2[l, 0], zc], axis=1),
                           jnp.concatenate([zc, cmp_w2[l, 1]], axis=1)], axis=0).astype(BF16)
    cpos = jnp.concatenate([cmp_pos[l, 0], cmp_pos[l, 1]], axis=1).reshape(2, CMP_STRIDE * LANES)
    kn = nsa_k_norm[l]
    ones64 = jnp.ones((HEAD_DIM,), F32)
    knw_c = jnp.concatenate([kn[0], ones64])[None, :]
    knw_sw = jnp.stack([jnp.concatenate([kn[1], ones64]), jnp.concatenate([kn[2], ones64])])
    qnw = jnp.tile(nsa_q_norm[l], N_HEADS)[None, :]
    wq = mla_wq_b[l].reshape(Q_LORA, N_HEADS, QK_DIM)
    half = ROPE_DIM // 2
    wqs = jnp.concatenate([jnp.zeros((Q_LORA, N_HEADS, HEAD_DIM), F32), wq[:, :, HEAD_DIM + half:],
                           wq[:, :, HEAD_DIM:HEAD_DIM + half]], axis=2)
    pad_q = lambda w: jnp.pad(w, ((0, 2 * LANES - Q_LORA), (0, 0), (0, LANES - QK_DIM))).reshape(
        2 * LANES, N_HEADS * LANES).astype(BF16)
    wkv = mla_wkv_b[l].reshape(KV_LORA, N_HEADS, 2 * HEAD_DIM)
    wk = jnp.pad(wkv[:, :, :HEAD_DIM], ((0, 0), (0, 0), (0, LANES - HEAD_DIM))).reshape(KV_LORA, N_HEADS * LANES)
    wvt = jnp.pad(wkv[:, :, HEAD_DIM:], ((0, 0), (0, 0), (0, DV_PAD - HEAD_DIM))).reshape(KV_LORA, -1).T
    return dict(
        cw1=cw1, cw2=cw2, cw2t=cw2.T, cpos=cpos, knw_c=knw_c, knw_sw=knw_sw, qnw=qnw,
        qaw=_pad_cols(mla_q_a_norm[l][None, :], 2 * LANES), kvw=mla_kv_norm[l][None, :],
        wq=pad_q(wq), wqs=pad_q(wqs), wk=wk.astype(BF16), wvt=wvt.astype(BF16),
        mqn=_pad_cols(mla_q_norm[l][None, :] * (QK_DIM ** -0.5 * LOG2E), LANES),
        mkn=_pad_cols(mla_k_norm[l][None, :], LANES),
        cw=conv_w[l], cb=conv_b[l][None, :], onw=out_norm_w[l][None, :], n2w=norm2_w[l][None, :])


TM_PROJ = 1024
TM_PREP = 1024
TM_POST = 512
TQ_NSA = 256
TQ_MLA = 256
TQ_SB = 256


def kernel(x, rel_bias, norm1_w, w_in, conv_w, conv_b, nsa_q_norm, nsa_k_norm, cmp_pos, cmp_w1, cmp_w2,
           mla_q_a_norm, mla_kv_norm, mla_wq_b, mla_wkv_b, mla_q_norm, mla_k_norm, out_norm_w, w_out, norm2_w,
           ffn_w1, ffn_w2):
    b, s, d = x.shape
    depth = w_in.shape[0]
    tabs = _tables(s, TQ_NSA)
    x2d = x.reshape(b * s, d)
    wo_b, w1_b, w2_b = w_out.astype(BF16), ffn_w1.astype(BF16), ffn_w2.astype(BF16)
    for l in range(depth):
        w = _layer_weights(l, conv_w, conv_b, nsa_q_norm, nsa_k_norm, cmp_pos, cmp_w1, cmp_w2,
                           mla_q_a_norm, mla_kv_norm, mla_wq_b, mla_wkv_b, mla_q_norm, mla_k_norm, out_norm_w,
                           norm2_w)
        proj = _inproj(x2d, norm1_w[l][None, :], w_in, l, TM_PROJ)
        kcvc, kcvct = _compress(proj, w["cpos"], w["cw1"], w["cw2"], w["cw2t"], w["knw_c"], b, s)
        nsa_shift = _shift_bound(HEAD_DIM, w["qnw"] * (HEAD_DIM ** -0.5 * LOG2E), w["knw_sw"][:, :HEAD_DIM],
                                 jnp.max(jnp.abs(rel_bias)) * LOG2E)
        nsa_args = (proj, kcvc, kcvct, rel_bias, nsa_shift, tabs["bidxct"], tabs["bidx2t"], tabs["ovt"], tabs["emt"],
                    w["qnw"], w["knw_sw"])
        yb = lax.cond(nsa_shift[0] <= MAX_SHIFT,
                      lambda *a: _nsa_attention(*a, b, s, TQ_NSA, True),
                      lambda *a: _nsa_attention(*a, b, s, TQ_NSA, False), *nsa_args)
        mla_shift = _shift_bound(QK_DIM, w["mqn"], w["mkn"])
        qm, km, vmt = _mla_prep(proj, mla_shift, w["qaw"], w["kvw"], w["wq"], w["wqs"], w["wk"], w["wvt"], w["mqn"],
                                w["mkn"], tabs["rope"], s, TM_PREP)
        yc = lax.cond(mla_shift[0] <= MAX_SHIFT,
                      lambda *a: _mla_attention(*a, b, s, TQ_MLA, True),
                      lambda *a: _mla_attention(*a, b, s, TQ_MLA, False), qm, km, vmt)
        yd = _sb_attention(proj, b, s, TQ_SB)
        x2d = _post(proj, yb, yc, yd, x2d, w["cw"], w["cb"], w["onw"], wo_b, w["n2w"], w1_b, w2_b, l, s, TM_POST)
    return x2d.reshape(b, s, d)
```
